```python
import jax, jax.numpy as jnp
from jax import lax
import numpy as np

D_MODEL = 1024
BATCH = 8
SEQ = 4096
DEPTH = 2

ATTN_HEADS = 8
HEAD_DIM = 64
KV_GROUPS = 2
HEADS_PER_GROUP = ATTN_HEADS // KV_GROUPS
ATTN_DIM = ATTN_HEADS * HEAD_DIM
KV_DIM = KV_GROUPS * HEAD_DIM
N_NSA_BRANCHES = 3
CMP_BLOCK = 32
CMP_STRIDE = 16
CMP_HIDDEN = 2 * HEAD_DIM
SLC_BLOCK = 64
N_SELECT = 16
WINDOW = 512
Q_BLOCK = 64
FORCE_BONUS = 1000.0
RWKV_HEADS = 8
RWKV_HEAD_DIM = 64
RWKV_DIM = RWKV_HEADS * RWKV_HEAD_DIM
W_LORA = 64
A_LORA = 64
G_LORA = 128
RWKV_MIX_DIM = 3 * RWKV_DIM + W_LORA + A_LORA + G_LORA
N_BRANCHES = 2
D_FF = 4 * D_MODEL
NSA_DIM = ATTN_DIM + 6 * KV_DIM + N_NSA_BRANCHES * ATTN_HEADS
PROJ_DIM = NSA_DIM + RWKV_MIX_DIM + N_BRANCHES * D_MODEL
RMS_EPS = 1e-6
GN_EPS = 64e-5
NEG_INF = -1e30

kernel_name = "nsa_rwkv7_gated_hybrid"


def rms_norm(x, gain):
    xf = x.astype(jnp.float32)
    y = xf * lax.rsqrt(jnp.mean(xf * xf, axis=-1, keepdims=True) + RMS_EPS)
    return (y * gain.astype(jnp.float32)).astype(x.dtype)


def masked_softmax(s, mask):
    s = jnp.where(mask, s.astype(jnp.float32), NEG_INF)
    return jax.nn.softmax(s, axis=-1) * mask


def compress_blocks(kv, pos, w1, w2):
    S = kv.shape[2]
    n_cmp = (S - CMP_BLOCK) // CMP_STRIDE + 1
    idx = (jnp.arange(n_cmp) * CMP_STRIDE)[:, None] + jnp.arange(CMP_BLOCK)[None, :]
    blocks = kv[:, :, idx] + pos
    flat = blocks.reshape(blocks.shape[:3] + (CMP_BLOCK * HEAD_DIM,))
    return jax.nn.gelu(flat @ w1) @ w2


def nsa_attention(q, kc, vc, ks, vs, kw, vw, gates, q_gain, k_gain, cmp_pos, cmp_w1, cmp_w2):
    B, S, _ = q.shape
    dt = q.dtype
    scale = HEAD_DIM ** -0.5
    qh = rms_norm(q.reshape(B, S, KV_GROUPS, HEADS_PER_GROUP, HEAD_DIM), q_gain)
    qh = qh.transpose(0, 2, 3, 1, 4)
    heads = lambda t: t.reshape(B, S, KV_GROUPS, HEAD_DIM).transpose(0, 2, 1, 3)
    k_cmp = rms_norm(compress_blocks(heads(kc), cmp_pos[0], cmp_w1[0], cmp_w2[0]), k_gain[0])
    v_cmp = compress_blocks(heads(vc), cmp_pos[1], cmp_w1[1], cmp_w2[1])
    k_slc = rms_norm(heads(ks), k_gain[1])
    v_slc = heads(vs)
    k_win = rms_norm(heads(kw), k_gain[2])
    v_win = heads(vw)

    n_cmp = k_cmp.shape[2]
    n_slc = S // SLC_BLOCK
    k_sel = min(N_SELECT, n_slc)
    cmp_start = jnp.arange(n_cmp) * CMP_STRIDE
    cmp_end = cmp_start + CMP_BLOCK - 1
    slc_start = jnp.arange(n_slc) * SLC_BLOCK
    overlap = ((cmp_start[:, None] <= slc_start[None, :] + SLC_BLOCK - 1)
               & (cmp_end[:, None] >= slc_start[None, :])).astype(jnp.float32)
    k_slc_b = k_slc.reshape(B, KV_GROUPS, n_slc, SLC_BLOCK, HEAD_DIM)
    v_slc_b = v_slc.reshape(B, KV_GROUPS, n_slc, SLC_BLOCK, HEAD_DIM)
    pad = ((0, 0), (0, 0), (WINDOW, 0), (0, 0))
    k_win_p = jnp.pad(k_win, pad)
    v_win_p = jnp.pad(v_win, pad)
    b_ix = jnp.arange(B)[:, None, None, None]
    g_ix = jnp.arange(KV_GROUPS)[None, :, None, None]
    blk_ids = jnp.arange(n_slc)

    def block(qb):
        t0 = qb * Q_BLOCK
        tpos = t0 + jnp.arange(Q_BLOCK)
        qblk = lax.dynamic_slice_in_dim(qh, t0, Q_BLOCK, axis=3)
        cmask = cmp_end[None, :] <= tpos[:, None]
        p_c = masked_softmax(jnp.einsum('bghtd,bgnd->bghtn', qblk, k_cmp) * scale, cmask)
        o_c = jnp.einsum('bghtn,bgnd->bghtd', p_c.astype(dt), v_cmp)
        imp = jnp.einsum('bghtn,nj->bgtj', p_c, overlap)
        cur = tpos // SLC_BLOCK
        forced = ((blk_ids[None, :] == 0) | (blk_ids[None, :] == cur[:, None])
                  | (blk_ids[None, :] == cur[:, None] - 1))
        causal_blk = slc_start[None, :] <= tpos[:, None]
        imp = jnp.where(causal_blk, imp + jnp.where(forced, FORCE_BONUS, 0.0), -1.0)
        _, idx = lax.top_k(imp, k_sel)
        kg = k_slc_b[b_ix, g_ix, idx]
        vg = v_slc_b[b_ix, g_ix, idx]
        tok = idx[..., None] * SLC_BLOCK + jnp.arange(SLC_BLOCK)
        smask = (tok <= tpos[None, None, :, None, None]).reshape(B, KV_GROUPS, 1, Q_BLOCK, k_sel * SLC_BLOCK)
        s_s = jnp.einsum('bghtd,bgtkpd->bghtkp', qblk, kg) * scale
        p_s = masked_softmax(s_s.reshape(B, KV_GROUPS, HEADS_PER_GROUP, Q_BLOCK, k_sel * SLC_BLOCK), smask)
        p_s = p_s.reshape(B, KV_GROUPS, HEADS_PER_GROUP, Q_BLOCK, k_sel, SLC_BLOCK).astype(dt)
        o_s = jnp.einsum('bghtkp,bgtkpd->bghtd', p_s, vg)
        kwin = lax.dynamic_slice_in_dim(k_win_p, t0, Q_BLOCK + WINDOW, axis=2)
        vwin = lax.dynamic_slice_in_dim(v_win_p, t0, Q_BLOCK + WINDOW, axis=2)
        kpos = t0 - WINDOW + jnp.arange(Q_BLOCK + WINDOW)
        diff = tpos[:, None] - kpos[None, :]
        wmask = (kpos[None, :] >= 0) & (diff >= 0) & (diff < WINDOW)
        p_w = masked_softmax(jnp.einsum('bghtd,bgsd->bghts', qblk, kwin) * scale, wmask)
        o_w = jnp.einsum('bghts,bgsd->bghtd', p_w.astype(dt), vwin)
        return jnp.stack([o_c, o_s, o_w], axis=-2)

    out = lax.map(block, jnp.arange(S // Q_BLOCK))
    out = out.transpose(1, 0, 4, 2, 3, 5, 6).reshape(B, S, ATTN_HEADS, N_NSA_BRANCHES, HEAD_DIM)
    g = jax.nn.sigmoid(gates.reshape(B, S, ATTN_HEADS, N_NSA_BRANCHES))
    return jnp.einsum('bshcd,bshc->bshd', out, g).reshape(B, S, ATTN_DIM)


def rwkv7_time_mix(p, mu, w0, w_lora_up, a0, a_lora_up, g_lora_up, k_k, k_a, r_k, ln_w, ln_b):
    B, S, _ = p.shape
    f32 = jnp.float32
    dt = p.dtype
    shifted = jnp.pad(p, ((0, 0), (1, 0), (0, 0)))[:, :-1]
    p = p + (shifted - p) * mu
    splits = np.cumsum((RWKV_DIM, RWKV_DIM, RWKV_DIM, W_LORA, A_LORA)).tolist()
    r, k, v, wl, al, gl = jnp.split(p, splits, axis=-1)
    w = -jax.nn.softplus(-(w0 + jnp.tanh(wl) @ w_lora_up)) - 0.5
    decay = jnp.exp(-jnp.exp(w.astype(f32)))
    a = jax.nn.sigmoid(a0 + al @ a_lora_up)
    g = jax.nn.sigmoid(gl) @ g_lora_up
    hd = lambda t: t.reshape(B, S, RWKV_HEADS, RWKV_HEAD_DIM).astype(f32)
    kk = hd(k * k_k)
    kk = kk * lax.rsqrt(jnp.maximum(jnp.sum(kk * kk, axis=-1, keepdims=True), 1e-24))
    k = k * (1.0 + (a - 1.0) * k_a)
    rh, wh, kh, vh, ah = hd(r), hd(decay), hd(k), hd(v), hd(a)

    def step(state, inp):
        r_t, w_t, k_t, v_t, kk_t, a_t = inp
        sa = jnp.einsum('bhvk,bhk->bhv', state, -kk_t)
        state = (state * w_t[:, :, None, :] + sa[..., None] * (kk_t * a_t)[:, :, None, :]
                 + v_t[..., None] * k_t[:, :, None, :])
        return state, jnp.einsum('bhvk,bhk->bhv', state, r_t)

    tm = lambda t: jnp.moveaxis(t, 1, 0)
    state0 = jnp.zeros((B, RWKV_HEADS, RWKV_HEAD_DIM, RWKV_HEAD_DIM), f32)
    _, y = lax.scan(step, state0, (tm(rh), tm(wh), tm(kh), tm(vh), tm(kk), tm(ah)))
    y = jnp.moveaxis(y, 0, 1)
    mean = jnp.mean(y, axis=-1, keepdims=True)
    var = jnp.mean(jnp.square(y - mean), axis=-1, keepdims=True)
    y = ((y - mean) * lax.rsqrt(var + GN_EPS)).reshape(B, S, RWKV_DIM) * ln_w + ln_b
    bonus = jnp.sum(rh * kh * r_k, axis=-1, keepdims=True) * vh
    y = (y + bonus.reshape(B, S, RWKV_DIM)) * g
    return y.astype(dt)


def setup_inputs(seed: int = 0) -> dict:
    key = jax.random.key(seed)
    ks = jax.random.split(key, 32)
    f32 = jnp.float32
    nrm = lambda k, shape, s: jax.random.normal(k, shape, f32) * s
    L = DEPTH
    ramp = jnp.arange(RWKV_DIM, dtype=f32) / (RWKV_DIM - 1)
    return {
        "x": nrm(ks[0], (BATCH, SEQ, D_MODEL), 1.0),
        "mix_norm": 1.0 + nrm(ks[1], (L, D_MODEL), 0.02),
        "w_in": nrm(ks[2], (L, D_MODEL, PROJ_DIM), D_MODEL ** -0.5),
        "q_gain": 1.0 + nrm(ks[3], (L, HEAD_DIM), 0.02),
        "k_gain": 1.0 + nrm(ks[4], (L, N_NSA_BRANCHES, HEAD_DIM), 0.02),
        "cmp_pos": nrm(ks[5], (L, 2, CMP_BLOCK, HEAD_DIM), 0.1),
        "cmp_w1": nrm(ks[6], (L, 2, CMP_BLOCK * HEAD_DIM, CMP_HIDDEN), (CMP_BLOCK * HEAD_DIM) ** -0.5),
        "cmp_w2": nrm(ks[7], (L, 2, CMP_HIDDEN, HEAD_DIM), CMP_HIDDEN ** -0.5),
        "w_attn_branch": nrm(ks[8], (L, ATTN_DIM, D_MODEL), ATTN_DIM ** -0.5),
        "tok_mix": jax.random.uniform(ks[9], (L, RWKV_MIX_DIM), f32),
        "w0": -6.0 + 5.0 * ramp ** 0.9 + nrm(ks[10], (L, RWKV_DIM), 0.1),
        "w_lora_up": nrm(ks[11], (L, W_LORA, RWKV_DIM), 0.5 * W_LORA ** -0.5),
        "a0": nrm(ks[12], (L, RWKV_DIM), 0.1),
        "a_lora_up": nrm(ks[13], (L, A_LORA, RWKV_DIM), A_LORA ** -0.5),
        "g_lora_up": nrm(ks[14], (L, G_LORA, RWKV_DIM), G_LORA ** -0.5),
        "k_k": 0.85 + nrm(ks[15], (L, RWKV_DIM), 0.02),
        "k_a": 1.0 + nrm(ks[16], (L, RWKV_DIM), 0.02),
        "r_k": nrm(ks[17], (L, RWKV_HEADS, RWKV_HEAD_DIM), 0.1),
        "ln_x_w": 1.0 + nrm(ks[18], (L, RWKV_DIM), 0.02),
        "ln_x_b": nrm(ks[19], (L, RWKV_DIM), 0.02),
        "w_rwkv_branch": nrm(ks[20], (L, RWKV_DIM, D_MODEL), RWKV_DIM ** -0.5),
        "w_out": nrm(ks[21], (L, D_MODEL, D_MODEL), D_MODEL ** -0.5),
        "ffn_norm": 1.0 + nrm(ks[22], (L, D_MODEL), 0.02),
        "w_ffn_up": nrm(ks[23], (L, D_MODEL, D_FF), D_MODEL ** -0.5),
        "w_ffn_down": nrm(ks[24], (L, D_FF, D_MODEL), D_FF ** -0.5),
    }


def reference(x, mix_norm, w_in, q_gain, k_gain, cmp_pos, cmp_w1, cmp_w2, w_attn_branch,
              tok_mix, w0, w_lora_up, a0, a_lora_up, g_lora_up, k_k, k_a, r_k, ln_x_w, ln_x_b,
              w_rwkv_branch, w_out, ffn_norm, w_ffn_up, w_ffn_down):
    nsa_splits = np.cumsum((ATTN_DIM,) + (KV_DIM,) * 6).tolist()
    for l in range(DEPTH):
        u = rms_norm(x, mix_norm[l])
        proj = u @ w_in[l]
        nsa_p, rwkv_p, gate_p = jnp.split(proj, [NSA_DIM, NSA_DIM + RWKV_MIX_DIM], axis=-1)
        q, kc, vc, ks_, vs_, kw, vw, nsa_g = jnp.split(nsa_p, nsa_splits, axis=-1)
        y_a = nsa_attention(q, kc, vc, ks_, vs_, kw, vw, nsa_g, q_gain[l], k_gain[l],
                            cmp_pos[l], cmp_w1[l], cmp_w2[l])
        y_b = rwkv7_time_mix(rwkv_p, tok_mix[l], w0[l], w_lora_up[l], a0[l], a_lora_up[l],
                             g_lora_up[l], k_k[l], k_a[l], r_k[l], ln_x_w[l], ln_x_b[l])
        g_a, g_b = jnp.split(gate_p, N_BRANCHES, axis=-1)
        merged = (jax.nn.sigmoid(g_a) * (y_a @ w_attn_branch[l])
                  + jax.nn.sigmoid(g_b) * (y_b @ w_rwkv_branch[l]))
        x = x + merged @ w_out[l]
        h = rms_norm(x, ffn_norm[l])
        x = x + jnp.square(jax.nn.relu(h @ w_ffn_up[l])) @ w_ffn_down[l]
    return x
```

```python
import functools

import jax
import jax.numpy as jnp
import numpy as np
from jax import lax
from jax.experimental import pallas as pl
from jax.experimental.pallas import tpu as pltpu

F32 = jnp.float32
BF16 = jnp.bfloat16
I32 = jnp.int32

D_MODEL = 1024
ATTN_HEADS = 8
HEAD_DIM = 64
KV_GROUPS = 2
HEADS_PER_GROUP = ATTN_HEADS // KV_GROUPS
ATTN_DIM = ATTN_HEADS * HEAD_DIM
KV_DIM = KV_GROUPS * HEAD_DIM
N_NSA_BRANCHES = 3
CMP_BLOCK = 32
CMP_STRIDE = 16
CMP_HIDDEN = 2 * HEAD_DIM
SLC_BLOCK = 64
N_SELECT = 16
WINDOW = 512
FORCE_BONUS = 1000.0
RWKV_HEADS = 8
RWKV_HEAD_DIM = 64
RWKV_DIM = RWKV_HEADS * RWKV_HEAD_DIM
W_LORA = 64
A_LORA = 64
G_LORA = 128
RWKV_MIX_DIM = 3 * RWKV_DIM + W_LORA + A_LORA + G_LORA
D_FF = 4 * D_MODEL
NSA_GATES = N_NSA_BRANCHES * ATTN_HEADS
NSA_DIM = ATTN_DIM + 6 * KV_DIM + NSA_GATES
RMS_EPS = 1e-6
GN_EPS = 64e-5
NEG_INF = -1e30

LOG2_64 = 6
LANES = 128
Q_PAD = ATTN_HEADS * LANES
RWKV_PAIRS = RWKV_HEADS // 2
RWKV_CHUNK = 64
VMEM_LIMIT = 48 * 1024 * 1024

C_Q = 0
C_KV = C_Q + Q_PAD
C_GATE = C_KV + 6 * KV_DIM
C_RWKV = C_GATE + LANES
C_MERGE = C_RWKV + RWKV_MIX_DIM
C_TOTAL = C_MERGE + 2 * D_MODEL


def _nn(a, b):
    return lax.dot_general(a, b, (((1,), (0,)), ((), ())), preferred_element_type=F32)


def _nt(a, b):
    return lax.dot_general(a, b, (((1,), (1,)), ((), ())), preferred_element_type=F32)


def _tn(a, b):
    return lax.dot_general(a, b, (((0,), (0,)), ((), ())), preferred_element_type=F32)


def _split2(x):
    hi = x.astype(BF16)
    lo = (x - hi.astype(F32)).astype(BF16)
    return hi, lo


def _iota(shape, dim):
    return lax.broadcasted_iota(I32, shape, dim)


def _params(*sem):
    return pltpu.CompilerParams(dimension_semantics=sem, vmem_limit_bytes=VMEM_LIMIT)


def _group_rms(x, gain):
    lane = _iota((1, LANES), 1)
    sq = x * x
    lo = lane < HEAD_DIM
    s0 = jnp.sum(jnp.where(lo, sq, 0.0), axis=-1, keepdims=True)
    s1 = jnp.sum(jnp.where(lo, 0.0, sq), axis=-1, keepdims=True)
    ms = jnp.where(lo, s0, s1) * (1.0 / HEAD_DIM)
    return x * lax.rsqrt(ms + RMS_EPS) * gain


def _proj_kernel(x_ref, g_ref, w_ref, qg_ref, kg_ref,
                 q_ref, kc_ref, vc_ref, ks_ref, vs_ref, kw_ref, vw_ref, gt_ref, rw_ref, mg_ref):
    x = x_ref[...]
    ms = jnp.mean(x * x, axis=-1, keepdims=True)
    u = (x * lax.rsqrt(ms + RMS_EPS) * g_ref[...]).astype(BF16)

    def col(c0, width):
        return _nn(u, w_ref[:, c0:c0 + width])

    scale = HEAD_DIM ** -0.5
    for j in range(ATTN_HEADS):
        qj = col(C_Q + j * LANES, LANES)
        msq = jnp.sum(qj * qj, axis=-1, keepdims=True) * (1.0 / HEAD_DIM)
        qn = qj * lax.rsqrt(msq + RMS_EPS) * qg_ref[:, j * LANES:(j + 1) * LANES]
        q_ref[:, j * LANES:(j + 1) * LANES] = (qn * scale).astype(BF16)
    kc_ref[...] = col(C_KV + 0 * KV_DIM, KV_DIM)
    vc_ref[...] = col(C_KV + 1 * KV_DIM, KV_DIM)
    ks_ref[...] = _group_rms(col(C_KV + 2 * KV_DIM, KV_DIM), kg_ref[0:1, :]).astype(BF16)
    vs_ref[...] = col(C_KV + 3 * KV_DIM, KV_DIM).astype(BF16)
    kw_ref[...] = _group_rms(col(C_KV + 4 * KV_DIM, KV_DIM), kg_ref[1:2, :]).astype(BF16)
    vw_ref[...] = col(C_KV + 5 * KV_DIM, KV_DIM).astype(BF16)
    gt_ref[...] = col(C_GATE, LANES)
    for c in range(RWKV_MIX_DIM // LANES):
        rw_ref[:, c * LANES:(c + 1) * LANES] = col(C_RWKV + c * LANES, LANES)
    for c in range(2 * D_MODEL // 512):
        mg_ref[:, c * 512:(c + 1) * 512] = col(C_MERGE + c * 512, 512)


def _proj_call(x2, gain, w_all, qg_pad, kg2, tm=256):
    n = x2.shape[0]
    row = lambda w: pl.BlockSpec((tm, w), lambda i: (i, 0))
    full = lambda a: pl.BlockSpec(a.shape, lambda i: (0,) * a.ndim)
    out_shapes = [
        jax.ShapeDtypeStruct((n, Q_PAD), BF16),
        jax.ShapeDtypeStruct((n, KV_DIM), F32), jax.ShapeDtypeStruct((n, KV_DIM), F32),
        jax.ShapeDtypeStruct((n, KV_DIM), BF16), jax.ShapeDtypeStruct((n, KV_DIM), BF16),
        jax.ShapeDtypeStruct((n, KV_DIM), BF16), jax.ShapeDtypeStruct((n, KV_DIM), BF16),
        jax.ShapeDtypeStruct((n, LANES), F32),
        jax.ShapeDtypeStruct((n, RWKV_MIX_DIM), F32),
        jax.ShapeDtypeStruct((n, 2 * D_MODEL), F32),
    ]
    return pl.pallas_call(
        _proj_kernel,
        grid=(n // tm,),
        in_specs=[row(D_MODEL), full(gain), full(w_all), full(qg_pad), full(kg2)],
        out_specs=[row(s.shape[1]) for s in out_shapes],
        out_shape=out_shapes,
        compiler_params=_params("parallel"),
        name="proj",
    )(x2, gain, w_all, qg_pad, kg2)


def _compress_kernel(x_ref, w1_ref, w2_ref, pos_ref, kg_ref, o_ref, *, nc):
    which = pl.program_id(0)
    half = CMP_STRIDE * HEAD_DIM
    x = x_ref[0, 0, 0]
    ha = _nn(x, w1_ref[0, 0:half, :])
    hb = _nn(x, w1_ref[0, half:2 * half, :])
    pos8 = jnp.broadcast_to(pos_ref[0], (8, 2 * half)).astype(BF16)
    bias = _nn(pos8, w1_ref[0])[0:1, :]
    hid = ha + pltpu.roll(hb, nc - 1, axis=0) + bias
    act = jax.nn.gelu(hid)
    out = _nn(act.astype(BF16), w2_ref[0])
    ms = jnp.mean(out * out, axis=-1, keepdims=True)
    normed = out * lax.rsqrt(ms + RMS_EPS) * kg_ref[...]
    out = jnp.where(which == 0, normed, out)
    rowi = _iota((nc, 1), 0)
    o_ref[0, 0, 0] = jnp.where(rowi < nc - 1, out, 0.0)


def _compress_call(xg, w1, w2, pos_flat, kgain):
    _, b, g, nc, width = xg.shape
    return pl.pallas_call(
        functools.partial(_compress_kernel, nc=nc),
        grid=(2, b, g),
        in_specs=[
            pl.BlockSpec((1, 1, 1, nc, width), lambda w, i, j: (w, i, j, 0, 0)),
            pl.BlockSpec((1, width * 2, CMP_HIDDEN), lambda w, i, j: (w, 0, 0)),
            pl.BlockSpec((1, CMP_HIDDEN, HEAD_DIM), lambda w, i, j: (w, 0, 0)),
            pl.BlockSpec((1, 1, width * 2), lambda w, i, j: (w, 0, 0)),
            pl.BlockSpec((1, HEAD_DIM), lambda w, i, j: (0, 0)),
        ],
        out_specs=pl.BlockSpec((1, 1, 1, nc, HEAD_DIM), lambda w, i, j: (w, i, j, 0, 0)),
        out_shape=jax.ShapeDtypeStruct((2, b, g, nc, HEAD_DIM), F32),
        compiler_params=_params("parallel", "parallel", "parallel"),
        name="compress",
    )(xg, w1, w2, pos_flat, kgain)


def _softmax_rows(s, mask):
    sm = jnp.where(mask[None], s, NEG_INF)
    mx = jnp.max(sm, axis=-1, keepdims=True)
    e = jnp.exp(sm - mx)
    den = jnp.sum(e, axis=-1, keepdims=True)
    return jnp.where(mask[None], e / den, 0.0)


def _nsa_kernel(q_ref, gt_ref, kc_ref, vc_ref, ks_ref, vs_ref, kw_ref, vw_ref, e3_ref, y_ref,
                *, tq, nb, nc):
    hp = HEADS_PER_GROUP
    rows = hp * tq
    wk = WINDOW + tq
    qi = pl.program_id(1)
    q0 = qi * tq
    tcol = q0 + _iota((tq, 1), 0)
    trow = q0 + _iota((1, tq), 1)
    lane = _iota((1, LANES), 1)
    gates = jax.nn.sigmoid(gt_ref[0])

    n_row = _iota((1, nc), 1)
    cmask = (n_row * CMP_STRIDE + (CMP_BLOCK - 1) <= tcol) & (n_row < nc - 1)
    jcol = _iota((nb, 1), 0)
    ncmp = _iota((nb, nc), 1)
    ov_t = ((ncmp * CMP_STRIDE <= jcol * SLC_BLOCK + (SLC_BLOCK - 1))
            & (ncmp * CMP_STRIDE + (CMP_BLOCK - 1) >= jcol * SLC_BLOCK))
    ov_t = jnp.where(ov_t, 1.0, 0.0).astype(BF16)
    cur = jnp.right_shift(trow, LOG2_64)
    forced = (jcol == 0) | (jcol == cur) | (jcol == cur - 1)
    causal_blk = jcol * SLC_BLOCK <= trow
    eye = jnp.where(_iota((tq, tq), 0) == _iota((tq, tq), 1), 1.0, 0.0).astype(BF16)

    for g in range(KV_GROUPS):
        qg = jnp.concatenate(
            [q_ref[0, :, (g * hp + h) * LANES:(g * hp + h + 1) * LANES] for h in range(hp)], axis=0)

        s_c = _nt(qg, kc_ref[0]).reshape(hp, tq, nc)
        p_c = _softmax_rows(s_c, cmask)
        o_c = _nn(p_c.reshape(rows, nc).astype(BF16), vc_ref[0]).reshape(hp, tq, LANES)

        psum = p_c[0] + p_c[1] + p_c[2] + p_c[3]
        hi, lo = _split2(psum)
        imp = _nt(ov_t, hi) + _nt(ov_t, lo)
        imp = jnp.where(causal_blk, imp + jnp.where(forced, FORCE_BONUS, 0.0), -1.0)
        rank = jnp.zeros((nb, tq), F32)
        for i in range(nb):
            ri = imp[i:i + 1, :]
            rank = rank + jnp.where(jcol > i, jnp.where(ri >= imp, 1.0, 0.0),
                                    jnp.where(ri > imp, 1.0, 0.0))
        sel_t = jnp.where(rank < float(min(N_SELECT, nb)), 1.0, 0.0).astype(BF16)
        sel = _nt(eye, sel_t).astype(BF16)

        def body(kt, carry):
            m, l, acc = carry
            k0 = pl.multiple_of(kt * tq, tq)
            s = _nt(qg, ks_ref[0, pl.ds(k0, tq), :]).reshape(hp, tq, tq)
            chosen = _nn(sel, e3_ref[kt]) > 0.5
            key = k0 + _iota((1, tq), 1)
            mask = chosen & (key <= tcol)
            sm = jnp.where(mask[None], s, NEG_INF)
            m_new = jnp.maximum(m, jnp.max(sm, axis=-1, keepdims=True))
            alpha = jnp.exp(m - m_new)
            p = jnp.where(mask[None], jnp.exp(sm - m_new), 0.0)
            l = alpha * l + jnp.sum(p, axis=-1, keepdims=True)
            pv = _nn(p.reshape(rows, tq).astype(BF16), vs_ref[0, pl.ds(k0, tq), :])
            return m_new, l, alpha * acc + pv.reshape(hp, tq, LANES)

        init = (jnp.full((hp, tq, 1), NEG_INF, F32), jnp.zeros((hp, tq, 1), F32),
                jnp.zeros((hp, tq, LANES), F32))
        _, l_s, acc_s = lax.fori_loop(0, qi + 1, body, init)
        o_s = jnp.where(l_s > 0.0, acc_s / l_s, 0.0)

        w0 = pl.multiple_of(jnp.maximum(q0 - WINDOW, 0), tq)
        s_w = _nt(qg, kw_ref[0, pl.ds(w0, wk), :]).reshape(hp, tq, wk)
        dist = tcol - (w0 + _iota((1, wk), 1))
        p_w = _softmax_rows(s_w, (dist >= 0) & (dist < WINDOW))
        o_w = _nn(p_w.reshape(rows, wk).astype(BF16), vw_ref[0, pl.ds(w0, wk), :]).reshape(hp, tq, LANES)

        keep = (lane >= g * HEAD_DIM) & (lane < (g + 1) * HEAD_DIM)
        for h in range(hp):
            c0 = (g * hp + h) * N_NSA_BRANCHES
            o = (gates[:, c0:c0 + 1] * o_c[h] + gates[:, c0 + 1:c0 + 2] * o_s[h]
                 + gates[:, c0 + 2:c0 + 3] * o_w[h])
            j = g * hp + h
            y_ref[0, :, j * LANES:(j + 1) * LANES] = jnp.where(keep, o, 0.0).astype(BF16)


def _nsa_call(q, gates, kc, vc, ks, vs, kw, vw, e3, tq=128):
    b, s, _ = q.shape
    nb = s // SLC_BLOCK
    nc = kc.shape[1]
    tile = lambda w: pl.BlockSpec((1, tq, w), lambda i, j: (i, j, 0))
    seq = lambda a: pl.BlockSpec((1,) + a.shape[1:], lambda i, j: (i, 0, 0))
    return pl.pallas_call(
        functools.partial(_nsa_kernel, tq=tq, nb=nb, nc=nc),
        grid=(b, s // tq),
        in_specs=[tile(Q_PAD), tile(LANES), seq(kc), seq(vc), seq(ks), seq(vs), seq(kw), seq(vw),
                  pl.BlockSpec(e3.shape, lambda i, j: (0, 0, 0))],
        out_specs=tile(Q_PAD),
        out_shape=jax.ShapeDtypeStruct((b, s, Q_PAD), BF16),
        compiler_params=_params("parallel", "arbitrary"),
        name="nsa",
    )(q, gates, kc, vc, ks, vs, kw, vw, e3)


def _seg_sum(x, bd):
    hi, lo = _split2(x)
    return _nn(hi, bd) + _nn(lo, bd)


def _stack_heads(x):
    lane = _iota((1, LANES), 1)
    lo = lane < RWKV_HEAD_DIM
    return jnp.concatenate([jnp.where(lo, x, 0.0), jnp.where(lo, 0.0, x)], axis=0)


def _rwkv_kernel(p_ref, mu_ref, w0_ref, wup_ref, a0_ref, aup_ref, gup_ref, kk_ref, ka_ref, rk_ref,
                 lnw_ref, lnb_ref, o_ref,
                 carry_ref, state_ref, r_s, lw_s, k_s, v_s, kk_s, a_s, y_s, *, tt):
    ch = RWKV_CHUNK
    d = RWKV_DIM
    first = pl.program_id(1) == 0

    @pl.when(first)
    def _():
        carry_ref[...] = jnp.zeros_like(carry_ref)
        state_ref[...] = jnp.zeros_like(state_ref)

    p = p_ref[0]
    prev = pltpu.roll(p, 1, axis=0)
    prev = jnp.where(_iota((tt, 1), 0) == 0, carry_ref[0:1, :], prev)
    carry_ref[0:1, :] = p[tt - 1:tt, :]
    pm = p + (prev - p) * mu_ref[...]

    r = pm[:, 0:d]
    k = pm[:, d:2 * d]
    v = pm[:, 2 * d:3 * d]
    wa = pm[:, 3 * d:3 * d + LANES]
    gl = pm[:, 3 * d + LANES:3 * d + 2 * LANES]

    z = w0_ref[...] + _nn(jnp.tanh(wa).astype(BF16), wup_ref[...])
    softplus = jnp.maximum(-z, 0.0) + jnp.log(1.0 + jnp.exp(-jnp.abs(z)))
    lw = -jnp.exp(-softplus - 0.5)
    a = jax.nn.sigmoid(a0_ref[...] + _nn(wa.astype(BF16), aup_ref[...]))
    gate = _nn(jax.nn.sigmoid(gl).astype(BF16), gup_ref[...])

    seg = jnp.right_shift(_iota((d, d), 0), LOG2_64) == jnp.right_shift(_iota((d, d), 1), LOG2_64)
    bd = jnp.where(seg, 1.0, 0.0).astype(BF16)
    kk = k * kk_ref[...]
    kk = kk * lax.rsqrt(jnp.maximum(_seg_sum(kk * kk, bd), 1e-24))
    k2 = k * (1.0 + (a - 1.0) * ka_ref[...])

    r_s[...] = r
    lw_s[...] = lw
    k_s[...] = k2
    v_s[...] = v
    kk_s[...] = kk
    a_s[...] = a

    c2 = 2 * ch
    ri = _iota((c2, c2), 0)
    ci = _iota((c2, c2), 1)
    same = jnp.right_shift(ri, LOG2_64) == jnp.right_shift(ci, LOG2_64)
    strict = jnp.where(same & (ci < ri), 1.0, 0.0)
    incl = jnp.where(same & (ci <= ri), 1.0, 0.0)
    ident = jnp.where(ri == ci, 1.0, 0.0)
    tri = jnp.where(_iota((ch, ch), 1) <= _iota((ch, ch), 0), 1.0, 0.0).astype(BF16)

    def chunk(c, _):
        t0 = pl.multiple_of(c * ch, ch)
        for pr in range(RWKV_PAIRS):
            ls = slice(pr * LANES, (pr + 1) * LANES)
            rr = r_s[pl.ds(t0, ch), ls]
            lww = lw_s[pl.ds(t0, ch), ls]
            kk2 = k_s[pl.ds(t0, ch), ls]
            vv = v_s[pl.ds(t0, ch), ls]
            kkn = kk_s[pl.ds(t0, ch), ls]
            aa = a_s[pl.ds(t0, ch), ls]
            g_st = state_ref[pr]

            hi, lo = _split2(lww)
            cum = _nn(tri, hi) + _nn(tri, lo)
            end = cum[ch - 1:ch, :]
            e_prev = jnp.exp(cum - lww)
            e_cur = jnp.exp(cum)
            e_inv = jnp.exp(-cum)
            e_end = jnp.exp(end - cum)
            bb = kkn * aa
            a_t = _stack_heads(-kkn * e_prev)
            r_t = _stack_heads(rr * e_cur)
            b_h = _stack_heads(bb * e_inv)
            k_h = _stack_heads(kk2 * e_inv)
            b_e = _stack_heads(bb * e_end)
            k_e = _stack_heads(kk2 * e_end)
            v_t = _stack_heads(vv)

            ar = jnp.concatenate([a_t, r_t], axis=0).astype(BF16)
            bk = jnp.concatenate([b_h, k_h], axis=0).astype(BF16)
            scores = _nt(ar, bk)
            m_ab = scores[0:c2, 0:c2] * strict
            m_ak = scores[0:c2, c2:2 * c2] * strict
            m_rb = scores[c2:2 * c2, 0:c2] * incl
            m_rk = scores[c2:2 * c2, c2:2 * c2] * incl

            tinv = ident + m_ab
            mp = m_ab
            for _i in range(5):
                mpb = mp.astype(BF16)
                mp = _nn(mpb, mpb)
                tinv = tinv + _nn(tinv.astype(BF16), mp.astype(BF16))

            v_b = v_t.astype(BF16)
            rhs = jnp.concatenate([a_t, _nn(m_ak.astype(BF16), v_b)], axis=1)
            wu = _nn(tinv.astype(BF16), rhs.astype(BF16))
            w_t = wu[:, 0:c2]
            u0 = wu[:, c2:2 * c2]
            g_b = g_st.astype(BF16)
            u = _nt(w_t.astype(BF16), g_b) + u0
            u_b = u.astype(BF16)
            y = (_nt(r_t.astype(BF16), g_b)
                 + _nn(jnp.concatenate([m_rb, m_rk], axis=1).astype(BF16),
                       jnp.concatenate([u_b, v_b], axis=0)))
            y_s[pl.ds(t0, ch), ls] = y[0:ch, :] + y[ch:c2, :]
            uv = jnp.concatenate([u_b, v_b], axis=0)
            bke = jnp.concatenate([b_e, k_e], axis=0).astype(BF16)
            state_ref[pr] = g_st * jnp.exp(end) + _tn(uv, bke)
        return 0

    lax.fori_loop(0, tt // ch, chunk, 0)

    y = y_s[...]
    inv = 1.0 / RWKV_HEAD_DIM
    mean = _seg_sum(y, bd) * inv
    yc = y - mean
    var = _seg_sum(yc * yc, bd) * inv
    yn = yc * lax.rsqrt(var + GN_EPS) * lnw_ref[...] + lnb_ref[...]
    bonus = _seg_sum(r * k2 * rk_ref[...], bd) * v
    o_ref[0] = ((yn + bonus) * gate).astype(BF16)


def _rwkv_call(p, mu, w0, wup, a0, aup, gup, k_k, k_a, r_k, ln_w, ln_b, tt=256):
    b, s, _ = p.shape
    full = lambda a: pl.BlockSpec(a.shape, lambda i, j: (0,) * a.ndim)
    consts = (mu, w0, wup, a0, aup, gup, k_k, k_a, r_k, ln_w, ln_b)
    tile = lambda: pltpu.VMEM((tt, RWKV_DIM), F32)
    return pl.pallas_call(
        functools.partial(_rwkv_kernel, tt=tt),
        grid=(b, s // tt),
        in_specs=[pl.BlockSpec((1, tt, RWKV_MIX_DIM), lambda i, j: (i, j, 0))]
                 + [full(c) for c in consts],
        out_specs=pl.BlockSpec((1, tt, RWKV_DIM), lambda i, j: (i, j, 0)),
        out_shape=jax.ShapeDtypeStruct((b, s, RWKV_DIM), BF16),
        scratch_shapes=[pltpu.VMEM((8, RWKV_MIX_DIM), F32),
                        pltpu.VMEM((RWKV_PAIRS, LANES, LANES), F32),
                        tile(), tile(), tile(), tile(), tile(), tile(), tile()],
        compiler_params=_params("arbitrary", "arbitrary"),
        name="rwkv",
    )(p, *consts)


def _merge_kernel(x_ref, ya_ref, yb_ref, mg_ref, wa_ref, wb_ref, wo_ref, fg_ref, xo_ref, h_ref):
    ga = jax.nn.sigmoid(mg_ref[:, 0:D_MODEL])
    gb = jax.nn.sigmoid(mg_ref[:, D_MODEL:2 * D_MODEL])
    merged = ga * _nn(ya_ref[...], wa_ref[...]) + gb * _nn(yb_ref[...], wb_ref[...])
    xn = x_ref[...] + _nn(merged.astype(BF16), wo_ref[...])
    xo_ref[...] = xn
    ms = jnp.mean(xn * xn, axis=-1, keepdims=True)
    h_ref[...] = (xn * lax.rsqrt(ms + RMS_EPS) * fg_ref[...]).astype(BF16)


def _merge_call(x2, ya, yb, mg, wa_pad, wb, wo, fgain, tm=512):
    n = x2.shape[0]
    row = lambda w: pl.BlockSpec((tm, w), lambda i: (i, 0))
    full = lambda a: pl.BlockSpec(a.shape, lambda i: (0,) * a.ndim)
    return pl.pallas_call(
        _merge_kernel,
        grid=(n // tm,),
        in_specs=[row(D_MODEL), row(Q_PAD), row(RWKV_DIM), row(2 * D_MODEL),
                  full(wa_pad), full(wb), full(wo), full(fgain)],
        out_specs=[row(D_MODEL), row(D_MODEL)],
        out_shape=[jax.ShapeDtypeStruct((n, D_MODEL), F32), jax.ShapeDtypeStruct((n, D_MODEL), BF16)],
        compiler_params=_params("parallel"),
        name="merge",
    )(x2, ya, yb, mg, wa_pad, wb, wo, fgain)


def _ffn_kernel(x_ref, h_ref, wu_ref, wd_ref, o_ref, *, fc):
    h = h_ref[...]
    acc = x_ref[...]
    for c in range(D_FF // fc):
        up = jnp.maximum(_nn(h, wu_ref[:, c * fc:(c + 1) * fc]), 0.0)
        acc = acc + _nn((up * up).astype(BF16), wd_ref[c * fc:(c + 1) * fc, :])
    o_ref[...] = acc


def _ffn_call(x2, h, wu, wd, tm=512, fc=1024):
    n = x2.shape[0]
    row = lambda w: pl.BlockSpec((tm, w), lambda i: (i, 0))
    full = lambda a: pl.BlockSpec(a.shape, lambda i: (0,) * a.ndim)
    return pl.pallas_call(
        functools.partial(_ffn_kernel, fc=fc),
        grid=(n // tm,),
        in_specs=[row(D_MODEL), row(D_MODEL), full(wu), full(wd)],
        out_specs=row(D_MODEL),
        out_shape=jax.ShapeDtypeStruct((n, D_MODEL), F32),
        compiler_params=_params("parallel"),
        name="ffn",
    )(x2, h, wu, wd)


def _pad_heads_cols(w):
    rows = w.shape[0]
    wh = w.reshape(rows, ATTN_HEADS, HEAD_DIM)
    slots = []
    for j in range(ATTN_HEADS):
        g = j // HEADS_PER_GROUP
        z = jnp.zeros((rows, HEAD_DIM), w.dtype)
        slots.append(jnp.concatenate([wh[:, j], z] if g == 0 else [z, wh[:, j]], axis=1))
    return jnp.concatenate(slots, axis=1)


def _proj_weight(w_in):
    q = _pad_heads_cols(w_in[:, 0:ATTN_DIM])
    kv = w_in[:, ATTN_DIM:ATTN_DIM + 6 * KV_DIM]
    gates = jnp.pad(w_in[:, ATTN_DIM + 6 * KV_DIM:NSA_DIM], ((0, 0), (0, LANES - NSA_GATES)))
    rest = w_in[:, NSA_DIM:]
    return jnp.concatenate([q, kv, gates, rest], axis=1).astype(BF16)


def _to_groups16(t, b, s):
    t = t.reshape(b, s // CMP_STRIDE, CMP_STRIDE, KV_GROUPS, HEAD_DIM)
    t = t.transpose(0, 3, 1, 2, 4)
    return t.reshape(b, KV_GROUPS, s // CMP_STRIDE, CMP_STRIDE * HEAD_DIM)


def kernel(x, mix_norm, w_in, q_gain, k_gain, cmp_pos, cmp_w1, cmp_w2, w_attn_branch, tok_mix, w0,
           w_lora_up, a0, a_lora_up, g_lora_up, k_k, k_a, r_k, ln_x_w, ln_x_b, w_rwkv_branch, w_out,
           ffn_norm, w_ffn_up, w_ffn_down):
    b, s, d = x.shape
    n = b * s
    depth = w_in.shape[0]
    nq = s // 128
    key_blk = (jnp.arange(s, dtype=I32) // SLC_BLOCK).reshape(nq, 1, 128)
    e3 = (key_blk == jnp.arange(s // SLC_BLOCK, dtype=I32)[None, :, None]).astype(BF16)
    row = lambda v: v.reshape(1, -1)
    x2 = x.reshape(n, d)
    for l in range(depth):
        qg_pad = _pad_heads_cols(jnp.tile(q_gain[l], ATTN_HEADS).reshape(1, ATTN_DIM))
        kg2 = jnp.tile(k_gain[l, 1:3], (1, KV_GROUPS))
        q, kc, vc, ks, vs, kw, vw, gates, rw, mg = _proj_call(
            x2, row(mix_norm[l]), _proj_weight(w_in[l]), qg_pad, kg2)

        xg = jnp.stack([_to_groups16(kc, b, s), _to_groups16(vc, b, s)]).astype(BF16)
        cmp = _compress_call(xg, cmp_w1[l].astype(BF16), cmp_w2[l].astype(BF16),
                             cmp_pos[l].reshape(2, 1, CMP_BLOCK * HEAD_DIM), row(k_gain[l, 0]))
        cmp = cmp.transpose(0, 1, 3, 2, 4).reshape(2, b, s // CMP_STRIDE, KV_DIM).astype(BF16)

        seq = lambda t: t.reshape(b, s, t.shape[-1])
        ya = _nsa_call(seq(q), seq(gates), cmp[0], cmp[1], seq(ks), seq(vs), seq(kw), seq(vw), e3)

        zero = jnp.zeros((W_LORA, RWKV_DIM), F32)
        wup = jnp.concatenate([w_lora_up[l], zero], axis=0).astype(BF16)
        aup = jnp.concatenate([zero, a_lora_up[l]], axis=0).astype(BF16)
        yb = _rwkv_call(seq(rw), row(tok_mix[l]), row(w0[l]), wup, row(a0[l]), aup,
                        g_lora_up[l].astype(BF16), row(k_k[l]), row(k_a[l]), row(r_k[l]),
                        row(ln_x_w[l]), row(ln_x_b[l]))

        wa_pad = _pad_heads_cols(w_attn_branch[l].T).T.astype(BF16)
        x2, h = _merge_call(x2, ya.reshape(n, Q_PAD), yb.reshape(n, RWKV_DIM), mg, wa_pad,
                            w_rwkv_branch[l].astype(BF16), w_out[l].astype(BF16), row(ffn_norm[l]))
        x2 = _ffn_call(x2, h, w_ffn_up[l].astype(BF16), w_ffn_down[l].astype(BF16))
    return x2.reshape(b, s, d)
```

```python
import functools

import jax
import jax.numpy as jnp
import numpy as np
from jax import lax
from jax.experimental import pallas as pl
from jax.experimental.pallas import tpu as pltpu

F32 = jnp.float32
BF16 = jnp.bfloat16
I32 = jnp.int32

D_MODEL = 1024
ATTN_HEADS = 8
HEAD_DIM = 64
KV_GROUPS = 2
HEADS_PER_GROUP = ATTN_HEADS // KV_GROUPS
ATTN_DIM = ATTN_HEADS * HEAD_DIM
KV_DIM = KV_GROUPS * HEAD_DIM
N_NSA_BRANCHES = 3
CMP_BLOCK = 32
CMP_STRIDE = 16
CMP_HIDDEN = 2 * HEAD_DIM
SLC_BLOCK = 64
N_SELECT = 16
WINDOW = 512
FORCE_BONUS = 1000.0
RWKV_HEADS = 8
RWKV_HEAD_DIM = 64
RWKV_DIM = RWKV_HEADS * RWKV_HEAD_DIM
W_LORA = 64
A_LORA = 64
G_LORA = 128
RWKV_MIX_DIM = 3 * RWKV_DIM + W_LORA + A_LORA + G_LORA
D_FF = 4 * D_MODEL
NSA_GATES = N_NSA_BRANCHES * ATTN_HEADS
NSA_DIM = ATTN_DIM + 6 * KV_DIM + NSA_GATES
RMS_EPS = 1e-6
GN_EPS = 64e-5
NEG_INF = -1e30

LOG2_64 = 6
LANES = 128
Q_PAD = ATTN_HEADS * LANES
RWKV_PAIRS = RWKV_HEADS // 2
RWKV_CHUNK = 64
NSA_KEY_TILE = 512
VMEM_LIMIT = 48 * 1024 * 1024

C_Q = 0
C_KV = C_Q + Q_PAD
C_GATE = C_KV + 6 * KV_DIM
C_RWKV = C_GATE + LANES
C_MERGE = C_RWKV + RWKV_MIX_DIM
C_TOTAL = C_MERGE + 2 * D_MODEL


def _nn(a, b):
    return lax.dot_general(a, b, (((1,), (0,)), ((), ())), preferred_element_type=F32)


def _nt(a, b):
    return lax.dot_general(a, b, (((1,), (1,)), ((), ())), preferred_element_type=F32)


def _tn(a, b):
    return lax.dot_general(a, b, (((0,), (0,)), ((), ())), preferred_element_type=F32)


def _bnn(a, b):
    return lax.dot_general(a, b, (((2,), (1,)), ((0,), (0,))), preferred_element_type=F32)


def _bnt(a, b):
    return lax.dot_general(a, b, (((2,), (2,)), ((0,), (0,))), preferred_element_type=F32)


def _btn(a, b):
    return lax.dot_general(a, b, (((1,), (1,)), ((0,), (0,))), preferred_element_type=F32)


def _split2(x):
    hi = x.astype(BF16)
    lo = (x - hi.astype(F32)).astype(BF16)
    return hi, lo


def _iota(shape, dim):
    return lax.broadcasted_iota(I32, shape, dim)


def _params(*sem):
    return pltpu.CompilerParams(dimension_semantics=sem, vmem_limit_bytes=VMEM_LIMIT)


def _group_rms(x, gain):
    lane = _iota((1, LANES), 1)
    sq = x * x
    lo = lane < HEAD_DIM
    s0 = jnp.sum(jnp.where(lo, sq, 0.0), axis=-1, keepdims=True)
    s1 = jnp.sum(jnp.where(lo, 0.0, sq), axis=-1, keepdims=True)
    ms = jnp.where(lo, s0, s1) * (1.0 / HEAD_DIM)
    return x * lax.rsqrt(ms + RMS_EPS) * gain


def _proj_kernel(x_ref, g_ref, w_ref, qg_ref, kg_ref,
                 q_ref, kc_ref, vc_ref, ks_ref, vs_ref, kw_ref, vw_ref, gt_ref, rw_ref, mg_ref):
    x = x_ref[...]
    ms = jnp.mean(x * x, axis=-1, keepdims=True)
    u = (x * lax.rsqrt(ms + RMS_EPS) * g_ref[...]).astype(BF16)

    def col(c0, width):
        return _nn(u, w_ref[:, c0:c0 + width])

    scale = HEAD_DIM ** -0.5
    for j in range(ATTN_HEADS):
        qj = col(C_Q + j * LANES, LANES)
        msq = jnp.sum(qj * qj, axis=-1, keepdims=True) * (1.0 / HEAD_DIM)
        qn = qj * lax.rsqrt(msq + RMS_EPS) * qg_ref[:, j * LANES:(j + 1) * LANES]
        q_ref[:, j * LANES:(j + 1) * LANES] = (qn * scale).astype(BF16)
    kc_ref[...] = col(C_KV + 0 * KV_DIM, KV_DIM)
    vc_ref[...] = col(C_KV + 1 * KV_DIM, KV_DIM)
    ks_ref[...] = _group_rms(col(C_KV + 2 * KV_DIM, KV_DIM), kg_ref[0:1, :]).astype(BF16)
    vs_ref[...] = col(C_KV + 3 * KV_DIM, KV_DIM).astype(BF16)
    kw_ref[...] = _group_rms(col(C_KV + 4 * KV_DIM, KV_DIM), kg_ref[1:2, :]).astype(BF16)
    vw_ref[...] = col(C_KV + 5 * KV_DIM, KV_DIM).astype(BF16)
    gt_ref[...] = col(C_GATE, LANES)
    for c in range(RWKV_MIX_DIM // LANES):
        rw_ref[:, c * LANES:(c + 1) * LANES] = col(C_RWKV + c * LANES, LANES)
    for c in range(2 * D_MODEL // 512):
        mg_ref[:, c * 512:(c + 1) * 512] = col(C_MERGE + c * 512, 512)


def _proj_call(x2, gain, w_all, qg_pad, kg2, tm=256):
    n = x2.shape[0]
    row = lambda w: pl.BlockSpec((tm, w), lambda i: (i, 0))
    full = lambda a: pl.BlockSpec(a.shape, lambda i: (0,) * a.ndim)
    out_shapes = [
        jax.ShapeDtypeStruct((n, Q_PAD), BF16),
        jax.ShapeDtypeStruct((n, KV_DIM), F32), jax.ShapeDtypeStruct((n, KV_DIM), F32),
        jax.ShapeDtypeStruct((n, KV_DIM), BF16), jax.ShapeDtypeStruct((n, KV_DIM), BF16),
        jax.ShapeDtypeStruct((n, KV_DIM), BF16), jax.ShapeDtypeStruct((n, KV_DIM), BF16),
        jax.ShapeDtypeStruct((n, LANES), F32),
        jax.ShapeDtypeStruct((n, RWKV_MIX_DIM), F32),
        jax.ShapeDtypeStruct((n, 2 * D_MODEL), F32),
    ]
    return pl.pallas_call(
        _proj_kernel,
        grid=(n // tm,),
        in_specs=[row(D_MODEL), full(gain), full(w_all), full(qg_pad), full(kg2)],
        out_specs=[row(s.shape[1]) for s in out_shapes],
        out_shape=out_shapes,
        compiler_params=_params("parallel"),
        name="proj",
    )(x2, gain, w_all, qg_pad, kg2)


def _compress_kernel(x_ref, w1_ref, w2_ref, pos_ref, kg_ref, o_ref, *, nc):
    which = pl.program_id(0)
    half = CMP_STRIDE * HEAD_DIM
    x = x_ref[0, 0, 0]
    ha = _nn(x, w1_ref[0, 0:half, :])
    hb = _nn(x, w1_ref[0, half:2 * half, :])
    pos8 = jnp.broadcast_to(pos_ref[0], (8, 2 * half)).astype(BF16)
    bias = _nn(pos8, w1_ref[0])[0:1, :]
    hid = ha + pltpu.roll(hb, nc - 1, axis=0) + bias
    act = jax.nn.gelu(hid)
    out = _nn(act.astype(BF16), w2_ref[0])
    ms = jnp.mean(out * out, axis=-1, keepdims=True)
    normed = out * lax.rsqrt(ms + RMS_EPS) * kg_ref[...]
    out = jnp.where(which == 0, normed, out)
    rowi = _iota((nc, 1), 0)
    o_ref[0, 0, 0] = jnp.where(rowi < nc - 1, out, 0.0)


def _compress_call(xg, w1, w2, pos_flat, kgain):
    _, b, g, nc, width = xg.shape
    return pl.pallas_call(
        functools.partial(_compress_kernel, nc=nc),
        grid=(2, b, g),
        in_specs=[
            pl.BlockSpec((1, 1, 1, nc, width), lambda w, i, j: (w, i, j, 0, 0)),
            pl.BlockSpec((1, width * 2, CMP_HIDDEN), lambda w, i, j: (w, 0, 0)),
            pl.BlockSpec((1, CMP_HIDDEN, HEAD_DIM), lambda w, i, j: (w, 0, 0)),
            pl.BlockSpec((1, 1, width * 2), lambda w, i, j: (w, 0, 0)),
            pl.BlockSpec((1, HEAD_DIM), lambda w, i, j: (0, 0)),
        ],
        out_specs=pl.BlockSpec((1, 1, 1, nc, HEAD_DIM), lambda w, i, j: (w, i, j, 0, 0)),
        out_shape=jax.ShapeDtypeStruct((2, b, g, nc, HEAD_DIM), F32),
        compiler_params=_params("parallel", "parallel", "parallel"),
        name="compress",
    )(xg, w1, w2, pos_flat, kgain)


def _softmax_rows(s, mask):
    sm = jnp.where(mask[None], s, NEG_INF)
    mx = jnp.max(sm, axis=-1, keepdims=True)
    e = jnp.exp(sm - mx)
    den = jnp.sum(e, axis=-1, keepdims=True)
    return jnp.where(mask[None], e / den, 0.0)


def _nsa_kernel(q_ref, gt_ref, kc_ref, vc_ref, ks_ref, vs_ref, kw_ref, vw_ref, e3_ref, y_ref,
                m_s, acc_s, *, tq, tk, nb, nc):
    nh = ATTN_HEADS
    hp = HEADS_PER_GROUP
    rows = hp * tq
    wk = WINDOW + tq
    qi = pl.program_id(1)
    q0 = qi * tq
    tcol = q0 + _iota((tq, 1), 0)
    trow = q0 + _iota((1, tq), 1)
    lane = _iota((1, LANES), 1)
    keep = [(lane >= g * HEAD_DIM) & (lane < (g + 1) * HEAD_DIM) for g in range(KV_GROUPS)]
    gates = jax.nn.sigmoid(gt_ref[0])
    qall = jnp.concatenate([q_ref[0, :, j * LANES:(j + 1) * LANES] for j in range(nh)], axis=0)

    n_row = _iota((1, nc), 1)
    cmask = (n_row * CMP_STRIDE + (CMP_BLOCK - 1) <= tcol) & (n_row < nc - 1)
    p_c = _softmax_rows(_nt(qall, kc_ref[0]).reshape(nh, tq, nc), cmask)
    o_c = [_nn(p_c[g * hp:(g + 1) * hp].reshape(rows, nc).astype(BF16), vc_ref[0])
           for g in range(KV_GROUPS)]

    jcol = _iota((nb, 1), 0)
    ncmp = _iota((nb, nc), 1)
    ov_t = ((ncmp * CMP_STRIDE <= jcol * SLC_BLOCK + (SLC_BLOCK - 1))
            & (ncmp * CMP_STRIDE + (CMP_BLOCK - 1) >= jcol * SLC_BLOCK))
    ov_t = jnp.where(ov_t, 1.0, 0.0).astype(BF16)
    cur = jnp.right_shift(trow, LOG2_64)
    forced = (jcol == 0) | (jcol == cur) | (jcol == cur - 1)
    causal_blk = jcol * SLC_BLOCK <= trow
    eye = jnp.where(_iota((tq, tq), 0) == _iota((tq, tq), 1), 1.0, 0.0).astype(BF16)
    sel_bias = []
    for g in range(KV_GROUPS):
        psum = p_c[g * hp] + p_c[g * hp + 1] + p_c[g * hp + 2] + p_c[g * hp + 3]
        hi, lo = _split2(psum)
        imp = _nt(ov_t, hi) + _nt(ov_t, lo)
        imp = jnp.where(causal_blk, imp + jnp.where(forced, FORCE_BONUS, 0.0), -1.0)
        rank = jnp.zeros((nb, tq), F32)
        for i in range(nb):
            ri = imp[i:i + 1, :]
            rank = rank + jnp.where(jcol > i, jnp.where(ri >= imp, 1.0, 0.0),
                                    jnp.where(ri > imp, 1.0, 0.0))
        bias_t = jnp.where(rank < float(min(N_SELECT, nb)), 0.0, NEG_INF).astype(BF16)
        sel_bias.append(_nt(eye, bias_t).astype(BF16))

    m_s[...] = jnp.full(m_s.shape, NEG_INF, F32)
    acc_s[...] = jnp.zeros(acc_s.shape, F32)

    def body(kt, _):
        k0 = pl.multiple_of(kt * tk, tk)
        v_t = vs_ref[0, pl.ds(k0, tk), :]
        s = _nt(qall, ks_ref[0, pl.ds(k0, tk), :])
        causal = (k0 + _iota((1, tk), 1)) <= tcol
        e_t = e3_ref[kt]
        for g in range(KV_GROUPS):
            bias = jnp.where(causal, _nn(sel_bias[g], e_t), NEG_INF)
            ps, alphas = [], []
            for h in range(hp):
                r0 = (g * hp + h) * tq
                sm = s[r0:r0 + tq] + bias
                m_prev = m_s[r0:r0 + tq, :]
                m_new = jnp.maximum(m_prev, jnp.max(sm, axis=-1, keepdims=True))
                m_s[r0:r0 + tq, :] = m_new
                alphas.append(jnp.exp(m_prev - m_new))
                ps.append(jnp.concatenate(
                    [jnp.exp(sm[:, c * LANES:(c + 1) * LANES] - m_new) for c in range(tk // LANES)],
                    axis=1).astype(BF16))
            pv = _nn(jnp.concatenate(ps, axis=0), jnp.where(keep[g], v_t, 1.0))
            g0 = g * rows
            acc_s[g0:g0 + rows, :] = jnp.concatenate(alphas, axis=0) * acc_s[g0:g0 + rows, :] + pv
        return 0

    lax.fori_loop(0, qi // (tk // tq) + 1, body, 0)

    w0 = pl.multiple_of(jnp.maximum(q0 - WINDOW, 0), tq)
    dist = tcol - (w0 + _iota((1, wk), 1))
    bias_w = jnp.where((dist >= 0) & (dist < WINDOW), 0.0, NEG_INF)
    s_w = _nt(qall, kw_ref[0, pl.ds(w0, wk), :]).reshape(nh, tq, wk) + bias_w[None]
    p_w = jnp.exp(s_w - jnp.max(s_w, axis=-1, keepdims=True)).astype(BF16)
    v_w = vw_ref[0, pl.ds(w0, wk), :]

    for g in range(KV_GROUPS):
        acc = acc_s[g * rows:(g + 1) * rows, :]
        o_s = acc / pltpu.roll(acc, HEAD_DIM, axis=1)
        acc_w = _nn(p_w[g * hp:(g + 1) * hp].reshape(rows, wk), jnp.where(keep[g], v_w, 1.0))
        o_w = acc_w / pltpu.roll(acc_w, HEAD_DIM, axis=1)
        for h in range(hp):
            j = g * hp + h
            c0 = j * N_NSA_BRANCHES
            hs = slice(h * tq, (h + 1) * tq)
            o = (gates[:, c0:c0 + 1] * o_c[g][hs] + gates[:, c0 + 1:c0 + 2] * o_s[hs]
                 + gates[:, c0 + 2:c0 + 3] * o_w[hs])
            y_ref[0, :, j * LANES:(j + 1) * LANES] = jnp.where(keep[g], o, 0.0).astype(BF16)


def _nsa_call(q, gates, kc, vc, ks, vs, kw, vw, e3, tq=128):
    b, s, _ = q.shape
    nb = s // SLC_BLOCK
    nc = kc.shape[1]
    tk = e3.shape[2]
    tile = lambda w: pl.BlockSpec((1, tq, w), lambda i, j: (i, j, 0))
    seq = lambda a: pl.BlockSpec((1,) + a.shape[1:], lambda i, j: (i, 0, 0))
    return pl.pallas_call(
        functools.partial(_nsa_kernel, tq=tq, tk=tk, nb=nb, nc=nc),
        grid=(b, s // tq),
        in_specs=[tile(Q_PAD), tile(LANES), seq(kc), seq(vc), seq(ks), seq(vs), seq(kw), seq(vw),
                  pl.BlockSpec(e3.shape, lambda i, j: (0, 0, 0))],
        out_specs=tile(Q_PAD),
        out_shape=jax.ShapeDtypeStruct((b, s, Q_PAD), BF16),
        scratch_shapes=[pltpu.VMEM((ATTN_HEADS * tq, LANES), F32),
                        pltpu.VMEM((ATTN_HEADS * tq, LANES), F32)],
        compiler_params=_params("parallel", "arbitrary"),
        name="nsa",
    )(q, gates, kc, vc, ks, vs, kw, vw, e3)


def _seg_sum(x, bd):
    hi, lo = _split2(x)
    return _nn(hi, bd) + _nn(lo, bd)


def _stack_heads(x):
    lane = _iota((1, LANES), 1)
    lo = lane < RWKV_HEAD_DIM
    return jnp.concatenate([jnp.where(lo, x, 0.0), jnp.where(lo, 0.0, x)], axis=0)


def _rwkv_kernel(p_ref, mu_ref, w0_ref, wup_ref, a0_ref, aup_ref, gup_ref, kk_ref, ka_ref, rk_ref,
                 lnw_ref, lnb_ref, o_ref, carry_ref, state_ref, *, tt):
    ch = RWKV_CHUNK
    c2 = 2 * ch
    d = RWKV_DIM
    nch = tt // ch
    npr = RWKV_PAIRS

    @pl.when(pl.program_id(1) == 0)
    def _():
        carry_ref[...] = jnp.zeros_like(carry_ref)
        state_ref[...] = jnp.zeros_like(state_ref)

    p = p_ref[0]
    prev = pltpu.roll(p, 1, axis=0)
    prev = jnp.where(_iota((tt, 1), 0) == 0, carry_ref[0:1, :], prev)
    carry_ref[0:1, :] = p[tt - 1:tt, :]
    pm = p + (prev - p) * mu_ref[...]

    r = pm[:, 0:d]
    k = pm[:, d:2 * d]
    v = pm[:, 2 * d:3 * d]
    wa = pm[:, 3 * d:3 * d + LANES]
    gl = pm[:, 3 * d + LANES:3 * d + 2 * LANES]

    z = w0_ref[...] + _nn(jnp.tanh(wa).astype(BF16), wup_ref[...])
    softplus = jnp.maximum(-z, 0.0) + jnp.log(1.0 + jnp.exp(-jnp.abs(z)))
    lw = -jnp.exp(-softplus - 0.5)
    a = jax.nn.sigmoid(a0_ref[...] + _nn(wa.astype(BF16), aup_ref[...]))
    gate = _nn(jax.nn.sigmoid(gl).astype(BF16), gup_ref[...])

    seg = jnp.right_shift(_iota((d, d), 0), LOG2_64) == jnp.right_shift(_iota((d, d), 1), LOG2_64)
    bd = jnp.where(seg, 1.0, 0.0).astype(BF16)
    kk = k * kk_ref[...]
    kk = kk * lax.rsqrt(jnp.maximum(_seg_sum(kk * kk, bd), 1e-24))
    k2 = k * (1.0 + (a - 1.0) * ka_ref[...])

    ti = _iota((tt, tt), 0)
    tj = _iota((tt, tt), 1)
    same_chunk = jnp.right_shift(ti, LOG2_64) == jnp.right_shift(tj, LOG2_64)
    tri = jnp.where(same_chunk & (tj <= ti), 1.0, 0.0).astype(BF16)
    blk = jnp.where(same_chunk, 1.0, 0.0).astype(BF16)
    hi, lo = _split2(lw)
    cum = _nn(tri, hi) + _nn(tri, lo)
    tot = _nn(blk, hi) + _nn(blk, lo)
    e_inv = jnp.exp(-cum)
    e_end = jnp.exp(tot - cum)
    dec = jnp.exp(tot)
    bb = kk * a

    def tiles(x):
        return jnp.stack([_stack_heads(x[c * ch:(c + 1) * ch, pr * LANES:(pr + 1) * LANES])
                          for c in range(nch) for pr in range(npr)]).astype(BF16)

    a_t = tiles(-kk * jnp.exp(cum - lw))
    r_t = tiles(r * jnp.exp(cum))
    b_h = tiles(bb * e_inv)
    k_h = tiles(k2 * e_inv)
    bke = jnp.concatenate([tiles(bb * e_end), tiles(k2 * e_end)], axis=1)
    v_t = tiles(v)

    ri = _iota((c2, c2), 0)
    ci = _iota((c2, c2), 1)
    same = jnp.right_shift(ri, LOG2_64) == jnp.right_shift(ci, LOG2_64)
    strict = jnp.where(same & (ci < ri), 1.0, 0.0)
    incl = jnp.where(same & (ci <= ri), 1.0, 0.0)
    ident = jnp.where(ri == ci, 1.0, 0.0)

    scores = _bnt(jnp.concatenate([a_t, r_t], axis=1), jnp.concatenate([b_h, k_h], axis=1))
    m_ab = scores[:, 0:c2, 0:c2] * strict
    m_ak = (scores[:, 0:c2, c2:2 * c2] * strict).astype(BF16)
    m_rbk = jnp.concatenate([scores[:, c2:2 * c2, 0:c2] * incl,
                             scores[:, c2:2 * c2, c2:2 * c2] * incl], axis=2).astype(BF16)
    tinv = ident + m_ab
    mp = m_ab
    for _ in range(5):
        mpb = mp.astype(BF16)
        mp = _bnn(mpb, mpb)
        tinv = tinv + _bnn(tinv.astype(BF16), mp.astype(BF16))
    rhs = jnp.concatenate([a_t, _bnn(m_ak, v_t).astype(BF16)], axis=2)
    wu = _bnn(tinv.astype(BF16), rhs)
    w_t = wu[:, :, 0:c2].astype(BF16)
    u0 = wu[:, :, c2:2 * c2]

    g_st = state_ref[...]
    y_rows = []
    for c in range(nch):
        sl = slice(c * npr, (c + 1) * npr)
        g_b = g_st.astype(BF16)
        u = _bnt(w_t[sl], g_b) + u0[sl]
        uv = jnp.concatenate([u.astype(BF16), v_t[sl]], axis=1)
        y = _bnt(r_t[sl], g_b) + _bnn(m_rbk[sl], uv)
        y = y[:, 0:ch, :] + y[:, ch:c2, :]
        y_rows.append(jnp.concatenate([y[pr] for pr in range(npr)], axis=1))
        dec_c = jnp.stack([dec[c * ch:c * ch + 1, pr * LANES:(pr + 1) * LANES] for pr in range(npr)])
        g_st = g_st * dec_c + _btn(uv, bke[sl])
    state_ref[...] = g_st
    y = jnp.concatenate(y_rows, axis=0)

    inv = 1.0 / RWKV_HEAD_DIM
    mean = _seg_sum(y, bd) * inv
    yc = y - mean
    var = _seg_sum(yc * yc, bd) * inv
    yn = yc * lax.rsqrt(var + GN_EPS) * lnw_ref[...] + lnb_ref[...]
    bonus = _seg_sum(r * k2 * rk_ref[...], bd) * v
    o_ref[0] = ((yn + bonus) * gate).astype(BF16)


def _rwkv_call(p, mu, w0, wup, a0, aup, gup, k_k, k_a, r_k, ln_w, ln_b, tt=256):
    b, s, _ = p.shape
    full = lambda a: pl.BlockSpec(a.shape, lambda i, j: (0,) * a.ndim)
    consts = (mu, w0, wup, a0, aup, gup, k_k, k_a, r_k, ln_w, ln_b)
    return pl.pallas_call(
        functools.partial(_rwkv_kernel, tt=tt),
        grid=(b, s // tt),
        in_specs=[pl.BlockSpec((1, tt, RWKV_MIX_DIM), lambda i, j: (i, j, 0))]
                 + [full(c) for c in consts],
        out_specs=pl.BlockSpec((1, tt, RWKV_DIM), lambda i, j: (i, j, 0)),
        out_shape=jax.ShapeDtypeStruct((b, s, RWKV_DIM), BF16),
        scratch_shapes=[pltpu.VMEM((8, RWKV_MIX_DIM), F32),
                        pltpu.VMEM((RWKV_PAIRS, LANES, LANES), F32)],
        compiler_params=_params("arbitrary", "arbitrary"),
        name="rwkv",
    )(p, *consts)


def _merge_kernel(x_ref, ya_ref, yb_ref, mg_ref, wa_ref, wb_ref, wo_ref, fg_ref, xo_ref, h_ref):
    ga = jax.nn.sigmoid(mg_ref[:, 0:D_MODEL])
    gb = jax.nn.sigmoid(mg_ref[:, D_MODEL:2 * D_MODEL])
    merged = ga * _nn(ya_ref[...], wa_ref[...]) + gb * _nn(yb_ref[...], wb_ref[...])
    xn = x_ref[...] + _nn(merged.astype(BF16), wo_ref[...])
    xo_ref[...] = xn
    ms = jnp.mean(xn * xn, axis=-1, keepdims=True)
    h_ref[...] = (xn * lax.rsqrt(ms + RMS_EPS) * fg_ref[...]).astype(BF16)


def _merge_call(x2, ya, yb, mg, wa_pad, wb, wo, fgain, tm=512):
    n = x2.shape[0]
    row = lambda w: pl.BlockSpec((tm, w), lambda i: (i, 0))
    full = lambda a: pl.BlockSpec(a.shape, lambda i: (0,) * a.ndim)
    return pl.pallas_call(
        _merge_kernel,
        grid=(n // tm,),
        in_specs=[row(D_MODEL), row(Q_PAD), row(RWKV_DIM), row(2 * D_MODEL),
                  full(wa_pad), full(wb), full(wo), full(fgain)],
        out_specs=[row(D_MODEL), row(D_MODEL)],
        out_shape=[jax.ShapeDtypeStruct((n, D_MODEL), F32), jax.ShapeDtypeStruct((n, D_MODEL), BF16)],
        compiler_params=_params("parallel"),
        name="merge",
    )(x2, ya, yb, mg, wa_pad, wb, wo, fgain)


def _ffn_kernel(x_ref, h_ref, wu_ref, wd_ref, o_ref, *, fc):
    h = h_ref[...]
    acc = x_ref[...]
    for c in range(D_FF // fc):
        up = jnp.maximum(_nn(h, wu_ref[:, c * fc:(c + 1) * fc]), 0.0)
        acc = acc + _nn((up * up).astype(BF16), wd_ref[c * fc:(c + 1) * fc, :])
    o_ref[...] = acc


def _ffn_call(x2, h, wu, wd, tm=512, fc=1024):
    n = x2.shape[0]
    row = lambda w: pl.BlockSpec((tm, w), lambda i: (i, 0))
    full = lambda a: pl.BlockSpec(a.shape, lambda i: (0,) * a.ndim)
    return pl.pallas_call(
        functools.partial(_ffn_kernel, fc=fc),
        grid=(n // tm,),
        in_specs=[row(D_MODEL), row(D_MODEL), full(wu), full(wd)],
        out_specs=row(D_MODEL),
        out_shape=jax.ShapeDtypeStruct((n, D_MODEL), F32),
        compiler_params=_params("parallel"),
        name="ffn",
    )(x2, h, wu, wd)


def _pad_heads_cols(w):
    rows = w.shape[0]
    wh = w.reshape(rows, ATTN_HEADS, HEAD_DIM)
    slots = []
    for j in range(ATTN_HEADS):
        g = j // HEADS_PER_GROUP
        z = jnp.zeros((rows, HEAD_DIM), w.dtype)
        slots.append(jnp.concatenate([wh[:, j], z] if g == 0 else [z, wh[:, j]], axis=1))
    return jnp.concatenate(slots, axis=1)


def _proj_weight(w_in):
    q = _pad_heads_cols(w_in[:, 0:ATTN_DIM])
    kv = w_in[:, ATTN_DIM:ATTN_DIM + 6 * KV_DIM]
    gates = jnp.pad(w_in[:, ATTN_DIM + 6 * KV_DIM:NSA_DIM], ((0, 0), (0, LANES - NSA_GATES)))
    rest = w_in[:, NSA_DIM:]
    return jnp.concatenate([q, kv, gates, rest], axis=1).astype(BF16)


def _to_groups16(t, b, s):
    t = t.reshape(b, s // CMP_STRIDE, CMP_STRIDE, KV_GROUPS, HEAD_DIM)
    t = t.transpose(0, 3, 1, 2, 4)
    return t.reshape(b, KV_GROUPS, s // CMP_STRIDE, CMP_STRIDE * HEAD_DIM)


def kernel(x, mix_norm, w_in, q_gain, k_gain, cmp_pos, cmp_w1, cmp_w2, w_attn_branch, tok_mix, w0,
           w_lora_up, a0, a_lora_up, g_lora_up, k_k, k_a, r_k, ln_x_w, ln_x_b, w_rwkv_branch, w_out,
           ffn_norm, w_ffn_up, w_ffn_down):
    b, s, d = x.shape
    n = b * s
    depth = w_in.shape[0]
    key_blk = (jnp.arange(s, dtype=I32) // SLC_BLOCK).reshape(s // NSA_KEY_TILE, 1, NSA_KEY_TILE)
    e3 = (key_blk == jnp.arange(s // SLC_BLOCK, dtype=I32)[None, :, None]).astype(BF16)
    row = lambda v: v.reshape(1, -1)
    x2 = x.reshape(n, d)
    for l in range(depth):
        qg_pad = _pad_heads_cols(jnp.tile(q_gain[l], ATTN_HEADS).reshape(1, ATTN_DIM))
        kg2 = jnp.tile(k_gain[l, 1:3], (1, KV_GROUPS))
        q, kc, vc, ks, vs, kw, vw, gates, rw, mg = _proj_call(
            x2, row(mix_norm[l]), _proj_weight(w_in[l]), qg_pad, kg2)

        xg = jnp.stack([_to_groups16(kc, b, s), _to_groups16(vc, b, s)]).astype(BF16)
        cmp = _compress_call(xg, cmp_w1[l].astype(BF16), cmp_w2[l].astype(BF16),
                             cmp_pos[l].reshape(2, 1, CMP_BLOCK * HEAD_DIM), row(k_gain[l, 0]))
        cmp = cmp.transpose(0, 1, 3, 2, 4).reshape(2, b, s // CMP_STRIDE, KV_DIM).astype(BF16)

        seq = lambda t: t.reshape(b, s, t.shape[-1])
        ya = _nsa_call(seq(q), seq(gates), cmp[0], cmp[1], seq(ks), seq(vs), seq(kw), seq(vw), e3)

        zero = jnp.zeros((W_LORA, RWKV_DIM), F32)
        wup = jnp.concatenate([w_lora_up[l], zero], axis=0).astype(BF16)
        aup = jnp.concatenate([zero, a_lora_up[l]], axis=0).astype(BF16)
        yb = _rwkv_call(seq(rw), row(tok_mix[l]), row(w0[l]), wup, row(a0[l]), aup,
                        g_lora_up[l].astype(BF16), row(k_k[l]), row(k_a[l]), row(r_k[l]),
                        row(ln_x_w[l]), row(ln_x_b[l]))

        wa_pad = _pad_heads_cols(w_attn_branch[l].T).T.astype(BF16)
        x2, h = _merge_call(x2, ya.reshape(n, Q_PAD), yb.reshape(n, RWKV_DIM), mg, wa_pad,
                            w_rwkv_branch[l].astype(BF16), w_out[l].astype(BF16), row(ffn_norm[l]))
        x2 = _ffn_call(x2, h, w_ffn_up[l].astype(BF16), w_ffn_down[l].astype(BF16))
    return x2.reshape(b, s, d)
```

```python
import functools

import jax
import jax.numpy as jnp
import numpy as np
from jax import lax
from jax.experimental import pallas as pl
from jax.experimental.pallas import tpu as pltpu

F32 = jnp.float32
BF16 = jnp.bfloat16
I32 = jnp.int32

D_MODEL = 1024
ATTN_HEADS = 8
HEAD_DIM = 64
KV_GROUPS = 2
HEADS_PER_GROUP = ATTN_HEADS // KV_GROUPS
ATTN_DIM = ATTN_HEADS * HEAD_DIM
KV_DIM = KV_GROUPS * HEAD_DIM
N_NSA_BRANCHES = 3
CMP_BLOCK = 32
CMP_STRIDE = 16
CMP_HIDDEN = 2 * HEAD_DIM
SLC_BLOCK = 64
N_SELECT = 16
WINDOW = 512
FORCE_BONUS = 1000.0
RWKV_HEADS = 8
RWKV_HEAD_DIM = 64
RWKV_DIM = RWKV_HEADS * RWKV_HEAD_DIM
W_LORA = 64
A_LORA = 64
G_LORA = 128
RWKV_MIX_DIM = 3 * RWKV_DIM + W_LORA + A_LORA + G_LORA
D_FF = 4 * D_MODEL
NSA_GATES = N_NSA_BRANCHES * ATTN_HEADS
NSA_DIM = ATTN_DIM + 6 * KV_DIM + NSA_GATES
RMS_EPS = 1e-6
GN_EPS = 64e-5
NEG_INF = -1e30

LOG2_64 = 6
LANES = 128
Q_PAD = ATTN_HEADS * LANES
RWKV_PAIRS = RWKV_HEADS // 2
RWKV_CHUNK = 64
NSA_QUERY_TILE = 128
NSA_KEY_TILE = 512
VMEM_LIMIT = 48 * 1024 * 1024

C_Q = 0
C_KV = C_Q + Q_PAD
C_GATE = C_KV + 6 * KV_DIM
C_RWKV = C_GATE + LANES
C_MERGE = C_RWKV + RWKV_MIX_DIM
C_TOTAL = C_MERGE + 2 * D_MODEL


def _nn(a, b):
    return lax.dot_general(a, b, (((1,), (0,)), ((), ())), preferred_element_type=F32)


def _nt(a, b):
    return lax.dot_general(a, b, (((1,), (1,)), ((), ())), preferred_element_type=F32)


def _bnn(a, b):
    return lax.dot_general(a, b, (((2,), (1,)), ((0,), (0,))), preferred_element_type=F32)


def _bnt(a, b):
    return lax.dot_general(a, b, (((2,), (2,)), ((0,), (0,))), preferred_element_type=F32)


def _btn(a, b):
    return lax.dot_general(a, b, (((1,), (1,)), ((0,), (0,))), preferred_element_type=F32)


def _split2(x):
    hi = x.astype(BF16)
    lo = (x - hi.astype(F32)).astype(BF16)
    return hi, lo


def _iota(shape, dim):
    return lax.broadcasted_iota(I32, shape, dim)


def _params(*sem):
    return pltpu.CompilerParams(dimension_semantics=sem, vmem_limit_bytes=VMEM_LIMIT)


def _group_rms(x, gain):
    lane = _iota((1, LANES), 1)
    sq = x * x
    lo = lane < HEAD_DIM
    s0 = jnp.sum(jnp.where(lo, sq, 0.0), axis=-1, keepdims=True)
    s1 = jnp.sum(jnp.where(lo, 0.0, sq), axis=-1, keepdims=True)
    ms = jnp.where(lo, s0, s1) * (1.0 / HEAD_DIM)
    return x * lax.rsqrt(ms + RMS_EPS) * gain


def _proj_kernel(x_ref, g_ref, w_ref, qg_ref, kg_ref,
                 q_ref, kc_ref, vc_ref, ks_ref, vs_ref, kw_ref, vw_ref, gt_ref, rw_ref, mg_ref):
    x = x_ref[...]
    ms = jnp.mean(x * x, axis=-1, keepdims=True)
    u = (x * lax.rsqrt(ms + RMS_EPS) * g_ref[...]).astype(BF16)

    def col(c0, width):
        return _nn(u, w_ref[:, c0:c0 + width])

    scale = HEAD_DIM ** -0.5
    for j in range(ATTN_HEADS):
        qj = col(C_Q + j * LANES, LANES)
        msq = jnp.sum(qj * qj, axis=-1, keepdims=True) * (1.0 / HEAD_DIM)
        qn = qj * lax.rsqrt(msq + RMS_EPS) * qg_ref[:, j * LANES:(j + 1) * LANES]
        q_ref[:, j * LANES:(j + 1) * LANES] = (qn * scale).astype(BF16)
    kc_ref[...] = col(C_KV + 0 * KV_DIM, KV_DIM)
    vc_ref[...] = col(C_KV + 1 * KV_DIM, KV_DIM)
    ks_ref[...] = _group_rms(col(C_KV + 2 * KV_DIM, KV_DIM), kg_ref[0:1, :]).astype(BF16)
    vs_ref[...] = col(C_KV + 3 * KV_DIM, KV_DIM).astype(BF16)
    kw_ref[...] = _group_rms(col(C_KV + 4 * KV_DIM, KV_DIM), kg_ref[1:2, :]).astype(BF16)
    vw_ref[...] = col(C_KV + 5 * KV_DIM, KV_DIM).astype(BF16)
    gt_ref[...] = col(C_GATE, LANES)
    for c in range(RWKV_MIX_DIM // LANES):
        rw_ref[:, c * LANES:(c + 1) * LANES] = col(C_RWKV + c * LANES, LANES)
    for c in range(2 * D_MODEL // 512):
        mg_ref[:, c * 512:(c + 1) * 512] = col(C_MERGE + c * 512, 512)


def _proj_call(x2, gain, w_all, qg_pad, kg2, tm=256):
    n = x2.shape[0]
    row = lambda w: pl.BlockSpec((tm, w), lambda i: (i, 0))
    full = lambda a: pl.BlockSpec(a.shape, lambda i: (0,) * a.ndim)
    out_shapes = [
        jax.ShapeDtypeStruct((n, Q_PAD), BF16),
        jax.ShapeDtypeStruct((n, KV_DIM), F32), jax.ShapeDtypeStruct((n, KV_DIM), F32),
        jax.ShapeDtypeStruct((n, KV_DIM), BF16), jax.ShapeDtypeStruct((n, KV_DIM), BF16),
        jax.ShapeDtypeStruct((n, KV_DIM), BF16), jax.ShapeDtypeStruct((n, KV_DIM), BF16),
        jax.ShapeDtypeStruct((n, LANES), F32),
        jax.ShapeDtypeStruct((n, RWKV_MIX_DIM), F32),
        jax.ShapeDtypeStruct((n, 2 * D_MODEL), F32),
    ]
    return pl.pallas_call(
        _proj_kernel,
        grid=(n // tm,),
        in_specs=[row(D_MODEL), full(gain), full(w_all), full(qg_pad), full(kg2)],
        out_specs=[row(s.shape[1]) for s in out_shapes],
        out_shape=out_shapes,
        compiler_params=_params("parallel"),
        name="proj",
    )(x2, gain, w_all, qg_pad, kg2)


def _compress_kernel(x_ref, w1_ref, w2_ref, pos_ref, kg_ref, o_ref, *, nc):
    which = pl.program_id(0)
    half = CMP_STRIDE * HEAD_DIM
    x = x_ref[0, 0, 0]
    ha = _nn(x, w1_ref[0, 0:half, :])
    hb = _nn(x, w1_ref[0, half:2 * half, :])
    pos8 = jnp.broadcast_to(pos_ref[0], (8, 2 * half)).astype(BF16)
    bias = _nn(pos8, w1_ref[0])[0:1, :]
    hid = ha + pltpu.roll(hb, nc - 1, axis=0) + bias
    act = jax.nn.gelu(hid)
    out = _nn(act.astype(BF16), w2_ref[0])
    ms = jnp.mean(out * out, axis=-1, keepdims=True)
    normed = out * lax.rsqrt(ms + RMS_EPS) * kg_ref[...]
    out = jnp.where(which == 0, normed, out)
    rowi = _iota((nc, 1), 0)
    o_ref[0, 0, 0] = jnp.where(rowi < nc - 1, out, 0.0)


def _compress_call(xg, w1, w2, pos_flat, kgain):
    _, b, g, nc, width = xg.shape
    return pl.pallas_call(
        functools.partial(_compress_kernel, nc=nc),
        grid=(2, b, g),
        in_specs=[
            pl.BlockSpec((1, 1, 1, nc, width), lambda w, i, j: (w, i, j, 0, 0)),
            pl.BlockSpec((1, width * 2, CMP_HIDDEN), lambda w, i, j: (w, 0, 0)),
            pl.BlockSpec((1, CMP_HIDDEN, HEAD_DIM), lambda w, i, j: (w, 0, 0)),
            pl.BlockSpec((1, 1, width * 2), lambda w, i, j: (w, 0, 0)),
            pl.BlockSpec((1, HEAD_DIM), lambda w, i, j: (0, 0)),
        ],
        out_specs=pl.BlockSpec((1, 1, 1, nc, HEAD_DIM), lambda w, i, j: (w, i, j, 0, 0)),
        out_shape=jax.ShapeDtypeStruct((2, b, g, nc, HEAD_DIM), F32),
        compiler_params=_params("parallel", "parallel", "parallel"),
        name="compress",
    )(xg, w1, w2, pos_flat, kgain)


def _nsa_kernel(q_ref, gt_ref, kc_ref, vct_ref, ks_ref, vst_ref, kw_ref, vwt_ref, y_ref,
                m_s, acc_s, sb_s, *, tq, tk, nb, nc):
    nh = ATTN_HEADS
    hp = HEADS_PER_GROUP
    cols = hp * tq
    wk = WINDOW + tq
    bpt = tk // SLC_BLOCK
    qi = pl.program_id(1)
    q0 = qi * tq
    tlane = q0 + _iota((1, tq), 1)
    tile_h = lambda a, reps: jnp.concatenate([a] * reps, axis=1)
    eye_d = jnp.where(_iota((LANES, LANES), 0) == _iota((LANES, LANES), 1), 1.0, 0.0).astype(BF16)
    eye_q = jnp.where(_iota((tq, tq), 0) == _iota((tq, tq), 1), 1.0, 0.0).astype(BF16)
    drow = _iota((LANES, 1), 0)
    own = [(drow >= g * HEAD_DIM) & (drow < (g + 1) * HEAD_DIM) for g in range(KV_GROUPS)]

    qall = jnp.concatenate([q_ref[0, :, j * LANES:(j + 1) * LANES] for j in range(nh)], axis=0)
    q_t = _nt(eye_d, qall).astype(BF16)

    ncol = _iota((nc, 1), 0)
    cvalid = (ncol * CMP_STRIDE + (CMP_BLOCK - 1) <= tlane) & (ncol < nc - 1)
    s_c = _nn(kc_ref[0], q_t) + tile_h(jnp.where(cvalid, 0.0, NEG_INF), nh)
    e_c = jnp.exp(s_c - jnp.max(s_c, axis=0, keepdims=True))
    p_c = e_c * (1.0 / jnp.sum(e_c, axis=0, keepdims=True)) * tile_h(jnp.where(cvalid, 1.0, 0.0), nh)
    p_cb = p_c.astype(BF16)
    o_c = [_nn(vct_ref[0], p_cb[:, g * cols:(g + 1) * cols]) for g in range(KV_GROUPS)]

    jcol = _iota((nb, 1), 0)
    ncmp = _iota((nb, nc), 1)
    ov = ((ncmp * CMP_STRIDE <= jcol * SLC_BLOCK + (SLC_BLOCK - 1))
          & (ncmp * CMP_STRIDE + (CMP_BLOCK - 1) >= jcol * SLC_BLOCK))
    ov = jnp.where(ov, 1.0, 0.0).astype(BF16)
    cur = jnp.right_shift(tlane, LOG2_64)
    forced = (jcol == 0) | (jcol == cur) | (jcol == cur - 1)
    causal_blk = jcol * SLC_BLOCK <= tlane
    for g in range(KV_GROUPS):
        c0 = g * cols
        psum = (p_c[:, c0:c0 + tq] + p_c[:, c0 + tq:c0 + 2 * tq]
                + p_c[:, c0 + 2 * tq:c0 + 3 * tq] + p_c[:, c0 + 3 * tq:c0 + 4 * tq])
        hi, lo = _split2(psum)
        imp = _nn(ov, hi) + _nn(ov, lo)
        imp = jnp.where(causal_blk, imp + jnp.where(forced, FORCE_BONUS, 0.0), -1.0)
        rank = jnp.zeros((nb, tq), F32)
        for i in range(nb):
            ri = imp[i:i + 1, :]
            rank = rank + jnp.where(jcol > i, jnp.where(ri >= imp, 1.0, 0.0),
                                    jnp.where(ri > imp, 1.0, 0.0))
        sb_s[g] = jnp.where(rank < float(min(N_SELECT, nb)), 0.0, NEG_INF)

    m_s[...] = jnp.full(m_s.shape, NEG_INF, F32)
    acc_s[...] = jnp.zeros(acc_s.shape, F32)

    def body(kt, _):
        k0 = pl.multiple_of(kt * tk, tk)
        s = _nn(ks_ref[0, pl.ds(k0, tk), :], q_t)
        causal = (k0 + _iota((tk, 1), 0)) <= tlane
        for g in range(KV_GROUPS):
            cs = slice(g * cols, (g + 1) * cols)
            bias = jnp.concatenate(
                [jnp.broadcast_to(sb_s[g, pl.ds(kt * bpt + i, 1), :], (SLC_BLOCK, tq)) for i in range(bpt)],
                axis=0)
            sm = s[:, cs] + tile_h(jnp.where(causal, bias, NEG_INF), hp)
            m_prev = m_s[0:1, cs]
            m_new = jnp.maximum(m_prev, jnp.max(sm, axis=0, keepdims=True))
            m_s[:, cs] = jnp.broadcast_to(m_new, (8, cols))
            p = jnp.exp(sm - m_new).astype(BF16)
            acc_s[:, cs] = jnp.exp(m_prev - m_new) * acc_s[:, cs] + _nn(vst_ref[0, g, kt], p)
        return 0

    lax.fori_loop(0, qi // (tk // tq) + 1, body, 0)

    w0 = pl.multiple_of(jnp.maximum(q0 - WINDOW, 0), tq)
    dist = tlane - (w0 + _iota((wk, 1), 0))
    s_w = (_nn(kw_ref[0, pl.ds(w0, wk), :], q_t)
           + tile_h(jnp.where((dist >= 0) & (dist < WINDOW), 0.0, NEG_INF), nh))
    p_w = jnp.exp(s_w - jnp.max(s_w, axis=0, keepdims=True)).astype(BF16)
    wb = w0 // tq

    g_hi, g_lo = _split2(jax.nn.sigmoid(gt_ref[0]))
    gate_t = _nt(eye_d, g_hi) + _nt(eye_d, g_lo)
    for g in range(KV_GROUPS):
        cs = slice(g * cols, (g + 1) * cols)
        den_row = (1 - g) * HEAD_DIM
        acc = acc_s[:, cs]
        o_s = acc * (1.0 / acc[den_row:den_row + 1, :])
        v_w = jnp.concatenate([vwt_ref[0, g, wb + i] for i in range(wk // tq)], axis=1)
        acc_w = _nn(v_w, p_w[:, cs])
        o_w = acc_w * (1.0 / acc_w[den_row:den_row + 1, :])
        for h in range(hp):
            j = g * hp + h
            r0 = j * N_NSA_BRANCHES
            hs = slice(h * tq, (h + 1) * tq)
            o = (gate_t[r0:r0 + 1, :] * o_c[g][:, hs] + gate_t[r0 + 1:r0 + 2, :] * o_s[:, hs]
                 + gate_t[r0 + 2:r0 + 3, :] * o_w[:, hs])
            o = jnp.where(own[g], o, 0.0).astype(BF16)
            y_ref[0, :, j * LANES:(j + 1) * LANES] = _nt(eye_q, o).astype(BF16)


def _nsa_call(q, gates, kc, vct, ks, vst, kw, vwt):
    b, s, _ = q.shape
    tq = NSA_QUERY_TILE
    nb = s // SLC_BLOCK
    nc = kc.shape[1]
    tk = vst.shape[-1]
    tile = lambda w: pl.BlockSpec((1, tq, w), lambda i, j: (i, j, 0))
    seq = lambda a: pl.BlockSpec((1,) + a.shape[1:], lambda i, j: (i,) + (0,) * (a.ndim - 1))
    return pl.pallas_call(
        functools.partial(_nsa_kernel, tq=tq, tk=tk, nb=nb, nc=nc),
        grid=(b, s // tq),
        in_specs=[tile(Q_PAD), tile(LANES), seq(kc), seq(vct), seq(ks), seq(vst), seq(kw), seq(vwt)],
        out_specs=tile(Q_PAD),
        out_shape=jax.ShapeDtypeStruct((b, s, Q_PAD), BF16),
        scratch_shapes=[pltpu.VMEM((8, ATTN_HEADS * tq), F32),
                        pltpu.VMEM((LANES, ATTN_HEADS * tq), F32),
                        pltpu.VMEM((KV_GROUPS, nb, tq), F32)],
        compiler_params=_params("parallel", "arbitrary"),
        name="nsa",
    )(q, gates, kc, vct, ks, vst, kw, vwt)


def _seg_sum(x, bd):
    hi, lo = _split2(x)
    return _nn(hi, bd) + _nn(lo, bd)


def _stack_heads(x):
    lane = _iota((1, LANES), 1)
    lo = lane < RWKV_HEAD_DIM
    return jnp.concatenate([jnp.where(lo, x, 0.0), jnp.where(lo, 0.0, x)], axis=0)


def _rwkv_kernel(p_ref, mu_ref, w0_ref, wup_ref, a0_ref, aup_ref, gup_ref, kk_ref, ka_ref, rk_ref,
                 lnw_ref, lnb_ref, o_ref, carry_ref, state_ref, *, tt):
    ch = RWKV_CHUNK
    c2 = 2 * ch
    d = RWKV_DIM
    nch = tt // ch
    npr = RWKV_PAIRS

    @pl.when(pl.program_id(1) == 0)
    def _():
        carry_ref[...] = jnp.zeros_like(carry_ref)
        state_ref[...] = jnp.zeros_like(state_ref)

    p = p_ref[0]
    prev = pltpu.roll(p, 1, axis=0)
    prev = jnp.where(_iota((tt, 1), 0) == 0, carry_ref[0:1, :], prev)
    carry_ref[0:1, :] = p[tt - 1:tt, :]
    pm = p + (prev - p) * mu_ref[...]

    r = pm[:, 0:d]
    k = pm[:, d:2 * d]
    v = pm[:, 2 * d:3 * d]
    wa = pm[:, 3 * d:3 * d + LANES]
    gl = pm[:, 3 * d + LANES:3 * d + 2 * LANES]

    z = w0_ref[...] + _nn(jnp.tanh(wa).astype(BF16), wup_ref[...])
    softplus = jnp.maximum(-z, 0.0) + jnp.log(1.0 + jnp.exp(-jnp.abs(z)))
    lw = -jnp.exp(-softplus - 0.5)
    a = jax.nn.sigmoid(a0_ref[...] + _nn(wa.astype(BF16), aup_ref[...]))
    gate = _nn(jax.nn.sigmoid(gl).astype(BF16), gup_ref[...])

    seg = jnp.right_shift(_iota((d, d), 0), LOG2_64) == jnp.right_shift(_iota((d, d), 1), LOG2_64)
    bd = jnp.where(seg, 1.0, 0.0).astype(BF16)
    kk = k * kk_ref[...]
    kk = kk * lax.rsqrt(jnp.maximum(_seg_sum(kk * kk, bd), 1e-24))
    k2 = k * (1.0 + (a - 1.0) * ka_ref[...])

    ti = _iota((tt, tt), 0)
    tj = _iota((tt, tt), 1)
    same_chunk = jnp.right_shift(ti, LOG2_64) == jnp.right_shift(tj, LOG2_64)
    tri = jnp.where(same_chunk & (tj <= ti), 1.0, 0.0).astype(BF16)
    blk = jnp.where(same_chunk, 1.0, 0.0).astype(BF16)
    hi, lo = _split2(lw)
    cum = _nn(tri, hi) + _nn(tri, lo)
    tot = _nn(blk, hi) + _nn(blk, lo)
    e_inv = jnp.exp(-cum)
    e_end = jnp.exp(tot - cum)
    dec = jnp.exp(tot)
    bb = kk * a

    def tiles(x):
        return jnp.stack([_stack_heads(x[c * ch:(c + 1) * ch, pr * LANES:(pr + 1) * LANES])
                          for c in range(nch) for pr in range(npr)]).astype(BF16)

    a_t = tiles(-kk * jnp.exp(cum - lw))
    r_t = tiles(r * jnp.exp(cum))
    b_h = tiles(bb * e_inv)
    k_h = tiles(k2 * e_inv)
    bke = jnp.concatenate([tiles(bb * e_end), tiles(k2 * e_end)], axis=1)
    v_t = tiles(v)

    ri = _iota((c2, c2), 0)
    ci = _iota((c2, c2), 1)
    same = jnp.right_shift(ri, LOG2_64) == jnp.right_shift(ci, LOG2_64)
    strict = jnp.where(same & (ci < ri), 1.0, 0.0)
    incl = jnp.where(same & (ci <= ri), 1.0, 0.0)
    ident = jnp.where(ri == ci, 1.0, 0.0)

    scores = _bnt(jnp.concatenate([a_t, r_t], axis=1), jnp.concatenate([b_h, k_h], axis=1))
    m_ab = scores[:, 0:c2, 0:c2] * strict
    m_ak = (scores[:, 0:c2, c2:2 * c2] * strict).astype(BF16)
    m_rbk = jnp.concatenate([scores[:, c2:2 * c2, 0:c2] * incl,
                             scores[:, c2:2 * c2, c2:2 * c2] * incl], axis=2).astype(BF16)
    tinv = ident + m_ab
    mp = m_ab
    for _ in range(5):
        mpb = mp.astype(BF16)
        mp = _bnn(mpb, mpb)
        tinv = tinv + _bnn(tinv.astype(BF16), mp.astype(BF16))
    rhs = jnp.concatenate([a_t, _bnn(m_ak, v_t).astype(BF16)], axis=2)
    wu = _bnn(tinv.astype(BF16), rhs)
    w_t = wu[:, :, 0:c2].astype(BF16)
    u0 = wu[:, :, c2:2 * c2]

    g_st = state_ref[...]
    y_rows = []
    for c in range(nch):
        sl = slice(c * npr, (c + 1) * npr)
        g_b = g_st.astype(BF16)
        u = _bnt(w_t[sl], g_b) + u0[sl]
        uv = jnp.concatenate([u.astype(BF16), v_t[sl]], axis=1)
        y = _bnt(r_t[sl], g_b) + _bnn(m_rbk[sl], uv)
        y = y[:, 0:ch, :] + y[:, ch:c2, :]
        y_rows.append(jnp.concatenate([y[pr] for pr in range(npr)], axis=1))
        dec_c = jnp.stack([dec[c * ch:c * ch + 1, pr * LANES:(pr + 1) * LANES] for pr in range(npr)])
        g_st = g_st * dec_c + _btn(uv, bke[sl])
    state_ref[...] = g_st
    y = jnp.concatenate(y_rows, axis=0)

    inv = 1.0 / RWKV_HEAD_DIM
    mean = _seg_sum(y, bd) * inv
    yc = y - mean
    var = _seg_sum(yc * yc, bd) * inv
    yn = yc * lax.rsqrt(var + GN_EPS) * lnw_ref[...] + lnb_ref[...]
    bonus = _seg_sum(r * k2 * rk_ref[...], bd) * v
    o_ref[0] = ((yn + bonus) * gate).astype(BF16)


def _rwkv_call(p, mu, w0, wup, a0, aup, gup, k_k, k_a, r_k, ln_w, ln_b, tt=256):
    b, s, _ = p.shape
    full = lambda a: pl.BlockSpec(a.shape, lambda i, j: (0,) * a.ndim)
    consts = (mu, w0, wup, a0, aup, gup, k_k, k_a, r_k, ln_w, ln_b)
    return pl.pallas_call(
        functools.partial(_rwkv_kernel, tt=tt),
        grid=(b, s // tt),
        in_specs=[pl.BlockSpec((1, tt, RWKV_MIX_DIM), lambda i, j: (i, j, 0))]
                 + [full(c) for c in consts],
        out_specs=pl.BlockSpec((1, tt, RWKV_DIM), lambda i, j: (i, j, 0)),
        out_shape=jax.ShapeDtypeStruct((b, s, RWKV_DIM), BF16),
        scratch_shapes=[pltpu.VMEM((8, RWKV_MIX_DIM), F32),
                        pltpu.VMEM((RWKV_PAIRS, LANES, LANES), F32)],
        compiler_params=_params("arbitrary", "arbitrary"),
        name="rwkv",
    )(p, *consts)


def _merge_kernel(x_ref, ya_ref, yb_ref, mg_ref, wa_ref, wb_ref, wo_ref, fg_ref, xo_ref, h_ref):
    ga = jax.nn.sigmoid(mg_ref[:, 0:D_MODEL])
    gb = jax.nn.sigmoid(mg_ref[:, D_MODEL:2 * D_MODEL])
    merged = ga * _nn(ya_ref[...], wa_ref[...]) + gb * _nn(yb_ref[...], wb_ref[...])
    xn = x_ref[...] + _nn(merged.astype(BF16), wo_ref[...])
    xo_ref[...] = xn
    ms = jnp.mean(xn * xn, axis=-1, keepdims=True)
    h_ref[...] = (xn * lax.rsqrt(ms + RMS_EPS) * fg_ref[...]).astype(BF16)


def _merge_call(x2, ya, yb, mg, wa_pad, wb, wo, fgain, tm=512):
    n = x2.shape[0]
    row = lambda w: pl.BlockSpec((tm, w), lambda i: (i, 0))
    full = lambda a: pl.BlockSpec(a.shape, lambda i: (0,) * a.ndim)
    return pl.pallas_call(
        _merge_kernel,
        grid=(n // tm,),
        in_specs=[row(D_MODEL), row(Q_PAD), row(RWKV_DIM), row(2 * D_MODEL),
                  full(wa_pad), full(wb), full(wo), full(fgain)],
        out_specs=[row(D_MODEL), row(D_MODEL)],
        out_shape=[jax.ShapeDtypeStruct((n, D_MODEL), F32), jax.ShapeDtypeStruct((n, D_MODEL), BF16)],
        compiler_params=_params("parallel"),
        name="merge",
    )(x2, ya, yb, mg, wa_pad, wb, wo, fgain)


def _ffn_kernel(x_ref, h_ref, wu_ref, wd_ref, o_ref, *, fc):
    h = h_ref[...]
    acc = x_ref[...]
    for c in range(D_FF // fc):
        up = jnp.maximum(_nn(h, wu_ref[:, c * fc:(c + 1) * fc]), 0.0)
        acc = acc + _nn((up * up).astype(BF16), wd_ref[c * fc:(c + 1) * fc, :])
    o_ref[...] = acc


def _ffn_call(x2, h, wu, wd, tm=512, fc=1024):
    n = x2.shape[0]
    row = lambda w: pl.BlockSpec((tm, w), lambda i: (i, 0))
    full = lambda a: pl.BlockSpec(a.shape, lambda i: (0,) * a.ndim)
    return pl.pallas_call(
        functools.partial(_ffn_kernel, fc=fc),
        grid=(n // tm,),
        in_specs=[row(D_MODEL), row(D_MODEL), full(wu), full(wd)],
        out_specs=row(D_MODEL),
        out_shape=jax.ShapeDtypeStruct((n, D_MODEL), F32),
        compiler_params=_params("parallel"),
        name="ffn",
    )(x2, h, wu, wd)


def _pad_heads_cols(w):
    rows = w.shape[0]
    wh = w.reshape(rows, ATTN_HEADS, HEAD_DIM)
    slots = []
    for j in range(ATTN_HEADS):
        g = j // HEADS_PER_GROUP
        z = jnp.zeros((rows, HEAD_DIM), w.dtype)
        slots.append(jnp.concatenate([wh[:, j], z] if g == 0 else [z, wh[:, j]], axis=1))
    return jnp.concatenate(slots, axis=1)


def _proj_weight(w_in):
    q = _pad_heads_cols(w_in[:, 0:ATTN_DIM])
    kv = w_in[:, ATTN_DIM:ATTN_DIM + 6 * KV_DIM]
    gates = jnp.pad(w_in[:, ATTN_DIM + 6 * KV_DIM:NSA_DIM], ((0, 0), (0, LANES - NSA_GATES)))
    rest = w_in[:, NSA_DIM:]
    return jnp.concatenate([q, kv, gates, rest], axis=1).astype(BF16)


def _values_t(v, tile):
    b, s, _ = v.shape
    vt = v.reshape(b, s // tile, tile, KV_DIM).transpose(0, 1, 3, 2)
    row = jnp.arange(KV_DIM)[:, None] // HEAD_DIM
    one = jnp.ones((), v.dtype)
    return jnp.stack([jnp.where(row == g, vt, one) for g in range(KV_GROUPS)], axis=1)


def _to_groups16(t, b, s):
    t = t.reshape(b, s // CMP_STRIDE, CMP_STRIDE, KV_GROUPS, HEAD_DIM)
    t = t.transpose(0, 3, 1, 2, 4)
    return t.reshape(b, KV_GROUPS, s // CMP_STRIDE, CMP_STRIDE * HEAD_DIM)


def kernel(x, mix_norm, w_in, q_gain, k_gain, cmp_pos, cmp_w1, cmp_w2, w_attn_branch, tok_mix, w0,
           w_lora_up, a0, a_lora_up, g_lora_up, k_k, k_a, r_k, ln_x_w, ln_x_b, w_rwkv_branch, w_out,
           ffn_norm, w_ffn_up, w_ffn_down):
    b, s, d = x.shape
    n = b * s
    depth = w_in.shape[0]
    row = lambda v: v.reshape(1, -1)
    x2 = x.reshape(n, d)
    for l in range(depth):
        qg_pad = _pad_heads_cols(jnp.tile(q_gain[l], ATTN_HEADS).reshape(1, ATTN_DIM))
        kg2 = jnp.tile(k_gain[l, 1:3], (1, KV_GROUPS))
        q, kc, vc, ks, vs, kw, vw, gates, rw, mg = _proj_call(
            x2, row(mix_norm[l]), _proj_weight(w_in[l]), qg_pad, kg2)

        xg = jnp.stack([_to_groups16(kc, b, s), _to_groups16(vc, b, s)]).astype(BF16)
        cmp = _compress_call(xg, cmp_w1[l].astype(BF16), cmp_w2[l].astype(BF16),
                             cmp_pos[l].reshape(2, 1, CMP_BLOCK * HEAD_DIM), row(k_gain[l, 0]))
        cmp = cmp.transpose(0, 1, 3, 2, 4).reshape(2, b, s // CMP_STRIDE, KV_DIM).astype(BF16)

        seq = lambda t: t.reshape(b, s, t.shape[-1])
        ya = _nsa_call(seq(q), seq(gates), cmp[0], cmp[1].transpose(0, 2, 1), seq(ks),
                       _values_t(seq(vs), NSA_KEY_TILE), seq(kw), _values_t(seq(vw), NSA_QUERY_TILE))

        zero = jnp.zeros((W_LORA, RWKV_DIM), F32)
        wup = jnp.concatenate([w_lora_up[l], zero], axis=0).astype(BF16)
        aup = jnp.concatenate([zero, a_lora_up[l]], axis=0).astype(BF16)
        yb = _rwkv_call(seq(rw), row(tok_mix[l]), row(w0[l]), wup, row(a0[l]), aup,
                        g_lora_up[l].astype(BF16), row(k_k[l]), row(k_a[l]), row(r_k[l]),
                        row(ln_x_w[l]), row(ln_x_b[l]))

        wa_pad = _pad_heads_cols(w_attn_branch[l].T).T.astype(BF16)
        x2, h = _merge_call(x2, ya.reshape(n, Q_PAD), yb.reshape(n, RWKV_DIM), mg, wa_pad,
                            w_rwkv_branch[l].astype(BF16), w_out[l].astype(BF16), row(ffn_norm[l]))
        x2 = _ffn_call(x2, h, w_ffn_up[l].astype(BF16), w_ffn_down[l].astype(BF16))
    return x2.reshape(b, s, d)
```

```python
import functools

import jax
import jax.numpy as jnp
import numpy as np
from jax import lax
from jax.experimental import pallas as pl
from jax.experimental.pallas import tpu as pltpu

F32 = jnp.float32
BF16 = jnp.bfloat16
I32 = jnp.int32

D_MODEL = 1024
ATTN_HEADS = 8
HEAD_DIM = 64
KV_GROUPS = 2
HEADS_PER_GROUP = ATTN_HEADS // KV_GROUPS
ATTN_DIM = ATTN_HEADS * HEAD_DIM
KV_DIM = KV_GROUPS * HEAD_DIM
N_NSA_BRANCHES = 3
CMP_BLOCK = 32
CMP_STRIDE = 16
CMP_HIDDEN = 2 * HEAD_DIM
SLC_BLOCK = 64
N_SELECT = 16
WINDOW = 512
FORCE_BONUS = 1000.0
RWKV_HEADS = 8
RWKV_HEAD_DIM = 64
RWKV_DIM = RWKV_HEADS * RWKV_HEAD_DIM
W_LORA = 64
A_LORA = 64
G_LORA = 128
RWKV_MIX_DIM = 3 * RWKV_DIM + W_LORA + A_LORA + G_LORA
D_FF = 4 * D_MODEL
NSA_GATES = N_NSA_BRANCHES * ATTN_HEADS
NSA_DIM = ATTN_DIM + 6 * KV_DIM + NSA_GATES
RMS_EPS = 1e-6
GN_EPS = 64e-5
NEG_INF = -1e30

LOG2_64 = 6
LANES = 128
SUBLANES = 8
MXU_COLS = 256
Q_SCALE = HEAD_DIM ** -0.5 * 1.4426950408889634
Q_PAD = ATTN_HEADS * LANES
RWKV_PAIRS = RWKV_HEADS // 2
RWKV_CHUNK = 64
NSA_QUERY_TILE = 128
NSA_KEY_TILE = 512
VMEM_LIMIT = 48 * 1024 * 1024

C_Q = 0
C_KV = C_Q + Q_PAD
C_GATE = C_KV + 6 * KV_DIM
C_RWKV = C_GATE + LANES
C_MERGE = C_RWKV + RWKV_MIX_DIM
C_TOTAL = C_MERGE + 2 * D_MODEL


def _nn(a, b):
    return lax.dot_general(a, b, (((1,), (0,)), ((), ())), preferred_element_type=F32)


def _nt(a, b):
    return lax.dot_general(a, b, (((1,), (1,)), ((), ())), preferred_element_type=F32)


def _bnn(a, b):
    return lax.dot_general(a, b, (((2,), (1,)), ((0,), (0,))), preferred_element_type=F32)


def _bnt(a, b):
    return lax.dot_general(a, b, (((2,), (2,)), ((0,), (0,))), preferred_element_type=F32)


def _btn(a, b):
    return lax.dot_general(a, b, (((1,), (1,)), ((0,), (0,))), preferred_element_type=F32)


def _split2(x):
    hi = x.astype(BF16)
    lo = (x - hi.astype(F32)).astype(BF16)
    return hi, lo


def _iota(shape, dim):
    return lax.broadcasted_iota(I32, shape, dim)


def _params(*sem):
    return pltpu.CompilerParams(dimension_semantics=sem, vmem_limit_bytes=VMEM_LIMIT)


def _group_rms(x, gain):
    lane = _iota((1, LANES), 1)
    sq = x * x
    lo = lane < HEAD_DIM
    s0 = jnp.sum(jnp.where(lo, sq, 0.0), axis=-1, keepdims=True)
    s1 = jnp.sum(jnp.where(lo, 0.0, sq), axis=-1, keepdims=True)
    ms = jnp.where(lo, s0, s1) * (1.0 / HEAD_DIM)
    return x * lax.rsqrt(ms + RMS_EPS) * gain


def _proj_kernel(x_ref, g_ref, w_ref, qg_ref, kg_ref,
                 q_ref, kc_ref, vc_ref, ks_ref, vs_ref, kw_ref, vw_ref, gt_ref, rw_ref, mg_ref):
    x = x_ref[...]
    ms = jnp.mean(x * x, axis=-1, keepdims=True)
    u = (x * lax.rsqrt(ms + RMS_EPS) * g_ref[...]).astype(BF16)

    def col(c0, width):
        return _nn(u, w_ref[:, c0:c0 + width])

    for jp in range(ATTN_HEADS // 2):
        qq = col(C_Q + jp * MXU_COLS, MXU_COLS)
        for half in range(2):
            j = 2 * jp + half
            qj = qq[:, half * LANES:(half + 1) * LANES]
            msq = jnp.sum(qj * qj, axis=-1, keepdims=True) * (1.0 / HEAD_DIM)
            qn = qj * lax.rsqrt(msq + RMS_EPS) * qg_ref[:, j * LANES:(j + 1) * LANES]
            q_ref[:, j * LANES:(j + 1) * LANES] = (qn * Q_SCALE).astype(BF16)
    kvc = col(C_KV, MXU_COLS)
    kc_ref[...] = kvc[:, 0:KV_DIM]
    vc_ref[...] = kvc[:, KV_DIM:2 * KV_DIM]
    kvs = col(C_KV + MXU_COLS, MXU_COLS)
    ks_ref[...] = _group_rms(kvs[:, 0:KV_DIM], kg_ref[0:1, :]).astype(BF16)
    vs_ref[...] = kvs[:, KV_DIM:2 * KV_DIM].astype(BF16)
    kvw = col(C_KV + 2 * MXU_COLS, MXU_COLS)
    kw_ref[...] = _group_rms(kvw[:, 0:KV_DIM], kg_ref[1:2, :]).astype(BF16)
    vw_ref[...] = kvw[:, KV_DIM:2 * KV_DIM].astype(BF16)
    gt_ref[...] = col(C_GATE, LANES)
    for c in range(RWKV_MIX_DIM // MXU_COLS):
        rw_ref[:, c * MXU_COLS:(c + 1) * MXU_COLS] = col(C_RWKV + c * MXU_COLS, MXU_COLS)
    for c in range(2 * D_MODEL // 512):
        mg_ref[:, c * 512:(c + 1) * 512] = col(C_MERGE + c * 512, 512).astype(BF16)


def _proj_call(x2, gain, w_all, qg_pad, kg2, tm=512):
    n = x2.shape[0]
    row = lambda w: pl.BlockSpec((tm, w), lambda i: (i, 0))
    full = lambda a: pl.BlockSpec(a.shape, lambda i: (0,) * a.ndim, pipeline_mode=pl.Buffered(1))
    out_shapes = [
        jax.ShapeDtypeStruct((n, Q_PAD), BF16),
        jax.ShapeDtypeStruct((n, KV_DIM), F32), jax.ShapeDtypeStruct((n, KV_DIM), F32),
        jax.ShapeDtypeStruct((n, KV_DIM), BF16), jax.ShapeDtypeStruct((n, KV_DIM), BF16),
        jax.ShapeDtypeStruct((n, KV_DIM), BF16), jax.ShapeDtypeStruct((n, KV_DIM), BF16),
        jax.ShapeDtypeStruct((n, LANES), F32),
        jax.ShapeDtypeStruct((n, RWKV_MIX_DIM), F32),
        jax.ShapeDtypeStruct((n, 2 * D_MODEL), BF16),
    ]
    return pl.pallas_call(
        _proj_kernel,
        grid=(n // tm,),
        in_specs=[row(D_MODEL), full(gain), full(w_all), full(qg_pad), full(kg2)],
        out_specs=[row(s.shape[1]) for s in out_shapes],
        out_shape=out_shapes,
        compiler_params=_params("parallel"),
        name="proj",
    )(x2, gain, w_all, qg_pad, kg2)


def _compress_kernel(x_ref, w1_ref, w2_ref, pos_ref, kg_ref, o_ref, *, nc):
    which = pl.program_id(0)
    half = CMP_STRIDE * HEAD_DIM
    x = x_ref[0, 0, 0]
    ha = _nn(x, w1_ref[0, 0:half, :])
    hb = _nn(x, w1_ref[0, half:2 * half, :])
    pos8 = jnp.broadcast_to(pos_ref[0], (8, 2 * half)).astype(BF16)
    bias = _nn(pos8, w1_ref[0])[0:1, :]
    hid = ha + pltpu.roll(hb, nc - 1, axis=0) + bias
    act = jax.nn.gelu(hid)
    out = _nn(act.astype(BF16), w2_ref[0])
    ms = jnp.mean(out * out, axis=-1, keepdims=True)
    normed = out * lax.rsqrt(ms + RMS_EPS) * kg_ref[...]
    out = jnp.where(which == 0, normed, out)
    rowi = _iota((nc, 1), 0)
    o_ref[0, 0, 0] = jnp.where(rowi < nc - 1, out, 0.0)


def _compress_call(xg, w1, w2, pos_flat, kgain):
    _, b, g, nc, width = xg.shape
    return pl.pallas_call(
        functools.partial(_compress_kernel, nc=nc),
        grid=(2, b, g),
        in_specs=[
            pl.BlockSpec((1, 1, 1, nc, width), lambda w, i, j: (w, i, j, 0, 0)),
            pl.BlockSpec((1, width * 2, CMP_HIDDEN), lambda w, i, j: (w, 0, 0)),
            pl.BlockSpec((1, CMP_HIDDEN, HEAD_DIM), lambda w, i, j: (w, 0, 0)),
            pl.BlockSpec((1, 1, width * 2), lambda w, i, j: (w, 0, 0)),
            pl.BlockSpec((1, HEAD_DIM), lambda w, i, j: (0, 0)),
        ],
        out_specs=pl.BlockSpec((1, 1, 1, nc, HEAD_DIM), lambda w, i, j: (w, i, j, 0, 0)),
        out_shape=jax.ShapeDtypeStruct((2, b, g, nc, HEAD_DIM), F32),
        compiler_params=_params("parallel", "parallel", "parallel"),
        name="compress",
    )(xg, w1, w2, pos_flat, kgain)


def _nsa_kernel(q_ref, gt_ref, kc_ref, vct_ref, ks_ref, vst_ref, kw_ref, vwt_ref, y_ref,
                m_s, acc_s, sb_s, *, tq, tk, nb, nc):
    nh = ATTN_HEADS
    hp = HEADS_PER_GROUP
    cols = hp * tq
    wk = WINDOW + tq
    bpt = tk // SLC_BLOCK
    qi = pl.program_id(1)
    q0 = qi * tq
    tlane = q0 + _iota((1, tq), 1)
    tile_h = lambda a, reps: jnp.concatenate([a] * reps, axis=1)
    eye_d = jnp.where(_iota((LANES, LANES), 0) == _iota((LANES, LANES), 1), 1.0, 0.0).astype(BF16)
    eye_q = jnp.where(_iota((tq, tq), 0) == _iota((tq, tq), 1), 1.0, 0.0).astype(BF16)
    drow = _iota((LANES, 1), 0)
    own = [(drow >= g * HEAD_DIM) & (drow < (g + 1) * HEAD_DIM) for g in range(KV_GROUPS)]

    qall = jnp.concatenate([q_ref[0, :, j * LANES:(j + 1) * LANES] for j in range(nh)], axis=0)
    q_t = _nt(eye_d, qall).astype(BF16)

    ncol = _iota((nc, 1), 0)
    cvalid = (ncol * CMP_STRIDE + (CMP_BLOCK - 1) <= tlane) & (ncol < nc - 1)
    s_c = _nn(kc_ref[0], q_t) + tile_h(jnp.where(cvalid, 0.0, NEG_INF), nh)
    e_c = jnp.exp2(s_c - jnp.max(s_c, axis=0, keepdims=True))
    p_c = e_c * (1.0 / jnp.sum(e_c, axis=0, keepdims=True)) * tile_h(jnp.where(cvalid, 1.0, 0.0), nh)
    p_cb = p_c.astype(BF16)
    o_c = [_nn(vct_ref[0], p_cb[:, g * cols:(g + 1) * cols]) for g in range(KV_GROUPS)]

    jcol = _iota((nb, 1), 0)
    ncmp = _iota((nb, nc), 1)
    ov = ((ncmp * CMP_STRIDE <= jcol * SLC_BLOCK + (SLC_BLOCK - 1))
          & (ncmp * CMP_STRIDE + (CMP_BLOCK - 1) >= jcol * SLC_BLOCK))
    ov = jnp.where(ov, 1.0, 0.0).astype(BF16)
    cur = jnp.right_shift(tlane, LOG2_64)
    forced = (jcol == 0) | (jcol == cur) | (jcol == cur - 1)
    causal_blk = jcol * SLC_BLOCK <= tlane
    for g in range(KV_GROUPS):
        c0 = g * cols
        psum = (p_c[:, c0:c0 + tq] + p_c[:, c0 + tq:c0 + 2 * tq]
                + p_c[:, c0 + 2 * tq:c0 + 3 * tq] + p_c[:, c0 + 3 * tq:c0 + 4 * tq])
        hi, lo = _split2(psum)
        imp = _nn(ov, hi) + _nn(ov, lo)
        imp = jnp.where(causal_blk, imp + jnp.where(forced, FORCE_BONUS, 0.0), -1.0)
        groups = [imp[r:r + SUBLANES, :] for r in range(0, nb, SUBLANES)]
        ranks = [jnp.zeros((SUBLANES, tq), F32) for _ in groups]
        jsub = _iota((SUBLANES, 1), 0)
        for i in range(nb):
            ri = imp[i:i + 1, :]
            for gi, grp in enumerate(groups):
                r0 = gi * SUBLANES
                if r0 + SUBLANES - 1 < i:
                    ahead = jnp.where(ri > grp, 1.0, 0.0)
                elif r0 > i:
                    ahead = jnp.where(ri >= grp, 1.0, 0.0)
                else:
                    ahead = jnp.where(jsub > i - r0, jnp.where(ri >= grp, 1.0, 0.0),
                                      jnp.where(ri > grp, 1.0, 0.0))
                ranks[gi] = ranks[gi] + ahead
        rank = jnp.concatenate(ranks, axis=0)
        sb_s[g] = jnp.where(rank < float(min(N_SELECT, nb)), 0.0, NEG_INF)

    m_s[...] = jnp.full(m_s.shape, NEG_INF, F32)
    acc_s[...] = jnp.zeros(acc_s.shape, F32)

    def body(kt, _):
        k0 = pl.multiple_of(kt * tk, tk)
        s = _nn(ks_ref[0, pl.ds(k0, tk), :], q_t)
        causal = (k0 + _iota((tk, 1), 0)) <= tlane
        for g in range(KV_GROUPS):
            cs = slice(g * cols, (g + 1) * cols)
            bias = jnp.concatenate(
                [jnp.broadcast_to(sb_s[g, pl.ds(kt * bpt + i, 1), :], (SLC_BLOCK, tq)) for i in range(bpt)],
                axis=0)
            sm = s[:, cs] + tile_h(jnp.where(causal, bias, NEG_INF), hp)
            m_prev = m_s[0:1, cs]
            m_new = jnp.maximum(m_prev, jnp.max(sm, axis=0, keepdims=True))
            m_s[:, cs] = jnp.broadcast_to(m_new, (8, cols))
            p = jnp.exp2(sm - m_new).astype(BF16)
            acc_s[:, cs] = jnp.exp2(m_prev - m_new) * acc_s[:, cs] + _nn(vst_ref[0, g, kt], p)
        return 0

    lax.fori_loop(0, qi // (tk // tq) + 1, body, 0)

    w0 = pl.multiple_of(jnp.maximum(q0 - WINDOW, 0), tq)
    dist = tlane - (w0 + _iota((wk, 1), 0))
    s_w = (_nn(kw_ref[0, pl.ds(w0, wk), :], q_t)
           + tile_h(jnp.where((dist >= 0) & (dist < WINDOW), 0.0, NEG_INF), nh))
    p_w = jnp.exp2(s_w - jnp.max(s_w, axis=0, keepdims=True)).astype(BF16)
    wb = w0 // tq

    g_hi, g_lo = _split2(jax.nn.sigmoid(gt_ref[0]))
    gate_t = _nt(eye_d, g_hi) + _nt(eye_d, g_lo)
    for g in range(KV_GROUPS):
        cs = slice(g * cols, (g + 1) * cols)
        den_row = (1 - g) * HEAD_DIM
        acc = acc_s[:, cs]
        o_s = acc * (1.0 / acc[den_row:den_row + 1, :])
        v_w = jnp.concatenate([vwt_ref[0, g, wb + i] for i in range(wk // tq)], axis=1)
        acc_w = _nn(v_w, p_w[:, cs])
        o_w = acc_w * (1.0 / acc_w[den_row:den_row + 1, :])
        for h in range(hp):
            j = g * hp + h
            r0 = j * N_NSA_BRANCHES
            hs = slice(h * tq, (h + 1) * tq)
            o = (gate_t[r0:r0 + 1, :] * o_c[g][:, hs] + gate_t[r0 + 1:r0 + 2, :] * o_s[:, hs]
                 + gate_t[r0 + 2:r0 + 3, :] * o_w[:, hs])
            o = jnp.where(own[g], o, 0.0).astype(BF16)
            y_ref[0, :, j * LANES:(j + 1) * LANES] = _nt(eye_q, o).astype(BF16)


def _nsa_call(q, gates, kc, vct, ks, vst, kw, vwt):
    b, s, _ = q.shape
    tq = NSA_QUERY_TILE
    nb = s // SLC_BLOCK
    nc = kc.shape[1]
    tk = vst.shape[-1]
    tile = lambda w: pl.BlockSpec((1, tq, w), lambda i, j: (i, j, 0))
    seq = lambda a: pl.BlockSpec((1,) + a.shape[1:], lambda i, j: (i,) + (0,) * (a.ndim - 1))
    return pl.pallas_call(
        functools.partial(_nsa_kernel, tq=tq, tk=tk, nb=nb, nc=nc),
        grid=(b, s // tq),
        in_specs=[tile(Q_PAD), tile(LANES), seq(kc), seq(vct), seq(ks), seq(vst), seq(kw), seq(vwt)],
        out_specs=tile(Q_PAD),
        out_shape=jax.ShapeDtypeStruct((b, s, Q_PAD), BF16),
        scratch_shapes=[pltpu.VMEM((8, ATTN_HEADS * tq), F32),
                        pltpu.VMEM((LANES, ATTN_HEADS * tq), F32),
                        pltpu.VMEM((KV_GROUPS, nb, tq), F32)],
        compiler_params=_params("parallel", "arbitrary"),
        name="nsa",
    )(q, gates, kc, vct, ks, vst, kw, vwt)


def _seg_sum(x, bd):
    hi, lo = _split2(x)
    return _nn(hi, bd) + _nn(lo, bd)


def _stack_heads(x):
    lane = _iota((1, LANES), 1)
    lo = lane < RWKV_HEAD_DIM
    return jnp.concatenate([jnp.where(lo, x, 0.0), jnp.where(lo, 0.0, x)], axis=0)


def _rwkv_kernel(p_ref, mu_ref, w0_ref, wup_ref, a0_ref, aup_ref, gup_ref, kk_ref, ka_ref, rk_ref,
                 lnw_ref, lnb_ref, o_ref, carry_ref, state_ref, *, tt):
    ch = RWKV_CHUNK
    c2 = 2 * ch
    d = RWKV_DIM
    nch = tt // ch
    npr = RWKV_PAIRS

    @pl.when(pl.program_id(1) == 0)
    def _():
        carry_ref[...] = jnp.zeros_like(carry_ref)
        state_ref[...] = jnp.zeros_like(state_ref)

    p = p_ref[0]
    prev = pltpu.roll(p, 1, axis=0)
    prev = jnp.where(_iota((tt, 1), 0) == 0, carry_ref[0:1, :], prev)
    carry_ref[0:1, :] = p[tt - 1:tt, :]
    pm = p + (prev - p) * mu_ref[...]

    r = pm[:, 0:d]
    k = pm[:, d:2 * d]
    v = pm[:, 2 * d:3 * d]
    wa = pm[:, 3 * d:3 * d + LANES]
    gl = pm[:, 3 * d + LANES:3 * d + 2 * LANES]

    z = w0_ref[...] + _nn(jnp.tanh(wa).astype(BF16), wup_ref[...])
    softplus = jnp.maximum(-z, 0.0) + jnp.log(1.0 + jnp.exp(-jnp.abs(z)))
    lw = -jnp.exp(-softplus - 0.5)
    a = jax.nn.sigmoid(a0_ref[...] + _nn(wa.astype(BF16), aup_ref[...]))
    gate = _nn(jax.nn.sigmoid(gl).astype(BF16), gup_ref[...])

    seg = jnp.right_shift(_iota((d, d), 0), LOG2_64) == jnp.right_shift(_iota((d, d), 1), LOG2_64)
    bd = jnp.where(seg, 1.0, 0.0).astype(BF16)
    kk = k * kk_ref[...]
    kk = kk * lax.rsqrt(jnp.maximum(_seg_sum(kk * kk, bd), 1e-24))
    k2 = k * (1.0 + (a - 1.0) * ka_ref[...])

    ti = _iota((tt, tt), 0)
    tj = _iota((tt, tt), 1)
    same_chunk = jnp.right_shift(ti, LOG2_64) == jnp.right_shift(tj, LOG2_64)
    tri = jnp.where(same_chunk & (tj <= ti), 1.0, 0.0).astype(BF16)
    blk = jnp.where(same_chunk, 1.0, 0.0).astype(BF16)
    hi, lo = _split2(lw)
    cum = _nn(tri, hi) + _nn(tri, lo)
    tot = _nn(blk, hi) + _nn(blk, lo)
    e_inv = jnp.exp(-cum)
    e_end = jnp.exp(tot - cum)
    dec = jnp.exp(tot)
    bb = kk * a

    def tiles(x):
        return jnp.stack([_stack_heads(x[c * ch:(c + 1) * ch, pr * LANES:(pr + 1) * LANES])
                          for c in range(nch) for pr in range(npr)]).astype(BF16)

    a_t = tiles(-kk * jnp.exp(cum - lw))
    r_t = tiles(r * jnp.exp(cum))
    b_h = tiles(bb * e_inv)
    k_h = tiles(k2 * e_inv)
    bke = jnp.concatenate([tiles(bb * e_end), tiles(k2 * e_end)], axis=1)
    v_t = tiles(v)

    ri = _iota((c2, c2), 0)
    ci = _iota((c2, c2), 1)
    same = jnp.right_shift(ri, LOG2_64) == jnp.right_shift(ci, LOG2_64)
    strict = jnp.where(same & (ci < ri), 1.0, 0.0)
    incl = jnp.where(same & (ci <= ri), 1.0, 0.0)
    ident = jnp.where(ri == ci, 1.0, 0.0)

    scores = _bnt(jnp.concatenate([a_t, r_t], axis=1), jnp.concatenate([b_h, k_h], axis=1))
    m_ab = scores[:, 0:c2, 0:c2] * strict
    m_ak = (scores[:, 0:c2, c2:2 * c2] * strict).astype(BF16)
    m_rbk = jnp.concatenate([scores[:, c2:2 * c2, 0:c2] * incl,
                             scores[:, c2:2 * c2, c2:2 * c2] * incl], axis=2).astype(BF16)
    tinv = ident + m_ab
    mp = m_ab
    for _ in range(5):
        mpb = mp.astype(BF16)
        mp = _bnn(mpb, mpb)
        tinv = tinv + _bnn(tinv.astype(BF16), mp.astype(BF16))
    rhs = jnp.concatenate([a_t, _bnn(m_ak, v_t).astype(BF16)], axis=2)
    wu = _bnn(tinv.astype(BF16), rhs)
    w_t = wu[:, :, 0:c2].astype(BF16)
    u0 = wu[:, :, c2:2 * c2]

    g_st = state_ref[...]
    y_rows = []
    for c in range(nch):
        sl = slice(c * npr, (c + 1) * npr)
        g_b = g_st.astype(BF16)
        u = _bnt(w_t[sl], g_b) + u0[sl]
        uv = jnp.concatenate([u.astype(BF16), v_t[sl]], axis=1)
        y = _bnt(r_t[sl], g_b) + _bnn(m_rbk[sl], uv)
        y = y[:, 0:ch, :] + y[:, ch:c2, :]
        y_rows.append(jnp.concatenate([y[pr] for pr in range(npr)], axis=1))
        dec_c = jnp.stack([dec[c * ch:c * ch + 1, pr * LANES:(pr + 1) * LANES] for pr in range(npr)])
        g_st = g_st * dec_c + _btn(uv, bke[sl])
    state_ref[...] = g_st
    y = jnp.concatenate(y_rows, axis=0)

    inv = 1.0 / RWKV_HEAD_DIM
    mean = _seg_sum(y, bd) * inv
    yc = y - mean
    var = _seg_sum(yc * yc, bd) * inv
    yn = yc * lax.rsqrt(var + GN_EPS) * lnw_ref[...] + lnb_ref[...]
    bonus = _seg_sum(r * k2 * rk_ref[...], bd) * v
    o_ref[0] = ((yn + bonus) * gate).astype(BF16)


def _rwkv_call(p, mu, w0, wup, a0, aup, gup, k_k, k_a, r_k, ln_w, ln_b, tt=256):
    b, s, _ = p.shape
    full = lambda a: pl.BlockSpec(a.shape, lambda i, j: (0,) * a.ndim)
    consts = (mu, w0, wup, a0, aup, gup, k_k, k_a, r_k, ln_w, ln_b)
    return pl.pallas_call(
        functools.partial(_rwkv_kernel, tt=tt),
        grid=(b, s // tt),
        in_specs=[pl.BlockSpec((1, tt, RWKV_MIX_DIM), lambda i, j: (i, j, 0))]
                 + [full(c) for c in consts],
        out_specs=pl.BlockSpec((1, tt, RWKV_DIM), lambda i, j: (i, j, 0)),
        out_shape=jax.ShapeDtypeStruct((b, s, RWKV_DIM), BF16),
        scratch_shapes=[pltpu.VMEM((8, RWKV_MIX_DIM), F32),
                        pltpu.VMEM((RWKV_PAIRS, LANES, LANES), F32)],
        compiler_params=_params("arbitrary", "arbitrary"),
        name="rwkv",
    )(p, *consts)


def _merge_kernel(x_ref, ya_ref, yb_ref, mg_ref, wa_ref, wb_ref, wo_ref, fg_ref, xo_ref, h_ref):
    ga = jax.nn.sigmoid(mg_ref[:, 0:D_MODEL].astype(F32))
    gb = jax.nn.sigmoid(mg_ref[:, D_MODEL:2 * D_MODEL].astype(F32))
    merged = ga * _nn(ya_ref[...], wa_ref[...]) + gb * _nn(yb_ref[...], wb_ref[...])
    xn = x_ref[...] + _nn(merged.astype(BF16), wo_ref[...])
    xo_ref[...] = xn
    ms = jnp.mean(xn * xn, axis=-1, keepdims=True)
    h_ref[...] = (xn * lax.rsqrt(ms + RMS_EPS) * fg_ref[...]).astype(BF16)


def _merge_call(x2, ya, yb, mg, wa_pad, wb, wo, fgain, tm=512):
    n = x2.shape[0]
    row = lambda w: pl.BlockSpec((tm, w), lambda i: (i, 0))
    full = lambda a: pl.BlockSpec(a.shape, lambda i: (0,) * a.ndim)
    return pl.pallas_call(
        _merge_kernel,
        grid=(n // tm,),
        in_specs=[row(D_MODEL), row(Q_PAD), row(RWKV_DIM), row(2 * D_MODEL),
                  full(wa_pad), full(wb), full(wo), full(fgain)],
        out_specs=[row(D_MODEL), row(D_MODEL)],
        out_shape=[jax.ShapeDtypeStruct((n, D_MODEL), F32), jax.ShapeDtypeStruct((n, D_MODEL), BF16)],
        compiler_params=_params("parallel"),
        name="merge",
    )(x2, ya, yb, mg, wa_pad, wb, wo, fgain)


def _ffn_kernel(x_ref, h_ref, wu_ref, wd_ref, o_ref, *, fc):
    h = h_ref[...]
    acc = x_ref[...]
    for c in range(D_FF // fc):
        up = jnp.maximum(_nn(h, wu_ref[:, c * fc:(c + 1) * fc]), 0.0)
        acc = acc + _nn((up * up).astype(BF16), wd_ref[c * fc:(c + 1) * fc, :])
    o_ref[...] = acc


def _ffn_call(x2, h, wu, wd, tm=512, fc=1024):
    n = x2.shape[0]
    row = lambda w: pl.BlockSpec((tm, w), lambda i: (i, 0))
    full = lambda a: pl.BlockSpec(a.shape, lambda i: (0,) * a.ndim)
    return pl.pallas_call(
        functools.partial(_ffn_kernel, fc=fc),
        grid=(n // tm,),
        in_specs=[row(D_MODEL), row(D_MODEL), full(wu), full(wd)],
        out_specs=row(D_MODEL),
        out_shape=jax.ShapeDtypeStruct((n, D_MODEL), F32),
        compiler_params=_params("parallel"),
        name="ffn",
    )(x2, h, wu, wd)


def _pad_heads_cols(w):
    rows = w.shape[0]
    wh = w.reshape(rows, ATTN_HEADS, HEAD_DIM)
    slots = []
    for j in range(ATTN_HEADS):
        g = j // HEADS_PER_GROUP
        z = jnp.zeros((rows, HEAD_DIM), w.dtype)
        slots.append(jnp.concatenate([wh[:, j], z] if g == 0 else [z, wh[:, j]], axis=1))
    return jnp.concatenate(slots, axis=1)


def _proj_weight(w_in):
    q = _pad_heads_cols(w_in[:, 0:ATTN_DIM])
    kv = w_in[:, ATTN_DIM:ATTN_DIM + 6 * KV_DIM]
    gates = jnp.pad(w_in[:, ATTN_DIM + 6 * KV_DIM:NSA_DIM], ((0, 0), (0, LANES - NSA_GATES)))
    rest = w_in[:, NSA_DIM:]
    return jnp.concatenate([q, kv, gates, rest], axis=1).astype(BF16)


def _values_t(v, tile):
    b, s, _ = v.shape
    vt = v.reshape(b, s // tile, tile, KV_DIM).transpose(0, 1, 3, 2)
    row = jnp.arange(KV_DIM)[:, None] // HEAD_DIM
    one = jnp.ones((), v.dtype)
    return jnp.stack([jnp.where(row == g, vt, one) for g in range(KV_GROUPS)], axis=1)


def _to_groups16(t, b, s):
    t = t.reshape(b, s // CMP_STRIDE, CMP_STRIDE, KV_GROUPS, HEAD_DIM)
    t = t.transpose(0, 3, 1, 2, 4)
    return t.reshape(b, KV_GROUPS, s // CMP_STRIDE, CMP_STRIDE * HEAD_DIM)


def kernel(x, mix_norm, w_in, q_gain, k_gain, cmp_pos, cmp_w1, cmp_w2, w_attn_branch, tok_mix, w0,
           w_lora_up, a0, a_lora_up, g_lora_up, k_k, k_a, r_k, ln_x_w, ln_x_b, w_rwkv_branch, w_out,
           ffn_norm, w_ffn_up, w_ffn_down):
    b, s, d = x.shape
    n = b * s
    depth = w_in.shape[0]
    row = lambda v: v.reshape(1, -1)
    x2 = x.reshape(n, d)
    for l in range(depth):
        qg_pad = _pad_heads_cols(jnp.tile(q_gain[l], ATTN_HEADS).reshape(1, ATTN_DIM))
        kg2 = jnp.tile(k_gain[l, 1:3], (1, KV_GROUPS))
        q, kc, vc, ks, vs, kw, vw, gates, rw, mg = _proj_call(
            x2, row(mix_norm[l]), _proj_weight(w_in[l]), qg_pad, kg2)

        xg = jnp.stack([_to_groups16(kc, b, s), _to_groups16(vc, b, s)]).astype(BF16)
        cmp = _compress_call(xg, cmp_w1[l].astype(BF16), cmp_w2[l].astype(BF16),
                             cmp_pos[l].reshape(2, 1, CMP_BLOCK * HEAD_DIM), row(k_gain[l, 0]))
        cmp = cmp.transpose(0, 1, 3, 2, 4).reshape(2, b, s // CMP_STRIDE, KV_DIM).astype(BF16)

        seq = lambda t: t.reshape(b, s, t.shape[-1])
        ya = _nsa_call(seq(q), seq(gates), cmp[0], cmp[1].transpose(0, 2, 1), seq(ks),
                       _values_t(seq(vs), NSA_KEY_TILE), seq(kw), _values_t(seq(vw), NSA_QUERY_TILE))

        zero = jnp.zeros((W_LORA, RWKV_DIM), F32)
        wup = jnp.concatenate([w_lora_up[l], zero], axis=0).astype(BF16)
        aup = jnp.concatenate([zero, a_lora_up[l]], axis=0).astype(BF16)
        yb = _rwkv_call(seq(rw), row(tok_mix[l]), row(w0[l]), wup, row(a0[l]), aup,
                        g_lora_up[l].astype(BF16), row(k_k[l]), row(k_a[l]), row(r_k[l]),
                        row(ln_x_w[l]), row(ln_x_b[l]))

        wa_pad = _pad_heads_cols(w_attn_branch[l].T).T.astype(BF16)
        x2, h = _merge_call(x2, ya.reshape(n, Q_PAD), yb.reshape(n, RWKV_DIM), mg, wa_pad,
                            w_rwkv_branch[l].astype(BF16), w_out[l].astype(BF16), row(ffn_norm[l]))
        x2 = _ffn_call(x2, h, w_ffn_up[l].astype(BF16), w_ffn_down[l].astype(BF16))
    return x2.reshape(b, s, d)
```

```python
import functools

import jax
import jax.numpy as jnp
import numpy as np
from jax import lax
from jax.experimental import pallas as pl
from jax.experimental.pallas import tpu as pltpu

F32 = jnp.float32
BF16 = jnp.bfloat16
I32 = jnp.int32

D_MODEL = 1024
ATTN_HEADS = 8
HEAD_DIM = 64
KV_GROUPS = 2
HEADS_PER_GROUP = ATTN_HEADS // KV_GROUPS
ATTN_DIM = ATTN_HEADS * HEAD_DIM
KV_DIM = KV_GROUPS * HEAD_DIM
N_NSA_BRANCHES = 3
CMP_BLOCK = 32
CMP_STRIDE = 16
CMP_HIDDEN = 2 * HEAD_DIM
SLC_BLOCK = 64
N_SELECT = 16
WINDOW = 512
FORCE_BONUS = 1000.0
RWKV_HEADS = 8
RWKV_HEAD_DIM = 64
RWKV_DIM = RWKV_HEADS * RWKV_HEAD_DIM
W_LORA = 64
A_LORA = 64
G_LORA = 128
RWKV_MIX_DIM = 3 * RWKV_DIM + W_LORA + A_LORA + G_LORA
D_FF = 4 * D_MODEL
NSA_GATES = N_NSA_BRANCHES * ATTN_HEADS
NSA_DIM = ATTN_DIM + 6 * KV_DIM + NSA_GATES
RMS_EPS = 1e-6
GN_EPS = 64e-5
NEG_INF = -1e30

LOG2_64 = 6
LANES = 128
SUBLANES = 8
MXU_COLS = 256
Q_SCALE = HEAD_DIM ** -0.5 * 1.4426950408889634
Q_PAD = ATTN_HEADS * LANES
RWKV_PAIRS = RWKV_HEADS // 2
RWKV_CHUNK = 64
ROW_TILE = 512
NSA_QUERY_TILE = 128
NSA_KEY_TILE = 512
VMEM_LIMIT = 48 * 1024 * 1024

C_Q = 0
C_KV = C_Q + Q_PAD
C_GATE = C_KV + 6 * KV_DIM
C_RWKV = C_GATE + LANES
C_MERGE = C_RWKV + RWKV_MIX_DIM
C_TOTAL = C_MERGE + 2 * D_MODEL


def _nn(a, b):
    return lax.dot_general(a, b, (((1,), (0,)), ((), ())), preferred_element_type=F32)


def _nt(a, b):
    return lax.dot_general(a, b, (((1,), (1,)), ((), ())), preferred_element_type=F32)


def _bnn(a, b):
    return lax.dot_general(a, b, (((2,), (1,)), ((0,), (0,))), preferred_element_type=F32)


def _bnt(a, b):
    return lax.dot_general(a, b, (((2,), (2,)), ((0,), (0,))), preferred_element_type=F32)


def _btn(a, b):
    return lax.dot_general(a, b, (((1,), (1,)), ((0,), (0,))), preferred_element_type=F32)


def _split2(x):
    hi = x.astype(BF16)
    lo = (x - hi.astype(F32)).astype(BF16)
    return hi, lo


def _iota(shape, dim):
    return lax.broadcasted_iota(I32, shape, dim)


def _params(*sem):
    return pltpu.CompilerParams(dimension_semantics=sem, vmem_limit_bytes=VMEM_LIMIT)


def _group_rms(x, gain):
    lane = _iota((1, LANES), 1)
    sq = x * x
    lo = lane < HEAD_DIM
    s0 = jnp.sum(jnp.where(lo, sq, 0.0), axis=-1, keepdims=True)
    s1 = jnp.sum(jnp.where(lo, 0.0, sq), axis=-1, keepdims=True)
    ms = jnp.where(lo, s0, s1) * (1.0 / HEAD_DIM)
    return x * lax.rsqrt(ms + RMS_EPS) * gain


def _proj_kernel(x_ref, g_ref, w_ref, qg_ref, kg_ref,
                 q_ref, kc_ref, vc_ref, ks_ref, vs_ref, kw_ref, vw_ref, gt_ref, rw_ref, mg_ref):
    x = x_ref[...]
    ms = jnp.mean(x * x, axis=-1, keepdims=True)
    u = (x * lax.rsqrt(ms + RMS_EPS) * g_ref[...]).astype(BF16)

    def col(c0, width):
        return _nn(u, w_ref[:, c0:c0 + width])

    for jp in range(ATTN_HEADS // 2):
        qq = col(C_Q + jp * MXU_COLS, MXU_COLS)
        for half in range(2):
            j = 2 * jp + half
            qj = qq[:, half * LANES:(half + 1) * LANES]
            msq = jnp.sum(qj * qj, axis=-1, keepdims=True) * (1.0 / HEAD_DIM)
            qn = qj * lax.rsqrt(msq + RMS_EPS) * qg_ref[:, j * LANES:(j + 1) * LANES]
            q_ref[:, j * LANES:(j + 1) * LANES] = (qn * Q_SCALE).astype(BF16)
    kvc = col(C_KV, MXU_COLS)
    kc_ref[...] = kvc[:, 0:KV_DIM]
    vc_ref[...] = kvc[:, KV_DIM:2 * KV_DIM]
    kvs = col(C_KV + MXU_COLS, MXU_COLS)
    ks_ref[...] = _group_rms(kvs[:, 0:KV_DIM], kg_ref[0:1, :]).astype(BF16)
    vs_ref[...] = kvs[:, KV_DIM:2 * KV_DIM].astype(BF16)
    kvw = col(C_KV + 2 * MXU_COLS, MXU_COLS)
    kw_ref[...] = _group_rms(kvw[:, 0:KV_DIM], kg_ref[1:2, :]).astype(BF16)
    vw_ref[...] = kvw[:, KV_DIM:2 * KV_DIM].astype(BF16)
    gt_ref[...] = col(C_GATE, LANES)
    for c in range(RWKV_MIX_DIM // MXU_COLS):
        rw_ref[:, c * MXU_COLS:(c + 1) * MXU_COLS] = col(C_RWKV + c * MXU_COLS, MXU_COLS)
    for c in range(2 * D_MODEL // 512):
        mg_ref[:, c * 512:(c + 1) * 512] = col(C_MERGE + c * 512, 512).astype(BF16)


def _row_spec(width):
    return pl.BlockSpec((None, ROW_TILE, width), lambda i, j: (i, j, 0))


def _const_spec(a, buffers=2):
    return pl.BlockSpec(a.shape, lambda i, j: (0,) * a.ndim, pipeline_mode=pl.Buffered(buffers))


def _proj_call(x, gain, w_all, qg_pad, kg2):
    b, s, _ = x.shape
    widths = [(Q_PAD, BF16), (KV_DIM, F32), (KV_DIM, F32), (KV_DIM, BF16), (KV_DIM, BF16),
              (KV_DIM, BF16), (KV_DIM, BF16), (LANES, F32), (RWKV_MIX_DIM, F32), (2 * D_MODEL, BF16)]
    return pl.pallas_call(
        _proj_kernel,
        grid=(b, s // ROW_TILE),
        in_specs=[_row_spec(D_MODEL), _const_spec(gain), _const_spec(w_all, buffers=1),
                  _const_spec(qg_pad), _const_spec(kg2)],
        out_specs=[_row_spec(w) for w, _ in widths],
        out_shape=[jax.ShapeDtypeStruct((b, s, w), dt) for w, dt in widths],
        compiler_params=_params("parallel", "parallel"),
        name="proj",
    )(x, gain, w_all, qg_pad, kg2)


def _compress_kernel(x_ref, w1_ref, w2_ref, pos_ref, kg_ref, o_ref, *, nc):
    which = pl.program_id(0)
    half = CMP_STRIDE * HEAD_DIM
    x = x_ref[0, 0, 0]
    ha = _nn(x, w1_ref[0, 0:half, :])
    hb = _nn(x, w1_ref[0, half:2 * half, :])
    pos8 = jnp.broadcast_to(pos_ref[0], (8, 2 * half)).astype(BF16)
    bias = _nn(pos8, w1_ref[0])[0:1, :]
    hid = ha + pltpu.roll(hb, nc - 1, axis=0) + bias
    act = jax.nn.gelu(hid)
    out = _nn(act.astype(BF16), w2_ref[0])
    ms = jnp.mean(out * out, axis=-1, keepdims=True)
    normed = out * lax.rsqrt(ms + RMS_EPS) * kg_ref[...]
    out = jnp.where(which == 0, normed, out)
    rowi = _iota((nc, 1), 0)
    o_ref[0, 0, 0] = jnp.where(rowi < nc - 1, out, 0.0)


def _compress_call(xg, w1, w2, pos_flat, kgain):
    _, b, g, nc, width = xg.shape
    return pl.pallas_call(
        functools.partial(_compress_kernel, nc=nc),
        grid=(2, b, g),
        in_specs=[
            pl.BlockSpec((1, 1, 1, nc, width), lambda w, i, j: (w, i, j, 0, 0)),
            pl.BlockSpec((1, width * 2, CMP_HIDDEN), lambda w, i, j: (w, 0, 0)),
            pl.BlockSpec((1, CMP_HIDDEN, HEAD_DIM), lambda w, i, j: (w, 0, 0)),
            pl.BlockSpec((1, 1, width * 2), lambda w, i, j: (w, 0, 0)),
            pl.BlockSpec((1, HEAD_DIM), lambda w, i, j: (0, 0)),
        ],
        out_specs=pl.BlockSpec((1, 1, 1, nc, HEAD_DIM), lambda w, i, j: (w, i, j, 0, 0)),
        out_shape=jax.ShapeDtypeStruct((2, b, g, nc, HEAD_DIM), F32),
        compiler_params=_params("parallel", "parallel", "parallel"),
        name="compress",
    )(xg, w1, w2, pos_flat, kgain)


def _nsa_kernel(q_ref, gt_ref, kc_ref, vct_ref, ks_ref, vst_ref, kw_ref, vwt_ref, y_ref,
                m_s, acc_s, sb_s, *, tq, tk, nb, nc):
    nh = ATTN_HEADS
    hp = HEADS_PER_GROUP
    cols = hp * tq
    wk = WINDOW + tq
    bpt = tk // SLC_BLOCK
    qi = pl.program_id(1)
    q0 = qi * tq
    tlane = q0 + _iota((1, tq), 1)
    tile_h = lambda a, reps: jnp.concatenate([a] * reps, axis=1)
    eye_d = jnp.where(_iota((LANES, LANES), 0) == _iota((LANES, LANES), 1), 1.0, 0.0).astype(BF16)
    eye_q = jnp.where(_iota((tq, tq), 0) == _iota((tq, tq), 1), 1.0, 0.0).astype(BF16)
    drow = _iota((LANES, 1), 0)
    own = [(drow >= g * HEAD_DIM) & (drow < (g + 1) * HEAD_DIM) for g in range(KV_GROUPS)]

    qall = jnp.concatenate([q_ref[0, :, j * LANES:(j + 1) * LANES] for j in range(nh)], axis=0)
    q_t = _nt(eye_d, qall).astype(BF16)

    ncol = _iota((nc, 1), 0)
    cvalid = (ncol * CMP_STRIDE + (CMP_BLOCK - 1) <= tlane) & (ncol < nc - 1)
    s_c = _nn(kc_ref[0], q_t) + tile_h(jnp.where(cvalid, 0.0, NEG_INF), nh)
    e_c = jnp.exp2(s_c - jnp.max(s_c, axis=0, keepdims=True))
    p_c = e_c * (1.0 / jnp.sum(e_c, axis=0, keepdims=True)) * tile_h(jnp.where(cvalid, 1.0, 0.0), nh)
    p_cb = p_c.astype(BF16)
    o_c = [_nn(vct_ref[0], p_cb[:, g * cols:(g + 1) * cols]) for g in range(KV_GROUPS)]

    jcol = _iota((nb, 1), 0)
    ncmp = _iota((nb, nc), 1)
    ov = ((ncmp * CMP_STRIDE <= jcol * SLC_BLOCK + (SLC_BLOCK - 1))
          & (ncmp * CMP_STRIDE + (CMP_BLOCK - 1) >= jcol * SLC_BLOCK))
    ov = jnp.where(ov, 1.0, 0.0).astype(BF16)
    cur = jnp.right_shift(tlane, LOG2_64)
    forced = (jcol == 0) | (jcol == cur) | (jcol == cur - 1)
    causal_blk = jcol * SLC_BLOCK <= tlane
    for g in range(KV_GROUPS):
        c0 = g * cols
        psum = (p_c[:, c0:c0 + tq] + p_c[:, c0 + tq:c0 + 2 * tq]
                + p_c[:, c0 + 2 * tq:c0 + 3 * tq] + p_c[:, c0 + 3 * tq:c0 + 4 * tq])
        hi, lo = _split2(psum)
        imp = _nn(ov, hi) + _nn(ov, lo)
        imp = jnp.where(causal_blk, imp + jnp.where(forced, FORCE_BONUS, 0.0), -1.0)
        groups = [imp[r:r + SUBLANES, :] for r in range(0, nb, SUBLANES)]
        ranks = [jnp.zeros((SUBLANES, tq), F32) for _ in groups]
        jsub = _iota((SUBLANES, 1), 0)
        for i in range(nb):
            ri = imp[i:i + 1, :]
            for gi, grp in enumerate(groups):
                r0 = gi * SUBLANES
                if r0 + SUBLANES - 1 < i:
                    ahead = jnp.where(ri > grp, 1.0, 0.0)
                elif r0 > i:
                    ahead = jnp.where(ri >= grp, 1.0, 0.0)
                else:
                    ahead = jnp.where(jsub > i - r0, jnp.where(ri >= grp, 1.0, 0.0),
                                      jnp.where(ri > grp, 1.0, 0.0))
                ranks[gi] = ranks[gi] + ahead
        rank = jnp.concatenate(ranks, axis=0)
        sb_s[g] = jnp.where(rank < float(min(N_SELECT, nb)), 0.0, NEG_INF)

    m_s[...] = jnp.full(m_s.shape, NEG_INF, F32)
    acc_s[...] = jnp.zeros(acc_s.shape, F32)

    def body(kt, _):
        k0 = pl.multiple_of(kt * tk, tk)
        s = _nn(ks_ref[0, pl.ds(k0, tk), :], q_t)
        causal = (k0 + _iota((tk, 1), 0)) <= tlane
        for g in range(KV_GROUPS):
            cs = slice(g * cols, (g + 1) * cols)
            bias = jnp.concatenate(
                [jnp.broadcast_to(sb_s[g, pl.ds(kt * bpt + i, 1), :], (SLC_BLOCK, tq)) for i in range(bpt)],
                axis=0)
            sm = s[:, cs] + tile_h(jnp.where(causal, bias, NEG_INF), hp)
            m_prev = m_s[0:1, cs]
            m_new = jnp.maximum(m_prev, jnp.max(sm, axis=0, keepdims=True))
            m_s[:, cs] = jnp.broadcast_to(m_new, (8, cols))
            p = jnp.exp2(sm - m_new).astype(BF16)
            acc_s[:, cs] = jnp.exp2(m_prev - m_new) * acc_s[:, cs] + _nn(vst_ref[0, g, kt], p)
        return 0

    lax.fori_loop(0, qi // (tk // tq) + 1, body, 0)

    w0 = pl.multiple_of(jnp.maximum(q0 - WINDOW, 0), tq)
    dist = tlane - (w0 + _iota((wk, 1), 0))
    s_w = (_nn(kw_ref[0, pl.ds(w0, wk), :], q_t)
           + tile_h(jnp.where((dist >= 0) & (dist < WINDOW), 0.0, NEG_INF), nh))
    p_w = jnp.exp2(s_w - jnp.max(s_w, axis=0, keepdims=True)).astype(BF16)
    wb = w0 // tq

    g_hi, g_lo = _split2(jax.nn.sigmoid(gt_ref[0]))
    gate_t = _nt(eye_d, g_hi) + _nt(eye_d, g_lo)
    for g in range(KV_GROUPS):
        cs = slice(g * cols, (g + 1) * cols)
        den_row = (1 - g) * HEAD_DIM
        acc = acc_s[:, cs]
        o_s = acc * (1.0 / acc[den_row:den_row + 1, :])
        v_w = jnp.concatenate([vwt_ref[0, g, wb + i] for i in range(wk // tq)], axis=1)
        acc_w = _nn(v_w, p_w[:, cs])
        o_w = acc_w * (1.0 / acc_w[den_row:den_row + 1, :])
        for h in range(hp):
            j = g * hp + h
            r0 = j * N_NSA_BRANCHES
            hs = slice(h * tq, (h + 1) * tq)
            o = (gate_t[r0:r0 + 1, :] * o_c[g][:, hs] + gate_t[r0 + 1:r0 + 2, :] * o_s[:, hs]
                 + gate_t[r0 + 2:r0 + 3, :] * o_w[:, hs])
            o = jnp.where(own[g], o, 0.0).astype(BF16)
            y_ref[0, :, j * LANES:(j + 1) * LANES] = _nt(eye_q, o).astype(BF16)


def _nsa_call(q, gates, kc, vct, ks, vst, kw, vwt):
    b, s, _ = q.shape
    tq = NSA_QUERY_TILE
    nb = s // SLC_BLOCK
    nc = kc.shape[1]
    tk = vst.shape[-1]
    tile = lambda w: pl.BlockSpec((1, tq, w), lambda i, j: (i, j, 0))
    seq = lambda a: pl.BlockSpec((1,) + a.shape[1:], lambda i, j: (i,) + (0,) * (a.ndim - 1))
    return pl.pallas_call(
        functools.partial(_nsa_kernel, tq=tq, tk=tk, nb=nb, nc=nc),
        grid=(b, s // tq),
        in_specs=[tile(Q_PAD), tile(LANES), seq(kc), seq(vct), seq(ks), seq(vst), seq(kw), seq(vwt)],
        out_specs=tile(Q_PAD),
        out_shape=jax.ShapeDtypeStruct((b, s, Q_PAD), BF16),
        scratch_shapes=[pltpu.VMEM((8, ATTN_HEADS * tq), F32),
                        pltpu.VMEM((LANES, ATTN_HEADS * tq), F32),
                        pltpu.VMEM((KV_GROUPS, nb, tq), F32)],
        compiler_params=_params("parallel", "arbitrary"),
        name="nsa",
    )(q, gates, kc, vct, ks, vst, kw, vwt)


def _seg_sum(x, bd):
    hi, lo = _split2(x)
    w = bd.shape[0]
    return jnp.concatenate([_nn(hi[:, c:c + w], bd) + _nn(lo[:, c:c + w], bd)
                            for c in range(0, x.shape[1], w)], axis=1)


def _stack_heads(x):
    lane = _iota((1, LANES), 1)
    lo = lane < RWKV_HEAD_DIM
    return jnp.concatenate([jnp.where(lo, x, 0.0), jnp.where(lo, 0.0, x)], axis=0)


def _rwkv_kernel(p_ref, mu_ref, w0_ref, wup_ref, a0_ref, aup_ref, gup_ref, kk_ref, ka_ref, rk_ref,
                 lnw_ref, lnb_ref, o_ref, carry_ref, state_ref, *, tt):
    ch = RWKV_CHUNK
    c2 = 2 * ch
    d = RWKV_DIM
    nch = tt // ch
    npr = RWKV_PAIRS

    @pl.when(pl.program_id(1) == 0)
    def _():
        carry_ref[...] = jnp.zeros_like(carry_ref)
        state_ref[...] = jnp.zeros_like(state_ref)

    p = p_ref[0]
    prev = pltpu.roll(p, 1, axis=0)
    prev = jnp.where(_iota((tt, 1), 0) == 0, carry_ref[0:1, :], prev)
    carry_ref[0:1, :] = p[tt - 1:tt, :]
    pm = p + (prev - p) * mu_ref[...]

    r = pm[:, 0:d]
    k = pm[:, d:2 * d]
    v = pm[:, 2 * d:3 * d]
    wa = pm[:, 3 * d:3 * d + LANES]
    gl = pm[:, 3 * d + LANES:3 * d + 2 * LANES]

    z = w0_ref[...] + _nn(jnp.tanh(wa).astype(BF16), wup_ref[...])
    softplus = jnp.maximum(-z, 0.0) + jnp.log(1.0 + jnp.exp(-jnp.abs(z)))
    lw = -jnp.exp(-softplus - 0.5)
    a = jax.nn.sigmoid(a0_ref[...] + _nn(wa.astype(BF16), aup_ref[...]))
    gate = _nn(jax.nn.sigmoid(gl).astype(BF16), gup_ref[...])

    seg = (jnp.right_shift(_iota((MXU_COLS, MXU_COLS), 0), LOG2_64)
           == jnp.right_shift(_iota((MXU_COLS, MXU_COLS), 1), LOG2_64))
    bd = jnp.where(seg, 1.0, 0.0).astype(BF16)
    kk = k * kk_ref[...]
    kk = kk * lax.rsqrt(jnp.maximum(_seg_sum(kk * kk, bd), 1e-24))
    k2 = k * (1.0 + (a - 1.0) * ka_ref[...])

    ti = _iota((tt, tt), 0)
    tj = _iota((tt, tt), 1)
    same_chunk = jnp.right_shift(ti, LOG2_64) == jnp.right_shift(tj, LOG2_64)
    tri = jnp.where(same_chunk & (tj <= ti), 1.0, 0.0).astype(BF16)
    hi, lo = _split2(lw)
    cum = _nn(tri, hi) + _nn(tri, lo)
    tot = jnp.concatenate([jnp.broadcast_to(cum[(c + 1) * ch - 1:(c + 1) * ch, :], (ch, d))
                           for c in range(nch)], axis=0)
    e_inv = jnp.exp(-cum)
    e_end = jnp.exp(tot - cum)
    dec = jnp.exp(tot)
    bb = kk * a

    def tiles(x):
        return jnp.stack([_stack_heads(x[c * ch:(c + 1) * ch, pr * LANES:(pr + 1) * LANES])
                          for c in range(nch) for pr in range(npr)]).astype(BF16)

    a_t = tiles(-kk * jnp.exp(cum - lw))
    r_t = tiles(r * jnp.exp(cum))
    b_h = tiles(bb * e_inv)
    k_h = tiles(k2 * e_inv)
    bke = jnp.concatenate([tiles(bb * e_end), tiles(k2 * e_end)], axis=1)
    v_t = tiles(v)

    ri = _iota((c2, c2), 0)
    ci = _iota((c2, c2), 1)
    same = jnp.right_shift(ri, LOG2_64) == jnp.right_shift(ci, LOG2_64)
    strict = jnp.where(same & (ci < ri), 1.0, 0.0)
    incl = jnp.where(same & (ci <= ri), 1.0, 0.0)
    ident = jnp.where(ri == ci, 1.0, 0.0)

    scores = _bnt(jnp.concatenate([a_t, r_t], axis=1), jnp.concatenate([b_h, k_h], axis=1))
    m_ab = scores[:, 0:c2, 0:c2] * strict
    m_ak = (scores[:, 0:c2, c2:2 * c2] * strict).astype(BF16)
    m_rbk = jnp.concatenate([scores[:, c2:2 * c2, 0:c2] * incl,
                             scores[:, c2:2 * c2, c2:2 * c2] * incl], axis=2).astype(BF16)
    tinv = ident + m_ab
    mp = m_ab
    for _ in range(5):
        mpb = mp.astype(BF16)
        mp = _bnn(mpb, mpb)
        tinv = tinv + _bnn(tinv.astype(BF16), mp.astype(BF16))
    rhs = jnp.concatenate([a_t, _bnn(m_ak, v_t).astype(BF16)], axis=2)
    wu = _bnn(tinv.astype(BF16), rhs)
    w_t = wu[:, :, 0:c2].astype(BF16)
    u0 = wu[:, :, c2:2 * c2]

    g_st = state_ref[...]
    y_rows = []
    for c in range(nch):
        sl = slice(c * npr, (c + 1) * npr)
        g_b = g_st.astype(BF16)
        u = _bnt(w_t[sl], g_b) + u0[sl]
        uv = jnp.concatenate([u.astype(BF16), v_t[sl]], axis=1)
        y = _bnt(r_t[sl], g_b) + _bnn(m_rbk[sl], uv)
        y = y[:, 0:ch, :] + y[:, ch:c2, :]
        y_rows.append(jnp.concatenate([y[pr] for pr in range(npr)], axis=1))
        dec_c = jnp.stack([dec[c * ch:c * ch + 1, pr * LANES:(pr + 1) * LANES] for pr in range(npr)])
        g_st = g_st * dec_c + _btn(uv, bke[sl])
    state_ref[...] = g_st
    y = jnp.concatenate(y_rows, axis=0)

    inv = 1.0 / RWKV_HEAD_DIM
    mean = _seg_sum(y, bd) * inv
    yc = y - mean
    var = _seg_sum(yc * yc, bd) * inv
    yn = yc * lax.rsqrt(var + GN_EPS) * lnw_ref[...] + lnb_ref[...]
    bonus = _seg_sum(r * k2 * rk_ref[...], bd) * v
    o_ref[0] = ((yn + bonus) * gate).astype(BF16)


def _rwkv_call(p, mu, w0, wup, a0, aup, gup, k_k, k_a, r_k, ln_w, ln_b, tt=256):
    b, s, _ = p.shape
    full = lambda a: pl.BlockSpec(a.shape, lambda i, j: (0,) * a.ndim)
    consts = (mu, w0, wup, a0, aup, gup, k_k, k_a, r_k, ln_w, ln_b)
    return pl.pallas_call(
        functools.partial(_rwkv_kernel, tt=tt),
        grid=(b, s // tt),
        in_specs=[pl.BlockSpec((1, tt, RWKV_MIX_DIM), lambda i, j: (i, j, 0))]
                 + [full(c) for c in consts],
        out_specs=pl.BlockSpec((1, tt, RWKV_DIM), lambda i, j: (i, j, 0)),
        out_shape=jax.ShapeDtypeStruct((b, s, RWKV_DIM), BF16),
        scratch_shapes=[pltpu.VMEM((8, RWKV_MIX_DIM), F32),
                        pltpu.VMEM((RWKV_PAIRS, LANES, LANES), F32)],
        compiler_params=_params("arbitrary", "arbitrary"),
        name="rwkv",
    )(p, *consts)


def _merge_kernel(x_ref, ya_ref, yb_ref, mg_ref, wa_ref, wb_ref, wo_ref, fg_ref, xo_ref, h_ref):
    ga = jax.nn.sigmoid(mg_ref[:, 0:D_MODEL].astype(F32))
    gb = jax.nn.sigmoid(mg_ref[:, D_MODEL:2 * D_MODEL].astype(F32))
    merged = ga * _nn(ya_ref[...], wa_ref[...]) + gb * _nn(yb_ref[...], wb_ref[...])
    xn = x_ref[...] + _nn(merged.astype(BF16), wo_ref[...])
    xo_ref[...] = xn
    ms = jnp.mean(xn * xn, axis=-1, keepdims=True)
    h_ref[...] = (xn * lax.rsqrt(ms + RMS_EPS) * fg_ref[...]).astype(BF16)


def _merge_call(x, ya, yb, mg, wa_pad, wb, wo, fgain):
    b, s, _ = x.shape
    return pl.pallas_call(
        _merge_kernel,
        grid=(b, s // ROW_TILE),
        in_specs=[_row_spec(D_MODEL), _row_spec(Q_PAD), _row_spec(RWKV_DIM), _row_spec(2 * D_MODEL),
                  _const_spec(wa_pad), _const_spec(wb), _const_spec(wo), _const_spec(fgain)],
        out_specs=[_row_spec(D_MODEL), _row_spec(D_MODEL)],
        out_shape=[jax.ShapeDtypeStruct((b, s, D_MODEL), F32),
                   jax.ShapeDtypeStruct((b, s, D_MODEL), BF16)],
        compiler_params=_params("parallel", "parallel"),
        name="merge",
    )(x, ya, yb, mg, wa_pad, wb, wo, fgain)


def _ffn_kernel(x_ref, h_ref, wu_ref, wd_ref, o_ref, *, fc):
    h = h_ref[...]
    acc = x_ref[...]
    for c in range(D_FF // fc):
        up = jnp.maximum(_nn(h, wu_ref[:, c * fc:(c + 1) * fc]), 0.0)
        acc = acc + _nn((up * up).astype(BF16), wd_ref[c * fc:(c + 1) * fc, :])
    o_ref[...] = acc


def _ffn_call(x, h, wu, wd, fc=1024):
    b, s, _ = x.shape
    return pl.pallas_call(
        functools.partial(_ffn_kernel, fc=fc),
        grid=(b, s // ROW_TILE),
        in_specs=[_row_spec(D_MODEL), _row_spec(D_MODEL), _const_spec(wu), _const_spec(wd)],
        out_specs=_row_spec(D_MODEL),
        out_shape=jax.ShapeDtypeStruct((b, s, D_MODEL), F32),
        compiler_params=_params("parallel", "parallel"),
        name="ffn",
    )(x, h, wu, wd)


def _pad_heads_cols(w):
    rows = w.shape[0]
    wh = w.reshape(rows, ATTN_HEADS, HEAD_DIM)
    slots = []
    for j in range(ATTN_HEADS):
        g = j // HEADS_PER_GROUP
        z = jnp.zeros((rows, HEAD_DIM), w.dtype)
        slots.append(jnp.concatenate([wh[:, j], z] if g == 0 else [z, wh[:, j]], axis=1))
    return jnp.concatenate(slots, axis=1)


def _proj_weight(w_in):
    q = _pad_heads_cols(w_in[:, 0:ATTN_DIM])
    kv = w_in[:, ATTN_DIM:ATTN_DIM + 6 * KV_DIM]
    gates = jnp.pad(w_in[:, ATTN_DIM + 6 * KV_DIM:NSA_DIM], ((0, 0), (0, LANES - NSA_GATES)))
    rest = w_in[:, NSA_DIM:]
    return jnp.concatenate([q, kv, gates, rest], axis=1).astype(BF16)


def _values_t(v, tile):
    b, s, _ = v.shape
    vt = v.reshape(b, s // tile, tile, KV_DIM).transpose(0, 1, 3, 2)
    row = jnp.arange(KV_DIM)[:, None] // HEAD_DIM
    one = jnp.ones((), v.dtype)
    return jnp.stack([jnp.where(row == g, vt, one) for g in range(KV_GROUPS)], axis=1)


def _to_groups16(t):
    b, s, _ = t.shape
    t = t.reshape(b, s // CMP_STRIDE, CMP_STRIDE, KV_GROUPS, HEAD_DIM)
    t = t.transpose(0, 3, 1, 2, 4)
    return t.reshape(b, KV_GROUPS, s // CMP_STRIDE, CMP_STRIDE * HEAD_DIM)


def kernel(x, mix_norm, w_in, q_gain, k_gain, cmp_pos, cmp_w1, cmp_w2, w_attn_branch, tok_mix, w0,
           w_lora_up, a0, a_lora_up, g_lora_up, k_k, k_a, r_k, ln_x_w, ln_x_b, w_rwkv_branch, w_out,
           ffn_norm, w_ffn_up, w_ffn_down):
    b, s, _ = x.shape
    depth = w_in.shape[0]
    row = lambda v: v.reshape(1, -1)
    for l in range(depth):
        qg_pad = _pad_heads_cols(jnp.tile(q_gain[l], ATTN_HEADS).reshape(1, ATTN_DIM))
        kg2 = jnp.tile(k_gain[l, 1:3], (1, KV_GROUPS))
        q, kc, vc, ks, vs, kw, vw, gates, rw, mg = _proj_call(
            x, row(mix_norm[l]), _proj_weight(w_in[l]), qg_pad, kg2)

        xg = jnp.stack([_to_groups16(kc), _to_groups16(vc)]).astype(BF16)
        cmp = _compress_call(xg, cmp_w1[l].astype(BF16), cmp_w2[l].astype(BF16),
                             cmp_pos[l].reshape(2, 1, CMP_BLOCK * HEAD_DIM), row(k_gain[l, 0]))
        cmp = cmp.transpose(0, 1, 3, 2, 4).reshape(2, b, s // CMP_STRIDE, KV_DIM).astype(BF16)

        ya = _nsa_call(q, gates, cmp[0], cmp[1].transpose(0, 2, 1), ks,
                       _values_t(vs, NSA_KEY_TILE), kw, _values_t(vw, NSA_QUERY_TILE))

        zero = jnp.zeros((W_LORA, RWKV_DIM), F32)
        wup = jnp.concatenate([w_lora_up[l], zero], axis=0).astype(BF16)
        aup = jnp.concatenate([zero, a_lora_up[l]], axis=0).astype(BF16)
        yb = _rwkv_call(rw, row(tok_mix[l]), row(w0[l]), wup, row(a0[l]), aup,
                        g_lora_up[l].astype(BF16), row(k_k[l]), row(k_a[l]), row(r_k[l]),
                        row(ln_x_w[l]), row(ln_x_b[l]))

        wa_pad = _pad_heads_cols(w_attn_branch[l].T).T.astype(BF16)
        x, h = _merge_call(x, ya, yb, mg, wa_pad, w_rwkv_branch[l].astype(BF16),
                           w_out[l].astype(BF16), row(ffn_norm[l]))
        x = _ffn_call(x, h, w_ffn_up[l].astype(BF16), w_ffn_down[l].astype(BF16))
    return x
```

```python
import functools

import jax
import jax.numpy as jnp
import numpy as np
from jax import lax
from jax.experimental import pallas as pl
from jax.experimental.pallas import tpu as pltpu

F32 = jnp.float32
BF16 = jnp.bfloat16
I32 = jnp.int32

D_MODEL = 1024
ATTN_HEADS = 8
HEAD_DIM = 64
KV_GROUPS = 2
HEADS_PER_GROUP = ATTN_HEADS // KV_GROUPS
ATTN_DIM = ATTN_HEADS * HEAD_DIM
KV_DIM = KV_GROUPS * HEAD_DIM
N_NSA_BRANCHES = 3
CMP_BLOCK = 32
CMP_STRIDE = 16
CMP_HIDDEN = 2 * HEAD_DIM
SLC_BLOCK = 64
N_SELECT = 16
WINDOW = 512
FORCE_BONUS = 1000.0
RWKV_HEADS = 8
RWKV_HEAD_DIM = 64
RWKV_DIM = RWKV_HEADS * RWKV_HEAD_DIM
W_LORA = 64
A_LORA = 64
G_LORA = 128
RWKV_MIX_DIM = 3 * RWKV_DIM + W_LORA + A_LORA + G_LORA
D_FF = 4 * D_MODEL
NSA_GATES = N_NSA_BRANCHES * ATTN_HEADS
NSA_DIM = ATTN_DIM + 6 * KV_DIM + NSA_GATES
RMS_EPS = 1e-6
GN_EPS = 64e-5
NEG_INF = -1e30

LOG2_64 = 6
LANES = 128
SUBLANES = 8
MXU_COLS = 256
Q_SCALE = HEAD_DIM ** -0.5 * 1.4426950408889634
Q_PAD = ATTN_HEADS * LANES
RWKV_PAIRS = RWKV_HEADS // 2
RWKV_CHUNK = 64
ROW_TILE = 512
NSA_QUERY_TILE = 128
NSA_KEY_TILE = 512
VMEM_LIMIT = 48 * 1024 * 1024

C_Q = 0
C_KV = C_Q + Q_PAD
C_GATE = C_KV + 6 * KV_DIM
C_RWKV = C_GATE + LANES
C_MERGE = C_RWKV + RWKV_MIX_DIM
C_TOTAL = C_MERGE + 2 * D_MODEL


def _nn(a, b):
    return lax.dot_general(a, b, (((1,), (0,)), ((), ())), preferred_element_type=F32)


def _nt(a, b):
    return lax.dot_general(a, b, (((1,), (1,)), ((), ())), preferred_element_type=F32)


def _bnn(a, b):
    return lax.dot_general(a, b, (((2,), (1,)), ((0,), (0,))), preferred_element_type=F32)


def _bnt(a, b):
    return lax.dot_general(a, b, (((2,), (2,)), ((0,), (0,))), preferred_element_type=F32)


def _btn(a, b):
    return lax.dot_general(a, b, (((1,), (1,)), ((0,), (0,))), preferred_element_type=F32)


def _split2(x):
    hi = x.astype(BF16)
    lo = (x - hi.astype(F32)).astype(BF16)
    return hi, lo


def _iota(shape, dim):
    return lax.broadcasted_iota(I32, shape, dim)


def _params(*sem):
    return pltpu.CompilerParams(dimension_semantics=sem, vmem_limit_bytes=VMEM_LIMIT)


def _group_rms(x, gain):
    lane = _iota((1, LANES), 1)
    sq = x * x
    lo = lane < HEAD_DIM
    s0 = jnp.sum(jnp.where(lo, sq, 0.0), axis=-1, keepdims=True)
    s1 = jnp.sum(jnp.where(lo, 0.0, sq), axis=-1, keepdims=True)
    ms = jnp.where(lo, s0, s1) * (1.0 / HEAD_DIM)
    return x * lax.rsqrt(ms + RMS_EPS) * gain


def _proj_kernel(x_ref, g_ref, w_ref, qg_ref, kg_ref,
                 q_ref, kc_ref, vc_ref, ks_ref, vs_ref, kw_ref, vw_ref, gt_ref, rw_ref, mg_ref):
    x = x_ref[...]
    ms = jnp.mean(x * x, axis=-1, keepdims=True)
    u = (x * lax.rsqrt(ms + RMS_EPS) * g_ref[...]).astype(BF16)

    def col(c0, width):
        return _nn(u, w_ref[:, c0:c0 + width])

    for jp in range(ATTN_HEADS // 2):
        qq = col(C_Q + jp * MXU_COLS, MXU_COLS)
        for half in range(2):
            j = 2 * jp + half
            qj = qq[:, half * LANES:(half + 1) * LANES]
            msq = jnp.sum(qj * qj, axis=-1, keepdims=True) * (1.0 / HEAD_DIM)
            qn = qj * lax.rsqrt(msq + RMS_EPS) * qg_ref[:, j * LANES:(j + 1) * LANES]
            q_ref[:, j * LANES:(j + 1) * LANES] = (qn * Q_SCALE).astype(BF16)
    kvc = col(C_KV, MXU_COLS)
    kc_ref[...] = kvc[:, 0:KV_DIM]
    vc_ref[...] = kvc[:, KV_DIM:2 * KV_DIM]
    kvs = col(C_KV + MXU_COLS, MXU_COLS)
    ks_ref[...] = _group_rms(kvs[:, 0:KV_DIM], kg_ref[0:1, :]).astype(BF16)
    vs_ref[...] = kvs[:, KV_DIM:2 * KV_DIM].astype(BF16)
    kvw = col(C_KV + 2 * MXU_COLS, MXU_COLS)
    kw_ref[...] = _group_rms(kvw[:, 0:KV_DIM], kg_ref[1:2, :]).astype(BF16)
    vw_ref[...] = kvw[:, KV_DIM:2 * KV_DIM].astype(BF16)
    gt_ref[...] = col(C_GATE, LANES)
    for c in range(RWKV_MIX_DIM // MXU_COLS):
        rw_ref[:, c * MXU_COLS:(c + 1) * MXU_COLS] = col(C_RWKV + c * MXU_COLS, MXU_COLS)
    for c in range(2 * D_MODEL // 512):
        mg_ref[:, c * 512:(c + 1) * 512] = col(C_MERGE + c * 512, 512).astype(BF16)


def _row_spec(width):
    return pl.BlockSpec((None, ROW_TILE, width), lambda i, j: (i, j, 0))


def _const_spec(a, buffers=2):
    return pl.BlockSpec(a.shape, lambda i, j: (0,) * a.ndim, pipeline_mode=pl.Buffered(buffers))


def _proj_call(x, gain, w_all, qg_pad, kg2):
    b, s, _ = x.shape
    widths = [(Q_PAD, BF16), (KV_DIM, F32), (KV_DIM, F32), (KV_DIM, BF16), (KV_DIM, BF16),
              (KV_DIM, BF16), (KV_DIM, BF16), (LANES, F32), (RWKV_MIX_DIM, F32), (2 * D_MODEL, BF16)]
    return pl.pallas_call(
        _proj_kernel,
        grid=(b, s // ROW_TILE),
        in_specs=[_row_spec(D_MODEL), _const_spec(gain), _const_spec(w_all, buffers=1),
                  _const_spec(qg_pad), _const_spec(kg2)],
        out_specs=[_row_spec(w) for w, _ in widths],
        out_shape=[jax.ShapeDtypeStruct((b, s, w), dt) for w, dt in widths],
        compiler_params=_params("parallel", "parallel"),
        name="proj",
    )(x, gain, w_all, qg_pad, kg2)


def _compress_kernel(kc_ref, vc_ref, w1g_ref, w1_ref, w2g_ref, pos_ref, kg_ref, ko_ref, vo_ref, *, nc):
    rowi = _iota((nc, 1), 0)
    for which, (src, dst) in enumerate(((kc_ref, ko_ref), (vc_ref, vo_ref))):
        xcat = jnp.concatenate([src[0, pl.ds(l, nc, stride=CMP_STRIDE), :] for l in range(CMP_STRIDE)],
                               axis=1).astype(BF16)
        pos8 = jnp.broadcast_to(pos_ref[which], (SUBLANES, CMP_BLOCK * HEAD_DIM)).astype(BF16)
        bias = _nn(pos8, w1_ref[which])[0:1, :]
        out = jnp.zeros((nc, KV_DIM), F32)
        for g in range(KV_GROUPS):
            hab = _nn(xcat, w1g_ref[which, g])
            hid = hab[:, 0:CMP_HIDDEN] + pltpu.roll(hab[:, CMP_HIDDEN:2 * CMP_HIDDEN], nc - 1, axis=0) + bias
            out = out + _nn(jax.nn.gelu(hid).astype(BF16), w2g_ref[which, g])
        if which == 0:
            out = _group_rms(out, kg_ref[...])
        dst[0] = jnp.where(rowi < nc - 1, out, 0.0).astype(BF16)


def _compress_call(kc, vc, w1g, w1, w2g, pos_flat, kgain2):
    b, s, _ = kc.shape
    nc = s // CMP_STRIDE
    seq = pl.BlockSpec((1, s, KV_DIM), lambda i: (i, 0, 0))
    full = lambda a: pl.BlockSpec(a.shape, lambda i: (0,) * a.ndim)
    out = pl.BlockSpec((1, nc, KV_DIM), lambda i: (i, 0, 0))
    return pl.pallas_call(
        functools.partial(_compress_kernel, nc=nc),
        grid=(b,),
        in_specs=[seq, seq, full(w1g), full(w1), full(w2g), full(pos_flat), full(kgain2)],
        out_specs=[out, out],
        out_shape=[jax.ShapeDtypeStruct((b, nc, KV_DIM), BF16)] * 2,
        compiler_params=_params("parallel"),
        name="compress",
    )(kc, vc, w1g, w1, w2g, pos_flat, kgain2)


def _nsa_kernel(q_ref, gt_ref, kc_ref, vct_ref, ks_ref, vst_ref, kw_ref, vwt_ref, y_ref,
                m_s, acc_s, sb_s, *, tq, tk, nb, nc):
    nh = ATTN_HEADS
    hp = HEADS_PER_GROUP
    cols = hp * tq
    wk = WINDOW + tq
    bpt = tk // SLC_BLOCK
    qi = pl.program_id(1)
    q0 = qi * tq
    tlane = q0 + _iota((1, tq), 1)
    tile_h = lambda a, reps: jnp.concatenate([a] * reps, axis=1)
    eye_d = jnp.where(_iota((LANES, LANES), 0) == _iota((LANES, LANES), 1), 1.0, 0.0).astype(BF16)
    eye_q = jnp.where(_iota((tq, tq), 0) == _iota((tq, tq), 1), 1.0, 0.0).astype(BF16)
    drow = _iota((LANES, 1), 0)
    own = [(drow >= g * HEAD_DIM) & (drow < (g + 1) * HEAD_DIM) for g in range(KV_GROUPS)]

    qall = jnp.concatenate([q_ref[0, :, j * LANES:(j + 1) * LANES] for j in range(nh)], axis=0)
    q_t = _nt(eye_d, qall).astype(BF16)

    ncol = _iota((nc, 1), 0)
    cvalid = (ncol * CMP_STRIDE + (CMP_BLOCK - 1) <= tlane) & (ncol < nc - 1)
    s_c = _nn(kc_ref[0], q_t) + tile_h(jnp.where(cvalid, 0.0, NEG_INF), nh)
    e_c = jnp.exp2(s_c - jnp.max(s_c, axis=0, keepdims=True))
    p_c = e_c * (1.0 / jnp.sum(e_c, axis=0, keepdims=True)) * tile_h(jnp.where(cvalid, 1.0, 0.0), nh)
    p_cb = p_c.astype(BF16)
    o_c = [_nn(vct_ref[0], p_cb[:, g * cols:(g + 1) * cols]) for g in range(KV_GROUPS)]

    jcol = _iota((nb, 1), 0)
    ncmp = _iota((nb, nc), 1)
    ov = ((ncmp * CMP_STRIDE <= jcol * SLC_BLOCK + (SLC_BLOCK - 1))
          & (ncmp * CMP_STRIDE + (CMP_BLOCK - 1) >= jcol * SLC_BLOCK))
    ov = jnp.where(ov, 1.0, 0.0).astype(BF16)
    cur = jnp.right_shift(tlane, LOG2_64)
    forced = (jcol == 0) | (jcol == cur) | (jcol == cur - 1)
    causal_blk = jcol * SLC_BLOCK <= tlane
    for g in range(KV_GROUPS):
        c0 = g * cols
        psum = (p_c[:, c0:c0 + tq] + p_c[:, c0 + tq:c0 + 2 * tq]
                + p_c[:, c0 + 2 * tq:c0 + 3 * tq] + p_c[:, c0 + 3 * tq:c0 + 4 * tq])
        hi, lo = _split2(psum)
        imp = _nn(ov, hi) + _nn(ov, lo)
        imp = jnp.where(causal_blk, imp + jnp.where(forced, FORCE_BONUS, 0.0), -1.0)
        groups = [imp[r:r + SUBLANES, :] for r in range(0, nb, SUBLANES)]
        ranks = [jnp.zeros((SUBLANES, tq), F32) for _ in groups]
        jsub = _iota((SUBLANES, 1), 0)
        for i in range(nb):
            ri = imp[i:i + 1, :]
            for gi, grp in enumerate(groups):
                r0 = gi * SUBLANES
                if r0 + SUBLANES - 1 < i:
                    ahead = jnp.where(ri > grp, 1.0, 0.0)
                elif r0 > i:
                    ahead = jnp.where(ri >= grp, 1.0, 0.0)
                else:
                    ahead = jnp.where(jsub > i - r0, jnp.where(ri >= grp, 1.0, 0.0),
                                      jnp.where(ri > grp, 1.0, 0.0))
                ranks[gi] = ranks[gi] + ahead
        rank = jnp.concatenate(ranks, axis=0)
        sb_s[g] = jnp.where(rank < float(min(N_SELECT, nb)), 0.0, NEG_INF)

    m_s[...] = jnp.full(m_s.shape, NEG_INF, F32)
    acc_s[...] = jnp.zeros(acc_s.shape, F32)

    def body(kt, _):
        k0 = pl.multiple_of(kt * tk, tk)
        s = _nn(ks_ref[0, pl.ds(k0, tk), :], q_t)
        causal = (k0 + _iota((tk, 1), 0)) <= tlane
        for g in range(KV_GROUPS):
            cs = slice(g * cols, (g + 1) * cols)
            bias = jnp.concatenate(
                [jnp.broadcast_to(sb_s[g, pl.ds(kt * bpt + i, 1), :], (SLC_BLOCK, tq)) for i in range(bpt)],
                axis=0)
            sm = s[:, cs] + tile_h(jnp.where(causal, bias, NEG_INF), hp)
            m_prev = m_s[0:1, cs]
            m_new = jnp.maximum(m_prev, jnp.max(sm, axis=0, keepdims=True))
            m_s[:, cs] = jnp.broadcast_to(m_new, (8, cols))
            p = jnp.exp2(sm - m_new).astype(BF16)
            acc_s[:, cs] = jnp.exp2(m_prev - m_new) * acc_s[:, cs] + _nn(vst_ref[0, g, kt], p)
        return 0

    lax.fori_loop(0, qi // (tk // tq) + 1, body, 0)

    w0 = pl.multiple_of(jnp.maximum(q0 - WINDOW, 0), tq)
    dist = tlane - (w0 + _iota((wk, 1), 0))
    s_w = (_nn(kw_ref[0, pl.ds(w0, wk), :], q_t)
           + tile_h(jnp.where((dist >= 0) & (dist < WINDOW), 0.0, NEG_INF), nh))
    p_w = jnp.exp2(s_w - jnp.max(s_w, axis=0, keepdims=True)).astype(BF16)
    wb = w0 // tq

    g_hi, g_lo = _split2(jax.nn.sigmoid(gt_ref[0]))
    gate_t = _nt(eye_d, g_hi) + _nt(eye_d, g_lo)
    for g in range(KV_GROUPS):
        cs = slice(g * cols, (g + 1) * cols)
        den_row = (1 - g) * HEAD_DIM
        acc = acc_s[:, cs]
        o_s = acc * (1.0 / acc[den_row:den_row + 1, :])
        v_w = jnp.concatenate([vwt_ref[0, g, wb + i] for i in range(wk // tq)], axis=1)
        acc_w = _nn(v_w, p_w[:, cs])
        o_w = acc_w * (1.0 / acc_w[den_row:den_row + 1, :])
        for h in range(hp):
            j = g * hp + h
            r0 = j * N_NSA_BRANCHES
            hs = slice(h * tq, (h + 1) * tq)
            o = (gate_t[r0:r0 + 1, :] * o_c[g][:, hs] + gate_t[r0 + 1:r0 + 2, :] * o_s[:, hs]
                 + gate_t[r0 + 2:r0 + 3, :] * o_w[:, hs])
            o = jnp.where(own[g], o, 0.0).astype(BF16)
            y_ref[0, :, j * LANES:(j + 1) * LANES] = _nt(eye_q, o).astype(BF16)


def _nsa_call(q, gates, kc, vct, ks, vst, kw, vwt):
    b, s, _ = q.shape
    tq = NSA_QUERY_TILE
    nb = s // SLC_BLOCK
    nc = kc.shape[1]
    tk = vst.shape[-1]
    tile = lambda w: pl.BlockSpec((1, tq, w), lambda i, j: (i, j, 0))
    seq = lambda a: pl.BlockSpec((1,) + a.shape[1:], lambda i, j: (i,) + (0,) * (a.ndim - 1))
    return pl.pallas_call(
        functools.partial(_nsa_kernel, tq=tq, tk=tk, nb=nb, nc=nc),
        grid=(b, s // tq),
        in_specs=[tile(Q_PAD), tile(LANES), seq(kc), seq(vct), seq(ks), seq(vst), seq(kw), seq(vwt)],
        out_specs=tile(Q_PAD),
        out_shape=jax.ShapeDtypeStruct((b, s, Q_PAD), BF16),
        scratch_shapes=[pltpu.VMEM((8, ATTN_HEADS * tq), F32),
                        pltpu.VMEM((LANES, ATTN_HEADS * tq), F32),
                        pltpu.VMEM((KV_GROUPS, nb, tq), F32)],
        compiler_params=_params("parallel", "arbitrary"),
        name="nsa",
    )(q, gates, kc, vct, ks, vst, kw, vwt)


def _seg_sum(x, bd):
    hi, lo = _split2(x)
    w = bd.shape[0]
    return jnp.concatenate([_nn(hi[:, c:c + w], bd) + _nn(lo[:, c:c + w], bd)
                            for c in range(0, x.shape[1], w)], axis=1)


def _stack_heads(x):
    lane = _iota((1, LANES), 1)
    lo = lane < RWKV_HEAD_DIM
    return jnp.concatenate([jnp.where(lo, x, 0.0), jnp.where(lo, 0.0, x)], axis=0)


def _rwkv_kernel(p_ref, mu_ref, w0_ref, wup_ref, a0_ref, aup_ref, gup_ref, kk_ref, ka_ref, rk_ref,
                 lnw_ref, lnb_ref, o_ref, carry_ref, state_ref, *, tt):
    ch = RWKV_CHUNK
    c2 = 2 * ch
    d = RWKV_DIM
    nch = tt // ch
    npr = RWKV_PAIRS

    @pl.when(pl.program_id(1) == 0)
    def _():
        carry_ref[...] = jnp.zeros_like(carry_ref)
        state_ref[...] = jnp.zeros_like(state_ref)

    p = p_ref[0]
    prev = pltpu.roll(p, 1, axis=0)
    prev = jnp.where(_iota((tt, 1), 0) == 0, carry_ref[0:1, :], prev)
    carry_ref[0:1, :] = p[tt - 1:tt, :]
    pm = p + (prev - p) * mu_ref[...]

    r = pm[:, 0:d]
    k = pm[:, d:2 * d]
    v = pm[:, 2 * d:3 * d]
    wa = pm[:, 3 * d:3 * d + LANES]
    gl = pm[:, 3 * d + LANES:3 * d + 2 * LANES]

    z = w0_ref[...] + _nn(jnp.tanh(wa).astype(BF16), wup_ref[...])
    softplus = jnp.maximum(-z, 0.0) + jnp.log(1.0 + jnp.exp(-jnp.abs(z)))
    lw = -jnp.exp(-softplus - 0.5)
    a = jax.nn.sigmoid(a0_ref[...] + _nn(wa.astype(BF16), aup_ref[...]))
    gate = _nn(jax.nn.sigmoid(gl).astype(BF16), gup_ref[...])

    seg = (jnp.right_shift(_iota((MXU_COLS, MXU_COLS), 0), LOG2_64)
           == jnp.right_shift(_iota((MXU_COLS, MXU_COLS), 1), LOG2_64))
    bd = jnp.where(seg, 1.0, 0.0).astype(BF16)
    kk = k * kk_ref[...]
    kk = kk * lax.rsqrt(jnp.maximum(_seg_sum(kk * kk, bd), 1e-24))
    k2 = k * (1.0 + (a - 1.0) * ka_ref[...])

    ti = _iota((tt, tt), 0)
    tj = _iota((tt, tt), 1)
    same_chunk = jnp.right_shift(ti, LOG2_64) == jnp.right_shift(tj, LOG2_64)
    tri = jnp.where(same_chunk & (tj <= ti), 1.0, 0.0).astype(BF16)
    hi, lo = _split2(lw)
    cum = _nn(tri, hi) + _nn(tri, lo)
    tot = jnp.concatenate([jnp.broadcast_to(cum[(c + 1) * ch - 1:(c + 1) * ch, :], (ch, d))
                           for c in range(nch)], axis=0)
    e_inv = jnp.exp(-cum)
    e_end = jnp.exp(tot - cum)
    dec = jnp.exp(tot)
    bb = kk * a

    def tiles(x):
        return jnp.stack([_stack_heads(x[c * ch:(c + 1) * ch, pr * LANES:(pr + 1) * LANES])
                          for c in range(nch) for pr in range(npr)]).astype(BF16)

    a_t = tiles(-kk * jnp.exp(cum - lw))
    r_t = tiles(r * jnp.exp(cum))
    b_h = tiles(bb * e_inv)
    k_h = tiles(k2 * e_inv)
    bke = jnp.concatenate([tiles(bb * e_end), tiles(k2 * e_end)], axis=1)
    v_t = tiles(v)

    ri = _iota((c2, c2), 0)
    ci = _iota((c2, c2), 1)
    same = jnp.right_shift(ri, LOG2_64) == jnp.right_shift(ci, LOG2_64)
    strict = jnp.where(same & (ci < ri), 1.0, 0.0)
    incl = jnp.where(same & (ci <= ri), 1.0, 0.0)
    ident = jnp.where(ri == ci, 1.0, 0.0)

    scores = _bnt(jnp.concatenate([a_t, r_t], axis=1), jnp.concatenate([b_h, k_h], axis=1))
    m_ab = scores[:, 0:c2, 0:c2] * strict
    m_ak = (scores[:, 0:c2, c2:2 * c2] * strict).astype(BF16)
    m_rbk = jnp.concatenate([scores[:, c2:2 * c2, 0:c2] * incl,
                             scores[:, c2:2 * c2, c2:2 * c2] * incl], axis=2).astype(BF16)
    tinv = ident + m_ab
    mp = m_ab
    for _ in range(5):
        mpb = mp.astype(BF16)
        mp = _bnn(mpb, mpb)
        tinv = tinv + _bnn(tinv.astype(BF16), mp.astype(BF16))
    rhs = jnp.concatenate([a_t, _bnn(m_ak, v_t).astype(BF16)], axis=2)
    wu = _bnn(tinv.astype(BF16), rhs)
    w_t = wu[:, :, 0:c2].astype(BF16)
    u0 = wu[:, :, c2:2 * c2]

    g_st = state_ref[...]
    y_rows = []
    for c in range(nch):
        sl = slice(c * npr, (c + 1) * npr)
        g_b = g_st.astype(BF16)
        u = _bnt(w_t[sl], g_b) + u0[sl]
        uv = jnp.concatenate([u.astype(BF16), v_t[sl]], axis=1)
        y = _bnt(r_t[sl], g_b) + _bnn(m_rbk[sl], uv)
        y = y[:, 0:ch, :] + y[:, ch:c2, :]
        y_rows.append(jnp.concatenate([y[pr] for pr in range(npr)], axis=1))
        dec_c = jnp.stack([dec[c * ch:c * ch + 1, pr * LANES:(pr + 1) * LANES] for pr in range(npr)])
        g_st = g_st * dec_c + _btn(uv, bke[sl])
    state_ref[...] = g_st
    y = jnp.concatenate(y_rows, axis=0)

    inv = 1.0 / RWKV_HEAD_DIM
    mean = _seg_sum(y, bd) * inv
    yc = y - mean
    var = _seg_sum(yc * yc, bd) * inv
    yn = yc * lax.rsqrt(var + GN_EPS) * lnw_ref[...] + lnb_ref[...]
    bonus = _seg_sum(r * k2 * rk_ref[...], bd) * v
    o_ref[0] = ((yn + bonus) * gate).astype(BF16)


def _rwkv_call(p, mu, w0, wup, a0, aup, gup, k_k, k_a, r_k, ln_w, ln_b, tt=256):
    b, s, _ = p.shape
    full = lambda a: pl.BlockSpec(a.shape, lambda i, j: (0,) * a.ndim)
    consts = (mu, w0, wup, a0, aup, gup, k_k, k_a, r_k, ln_w, ln_b)
    return pl.pallas_call(
        functools.partial(_rwkv_kernel, tt=tt),
        grid=(b, s // tt),
        in_specs=[pl.BlockSpec((1, tt, RWKV_MIX_DIM), lambda i, j: (i, j, 0))]
                 + [full(c) for c in consts],
        out_specs=pl.BlockSpec((1, tt, RWKV_DIM), lambda i, j: (i, j, 0)),
        out_shape=jax.ShapeDtypeStruct((b, s, RWKV_DIM), BF16),
        scratch_shapes=[pltpu.VMEM((8, RWKV_MIX_DIM), F32),
                        pltpu.VMEM((RWKV_PAIRS, LANES, LANES), F32)],
        compiler_params=_params("arbitrary", "arbitrary"),
        name="rwkv",
    )(p, *consts)


def _merge_kernel(x_ref, ya_ref, yb_ref, mg_ref, wa_ref, wb_ref, wo_ref, fg_ref, xo_ref, h_ref):
    ga = jax.nn.sigmoid(mg_ref[:, 0:D_MODEL].astype(F32))
    gb = jax.nn.sigmoid(mg_ref[:, D_MODEL:2 * D_MODEL].astype(F32))
    merged = ga * _nn(ya_ref[...], wa_ref[...]) + gb * _nn(yb_ref[...], wb_ref[...])
    xn = x_ref[...] + _nn(merged.astype(BF16), wo_ref[...])
    xo_ref[...] = xn
    ms = jnp.mean(xn * xn, axis=-1, keepdims=True)
    h_ref[...] = (xn * lax.rsqrt(ms + RMS_EPS) * fg_ref[...]).astype(BF16)


def _merge_call(x, ya, yb, mg, wa_pad, wb, wo, fgain):
    b, s, _ = x.shape
    return pl.pallas_call(
        _merge_kernel,
        grid=(b, s // ROW_TILE),
        in_specs=[_row_spec(D_MODEL), _row_spec(Q_PAD), _row_spec(RWKV_DIM), _row_spec(2 * D_MODEL),
                  _const_spec(wa_pad), _const_spec(wb), _const_spec(wo), _const_spec(fgain)],
        out_specs=[_row_spec(D_MODEL), _row_spec(D_MODEL)],
        out_shape=[jax.ShapeDtypeStruct((b, s, D_MODEL), F32),
                   jax.ShapeDtypeStruct((b, s, D_MODEL), BF16)],
        compiler_params=_params("parallel", "parallel"),
        name="merge",
    )(x, ya, yb, mg, wa_pad, wb, wo, fgain)


def _ffn_kernel(x_ref, h_ref, wu_ref, wd_ref, o_ref, *, fc):
    h = h_ref[...]
    acc = x_ref[...]
    for c in range(D_FF // fc):
        up = jnp.maximum(_nn(h, wu_ref[:, c * fc:(c + 1) * fc]), 0.0)
        acc = acc + _nn((up * up).astype(BF16), wd_ref[c * fc:(c + 1) * fc, :])
    o_ref[...] = acc


def _ffn_call(x, h, wu, wd, fc=1024):
    b, s, _ = x.shape
    return pl.pallas_call(
        functools.partial(_ffn_kernel, fc=fc),
        grid=(b, s // ROW_TILE),
        in_specs=[_row_spec(D_MODEL), _row_spec(D_MODEL), _const_spec(wu), _const_spec(wd)],
        out_specs=_row_spec(D_MODEL),
        out_shape=jax.ShapeDtypeStruct((b, s, D_MODEL), F32),
        compiler_params=_params("parallel", "parallel"),
        name="ffn",
    )(x, h, wu, wd)


def _pad_heads_cols(w):
    rows = w.shape[0]
    wh = w.reshape(rows, ATTN_HEADS, HEAD_DIM)
    slots = []
    for j in range(ATTN_HEADS):
        g = j // HEADS_PER_GROUP
        z = jnp.zeros((rows, HEAD_DIM), w.dtype)
        slots.append(jnp.concatenate([wh[:, j], z] if g == 0 else [z, wh[:, j]], axis=1))
    return jnp.concatenate(slots, axis=1)


def _proj_weight(w_in):
    q = _pad_heads_cols(w_in[:, 0:ATTN_DIM])
    kv = w_in[:, ATTN_DIM:ATTN_DIM + 6 * KV_DIM]
    gates = jnp.pad(w_in[:, ATTN_DIM + 6 * KV_DIM:NSA_DIM], ((0, 0), (0, LANES - NSA_GATES)))
    rest = w_in[:, NSA_DIM:]
    return jnp.concatenate([q, kv, gates, rest], axis=1).astype(BF16)


def _values_t(v, tile):
    b, s, _ = v.shape
    vt = v.reshape(b, s // tile, tile, KV_DIM).transpose(0, 1, 3, 2)
    row = jnp.arange(KV_DIM)[:, None] // HEAD_DIM
    one = jnp.ones((), v.dtype)
    return jnp.stack([jnp.where(row == g, vt, one) for g in range(KV_GROUPS)], axis=1)


def _compress_weights(w1, w2):
    w = w1.reshape(2, 2, CMP_STRIDE, HEAD_DIM, CMP_HIDDEN).transpose(0, 2, 3, 1, 4)
    zw = jnp.zeros_like(w)
    w1g = jnp.stack([jnp.stack([w if gg == g else zw for gg in range(KV_GROUPS)], axis=2)
                     for g in range(KV_GROUPS)], axis=1)
    w1g = w1g.reshape(2, KV_GROUPS, CMP_STRIDE * KV_DIM, 2 * CMP_HIDDEN)
    z2 = jnp.zeros_like(w2)
    w2g = jnp.stack([jnp.concatenate([w2 if gg == g else z2 for gg in range(KV_GROUPS)], axis=2)
                     for g in range(KV_GROUPS)], axis=1)
    return w1g.astype(BF16), w2g.astype(BF16)


def kernel(x, mix_norm, w_in, q_gain, k_gain, cmp_pos, cmp_w1, cmp_w2, w_attn_branch, tok_mix, w0,
           w_lora_up, a0, a_lora_up, g_lora_up, k_k, k_a, r_k, ln_x_w, ln_x_b, w_rwkv_branch, w_out,
           ffn_norm, w_ffn_up, w_ffn_down):
    b, s, _ = x.shape
    depth = w_in.shape[0]
    row = lambda v: v.reshape(1, -1)
    for l in range(depth):
        qg_pad = _pad_heads_cols(jnp.tile(q_gain[l], ATTN_HEADS).reshape(1, ATTN_DIM))
        kg2 = jnp.tile(k_gain[l, 1:3], (1, KV_GROUPS))
        q, kc, vc, ks, vs, kw, vw, gates, rw, mg = _proj_call(
            x, row(mix_norm[l]), _proj_weight(w_in[l]), qg_pad, kg2)

        w1g, w2g = _compress_weights(cmp_w1[l], cmp_w2[l])
        k_cmp, v_cmp = _compress_call(kc, vc, w1g, cmp_w1[l].astype(BF16), w2g,
                                      cmp_pos[l].reshape(2, 1, CMP_BLOCK * HEAD_DIM),
                                      jnp.tile(k_gain[l, 0], KV_GROUPS).reshape(1, KV_DIM))

        ya = _nsa_call(q, gates, k_cmp, v_cmp.transpose(0, 2, 1), ks,
                       _values_t(vs, NSA_KEY_TILE), kw, _values_t(vw, NSA_QUERY_TILE))

        zero = jnp.zeros((W_LORA, RWKV_DIM), F32)
        wup = jnp.concatenate([w_lora_up[l], zero], axis=0).astype(BF16)
        aup = jnp.concatenate([zero, a_lora_up[l]], axis=0).astype(BF16)
        yb = _rwkv_call(rw, row(tok_mix[l]), row(w0[l]), wup, row(a0[l]), aup,
                        g_lora_up[l].astype(BF16), row(k_k[l]), row(k_a[l]), row(r_k[l]),
                        row(ln_x_w[l]), row(ln_x_b[l]))

        wa_pad = _pad_heads_cols(w_attn_branch[l].T).T.astype(BF16)
        x, h = _merge_call(x, ya, yb, mg, wa_pad, w_rwkv_branch[l].astype(BF16),
                           w_out[l].astype(BF16), row(ffn_norm[l]))
        x = _ffn_call(x, h, w_ffn_up[l].astype(BF16), w_ffn_down[l].astype(BF16))
    return x
```

```python
import functools

import jax
import jax.numpy as jnp
import numpy as np
from jax import lax
from jax.experimental import pallas as pl
from jax.experimental.pallas import tpu as pltpu

F32 = jnp.float32
BF16 = jnp.bfloat16
I32 = jnp.int32

D_MODEL = 1024
ATTN_HEADS = 8
HEAD_DIM = 64
KV_GROUPS = 2
HEADS_PER_GROUP = ATTN_HEADS // KV_GROUPS
ATTN_DIM = ATTN_HEADS * HEAD_DIM
KV_DIM = KV_GROUPS * HEAD_DIM
N_NSA_BRANCHES = 3
CMP_BLOCK = 32
CMP_STRIDE = 16
CMP_HIDDEN = 2 * HEAD_DIM
SLC_BLOCK = 64
N_SELECT = 16
WINDOW = 512
FORCE_BONUS = 1000.0
RWKV_HEADS = 8
RWKV_HEAD_DIM = 64
RWKV_DIM = RWKV_HEADS * RWKV_HEAD_DIM
W_LORA = 64
A_LORA = 64
G_LORA = 128
RWKV_MIX_DIM = 3 * RWKV_DIM + W_LORA + A_LORA + G_LORA
D_FF = 4 * D_MODEL
NSA_GATES = N_NSA_BRANCHES * ATTN_HEADS
NSA_DIM = ATTN_DIM + 6 * KV_DIM + NSA_GATES
RMS_EPS = 1e-6
GN_EPS = 64e-5
NEG_INF = -1e30

LOG2_64 = 6
LANES = 128
SUBLANES = 8
MXU_COLS = 256
Q_SCALE = HEAD_DIM ** -0.5 * 1.4426950408889634
Q_PAD = ATTN_HEADS * LANES
RWKV_PAIRS = RWKV_HEADS // 2
RWKV_CHUNK = 64
ROW_TILE = 512
NSA_QUERY_TILE = 128
NSA_KEY_TILE = 512
VMEM_LIMIT = 48 * 1024 * 1024

C_Q = 0
C_KV = C_Q + Q_PAD
C_GATE = C_KV + 6 * KV_DIM
C_RWKV = C_GATE + LANES
C_MERGE = C_RWKV + RWKV_MIX_DIM
C_TOTAL = C_MERGE + 2 * D_MODEL


def _nn(a, b):
    return lax.dot_general(a, b, (((1,), (0,)), ((), ())), preferred_element_type=F32)


def _nt(a, b):
    return lax.dot_general(a, b, (((1,), (1,)), ((), ())), preferred_element_type=F32)


def _bnn(a, b):
    return lax.dot_general(a, b, (((2,), (1,)), ((0,), (0,))), preferred_element_type=F32)


def _bnt(a, b):
    return lax.dot_general(a, b, (((2,), (2,)), ((0,), (0,))), preferred_element_type=F32)


def _btn(a, b):
    return lax.dot_general(a, b, (((1,), (1,)), ((0,), (0,))), preferred_element_type=F32)


def _split2(x):
    hi = x.astype(BF16)
    lo = (x - hi.astype(F32)).astype(BF16)
    return hi, lo


def _iota(shape, dim):
    return lax.broadcasted_iota(I32, shape, dim)


def _params(*sem):
    return pltpu.CompilerParams(dimension_semantics=sem, vmem_limit_bytes=VMEM_LIMIT)


def _group_rms(x, gain):
    lane = _iota((1, LANES), 1)
    sq = x * x
    lo = lane < HEAD_DIM
    s0 = jnp.sum(jnp.where(lo, sq, 0.0), axis=-1, keepdims=True)
    s1 = jnp.sum(jnp.where(lo, 0.0, sq), axis=-1, keepdims=True)
    ms = jnp.where(lo, s0, s1) * (1.0 / HEAD_DIM)
    return x * lax.rsqrt(ms + RMS_EPS) * gain


def _proj_kernel(x_ref, g_ref, w_ref, qg_ref, kg_ref,
                 q_ref, kc_ref, vc_ref, ks_ref, vs_ref, kw_ref, vw_ref, gt_ref, rw_ref, mg_ref):
    x = x_ref[...]
    ms = jnp.mean(x * x, axis=-1, keepdims=True)
    u = (x * lax.rsqrt(ms + RMS_EPS) * g_ref[...]).astype(BF16)

    def col(c0, width):
        return _nn(u, w_ref[:, c0:c0 + width])

    for jp in range(ATTN_HEADS // 2):
        qq = col(C_Q + jp * MXU_COLS, MXU_COLS)
        for half in range(2):
            j = 2 * jp + half
            qj = qq[:, half * LANES:(half + 1) * LANES]
            msq = jnp.sum(qj * qj, axis=-1, keepdims=True) * (1.0 / HEAD_DIM)
            qn = qj * lax.rsqrt(msq + RMS_EPS) * qg_ref[:, j * LANES:(j + 1) * LANES]
            q_ref[:, j * LANES:(j + 1) * LANES] = (qn * Q_SCALE).astype(BF16)
    kvc = col(C_KV, MXU_COLS)
    kc_ref[...] = kvc[:, 0:KV_DIM]
    vc_ref[...] = kvc[:, KV_DIM:2 * KV_DIM]
    kvs = col(C_KV + MXU_COLS, MXU_COLS)
    ks_ref[...] = _group_rms(kvs[:, 0:KV_DIM], kg_ref[0:1, :]).astype(BF16)
    vs_ref[...] = kvs[:, KV_DIM:2 * KV_DIM].astype(BF16)
    kvw = col(C_KV + 2 * MXU_COLS, MXU_COLS)
    kw_ref[...] = _group_rms(kvw[:, 0:KV_DIM], kg_ref[1:2, :]).astype(BF16)
    vw_ref[...] = kvw[:, KV_DIM:2 * KV_DIM].astype(BF16)
    gt_ref[...] = col(C_GATE, LANES)
    for c in range(RWKV_MIX_DIM // MXU_COLS):
        rw_ref[:, c * MXU_COLS:(c + 1) * MXU_COLS] = col(C_RWKV + c * MXU_COLS, MXU_COLS)
    for c in range(2 * D_MODEL // 512):
        mg_ref[:, c * 512:(c + 1) * 512] = col(C_MERGE + c * 512, 512).astype(BF16)


def _row_spec(width):
    return pl.BlockSpec((None, ROW_TILE, width), lambda i, j: (i, j, 0))


def _const_spec(a, buffers=2):
    return pl.BlockSpec(a.shape, lambda i, j: (0,) * a.ndim, pipeline_mode=pl.Buffered(buffers))


def _proj_call(x, gain, w_all, qg_pad, kg2):
    b, s, _ = x.shape
    widths = [(Q_PAD, BF16), (KV_DIM, F32), (KV_DIM, F32), (KV_DIM, BF16), (KV_DIM, BF16),
              (KV_DIM, BF16), (KV_DIM, BF16), (LANES, F32), (RWKV_MIX_DIM, F32), (2 * D_MODEL, BF16)]
    return pl.pallas_call(
        _proj_kernel,
        grid=(b, s // ROW_TILE),
        in_specs=[_row_spec(D_MODEL), _const_spec(gain), _const_spec(w_all, buffers=1),
                  _const_spec(qg_pad), _const_spec(kg2)],
        out_specs=[_row_spec(w) for w, _ in widths],
        out_shape=[jax.ShapeDtypeStruct((b, s, w), dt) for w, dt in widths],
        compiler_params=_params("parallel", "parallel"),
        name="proj",
    )(x, gain, w_all, qg_pad, kg2)


def _compress_kernel(kc_ref, vc_ref, w1g_ref, w1_ref, w2g_ref, pos_ref, kg_ref, ko_ref, vo_ref, *, nc):
    rowi = _iota((nc, 1), 0)
    for which, (src, dst) in enumerate(((kc_ref, ko_ref), (vc_ref, vo_ref))):
        xcat = jnp.concatenate([src[0, pl.ds(l, nc, stride=CMP_STRIDE), :] for l in range(CMP_STRIDE)],
                               axis=1).astype(BF16)
        pos8 = jnp.broadcast_to(pos_ref[which], (SUBLANES, CMP_BLOCK * HEAD_DIM)).astype(BF16)
        bias = _nn(pos8, w1_ref[which])[0:1, :]
        out = jnp.zeros((nc, KV_DIM), F32)
        for g in range(KV_GROUPS):
            hab = _nn(xcat, w1g_ref[which, g])
            hid = hab[:, 0:CMP_HIDDEN] + pltpu.roll(hab[:, CMP_HIDDEN:2 * CMP_HIDDEN], nc - 1, axis=0) + bias
            out = out + _nn(jax.nn.gelu(hid).astype(BF16), w2g_ref[which, g])
        if which == 0:
            out = _group_rms(out, kg_ref[...])
        dst[0] = jnp.where(rowi < nc - 1, out, 0.0).astype(BF16)


def _compress_call(kc, vc, w1g, w1, w2g, pos_flat, kgain2):
    b, s, _ = kc.shape
    nc = s // CMP_STRIDE
    seq = pl.BlockSpec((1, s, KV_DIM), lambda i: (i, 0, 0))
    full = lambda a: pl.BlockSpec(a.shape, lambda i: (0,) * a.ndim)
    out = pl.BlockSpec((1, nc, KV_DIM), lambda i: (i, 0, 0))
    return pl.pallas_call(
        functools.partial(_compress_kernel, nc=nc),
        grid=(b,),
        in_specs=[seq, seq, full(w1g), full(w1), full(w2g), full(pos_flat), full(kgain2)],
        out_specs=[out, out],
        out_shape=[jax.ShapeDtypeStruct((b, nc, KV_DIM), BF16)] * 2,
        compiler_params=_params("parallel"),
        name="compress",
    )(kc, vc, w1g, w1, w2g, pos_flat, kgain2)


def _nsa_kernel(q_ref, gt_ref, kc_ref, vct_ref, ks_ref, vst_ref, kw_ref, vwt_ref, y_ref,
                m_s, acc_s, sb_s, *, tq, tk, nb, nc):
    nh = ATTN_HEADS
    hp = HEADS_PER_GROUP
    cols = hp * tq
    wk = WINDOW + tq
    bpt = tk // SLC_BLOCK
    qi = pl.program_id(1)
    q0 = qi * tq
    tlane = q0 + _iota((1, tq), 1)
    tile_h = lambda a, reps: jnp.concatenate([a] * reps, axis=1)
    eye_d = jnp.where(_iota((LANES, LANES), 0) == _iota((LANES, LANES), 1), 1.0, 0.0).astype(BF16)
    eye_q = jnp.where(_iota((tq, tq), 0) == _iota((tq, tq), 1), 1.0, 0.0).astype(BF16)
    drow = _iota((LANES, 1), 0)
    own = [(drow >= g * HEAD_DIM) & (drow < (g + 1) * HEAD_DIM) for g in range(KV_GROUPS)]

    qall = jnp.concatenate([q_ref[0, :, j * LANES:(j + 1) * LANES] for j in range(nh)], axis=0)
    q_t = _nt(eye_d, qall).astype(BF16)

    ncol = _iota((nc, 1), 0)
    cvalid = (ncol * CMP_STRIDE + (CMP_BLOCK - 1) <= tlane) & (ncol < nc - 1)
    s_c = _nn(kc_ref[0], q_t) + tile_h(jnp.where(cvalid, 0.0, NEG_INF), nh)
    e_c = jnp.exp2(s_c - jnp.max(s_c, axis=0, keepdims=True))
    p_c = e_c * (1.0 / jnp.sum(e_c, axis=0, keepdims=True)) * tile_h(jnp.where(cvalid, 1.0, 0.0), nh)
    p_cb = p_c.astype(BF16)
    o_c = [_nn(vct_ref[0], p_cb[:, g * cols:(g + 1) * cols]) for g in range(KV_GROUPS)]

    jcol = _iota((nb, 1), 0)
    ncmp = _iota((nb, nc), 1)
    ov = ((ncmp * CMP_STRIDE <= jcol * SLC_BLOCK + (SLC_BLOCK - 1))
          & (ncmp * CMP_STRIDE + (CMP_BLOCK - 1) >= jcol * SLC_BLOCK))
    ov = jnp.where(ov, 1.0, 0.0).astype(BF16)
    cur = jnp.right_shift(tlane, LOG2_64)
    forced = (jcol == 0) | (jcol == cur) | (jcol == cur - 1)
    causal_blk = jcol * SLC_BLOCK <= tlane
    for g in range(KV_GROUPS):
        c0 = g * cols
        psum = (p_c[:, c0:c0 + tq] + p_c[:, c0 + tq:c0 + 2 * tq]
                + p_c[:, c0 + 2 * tq:c0 + 3 * tq] + p_c[:, c0 + 3 * tq:c0 + 4 * tq])
        hi, lo = _split2(psum)
        imp = _nn(ov, hi) + _nn(ov, lo)
        imp = jnp.where(causal_blk, imp + jnp.where(forced, FORCE_BONUS, 0.0), -1.0)
        groups = [imp[r:r + SUBLANES, :] for r in range(0, nb, SUBLANES)]
        ranks = [jnp.zeros((SUBLANES, tq), F32) for _ in groups]
        jsub = _iota((SUBLANES, 1), 0)
        for i in range(nb):
            ri = imp[i:i + 1, :]
            for gi, grp in enumerate(groups):
                r0 = gi * SUBLANES
                if r0 + SUBLANES - 1 < i:
                    ahead = jnp.where(ri > grp, 1.0, 0.0)
                elif r0 > i:
                    ahead = jnp.where(ri >= grp, 1.0, 0.0)
                else:
                    ahead = jnp.where(jsub > i - r0, jnp.where(ri >= grp, 1.0, 0.0),
                                      jnp.where(ri > grp, 1.0, 0.0))
                ranks[gi] = ranks[gi] + ahead
        rank = jnp.concatenate(ranks, axis=0)
        sb_s[g] = jnp.where(rank < float(min(N_SELECT, nb)), 0.0, NEG_INF)

    m_s[...] = jnp.full(m_s.shape, NEG_INF, F32)
    acc_s[...] = jnp.zeros(acc_s.shape, F32)

    def body(kt, _):
        k0 = pl.multiple_of(kt * tk, tk)
        s = _nn(ks_ref[0, pl.ds(k0, tk), :], q_t).astype(BF16)
        causal = (k0 + _iota((tk, 1), 0)) <= tlane
        for g in range(KV_GROUPS):
            cs = slice(g * cols, (g + 1) * cols)
            bias = jnp.concatenate(
                [jnp.broadcast_to(sb_s[g, pl.ds(kt * bpt + i, 1), :], (SLC_BLOCK, tq)) for i in range(bpt)],
                axis=0)
            sm = s[:, cs] + tile_h(jnp.where(causal, bias, NEG_INF).astype(BF16), hp)
            m_prev = m_s[0:1, cs]
            m_new = jnp.maximum(m_prev, jnp.max(sm, axis=0, keepdims=True).astype(F32))
            m_s[:, cs] = jnp.broadcast_to(m_new, (SUBLANES, cols))
            p = jnp.exp2(sm - m_new.astype(BF16))
            acc_s[:, cs] = jnp.exp2(m_prev - m_new) * acc_s[:, cs] + _nn(vst_ref[0, g, kt], p)
        return 0

    lax.fori_loop(0, qi // (tk // tq) + 1, body, 0)

    w0 = pl.multiple_of(jnp.maximum(q0 - WINDOW, 0), tq)
    dist = tlane - (w0 + _iota((wk, 1), 0))
    s_w = (_nn(kw_ref[0, pl.ds(w0, wk), :], q_t).astype(BF16)
           + tile_h(jnp.where((dist >= 0) & (dist < WINDOW), 0.0, NEG_INF).astype(BF16), nh))
    p_w = jnp.exp2(s_w - jnp.max(s_w, axis=0, keepdims=True))
    wb = w0 // tq

    g_hi, g_lo = _split2(jax.nn.sigmoid(gt_ref[0]))
    gate_t = _nt(eye_d, g_hi) + _nt(eye_d, g_lo)
    for g in range(KV_GROUPS):
        cs = slice(g * cols, (g + 1) * cols)
        den_row = (1 - g) * HEAD_DIM
        acc = acc_s[:, cs]
        o_s = acc * (1.0 / acc[den_row:den_row + 1, :])
        v_w = jnp.concatenate([vwt_ref[0, g, wb + i] for i in range(wk // tq)], axis=1)
        acc_w = _nn(v_w, p_w[:, cs])
        o_w = acc_w * (1.0 / acc_w[den_row:den_row + 1, :])
        for h in range(hp):
            j = g * hp + h
            r0 = j * N_NSA_BRANCHES
            hs = slice(h * tq, (h + 1) * tq)
            o = (gate_t[r0:r0 + 1, :] * o_c[g][:, hs] + gate_t[r0 + 1:r0 + 2, :] * o_s[:, hs]
                 + gate_t[r0 + 2:r0 + 3, :] * o_w[:, hs])
            o = jnp.where(own[g], o, 0.0).astype(BF16)
            y_ref[0, :, j * LANES:(j + 1) * LANES] = _nt(eye_q, o).astype(BF16)


def _nsa_call(q, gates, kc, vct, ks, vst, kw, vwt):
    b, s, _ = q.shape
    tq = NSA_QUERY_TILE
    nb = s // SLC_BLOCK
    nc = kc.shape[1]
    tk = vst.shape[-1]
    tile = lambda w: pl.BlockSpec((1, tq, w), lambda i, j: (i, j, 0))
    seq = lambda a: pl.BlockSpec((1,) + a.shape[1:], lambda i, j: (i,) + (0,) * (a.ndim - 1))
    return pl.pallas_call(
        functools.partial(_nsa_kernel, tq=tq, tk=tk, nb=nb, nc=nc),
        grid=(b, s // tq),
        in_specs=[tile(Q_PAD), tile(LANES), seq(kc), seq(vct), seq(ks), seq(vst), seq(kw), seq(vwt)],
        out_specs=tile(Q_PAD),
        out_shape=jax.ShapeDtypeStruct((b, s, Q_PAD), BF16),
        scratch_shapes=[pltpu.VMEM((8, ATTN_HEADS * tq), F32),
                        pltpu.VMEM((LANES, ATTN_HEADS * tq), F32),
                        pltpu.VMEM((KV_GROUPS, nb, tq), F32)],
        compiler_params=_params("parallel", "arbitrary"),
        name="nsa",
    )(q, gates, kc, vct, ks, vst, kw, vwt)


def _seg_sum(x, bd):
    hi, lo = _split2(x)
    w = bd.shape[0]
    return jnp.concatenate([_nn(hi[:, c:c + w], bd) + _nn(lo[:, c:c + w], bd)
                            for c in range(0, x.shape[1], w)], axis=1)


def _stack_heads(x):
    lane = _iota((1, LANES), 1)
    lo = lane < RWKV_HEAD_DIM
    return jnp.concatenate([jnp.where(lo, x, 0.0), jnp.where(lo, 0.0, x)], axis=0)


def _rwkv_kernel(p_ref, mu_ref, w0_ref, wup_ref, a0_ref, aup_ref, gup_ref, kk_ref, ka_ref, rk_ref,
                 lnw_ref, lnb_ref, o_ref, carry_ref, state_ref, *, tt):
    ch = RWKV_CHUNK
    c2 = 2 * ch
    d = RWKV_DIM
    nch = tt // ch
    npr = RWKV_PAIRS

    @pl.when(pl.program_id(1) == 0)
    def _():
        carry_ref[...] = jnp.zeros_like(carry_ref)
        state_ref[...] = jnp.zeros_like(state_ref)

    p = p_ref[0]
    prev = pltpu.roll(p, 1, axis=0)
    prev = jnp.where(_iota((tt, 1), 0) == 0, carry_ref[0:1, :], prev)
    carry_ref[0:1, :] = p[tt - 1:tt, :]
    pm = p + (prev - p) * mu_ref[...]

    r = pm[:, 0:d]
    k = pm[:, d:2 * d]
    v = pm[:, 2 * d:3 * d]
    wa = pm[:, 3 * d:3 * d + LANES]
    gl = pm[:, 3 * d + LANES:3 * d + 2 * LANES]

    z = w0_ref[...] + _nn(jnp.tanh(wa).astype(BF16), wup_ref[...])
    softplus = jnp.maximum(-z, 0.0) + jnp.log(1.0 + jnp.exp(-jnp.abs(z)))
    lw = -jnp.exp(-softplus - 0.5)
    a = jax.nn.sigmoid(a0_ref[...] + _nn(wa.astype(BF16), aup_ref[...]))
    gate = _nn(jax.nn.sigmoid(gl).astype(BF16), gup_ref[...])

    seg = (jnp.right_shift(_iota((MXU_COLS, MXU_COLS), 0), LOG2_64)
           == jnp.right_shift(_iota((MXU_COLS, MXU_COLS), 1), LOG2_64))
    bd = jnp.where(seg, 1.0, 0.0).astype(BF16)
    kk = k * kk_ref[...]
    kk = kk * lax.rsqrt(jnp.maximum(_seg_sum(kk * kk, bd), 1e-24))
    k2 = k * (1.0 + (a - 1.0) * ka_ref[...])

    ti = _iota((tt, tt), 0)
    tj = _iota((tt, tt), 1)
    same_chunk = jnp.right_shift(ti, LOG2_64) == jnp.right_shift(tj, LOG2_64)
    tri = jnp.where(same_chunk & (tj <= ti), 1.0, 0.0).astype(BF16)
    hi, lo = _split2(lw)
    cum = _nn(tri, hi) + _nn(tri, lo)
    tot = jnp.concatenate([jnp.broadcast_to(cum[(c + 1) * ch - 1:(c + 1) * ch, :], (ch, d))
                           for c in range(nch)], axis=0)
    e_inv = jnp.exp(-cum)
    e_end = jnp.exp(tot - cum)
    dec = jnp.exp(tot)
    bb = kk * a

    def tiles(x):
        return jnp.stack([_stack_heads(x[c * ch:(c + 1) * ch, pr * LANES:(pr + 1) * LANES])
                          for c in range(nch) for pr in range(npr)]).astype(BF16)

    a_t = tiles(-kk * jnp.exp(cum - lw))
    r_t = tiles(r * jnp.exp(cum))
    b_h = tiles(bb * e_inv)
    k_h = tiles(k2 * e_inv)
    bke = jnp.concatenate([tiles(bb * e_end), tiles(k2 * e_end)], axis=1)
    v_t = tiles(v)

    ri = _iota((c2, c2), 0)
    ci = _iota((c2, c2), 1)
    same = jnp.right_shift(ri, LOG2_64) == jnp.right_shift(ci, LOG2_64)
    strict = jnp.where(same & (ci < ri), 1.0, 0.0)
    incl = jnp.where(same & (ci <= ri), 1.0, 0.0)
    ident = jnp.where(ri == ci, 1.0, 0.0)

    scores = _bnt(jnp.concatenate([a_t, r_t], axis=1), jnp.concatenate([b_h, k_h], axis=1))
    m_ab = scores[:, 0:c2, 0:c2] * strict
    m_ak = (scores[:, 0:c2, c2:2 * c2] * strict).astype(BF16)
    m_rbk = jnp.concatenate([scores[:, c2:2 * c2, 0:c2] * incl,
                             scores[:, c2:2 * c2, c2:2 * c2] * incl], axis=2).astype(BF16)
    tinv = ident + m_ab
    mp = m_ab
    for _ in range(5):
        mpb = mp.astype(BF16)
        mp = _bnn(mpb, mpb)
        tinv = tinv + _bnn(tinv.astype(BF16), mp.astype(BF16))
    rhs = jnp.concatenate([a_t, _bnn(m_ak, v_t).astype(BF16)], axis=2)
    wu = _bnn(tinv.astype(BF16), rhs)
    w_t = wu[:, :, 0:c2].astype(BF16)
    u0 = wu[:, :, c2:2 * c2]

    g_st = state_ref[...]
    y_rows = []
    for c in range(nch):
        sl = slice(c * npr, (c + 1) * npr)
        g_b = g_st.astype(BF16)
        u = _bnt(w_t[sl], g_b) + u0[sl]
        uv = jnp.concatenate([u.astype(BF16), v_t[sl]], axis=1)
        y = _bnt(r_t[sl], g_b) + _bnn(m_rbk[sl], uv)
        y = y[:, 0:ch, :] + y[:, ch:c2, :]
        y_rows.append(jnp.concatenate([y[pr] for pr in range(npr)], axis=1))
        dec_c = jnp.stack([dec[c * ch:c * ch + 1, pr * LANES:(pr + 1) * LANES] for pr in range(npr)])
        g_st = g_st * dec_c + _btn(uv, bke[sl])
    state_ref[...] = g_st
    y = jnp.concatenate(y_rows, axis=0)

    inv = 1.0 / RWKV_HEAD_DIM
    mean = _seg_sum(y, bd) * inv
    yc = y - mean
    var = _seg_sum(yc * yc, bd) * inv
    yn = yc * lax.rsqrt(var + GN_EPS) * lnw_ref[...] + lnb_ref[...]
    bonus = _seg_sum(r * k2 * rk_ref[...], bd) * v
    o_ref[0] = ((yn + bonus) * gate).astype(BF16)


def _rwkv_call(p, mu, w0, wup, a0, aup, gup, k_k, k_a, r_k, ln_w, ln_b, tt=256):
    b, s, _ = p.shape
    full = lambda a: pl.BlockSpec(a.shape, lambda i, j: (0,) * a.ndim)
    consts = (mu, w0, wup, a0, aup, gup, k_k, k_a, r_k, ln_w, ln_b)
    return pl.pallas_call(
        functools.partial(_rwkv_kernel, tt=tt),
        grid=(b, s // tt),
        in_specs=[pl.BlockSpec((1, tt, RWKV_MIX_DIM), lambda i, j: (i, j, 0))]
                 + [full(c) for c in consts],
        out_specs=pl.BlockSpec((1, tt, RWKV_DIM), lambda i, j: (i, j, 0)),
        out_shape=jax.ShapeDtypeStruct((b, s, RWKV_DIM), BF16),
        scratch_shapes=[pltpu.VMEM((8, RWKV_MIX_DIM), F32),
                        pltpu.VMEM((RWKV_PAIRS, LANES, LANES), F32)],
        compiler_params=_params("arbitrary", "arbitrary"),
        name="rwkv",
    )(p, *consts)


def _merge_kernel(x_ref, ya_ref, yb_ref, mg_ref, wa_ref, wb_ref, wo_ref, fg_ref, xo_ref, h_ref):
    ga = jax.nn.sigmoid(mg_ref[:, 0:D_MODEL].astype(F32))
    gb = jax.nn.sigmoid(mg_ref[:, D_MODEL:2 * D_MODEL].astype(F32))
    merged = ga * _nn(ya_ref[...], wa_ref[...]) + gb * _nn(yb_ref[...], wb_ref[...])
    xn = x_ref[...] + _nn(merged.astype(BF16), wo_ref[...])
    xo_ref[...] = xn
    ms = jnp.mean(xn * xn, axis=-1, keepdims=True)
    h_ref[...] = (xn * lax.rsqrt(ms + RMS_EPS) * fg_ref[...]).astype(BF16)


def _merge_call(x, ya, yb, mg, wa_pad, wb, wo, fgain):
    b, s, _ = x.shape
    return pl.pallas_call(
        _merge_kernel,
        grid=(b, s // ROW_TILE),
        in_specs=[_row_spec(D_MODEL), _row_spec(Q_PAD), _row_spec(RWKV_DIM), _row_spec(2 * D_MODEL),
                  _const_spec(wa_pad), _const_spec(wb), _const_spec(wo), _const_spec(fgain)],
        out_specs=[_row_spec(D_MODEL), _row_spec(D_MODEL)],
        out_shape=[jax.ShapeDtypeStruct((b, s, D_MODEL), F32),
                   jax.ShapeDtypeStruct((b, s, D_MODEL), BF16)],
        compiler_params=_params("parallel", "parallel"),
        name="merge",
    )(x, ya, yb, mg, wa_pad, wb, wo, fgain)


def _ffn_kernel(x_ref, h_ref, wu_ref, wd_ref, o_ref, *, fc):
    h = h_ref[...]
    acc = x_ref[...]
    for c in range(D_FF // fc):
        up = jnp.maximum(_nn(h, wu_ref[:, c * fc:(c + 1) * fc]), 0.0)
        acc = acc + _nn((up * up).astype(BF16), wd_ref[c * fc:(c + 1) * fc, :])
    o_ref[...] = acc


def _ffn_call(x, h, wu, wd, fc=1024):
    b, s, _ = x.shape
    return pl.pallas_call(
        functools.partial(_ffn_kernel, fc=fc),
        grid=(b, s // ROW_TILE),
        in_specs=[_row_spec(D_MODEL), _row_spec(D_MODEL), _const_spec(wu), _const_spec(wd)],
        out_specs=_row_spec(D_MODEL),
        out_shape=jax.ShapeDtypeStruct((b, s, D_MODEL), F32),
        compiler_params=_params("parallel", "parallel"),
        name="ffn",
    )(x, h, wu, wd)


def _pad_heads_cols(w):
    rows = w.shape[0]
    wh = w.reshape(rows, ATTN_HEADS, HEAD_DIM)
    slots = []
    for j in range(ATTN_HEADS):
        g = j // HEADS_PER_GROUP
        z = jnp.zeros((rows, HEAD_DIM), w.dtype)
        slots.append(jnp.concatenate([wh[:, j], z] if g == 0 else [z, wh[:, j]], axis=1))
    return jnp.concatenate(slots, axis=1)


def _proj_weight(w_in):
    q = _pad_heads_cols(w_in[:, 0:ATTN_DIM])
    kv = w_in[:, ATTN_DIM:ATTN_DIM + 6 * KV_DIM]
    gates = jnp.pad(w_in[:, ATTN_DIM + 6 * KV_DIM:NSA_DIM], ((0, 0), (0, LANES - NSA_GATES)))
    rest = w_in[:, NSA_DIM:]
    return jnp.concatenate([q, kv, gates, rest], axis=1).astype(BF16)


def _values_t(v, tile):
    b, s, _ = v.shape
    vt = v.reshape(b, s // tile, tile, KV_DIM).transpose(0, 1, 3, 2)
    row = jnp.arange(KV_DIM)[:, None] // HEAD_DIM
    one = jnp.ones((), v.dtype)
    return jnp.stack([jnp.where(row == g, vt, one) for g in range(KV_GROUPS)], axis=1)


def _compress_weights(w1, w2):
    w = w1.reshape(2, 2, CMP_STRIDE, HEAD_DIM, CMP_HIDDEN).transpose(0, 2, 3, 1, 4)
    zw = jnp.zeros_like(w)
    w1g = jnp.stack([jnp.stack([w if gg == g else zw for gg in range(KV_GROUPS)], axis=2)
                     for g in range(KV_GROUPS)], axis=1)
    w1g = w1g.reshape(2, KV_GROUPS, CMP_STRIDE * KV_DIM, 2 * CMP_HIDDEN)
    z2 = jnp.zeros_like(w2)
    w2g = jnp.stack([jnp.concatenate([w2 if gg == g else z2 for gg in range(KV_GROUPS)], axis=2)
                     for g in range(KV_GROUPS)], axis=1)
    return w1g.astype(BF16), w2g.astype(BF16)


def kernel(x, mix_norm, w_in, q_gain, k_gain, cmp_pos, cmp_w1, cmp_w2, w_attn_branch, tok_mix, w0,
           w_lora_up, a0, a_lora_up, g_lora_up, k_k, k_a, r_k, ln_x_w, ln_x_b, w_rwkv_branch, w_out,
           ffn_norm, w_ffn_up, w_ffn_down):
    b, s, _ = x.shape
    depth = w_in.shape[0]
    row = lambda v: v.reshape(1, -1)
    for l in range(depth):
        qg_pad = _pad_heads_cols(jnp.tile(q_gain[l], ATTN_HEADS).reshape(1, ATTN_DIM))
        kg2 = jnp.tile(k_gain[l, 1:3], (1, KV_GROUPS))
        q, kc, vc, ks, vs, kw, vw, gates, rw, mg = _proj_call(
            x, row(mix_norm[l]), _proj_weight(w_in[l]), qg_pad, kg2)

        w1g, w2g = _compress_weights(cmp_w1[l], cmp_w2[l])
        k_cmp, v_cmp = _compress_call(kc, vc, w1g, cmp_w1[l].astype(BF16), w2g,
                                      cmp_pos[l].reshape(2, 1, CMP_BLOCK * HEAD_DIM),
                                      jnp.tile(k_gain[l, 0], KV_GROUPS).reshape(1, KV_DIM))

        ya = _nsa_call(q, gates, k_cmp, v_cmp.transpose(0, 2, 1), ks,
                       _values_t(vs, NSA_KEY_TILE), kw, _values_t(vw, NSA_QUERY_TILE))

        zero = jnp.zeros((W_LORA, RWKV_DIM), F32)
        wup = jnp.concatenate([w_lora_up[l], zero], axis=0).astype(BF16)
        aup = jnp.concatenate([zero, a_lora_up[l]], axis=0).astype(BF16)
        yb = _rwkv_call(rw, row(tok_mix[l]), row(w0[l]), wup, row(a0[l]), aup,
                        g_lora_up[l].astype(BF16), row(k_k[l]), row(k_a[l]), row(r_k[l]),
                        row(ln_x_w[l]), row(ln_x_b[l]))

        wa_pad = _pad_heads_cols(w_attn_branch[l].T).T.astype(BF16)
        x, h = _merge_call(x, ya, yb, mg, wa_pad, w_rwkv_branch[l].astype(BF16),
                           w_out[l].astype(BF16), row(ffn_norm[l]))
        x = _ffn_call(x, h, w_ffn_up[l].astype(BF16), w_ffn_down[l].astype(BF16))
    return x
```

```python
import functools

import jax
import jax.numpy as jnp
import numpy as np
from jax import lax
from jax.experimental import pallas as pl
from jax.experimental.pallas import tpu as pltpu

F32 = jnp.float32
BF16 = jnp.bfloat16
I32 = jnp.int32

D_MODEL = 1024
ATTN_HEADS = 8
HEAD_DIM = 64
KV_GROUPS = 2
HEADS_PER_GROUP = ATTN_HEADS // KV_GROUPS
ATTN_DIM = ATTN_HEADS * HEAD_DIM
KV_DIM = KV_GROUPS * HEAD_DIM
N_NSA_BRANCHES = 3
CMP_BLOCK = 32
CMP_STRIDE = 16
CMP_HIDDEN = 2 * HEAD_DIM
SLC_BLOCK = 64
N_SELECT = 16
WINDOW = 512
FORCE_BONUS = 1000.0
RWKV_HEADS = 8
RWKV_HEAD_DIM = 64
RWKV_DIM = RWKV_HEADS * RWKV_HEAD_DIM
W_LORA = 64
A_LORA = 64
G_LORA = 128
RWKV_MIX_DIM = 3 * RWKV_DIM + W_LORA + A_LORA + G_LORA
D_FF = 4 * D_MODEL
NSA_GATES = N_NSA_BRANCHES * ATTN_HEADS
NSA_DIM = ATTN_DIM + 6 * KV_DIM + NSA_GATES
RMS_EPS = 1e-6
GN_EPS = 64e-5
NEG_INF = -1e30

LOG2_64 = 6
LANES = 128
SUBLANES = 8
MXU_COLS = 256
Q_SCALE = HEAD_DIM ** -0.5 * 1.4426950408889634
Q_PAD = ATTN_HEADS * LANES
RWKV_PAIRS = RWKV_HEADS // 2
RWKV_CHUNK = 64
ROW_TILE = 512
NSA_QUERY_TILE = 128
NSA_KEY_TILE = 512
VMEM_LIMIT = 48 * 1024 * 1024

C_Q = 0
C_KV = C_Q + Q_PAD
C_GATE = C_KV + 6 * KV_DIM
C_RWKV = C_GATE + LANES
C_MERGE = C_RWKV + RWKV_MIX_DIM
C_TOTAL = C_MERGE + 2 * D_MODEL


def _nn(a, b):
    return lax.dot_general(a, b, (((1,), (0,)), ((), ())), preferred_element_type=F32)


def _nt(a, b):
    return lax.dot_general(a, b, (((1,), (1,)), ((), ())), preferred_element_type=F32)


def _bnn(a, b):
    return lax.dot_general(a, b, (((2,), (1,)), ((0,), (0,))), preferred_element_type=F32)


def _bnt(a, b):
    return lax.dot_general(a, b, (((2,), (2,)), ((0,), (0,))), preferred_element_type=F32)


def _btn(a, b):
    return lax.dot_general(a, b, (((1,), (1,)), ((0,), (0,))), preferred_element_type=F32)


def _split2(x):
    hi = x.astype(BF16)
    lo = (x - hi.astype(F32)).astype(BF16)
    return hi, lo


def _iota(shape, dim):
    return lax.broadcasted_iota(I32, shape, dim)


def _params(*sem):
    return pltpu.CompilerParams(dimension_semantics=sem, vmem_limit_bytes=VMEM_LIMIT)


def _group_rms(x, gain):
    lane = _iota((1, LANES), 1)
    sq = x * x
    lo = lane < HEAD_DIM
    s0 = jnp.sum(jnp.where(lo, sq, 0.0), axis=-1, keepdims=True)
    s1 = jnp.sum(jnp.where(lo, 0.0, sq), axis=-1, keepdims=True)
    ms = jnp.where(lo, s0, s1) * (1.0 / HEAD_DIM)
    return x * lax.rsqrt(ms + RMS_EPS) * gain


def _proj_kernel(x_ref, g_ref, w_ref, qg_ref, kg_ref,
                 q_ref, kc_ref, vc_ref, ks_ref, vs_ref, kw_ref, vw_ref, gt_ref, rw_ref, mg_ref):
    x = x_ref[...]
    ms = jnp.mean(x * x, axis=-1, keepdims=True)
    u = (x * lax.rsqrt(ms + RMS_EPS) * g_ref[...]).astype(BF16)

    def col(c0, width):
        return _nn(u, w_ref[:, c0:c0 + width])

    for jp in range(ATTN_HEADS // 2):
        qq = col(C_Q + jp * MXU_COLS, MXU_COLS)
        for half in range(2):
            j = 2 * jp + half
            qj = qq[:, half * LANES:(half + 1) * LANES]
            msq = jnp.sum(qj * qj, axis=-1, keepdims=True) * (1.0 / HEAD_DIM)
            qn = qj * lax.rsqrt(msq + RMS_EPS) * qg_ref[:, j * LANES:(j + 1) * LANES]
            q_ref[:, j * LANES:(j + 1) * LANES] = (qn * Q_SCALE).astype(BF16)
    kvc = col(C_KV, MXU_COLS)
    kc_ref[...] = kvc[:, 0:KV_DIM]
    vc_ref[...] = kvc[:, KV_DIM:2 * KV_DIM]
    kvs = col(C_KV + MXU_COLS, MXU_COLS)
    ks_ref[...] = _group_rms(kvs[:, 0:KV_DIM], kg_ref[0:1, :]).astype(BF16)
    vs_ref[...] = kvs[:, KV_DIM:2 * KV_DIM].astype(BF16)
    kvw = col(C_KV + 2 * MXU_COLS, MXU_COLS)
    kw_ref[...] = _group_rms(kvw[:, 0:KV_DIM], kg_ref[1:2, :]).astype(BF16)
    vw_ref[...] = kvw[:, KV_DIM:2 * KV_DIM].astype(BF16)
    gt_ref[...] = col(C_GATE, LANES)
    for c in range(RWKV_MIX_DIM // MXU_COLS):
        rw_ref[:, c * MXU_COLS:(c + 1) * MXU_COLS] = col(C_RWKV + c * MXU_COLS, MXU_COLS)
    for c in range(2 * D_MODEL // 512):
        mg_ref[:, c * 512:(c + 1) * 512] = col(C_MERGE + c * 512, 512).astype(BF16)


def _row_spec(width):
    return pl.BlockSpec((None, ROW_TILE, width), lambda i, j: (i, j, 0))


def _const_spec(a, buffers=2):
    return pl.BlockSpec(a.shape, lambda i, j: (0,) * a.ndim, pipeline_mode=pl.Buffered(buffers))


def _proj_call(x, gain, w_all, qg_pad, kg2):
    b, s, _ = x.shape
    widths = [(Q_PAD, BF16), (KV_DIM, F32), (KV_DIM, F32), (KV_DIM, BF16), (KV_DIM, BF16),
              (KV_DIM, BF16), (KV_DIM, BF16), (LANES, F32), (RWKV_MIX_DIM, F32), (2 * D_MODEL, BF16)]
    return pl.pallas_call(
        _proj_kernel,
        grid=(b, s // ROW_TILE),
        in_specs=[_row_spec(D_MODEL), _const_spec(gain), _const_spec(w_all, buffers=1),
                  _const_spec(qg_pad), _const_spec(kg2)],
        out_specs=[_row_spec(w) for w, _ in widths],
        out_shape=[jax.ShapeDtypeStruct((b, s, w), dt) for w, dt in widths],
        compiler_params=_params("parallel", "parallel"),
        name="proj",
    )(x, gain, w_all, qg_pad, kg2)


def _compress_kernel(kc_ref, vc_ref, w1g_ref, w1_ref, w2g_ref, pos_ref, kg_ref, ko_ref, vo_ref, *, nc):
    rowi = _iota((nc, 1), 0)
    for which, (src, dst) in enumerate(((kc_ref, ko_ref), (vc_ref, vo_ref))):
        xcat = jnp.concatenate([src[0, pl.ds(l, nc, stride=CMP_STRIDE), :] for l in range(CMP_STRIDE)],
                               axis=1).astype(BF16)
        pos8 = jnp.broadcast_to(pos_ref[which], (SUBLANES, CMP_BLOCK * HEAD_DIM)).astype(BF16)
        bias = _nn(pos8, w1_ref[which])[0:1, :]
        out = jnp.zeros((nc, KV_DIM), F32)
        for g in range(KV_GROUPS):
            hab = _nn(xcat, w1g_ref[which, g])
            hid = hab[:, 0:CMP_HIDDEN] + pltpu.roll(hab[:, CMP_HIDDEN:2 * CMP_HIDDEN], nc - 1, axis=0) + bias
            out = out + _nn(jax.nn.gelu(hid).astype(BF16), w2g_ref[which, g])
        if which == 0:
            out = _group_rms(out, kg_ref[...])
        dst[0] = jnp.where(rowi < nc - 1, out, 0.0).astype(BF16)


def _compress_call(kc, vc, w1g, w1, w2g, pos_flat, kgain2):
    b, s, _ = kc.shape
    nc = s // CMP_STRIDE
    seq = pl.BlockSpec((1, s, KV_DIM), lambda i: (i, 0, 0))
    full = lambda a: pl.BlockSpec(a.shape, lambda i: (0,) * a.ndim)
    out = pl.BlockSpec((1, nc, KV_DIM), lambda i: (i, 0, 0))
    return pl.pallas_call(
        functools.partial(_compress_kernel, nc=nc),
        grid=(b,),
        in_specs=[seq, seq, full(w1g), full(w1), full(w2g), full(pos_flat), full(kgain2)],
        out_specs=[out, out],
        out_shape=[jax.ShapeDtypeStruct((b, nc, KV_DIM), BF16)] * 2,
        compiler_params=_params("parallel"),
        name="compress",
    )(kc, vc, w1g, w1, w2g, pos_flat, kgain2)


def _nsa_kernel(q_ref, gt_ref, kc_ref, vct_ref, ks_ref, vst_ref, kw_ref, vwt_ref, y_ref,
                m_s, acc_s, sb_s, *, tq, tk, nb, nc):
    nh = ATTN_HEADS
    hp = HEADS_PER_GROUP
    cols = hp * tq
    wk = WINDOW + tq
    bpt = tk // SLC_BLOCK
    qi = pl.program_id(1)
    q0 = qi * tq
    tlane = q0 + _iota((1, tq), 1)
    tile_h = lambda a, reps: jnp.concatenate([a] * reps, axis=1)
    eye_d = jnp.where(_iota((LANES, LANES), 0) == _iota((LANES, LANES), 1), 1.0, 0.0).astype(BF16)
    eye_q = jnp.where(_iota((tq, tq), 0) == _iota((tq, tq), 1), 1.0, 0.0).astype(BF16)
    drow = _iota((LANES, 1), 0)
    own = [(drow >= g * HEAD_DIM) & (drow < (g + 1) * HEAD_DIM) for g in range(KV_GROUPS)]

    qall = jnp.concatenate([q_ref[0, :, j * LANES:(j + 1) * LANES] for j in range(nh)], axis=0)
    q_t = _nt(eye_d, qall).astype(BF16)

    ncol = _iota((nc, 1), 0)
    cvalid = (ncol * CMP_STRIDE + (CMP_BLOCK - 1) <= tlane) & (ncol < nc - 1)
    s_c = _nn(kc_ref[0], q_t) + tile_h(jnp.where(cvalid, 0.0, NEG_INF), nh)
    e_c = jnp.exp2(s_c - jnp.max(s_c, axis=0, keepdims=True))
    p_c = e_c * (1.0 / jnp.sum(e_c, axis=0, keepdims=True)) * tile_h(jnp.where(cvalid, 1.0, 0.0), nh)
    p_cb = p_c.astype(BF16)
    o_c = [_nn(vct_ref[0], p_cb[:, g * cols:(g + 1) * cols]) for g in range(KV_GROUPS)]

    jcol = _iota((nb, 1), 0)
    ncmp = _iota((nb, nc), 1)
    ov = ((ncmp * CMP_STRIDE <= jcol * SLC_BLOCK + (SLC_BLOCK - 1))
          & (ncmp * CMP_STRIDE + (CMP_BLOCK - 1) >= jcol * SLC_BLOCK))
    ov = jnp.where(ov, 1.0, 0.0).astype(BF16)
    cur = jnp.right_shift(tlane, LOG2_64)
    forced = (jcol == 0) | (jcol == cur) | (jcol == cur - 1)
    causal_blk = jcol * SLC_BLOCK <= tlane
    for g in range(KV_GROUPS):
        c0 = g * cols
        psum = (p_c[:, c0:c0 + tq] + p_c[:, c0 + tq:c0 + 2 * tq]
                + p_c[:, c0 + 2 * tq:c0 + 3 * tq] + p_c[:, c0 + 3 * tq:c0 + 4 * tq])
        hi, lo = _split2(psum)
        imp = _nn(ov, hi) + _nn(ov, lo)
        imp = jnp.where(causal_blk, imp + jnp.where(forced, FORCE_BONUS, 0.0), -1.0)
        groups = [imp[r:r + SUBLANES, :] for r in range(0, nb, SUBLANES)]
        ranks = [jnp.zeros((SUBLANES, tq), F32) for _ in groups]
        jsub = _iota((SUBLANES, 1), 0)
        for i in range(nb):
            ri = imp[i:i + 1, :]
            for gi, grp in enumerate(groups):
                r0 = gi * SUBLANES
                if r0 + SUBLANES - 1 < i:
                    ahead = jnp.where(ri > grp, 1.0, 0.0)
                elif r0 > i:
                    ahead = jnp.where(ri >= grp, 1.0, 0.0)
                else:
                    ahead = jnp.where(jsub > i - r0, jnp.where(ri >= grp, 1.0, 0.0),
                                      jnp.where(ri > grp, 1.0, 0.0))
                ranks[gi] = ranks[gi] + ahead
        rank = jnp.concatenate(ranks, axis=0)
        sb_s[g] = jnp.where(rank < float(min(N_SELECT, nb)), 0.0, NEG_INF)

    m_s[...] = jnp.full(m_s.shape, NEG_INF, F32)
    acc_s[...] = jnp.zeros(acc_s.shape, F32)

    def sweep(tiles):
        scores = [_nn(ks_ref[0, pl.ds(pl.multiple_of(kt * tk, tk), tk), :], q_t).astype(BF16)
                  for kt in tiles]
        m_run = m_s[0:1, :]
        acc_run = [acc_s[:, g * cols:(g + 1) * cols] for g in range(KV_GROUPS)]
        m_out = [None] * KV_GROUPS
        for kt, s in zip(tiles, scores):
            causal = (kt * tk + _iota((tk, 1), 0)) <= tlane
            for g in range(KV_GROUPS):
                cs = slice(g * cols, (g + 1) * cols)
                bias = jnp.concatenate(
                    [jnp.broadcast_to(sb_s[g, pl.ds(kt * bpt + i, 1), :], (SLC_BLOCK, tq))
                     for i in range(bpt)], axis=0)
                sm = s[:, cs] + tile_h(jnp.where(causal, bias, NEG_INF).astype(BF16), hp)
                m_prev = m_run[:, cs] if m_out[g] is None else m_out[g]
                m_new = jnp.maximum(m_prev, jnp.max(sm, axis=0, keepdims=True).astype(F32))
                p = jnp.exp2(sm - m_new.astype(BF16))
                acc_run[g] = jnp.exp2(m_prev - m_new) * acc_run[g] + _nn(vst_ref[0, g, kt], p)
                m_out[g] = m_new
        for g in range(KV_GROUPS):
            cs = slice(g * cols, (g + 1) * cols)
            m_s[:, cs] = jnp.broadcast_to(m_out[g], (SUBLANES, cols))
            acc_s[:, cs] = acc_run[g]

    n_tiles = qi // (tk // tq) + 1

    def pair(i, _):
        sweep([2 * i, 2 * i + 1])
        return 0

    lax.fori_loop(0, n_tiles // 2, pair, 0)

    @pl.when(n_tiles % 2 == 1)
    def _():
        sweep([n_tiles - 1])

    w0 = pl.multiple_of(jnp.maximum(q0 - WINDOW, 0), tq)
    dist = tlane - (w0 + _iota((wk, 1), 0))
    s_w = (_nn(kw_ref[0, pl.ds(w0, wk), :], q_t).astype(BF16)
           + tile_h(jnp.where((dist >= 0) & (dist < WINDOW), 0.0, NEG_INF).astype(BF16), nh))
    p_w = jnp.exp2(s_w - jnp.max(s_w, axis=0, keepdims=True))
    wb = w0 // tq

    g_hi, g_lo = _split2(jax.nn.sigmoid(gt_ref[0]))
    gate_t = _nt(eye_d, g_hi) + _nt(eye_d, g_lo)
    for g in range(KV_GROUPS):
        cs = slice(g * cols, (g + 1) * cols)
        den_row = (1 - g) * HEAD_DIM
        acc = acc_s[:, cs]
        o_s = acc * (1.0 / acc[den_row:den_row + 1, :])
        v_w = jnp.concatenate([vwt_ref[0, g, wb + i] for i in range(wk // tq)], axis=1)
        acc_w = _nn(v_w, p_w[:, cs])
        o_w = acc_w * (1.0 / acc_w[den_row:den_row + 1, :])
        for h in range(hp):
            j = g * hp + h
            r0 = j * N_NSA_BRANCHES
            hs = slice(h * tq, (h + 1) * tq)
            o = (gate_t[r0:r0 + 1, :] * o_c[g][:, hs] + gate_t[r0 + 1:r0 + 2, :] * o_s[:, hs]
                 + gate_t[r0 + 2:r0 + 3, :] * o_w[:, hs])
            o = jnp.where(own[g], o, 0.0).astype(BF16)
            y_ref[0, :, j * LANES:(j + 1) * LANES] = _nt(eye_q, o).astype(BF16)


def _nsa_call(q, gates, kc, vct, ks, vst, kw, vwt):
    b, s, _ = q.shape
    tq = NSA_QUERY_TILE
    nb = s // SLC_BLOCK
    nc = kc.shape[1]
    tk = vst.shape[-1]
    tile = lambda w: pl.BlockSpec((1, tq, w), lambda i, j: (i, j, 0))
    seq = lambda a: pl.BlockSpec((1,) + a.shape[1:], lambda i, j: (i,) + (0,) * (a.ndim - 1))
    return pl.pallas_call(
        functools.partial(_nsa_kernel, tq=tq, tk=tk, nb=nb, nc=nc),
        grid=(b, s // tq),
        in_specs=[tile(Q_PAD), tile(LANES), seq(kc), seq(vct), seq(ks), seq(vst), seq(kw), seq(vwt)],
        out_specs=tile(Q_PAD),
        out_shape=jax.ShapeDtypeStruct((b, s, Q_PAD), BF16),
        scratch_shapes=[pltpu.VMEM((8, ATTN_HEADS * tq), F32),
                        pltpu.VMEM((LANES, ATTN_HEADS * tq), F32),
                        pltpu.VMEM((KV_GROUPS, nb, tq), F32)],
        compiler_params=_params("parallel", "arbitrary"),
        name="nsa",
    )(q, gates, kc, vct, ks, vst, kw, vwt)


def _seg_sum(x, bd):
    hi, lo = _split2(x)
    w = bd.shape[0]
    return jnp.concatenate([_nn(hi[:, c:c + w], bd) + _nn(lo[:, c:c + w], bd)
                            for c in range(0, x.shape[1], w)], axis=1)


def _stack_heads(x):
    lane = _iota((1, LANES), 1)
    lo = lane < RWKV_HEAD_DIM
    return jnp.concatenate([jnp.where(lo, x, 0.0), jnp.where(lo, 0.0, x)], axis=0)


def _rwkv_kernel(p_ref, mu_ref, w0_ref, wup_ref, a0_ref, aup_ref, gup_ref, kk_ref, ka_ref, rk_ref,
                 lnw_ref, lnb_ref, o_ref, carry_ref, state_ref, *, tt):
    ch = RWKV_CHUNK
    c2 = 2 * ch
    d = RWKV_DIM
    nch = tt // ch
    npr = RWKV_PAIRS

    @pl.when(pl.program_id(1) == 0)
    def _():
        carry_ref[...] = jnp.zeros_like(carry_ref)
        state_ref[...] = jnp.zeros_like(state_ref)

    p = p_ref[0]
    prev = pltpu.roll(p, 1, axis=0)
    prev = jnp.where(_iota((tt, 1), 0) == 0, carry_ref[0:1, :], prev)
    carry_ref[0:1, :] = p[tt - 1:tt, :]
    pm = p + (prev - p) * mu_ref[...]

    r = pm[:, 0:d]
    k = pm[:, d:2 * d]
    v = pm[:, 2 * d:3 * d]
    wa = pm[:, 3 * d:3 * d + LANES]
    gl = pm[:, 3 * d + LANES:3 * d + 2 * LANES]

    z = w0_ref[...] + _nn(jnp.tanh(wa).astype(BF16), wup_ref[...])
    softplus = jnp.maximum(-z, 0.0) + jnp.log(1.0 + jnp.exp(-jnp.abs(z)))
    lw = -jnp.exp(-softplus - 0.5)
    a = jax.nn.sigmoid(a0_ref[...] + _nn(wa.astype(BF16), aup_ref[...]))
    gate = _nn(jax.nn.sigmoid(gl).astype(BF16), gup_ref[...])

    seg = (jnp.right_shift(_iota((MXU_COLS, MXU_COLS), 0), LOG2_64)
           == jnp.right_shift(_iota((MXU_COLS, MXU_COLS), 1), LOG2_64))
    bd = jnp.where(seg, 1.0, 0.0).astype(BF16)
    kk = k * kk_ref[...]
    kk = kk * lax.rsqrt(jnp.maximum(_seg_sum(kk * kk, bd), 1e-24))
    k2 = k * (1.0 + (a - 1.0) * ka_ref[...])

    ti = _iota((tt, tt), 0)
    tj = _iota((tt, tt), 1)
    same_chunk = jnp.right_shift(ti, LOG2_64) == jnp.right_shift(tj, LOG2_64)
    tri = jnp.where(same_chunk & (tj <= ti), 1.0, 0.0).astype(BF16)
    hi, lo = _split2(lw)
    cum = _nn(tri, hi) + _nn(tri, lo)
    tot = jnp.concatenate([jnp.broadcast_to(cum[(c + 1) * ch - 1:(c + 1) * ch, :], (ch, d))
                           for c in range(nch)], axis=0)
    e_inv = jnp.exp(-cum)
    e_end = jnp.exp(tot - cum)
    dec = jnp.exp(tot)
    bb = kk * a

    def tiles(x):
        return jnp.stack([_stack_heads(x[c * ch:(c + 1) * ch, pr * LANES:(pr + 1) * LANES])
                          for c in range(nch) for pr in range(npr)]).astype(BF16)

    a_t = tiles(-kk * jnp.exp(cum - lw))
    r_t = tiles(r * jnp.exp(cum))
    b_h = tiles(bb * e_inv)
    k_h = tiles(k2 * e_inv)
    bke = jnp.concatenate([tiles(bb * e_end), tiles(k2 * e_end)], axis=1)
    v_t = tiles(v)

    ri = _iota((c2, c2), 0)
    ci = _iota((c2, c2), 1)
    same = jnp.right_shift(ri, LOG2_64) == jnp.right_shift(ci, LOG2_64)
    strict = jnp.where(same & (ci < ri), 1.0, 0.0)
    incl = jnp.where(same & (ci <= ri), 1.0, 0.0)
    ident = jnp.where(ri == ci, 1.0, 0.0)

    scores = _bnt(jnp.concatenate([a_t, r_t], axis=1), jnp.concatenate([b_h, k_h], axis=1))
    m_ab = scores[:, 0:c2, 0:c2] * strict
    m_ak = (scores[:, 0:c2, c2:2 * c2] * strict).astype(BF16)
    m_rbk = jnp.concatenate([scores[:, c2:2 * c2, 0:c2] * incl,
                             scores[:, c2:2 * c2, c2:2 * c2] * incl], axis=2).astype(BF16)
    tinv = ident + m_ab
    mp = m_ab
    for _ in range(5):
        mpb = mp.astype(BF16)
        mp = _bnn(mpb, mpb)
        tinv = tinv + _bnn(tinv.astype(BF16), mp.astype(BF16))
    rhs = jnp.concatenate([a_t, _bnn(m_ak, v_t).astype(BF16)], axis=2)
    wu = _bnn(tinv.astype(BF16), rhs)
    w_t = wu[:, :, 0:c2].astype(BF16)
    u0 = wu[:, :, c2:2 * c2]

    g_st = state_ref[...]
    y_rows = []
    for c in range(nch):
        sl = slice(c * npr, (c + 1) * npr)
        g_b = g_st.astype(BF16)
        u = _bnt(w_t[sl], g_b) + u0[sl]
        uv = jnp.concatenate([u.astype(BF16), v_t[sl]], axis=1)
        y = _bnt(r_t[sl], g_b) + _bnn(m_rbk[sl], uv)
        y = y[:, 0:ch, :] + y[:, ch:c2, :]
        y_rows.append(jnp.concatenate([y[pr] for pr in range(npr)], axis=1))
        dec_c = jnp.stack([dec[c * ch:c * ch + 1, pr * LANES:(pr + 1) * LANES] for pr in range(npr)])
        g_st = g_st * dec_c + _btn(uv, bke[sl])
    state_ref[...] = g_st
    y = jnp.concatenate(y_rows, axis=0)

    inv = 1.0 / RWKV_HEAD_DIM
    mean = _seg_sum(y, bd) * inv
    yc = y - mean
    var = _seg_sum(yc * yc, bd) * inv
    yn = yc * lax.rsqrt(var + GN_EPS) * lnw_ref[...] + lnb_ref[...]
    bonus = _seg_sum(r * k2 * rk_ref[...], bd) * v
    o_ref[0] = ((yn + bonus) * gate).astype(BF16)


def _rwkv_call(p, mu, w0, wup, a0, aup, gup, k_k, k_a, r_k, ln_w, ln_b, tt=256):
    b, s, _ = p.shape
    full = lambda a: pl.BlockSpec(a.shape, lambda i, j: (0,) * a.ndim)
    consts = (mu, w0, wup, a0, aup, gup, k_k, k_a, r_k, ln_w, ln_b)
    return pl.pallas_call(
        functools.partial(_rwkv_kernel, tt=tt),
        grid=(b, s // tt),
        in_specs=[pl.BlockSpec((1, tt, RWKV_MIX_DIM), lambda i, j: (i, j, 0))]
                 + [full(c) for c in consts],
        out_specs=pl.BlockSpec((1, tt, RWKV_DIM), lambda i, j: (i, j, 0)),
        out_shape=jax.ShapeDtypeStruct((b, s, RWKV_DIM), BF16),
        scratch_shapes=[pltpu.VMEM((8, RWKV_MIX_DIM), F32),
                        pltpu.VMEM((RWKV_PAIRS, LANES, LANES), F32)],
        compiler_params=_params("arbitrary", "arbitrary"),
        name="rwkv",
    )(p, *consts)


def _merge_kernel(x_ref, ya_ref, yb_ref, mg_ref, wa_ref, wb_ref, wo_ref, fg_ref, xo_ref, h_ref):
    ga = jax.nn.sigmoid(mg_ref[:, 0:D_MODEL].astype(F32))
    gb = jax.nn.sigmoid(mg_ref[:, D_MODEL:2 * D_MODEL].astype(F32))
    merged = ga * _nn(ya_ref[...], wa_ref[...]) + gb * _nn(yb_ref[...], wb_ref[...])
    xn = x_ref[...] + _nn(merged.astype(BF16), wo_ref[...])
    xo_ref[...] = xn
    ms = jnp.mean(xn * xn, axis=-1, keepdims=True)
    h_ref[...] = (xn * lax.rsqrt(ms + RMS_EPS) * fg_ref[...]).astype(BF16)


def _merge_call(x, ya, yb, mg, wa_pad, wb, wo, fgain):
    b, s, _ = x.shape
    return pl.pallas_call(
        _merge_kernel,
        grid=(b, s // ROW_TILE),
        in_specs=[_row_spec(D_MODEL), _row_spec(Q_PAD), _row_spec(RWKV_DIM), _row_spec(2 * D_MODEL),
                  _const_spec(wa_pad), _const_spec(wb), _const_spec(wo), _const_spec(fgain)],
        out_specs=[_row_spec(D_MODEL), _row_spec(D_MODEL)],
        out_shape=[jax.ShapeDtypeStruct((b, s, D_MODEL), F32),
                   jax.ShapeDtypeStruct((b, s, D_MODEL), BF16)],
        compiler_params=_params("parallel", "parallel"),
        name="merge",
    )(x, ya, yb, mg, wa_pad, wb, wo, fgain)


def _ffn_kernel(x_ref, h_ref, wu_ref, wd_ref, o_ref, *, fc):
    h = h_ref[...]
    acc = x_ref[...]
    for c in range(D_FF // fc):
        up = jnp.maximum(_nn(h, wu_ref[:, c * fc:(c + 1) * fc]), 0.0)
        acc = acc + _nn((up * up).astype(BF16), wd_ref[c * fc:(c + 1) * fc, :])
    o_ref[...] = acc


def _ffn_call(x, h, wu, wd, fc=1024):
    b, s, _ = x.shape
    return pl.pallas_call(
        functools.partial(_ffn_kernel, fc=fc),
        grid=(b, s // ROW_TILE),
        in_specs=[_row_spec(D_MODEL), _row_spec(D_MODEL), _const_spec(wu), _const_spec(wd)],
        out_specs=_row_spec(D_MODEL),
        out_shape=jax.ShapeDtypeStruct((b, s, D_MODEL), F32),
        compiler_params=_params("parallel", "parallel"),
        name="ffn",
    )(x, h, wu, wd)


def _pad_heads_cols(w):
    rows = w.shape[0]
    wh = w.reshape(rows, ATTN_HEADS, HEAD_DIM)
    slots = []
    for j in range(ATTN_HEADS):
        g = j // HEADS_PER_GROUP
        z = jnp.zeros((rows, HEAD_DIM), w.dtype)
        slots.append(jnp.concatenate([wh[:, j], z] if g == 0 else [z, wh[:, j]], axis=1))
    return jnp.concatenate(slots, axis=1)


def _proj_weight(w_in):
    q = _pad_heads_cols(w_in[:, 0:ATTN_DIM])
    kv = w_in[:, ATTN_DIM:ATTN_DIM + 6 * KV_DIM]
    gates = jnp.pad(w_in[:, ATTN_DIM + 6 * KV_DIM:NSA_DIM], ((0, 0), (0, LANES - NSA_GATES)))
    rest = w_in[:, NSA_DIM:]
    return jnp.concatenate([q, kv, gates, rest], axis=1).astype(BF16)


def _values_t(v, tile):
    b, s, _ = v.shape
    vt = v.reshape(b, s // tile, tile, KV_DIM).transpose(0, 1, 3, 2)
    row = jnp.arange(KV_DIM)[:, None] // HEAD_DIM
    one = jnp.ones((), v.dtype)
    return jnp.stack([jnp.where(row == g, vt, one) for g in range(KV_GROUPS)], axis=1)


def _compress_weights(w1, w2):
    w = w1.reshape(2, 2, CMP_STRIDE, HEAD_DIM, CMP_HIDDEN).transpose(0, 2, 3, 1, 4)
    zw = jnp.zeros_like(w)
    w1g = jnp.stack([jnp.stack([w if gg == g else zw for gg in range(KV_GROUPS)], axis=2)
                     for g in range(KV_GROUPS)], axis=1)
    w1g = w1g.reshape(2, KV_GROUPS, CMP_STRIDE * KV_DIM, 2 * CMP_HIDDEN)
    z2 = jnp.zeros_like(w2)
    w2g = jnp.stack([jnp.concatenate([w2 if gg == g else z2 for gg in range(KV_GROUPS)], axis=2)
                     for g in range(KV_GROUPS)], axis=1)
    return w1g.astype(BF16), w2g.astype(BF16)


def kernel(x, mix_norm, w_in, q_gain, k_gain, cmp_pos, cmp_w1, cmp_w2, w_attn_branch, tok_mix, w0,
           w_lora_up, a0, a_lora_up, g_lora_up, k_k, k_a, r_k, ln_x_w, ln_x_b, w_rwkv_branch, w_out,
           ffn_norm, w_ffn_up, w_ffn_down):
    b, s, _ = x.shape
    depth = w_in.shape[0]
    row = lambda v: v.reshape(1, -1)
    for l in range(depth):
        qg_pad = _pad_heads_cols(jnp.tile(q_gain[l], ATTN_HEADS).reshape(1, ATTN_DIM))
        kg2 = jnp.tile(k_gain[l, 1:3], (1, KV_GROUPS))
        q, kc, vc, ks, vs, kw, vw, gates, rw, mg = _proj_call(
            x, row(mix_norm[l]), _proj_weight(w_in[l]), qg_pad, kg2)

        w1g, w2g = _compress_weights(cmp_w1[l], cmp_w2[l])
        k_cmp, v_cmp = _compress_call(kc, vc, w1g, cmp_w1[l].astype(BF16), w2g,
                                      cmp_pos[l].reshape(2, 1, CMP_BLOCK * HEAD_DIM),
                                      jnp.tile(k_gain[l, 0], KV_GROUPS).reshape(1, KV_DIM))

        ya = _nsa_call(q, gates, k_cmp, v_cmp.transpose(0, 2, 1), ks,
                       _values_t(vs, NSA_KEY_TILE), kw, _values_t(vw, NSA_QUERY_TILE))

        zero = jnp.zeros((W_LORA, RWKV_DIM), F32)
        wup = jnp.concatenate([w_lora_up[l], zero], axis=0).astype(BF16)
        aup = jnp.concatenate([zero, a_lora_up[l]], axis=0).astype(BF16)
        yb = _rwkv_call(rw, row(tok_mix[l]), row(w0[l]), wup, row(a0[l]), aup,
                        g_lora_up[l].astype(BF16), row(k_k[l]), row(k_a[l]), row(r_k[l]),
                        row(ln_x_w[l]), row(ln_x_b[l]))

        wa_pad = _pad_heads_cols(w_attn_branch[l].T).T.astype(BF16)
        x, h = _merge_call(x, ya, yb, mg, wa_pad, w_rwkv_branch[l].astype(BF16),
                           w_out[l].astype(BF16), row(ffn_norm[l]))
        x = _ffn_call(x, h, w_ffn_up[l].astype(BF16), w_ffn_down[l].astype(BF16))
    return x
```

```python
import functools

import jax
import jax.numpy as jnp
import numpy as np
from jax import lax
from jax.experimental import pallas as pl
from jax.experimental.pallas import tpu as pltpu

F32 = jnp.float32
BF16 = jnp.bfloat16
I32 = jnp.int32

D_MODEL = 1024
ATTN_HEADS = 8
HEAD_DIM = 64
KV_GROUPS = 2
HEADS_PER_GROUP = ATTN_HEADS // KV_GROUPS
ATTN_DIM = ATTN_HEADS * HEAD_DIM
KV_DIM = KV_GROUPS * HEAD_DIM
N_NSA_BRANCHES = 3
CMP_BLOCK = 32
CMP_STRIDE = 16
CMP_HIDDEN = 2 * HEAD_DIM
SLC_BLOCK = 64
N_SELECT = 16
WINDOW = 512
FORCE_BONUS = 1000.0
RWKV_HEADS = 8
RWKV_HEAD_DIM = 64
RWKV_DIM = RWKV_HEADS * RWKV_HEAD_DIM
W_LORA = 64
A_LORA = 64
G_LORA = 128
RWKV_MIX_DIM = 3 * RWKV_DIM + W_LORA + A_LORA + G_LORA
D_FF = 4 * D_MODEL
NSA_GATES = N_NSA_BRANCHES * ATTN_HEADS
NSA_DIM = ATTN_DIM + 6 * KV_DIM + NSA_GATES
RMS_EPS = 1e-6
GN_EPS = 64e-5
NEG_INF = -1e30

LOG2_64 = 6
LANES = 128
SUBLANES = 8
MXU_COLS = 256
Q_SCALE = HEAD_DIM ** -0.5 * 1.4426950408889634
Q_PAD = ATTN_HEADS * LANES
RWKV_PAIRS = RWKV_HEADS // 2
RWKV_CHUNK = 64
ROW_TILE = 512
NSA_QUERY_TILE = 256
NSA_KEY_TILE = 512
VMEM_LIMIT = 48 * 1024 * 1024

C_Q = 0
C_KV = C_Q + Q_PAD
C_GATE = C_KV + 6 * KV_DIM
C_RWKV = C_GATE + LANES
C_MERGE = C_RWKV + RWKV_MIX_DIM
C_TOTAL = C_MERGE + 2 * D_MODEL


def _nn(a, b):
    return lax.dot_general(a, b, (((1,), (0,)), ((), ())), preferred_element_type=F32)


def _nt(a, b):
    return lax.dot_general(a, b, (((1,), (1,)), ((), ())), preferred_element_type=F32)


def _bnn(a, b):
    return lax.dot_general(a, b, (((2,), (1,)), ((0,), (0,))), preferred_element_type=F32)


def _bnt(a, b):
    return lax.dot_general(a, b, (((2,), (2,)), ((0,), (0,))), preferred_element_type=F32)


def _btn(a, b):
    return lax.dot_general(a, b, (((1,), (1,)), ((0,), (0,))), preferred_element_type=F32)


def _split2(x):
    hi = x.astype(BF16)
    lo = (x - hi.astype(F32)).astype(BF16)
    return hi, lo


def _iota(shape, dim):
    return lax.broadcasted_iota(I32, shape, dim)


def _params(*sem):
    return pltpu.CompilerParams(dimension_semantics=sem, vmem_limit_bytes=VMEM_LIMIT)


def _group_rms(x, gain):
    lane = _iota((1, LANES), 1)
    sq = x * x
    lo = lane < HEAD_DIM
    s0 = jnp.sum(jnp.where(lo, sq, 0.0), axis=-1, keepdims=True)
    s1 = jnp.sum(jnp.where(lo, 0.0, sq), axis=-1, keepdims=True)
    ms = jnp.where(lo, s0, s1) * (1.0 / HEAD_DIM)
    return x * lax.rsqrt(ms + RMS_EPS) * gain


def _proj_kernel(x_ref, g_ref, w_ref, qg_ref, kg_ref,
                 q_ref, kc_ref, vc_ref, ks_ref, vs_ref, kw_ref, vw_ref, gt_ref, rw_ref, mg_ref):
    x = x_ref[...]
    ms = jnp.mean(x * x, axis=-1, keepdims=True)
    u = (x * lax.rsqrt(ms + RMS_EPS) * g_ref[...]).astype(BF16)

    def col(c0, width):
        return _nn(u, w_ref[:, c0:c0 + width])

    for jp in range(ATTN_HEADS // 2):
        qq = col(C_Q + jp * MXU_COLS, MXU_COLS)
        for half in range(2):
            j = 2 * jp + half
            qj = qq[:, half * LANES:(half + 1) * LANES]
            msq = jnp.sum(qj * qj, axis=-1, keepdims=True) * (1.0 / HEAD_DIM)
            qn = qj * lax.rsqrt(msq + RMS_EPS) * qg_ref[:, j * LANES:(j + 1) * LANES]
            q_ref[:, j * LANES:(j + 1) * LANES] = (qn * Q_SCALE).astype(BF16)
    kvc = col(C_KV, MXU_COLS)
    kc_ref[...] = kvc[:, 0:KV_DIM]
    vc_ref[...] = kvc[:, KV_DIM:2 * KV_DIM]
    kvs = col(C_KV + MXU_COLS, MXU_COLS)
    ks_ref[...] = _group_rms(kvs[:, 0:KV_DIM], kg_ref[0:1, :]).astype(BF16)
    vs_ref[...] = kvs[:, KV_DIM:2 * KV_DIM].astype(BF16)
    kvw = col(C_KV + 2 * MXU_COLS, MXU_COLS)
    kw_ref[...] = _group_rms(kvw[:, 0:KV_DIM], kg_ref[1:2, :]).astype(BF16)
    vw_ref[...] = kvw[:, KV_DIM:2 * KV_DIM].astype(BF16)
    gt_ref[...] = col(C_GATE, LANES)
    for c in range(RWKV_MIX_DIM // MXU_COLS):
        rw_ref[:, c * MXU_COLS:(c + 1) * MXU_COLS] = col(C_RWKV + c * MXU_COLS, MXU_COLS)
    for c in range(2 * D_MODEL // 512):
        mg_ref[:, c * 512:(c + 1) * 512] = col(C_MERGE + c * 512, 512).astype(BF16)


def _row_spec(width):
    return pl.BlockSpec((None, ROW_TILE, width), lambda i, j: (i, j, 0))


def _const_spec(a, buffers=2):
    return pl.BlockSpec(a.shape, lambda i, j: (0,) * a.ndim, pipeline_mode=pl.Buffered(buffers))


def _proj_call(x, gain, w_all, qg_pad, kg2):
    b, s, _ = x.shape
    widths = [(Q_PAD, BF16), (KV_DIM, F32), (KV_DIM, F32), (KV_DIM, BF16), (KV_DIM, BF16),
              (KV_DIM, BF16), (KV_DIM, BF16), (LANES, F32), (RWKV_MIX_DIM, F32), (2 * D_MODEL, BF16)]
    return pl.pallas_call(
        _proj_kernel,
        grid=(b, s // ROW_TILE),
        in_specs=[_row_spec(D_MODEL), _const_spec(gain), _const_spec(w_all, buffers=1),
                  _const_spec(qg_pad), _const_spec(kg2)],
        out_specs=[_row_spec(w) for w, _ in widths],
        out_shape=[jax.ShapeDtypeStruct((b, s, w), dt) for w, dt in widths],
        compiler_params=_params("parallel", "parallel"),
        name="proj",
    )(x, gain, w_all, qg_pad, kg2)


def _compress_kernel(kc_ref, vc_ref, w1g_ref, w1_ref, w2g_ref, pos_ref, kg_ref, ko_ref, vo_ref, *, nc):
    rowi = _iota((nc, 1), 0)
    for which, (src, dst) in enumerate(((kc_ref, ko_ref), (vc_ref, vo_ref))):
        xcat = jnp.concatenate([src[0, pl.ds(l, nc, stride=CMP_STRIDE), :] for l in range(CMP_STRIDE)],
                               axis=1).astype(BF16)
        pos8 = jnp.broadcast_to(pos_ref[which], (SUBLANES, CMP_BLOCK * HEAD_DIM)).astype(BF16)
        bias = _nn(pos8, w1_ref[which])[0:1, :]
        out = jnp.zeros((nc, KV_DIM), F32)
        for g in range(KV_GROUPS):
            hab = _nn(xcat, w1g_ref[which, g])
            hid = hab[:, 0:CMP_HIDDEN] + pltpu.roll(hab[:, CMP_HIDDEN:2 * CMP_HIDDEN], nc - 1, axis=0) + bias
            out = out + _nn(jax.nn.gelu(hid).astype(BF16), w2g_ref[which, g])
        if which == 0:
            out = _group_rms(out, kg_ref[...])
        dst[0] = jnp.where(rowi < nc - 1, out, 0.0).astype(BF16)


def _compress_call(kc, vc, w1g, w1, w2g, pos_flat, kgain2):
    b, s, _ = kc.shape
    nc = s // CMP_STRIDE
    seq = pl.BlockSpec((1, s, KV_DIM), lambda i: (i, 0, 0))
    full = lambda a: pl.BlockSpec(a.shape, lambda i: (0,) * a.ndim)
    out = pl.BlockSpec((1, nc, KV_DIM), lambda i: (i, 0, 0))
    return pl.pallas_call(
        functools.partial(_compress_kernel, nc=nc),
        grid=(b,),
        in_specs=[seq, seq, full(w1g), full(w1), full(w2g), full(pos_flat), full(kgain2)],
        out_specs=[out, out],
        out_shape=[jax.ShapeDtypeStruct((b, nc, KV_DIM), BF16)] * 2,
        compiler_params=_params("parallel"),
        name="compress",
    )(kc, vc, w1g, w1, w2g, pos_flat, kgain2)


def _nsa_kernel(q_ref, gt_ref, kc_ref, vct_ref, ks_ref, vst_ref, kw_ref, vwt_ref, y_ref,
                m_s, acc_s, sb_s, *, tq, tk, nb, nc):
    nh = ATTN_HEADS
    hp = HEADS_PER_GROUP
    cols = hp * tq
    wk = WINDOW + tq
    bpt = tk // SLC_BLOCK
    qi = pl.program_id(1)
    q0 = qi * tq
    tlane = q0 + _iota((1, tq), 1)
    tile_h = lambda a, reps: jnp.concatenate([a] * reps, axis=1)
    eye_d = jnp.where(_iota((LANES, LANES), 0) == _iota((LANES, LANES), 1), 1.0, 0.0).astype(BF16)
    eye_q = jnp.where(_iota((tq, tq), 0) == _iota((tq, tq), 1), 1.0, 0.0).astype(BF16)
    drow = _iota((LANES, 1), 0)
    own = [(drow >= g * HEAD_DIM) & (drow < (g + 1) * HEAD_DIM) for g in range(KV_GROUPS)]

    qall = jnp.concatenate([q_ref[0, :, j * LANES:(j + 1) * LANES] for j in range(nh)], axis=0)
    q_t = _nt(eye_d, qall).astype(BF16)

    ncol = _iota((nc, 1), 0)
    cvalid = (ncol * CMP_STRIDE + (CMP_BLOCK - 1) <= tlane) & (ncol < nc - 1)
    s_c = _nn(kc_ref[0], q_t) + tile_h(jnp.where(cvalid, 0.0, NEG_INF), nh)
    e_c = jnp.exp2(s_c - jnp.max(s_c, axis=0, keepdims=True))
    p_c = e_c * (1.0 / jnp.sum(e_c, axis=0, keepdims=True)) * tile_h(jnp.where(cvalid, 1.0, 0.0), nh)
    p_cb = p_c.astype(BF16)
    o_c = [_nn(vct_ref[0], p_cb[:, g * cols:(g + 1) * cols]) for g in range(KV_GROUPS)]

    jcol = _iota((nb, 1), 0)
    ncmp = _iota((nb, nc), 1)
    ov = ((ncmp * CMP_STRIDE <= jcol * SLC_BLOCK + (SLC_BLOCK - 1))
          & (ncmp * CMP_STRIDE + (CMP_BLOCK - 1) >= jcol * SLC_BLOCK))
    ov = jnp.where(ov, 1.0, 0.0).astype(BF16)
    cur = jnp.right_shift(tlane, LOG2_64)
    forced = (jcol == 0) | (jcol == cur) | (jcol == cur - 1)
    causal_blk = jcol * SLC_BLOCK <= tlane
    for g in range(KV_GROUPS):
        c0 = g * cols
        psum = (p_c[:, c0:c0 + tq] + p_c[:, c0 + tq:c0 + 2 * tq]
                + p_c[:, c0 + 2 * tq:c0 + 3 * tq] + p_c[:, c0 + 3 * tq:c0 + 4 * tq])
        hi, lo = _split2(psum)
        imp = _nn(ov, hi) + _nn(ov, lo)
        imp = jnp.where(causal_blk, imp + jnp.where(forced, FORCE_BONUS, 0.0), -1.0)
        groups = [imp[r:r + SUBLANES, :] for r in range(0, nb, SUBLANES)]
        ranks = [jnp.zeros((SUBLANES, tq), F32) for _ in groups]
        jsub = _iota((SUBLANES, 1), 0)
        for i in range(nb):
            ri = imp[i:i + 1, :]
            for gi, grp in enumerate(groups):
                r0 = gi * SUBLANES
                if r0 + SUBLANES - 1 < i:
                    ahead = jnp.where(ri > grp, 1.0, 0.0)
                elif r0 > i:
                    ahead = jnp.where(ri >= grp, 1.0, 0.0)
                else:
                    ahead = jnp.where(jsub > i - r0, jnp.where(ri >= grp, 1.0, 0.0),
                                      jnp.where(ri > grp, 1.0, 0.0))
                ranks[gi] = ranks[gi] + ahead
        rank = jnp.concatenate(ranks, axis=0)
        sb_s[g] = jnp.where(rank < float(min(N_SELECT, nb)), 0.0, NEG_INF)

    m_s[...] = jnp.full(m_s.shape, NEG_INF, F32)
    acc_s[...] = jnp.zeros(acc_s.shape, F32)

    def sweep(tiles):
        scores = [_nn(ks_ref[0, pl.ds(pl.multiple_of(kt * tk, tk), tk), :], q_t).astype(BF16)
                  for kt in tiles]
        m_run = m_s[0:1, :]
        acc_run = [acc_s[:, g * cols:(g + 1) * cols] for g in range(KV_GROUPS)]
        m_out = [None] * KV_GROUPS
        for kt, s in zip(tiles, scores):
            causal = (kt * tk + _iota((tk, 1), 0)) <= tlane
            for g in range(KV_GROUPS):
                cs = slice(g * cols, (g + 1) * cols)
                bias = jnp.concatenate(
                    [jnp.broadcast_to(sb_s[g, pl.ds(kt * bpt + i, 1), :], (SLC_BLOCK, tq))
                     for i in range(bpt)], axis=0)
                sm = s[:, cs] + tile_h(jnp.where(causal, bias, NEG_INF).astype(BF16), hp)
                m_prev = m_run[:, cs] if m_out[g] is None else m_out[g]
                m_new = jnp.maximum(m_prev, jnp.max(sm, axis=0, keepdims=True).astype(F32))
                p = jnp.exp2(sm - m_new.astype(BF16))
                acc_run[g] = jnp.exp2(m_prev - m_new) * acc_run[g] + _nn(vst_ref[0, g, kt], p)
                m_out[g] = m_new
        for g in range(KV_GROUPS):
            cs = slice(g * cols, (g + 1) * cols)
            m_s[:, cs] = jnp.broadcast_to(m_out[g], (SUBLANES, cols))
            acc_s[:, cs] = acc_run[g]

    n_tiles = qi // (tk // tq) + 1

    def pair(i, _):
        sweep([2 * i, 2 * i + 1])
        return 0

    lax.fori_loop(0, n_tiles // 2, pair, 0)

    @pl.when(n_tiles % 2 == 1)
    def _():
        sweep([n_tiles - 1])

    w0 = pl.multiple_of(jnp.maximum(q0 - WINDOW, 0), tq)
    dist = tlane - (w0 + _iota((wk, 1), 0))
    s_w = (_nn(kw_ref[0, pl.ds(w0, wk), :], q_t).astype(BF16)
           + tile_h(jnp.where((dist >= 0) & (dist < WINDOW), 0.0, NEG_INF).astype(BF16), nh))
    p_w = jnp.exp2(s_w - jnp.max(s_w, axis=0, keepdims=True))
    wb = w0 // tq

    g_hi, g_lo = _split2(jax.nn.sigmoid(gt_ref[0]))
    gate_t = _nt(eye_d, g_hi) + _nt(eye_d, g_lo)
    for g in range(KV_GROUPS):
        cs = slice(g * cols, (g + 1) * cols)
        den_row = (1 - g) * HEAD_DIM
        acc = acc_s[:, cs]
        o_s = acc * (1.0 / acc[den_row:den_row + 1, :])
        v_w = jnp.concatenate([vwt_ref[0, g, wb + i] for i in range(wk // tq)], axis=1)
        acc_w = _nn(v_w, p_w[:, cs])
        o_w = acc_w * (1.0 / acc_w[den_row:den_row + 1, :])
        for h in range(hp):
            j = g * hp + h
            r0 = j * N_NSA_BRANCHES
            hs = slice(h * tq, (h + 1) * tq)
            o = (gate_t[r0:r0 + 1, :] * o_c[g][:, hs] + gate_t[r0 + 1:r0 + 2, :] * o_s[:, hs]
                 + gate_t[r0 + 2:r0 + 3, :] * o_w[:, hs])
            o = jnp.where(own[g], o, 0.0).astype(BF16)
            y_ref[0, :, j * LANES:(j + 1) * LANES] = _nt(eye_q, o).astype(BF16)


def _nsa_call(q, gates, kc, vct, ks, vst, kw, vwt):
    b, s, _ = q.shape
    tq = NSA_QUERY_TILE
    nb = s // SLC_BLOCK
    nc = kc.shape[1]
    tk = vst.shape[-1]
    tile = lambda w: pl.BlockSpec((1, tq, w), lambda i, j: (i, j, 0))
    seq = lambda a: pl.BlockSpec((1,) + a.shape[1:], lambda i, j: (i,) + (0,) * (a.ndim - 1))
    return pl.pallas_call(
        functools.partial(_nsa_kernel, tq=tq, tk=tk, nb=nb, nc=nc),
        grid=(b, s // tq),
        in_specs=[tile(Q_PAD), tile(LANES), seq(kc), seq(vct), seq(ks), seq(vst), seq(kw), seq(vwt)],
        out_specs=tile(Q_PAD),
        out_shape=jax.ShapeDtypeStruct((b, s, Q_PAD), BF16),
        scratch_shapes=[pltpu.VMEM((8, ATTN_HEADS * tq), F32),
                        pltpu.VMEM((LANES, ATTN_HEADS * tq), F32),
                        pltpu.VMEM((KV_GROUPS, nb, tq), F32)],
        compiler_params=_params("parallel", "arbitrary"),
        name="nsa",
    )(q, gates, kc, vct, ks, vst, kw, vwt)


def _seg_sum(x, bd):
    hi, lo = _split2(x)
    w = bd.shape[0]
    return jnp.concatenate([_nn(hi[:, c:c + w], bd) + _nn(lo[:, c:c + w], bd)
                            for c in range(0, x.shape[1], w)], axis=1)


def _stack_heads(x):
    lane = _iota((1, LANES), 1)
    lo = lane < RWKV_HEAD_DIM
    return jnp.concatenate([jnp.where(lo, x, 0.0), jnp.where(lo, 0.0, x)], axis=0)


def _rwkv_kernel(p_ref, mu_ref, w0_ref, wup_ref, a0_ref, aup_ref, gup_ref, kk_ref, ka_ref, rk_ref,
                 lnw_ref, lnb_ref, o_ref, carry_ref, state_ref, *, tt):
    ch = RWKV_CHUNK
    c2 = 2 * ch
    d = RWKV_DIM
    nch = tt // ch
    npr = RWKV_PAIRS

    @pl.when(pl.program_id(1) == 0)
    def _():
        carry_ref[...] = jnp.zeros_like(carry_ref)
        state_ref[...] = jnp.zeros_like(state_ref)

    p = p_ref[0]
    prev = pltpu.roll(p, 1, axis=0)
    prev = jnp.where(_iota((tt, 1), 0) == 0, carry_ref[0:1, :], prev)
    carry_ref[0:1, :] = p[tt - 1:tt, :]
    pm = p + (prev - p) * mu_ref[...]

    r = pm[:, 0:d]
    k = pm[:, d:2 * d]
    v = pm[:, 2 * d:3 * d]
    wa = pm[:, 3 * d:3 * d + LANES]
    gl = pm[:, 3 * d + LANES:3 * d + 2 * LANES]

    z = w0_ref[...] + _nn(jnp.tanh(wa).astype(BF16), wup_ref[...])
    softplus = jnp.maximum(-z, 0.0) + jnp.log(1.0 + jnp.exp(-jnp.abs(z)))
    lw = -jnp.exp(-softplus - 0.5)
    a = jax.nn.sigmoid(a0_ref[...] + _nn(wa.astype(BF16), aup_ref[...]))
    gate = _nn(jax.nn.sigmoid(gl).astype(BF16), gup_ref[...])

    seg = (jnp.right_shift(_iota((MXU_COLS, MXU_COLS), 0), LOG2_64)
           == jnp.right_shift(_iota((MXU_COLS, MXU_COLS), 1), LOG2_64))
    bd = jnp.where(seg, 1.0, 0.0).astype(BF16)
    kk = k * kk_ref[...]
    kk = kk * lax.rsqrt(jnp.maximum(_seg_sum(kk * kk, bd), 1e-24))
    k2 = k * (1.0 + (a - 1.0) * ka_ref[...])

    ti = _iota((tt, tt), 0)
    tj = _iota((tt, tt), 1)
    same_chunk = jnp.right_shift(ti, LOG2_64) == jnp.right_shift(tj, LOG2_64)
    tri = jnp.where(same_chunk & (tj <= ti), 1.0, 0.0).astype(BF16)
    hi, lo = _split2(lw)
    cum = _nn(tri, hi) + _nn(tri, lo)
    tot = jnp.concatenate([jnp.broadcast_to(cum[(c + 1) * ch - 1:(c + 1) * ch, :], (ch, d))
                           for c in range(nch)], axis=0)
    e_inv = jnp.exp(-cum)
    e_end = jnp.exp(tot - cum)
    dec = jnp.exp(tot)
    bb = kk * a

    def tiles(x):
        return jnp.stack([_stack_heads(x[c * ch:(c + 1) * ch, pr * LANES:(pr + 1) * LANES])
                          for c in range(nch) for pr in range(npr)]).astype(BF16)

    a_t = tiles(-kk * jnp.exp(cum - lw))
    r_t = tiles(r * jnp.exp(cum))
    b_h = tiles(bb * e_inv)
    k_h = tiles(k2 * e_inv)
    bke = jnp.concatenate([tiles(bb * e_end), tiles(k2 * e_end)], axis=1)
    v_t = tiles(v)

    ri = _iota((c2, c2), 0)
    ci = _iota((c2, c2), 1)
    same = jnp.right_shift(ri, LOG2_64) == jnp.right_shift(ci, LOG2_64)
    strict = jnp.where(same & (ci < ri), 1.0, 0.0)
    incl = jnp.where(same & (ci <= ri), 1.0, 0.0)
    ident = jnp.where(ri == ci, 1.0, 0.0)

    scores = _bnt(jnp.concatenate([a_t, r_t], axis=1), jnp.concatenate([b_h, k_h], axis=1))
    m_ab = scores[:, 0:c2, 0:c2] * strict
    m_ak = (scores[:, 0:c2, c2:2 * c2] * strict).astype(BF16)
    m_rbk = jnp.concatenate([scores[:, c2:2 * c2, 0:c2] * incl,
                             scores[:, c2:2 * c2, c2:2 * c2] * incl], axis=2).astype(BF16)
    tinv = ident + m_ab
    mp = m_ab
    for _ in range(5):
        mpb = mp.astype(BF16)
        mp = _bnn(mpb, mpb)
        tinv = tinv + _bnn(tinv.astype(BF16), mp.astype(BF16))
    rhs = jnp.concatenate([a_t, _bnn(m_ak, v_t).astype(BF16)], axis=2)
    wu = _bnn(tinv.astype(BF16), rhs)
    w_t = wu[:, :, 0:c2].astype(BF16)
    u0 = wu[:, :, c2:2 * c2]

    g_st = state_ref[...]
    y_rows = []
    for c in range(nch):
        sl = slice(c * npr, (c + 1) * npr)
        g_b = g_st.astype(BF16)
        u = _bnt(w_t[sl], g_b) + u0[sl]
        uv = jnp.concatenate([u.astype(BF16), v_t[sl]], axis=1)
        y = _bnt(r_t[sl], g_b) + _bnn(m_rbk[sl], uv)
        y = y[:, 0:ch, :] + y[:, ch:c2, :]
        y_rows.append(jnp.concatenate([y[pr] for pr in range(npr)], axis=1))
        dec_c = jnp.stack([dec[c * ch:c * ch + 1, pr * LANES:(pr + 1) * LANES] for pr in range(npr)])
        g_st = g_st * dec_c + _btn(uv, bke[sl])
    state_ref[...] = g_st
    y = jnp.concatenate(y_rows, axis=0)

    inv = 1.0 / RWKV_HEAD_DIM
    mean = _seg_sum(y, bd) * inv
    yc = y - mean
    var = _seg_sum(yc * yc, bd) * inv
    yn = yc * lax.rsqrt(var + GN_EPS) * lnw_ref[...] + lnb_ref[...]
    bonus = _seg_sum(r * k2 * rk_ref[...], bd) * v
    o_ref[0] = ((yn + bonus) * gate).astype(BF16)


def _rwkv_call(p, mu, w0, wup, a0, aup, gup, k_k, k_a, r_k, ln_w, ln_b, tt=256):
    b, s, _ = p.shape
    full = lambda a: pl.BlockSpec(a.shape, lambda i, j: (0,) * a.ndim)
    consts = (mu, w0, wup, a0, aup, gup, k_k, k_a, r_k, ln_w, ln_b)
    return pl.pallas_call(
        functools.partial(_rwkv_kernel, tt=tt),
        grid=(b, s // tt),
        in_specs=[pl.BlockSpec((1, tt, RWKV_MIX_DIM), lambda i, j: (i, j, 0))]
                 + [full(c) for c in consts],
        out_specs=pl.BlockSpec((1, tt, RWKV_DIM), lambda i, j: (i, j, 0)),
        out_shape=jax.ShapeDtypeStruct((b, s, RWKV_DIM), BF16),
        scratch_shapes=[pltpu.VMEM((8, RWKV_MIX_DIM), F32),
                        pltpu.VMEM((RWKV_PAIRS, LANES, LANES), F32)],
        compiler_params=_params("arbitrary", "arbitrary"),
        name="rwkv",
    )(p, *consts)


def _merge_kernel(x_ref, ya_ref, yb_ref, mg_ref, wa_ref, wb_ref, wo_ref, fg_ref, xo_ref, h_ref):
    ga = jax.nn.sigmoid(mg_ref[:, 0:D_MODEL].astype(F32))
    gb = jax.nn.sigmoid(mg_ref[:, D_MODEL:2 * D_MODEL].astype(F32))
    merged = ga * _nn(ya_ref[...], wa_ref[...]) + gb * _nn(yb_ref[...], wb_ref[...])
    xn = x_ref[...] + _nn(merged.astype(BF16), wo_ref[...])
    xo_ref[...] = xn
    ms = jnp.mean(xn * xn, axis=-1, keepdims=True)
    h_ref[...] = (xn * lax.rsqrt(ms + RMS_EPS) * fg_ref[...]).astype(BF16)


def _merge_call(x, ya, yb, mg, wa_pad, wb, wo, fgain):
    b, s, _ = x.shape
    return pl.pallas_call(
        _merge_kernel,
        grid=(b, s // ROW_TILE),
        in_specs=[_row_spec(D_MODEL), _row_spec(Q_PAD), _row_spec(RWKV_DIM), _row_spec(2 * D_MODEL),
                  _const_spec(wa_pad), _const_spec(wb), _const_spec(wo), _const_spec(fgain)],
        out_specs=[_row_spec(D_MODEL), _row_spec(D_MODEL)],
        out_shape=[jax.ShapeDtypeStruct((b, s, D_MODEL), F32),
                   jax.ShapeDtypeStruct((b, s, D_MODEL), BF16)],
        compiler_params=_params("parallel", "parallel"),
        name="merge",
    )(x, ya, yb, mg, wa_pad, wb, wo, fgain)


def _ffn_kernel(x_ref, h_ref, wu_ref, wd_ref, o_ref, *, fc):
    h = h_ref[...]
    acc = x_ref[...]
    for c in range(D_FF // fc):
        up = jnp.maximum(_nn(h, wu_ref[:, c * fc:(c + 1) * fc]), 0.0)
        acc = acc + _nn((up * up).astype(BF16), wd_ref[c * fc:(c + 1) * fc, :])
    o_ref[...] = acc


def _ffn_call(x, h, wu, wd, fc=1024):
    b, s, _ = x.shape
    return pl.pallas_call(
        functools.partial(_ffn_kernel, fc=fc),
        grid=(b, s // ROW_TILE),
        in_specs=[_row_spec(D_MODEL), _row_spec(D_MODEL), _const_spec(wu), _const_spec(wd)],
        out_specs=_row_spec(D_MODEL),
        out_shape=jax.ShapeDtypeStruct((b, s, D_MODEL), F32),
        compiler_params=_params("parallel", "parallel"),
        name="ffn",
    )(x, h, wu, wd)


def _pad_heads_cols(w):
    rows = w.shape[0]
    wh = w.reshape(rows, ATTN_HEADS, HEAD_DIM)
    slots = []
    for j in range(ATTN_HEADS):
        g = j // HEADS_PER_GROUP
        z = jnp.zeros((rows, HEAD_DIM), w.dtype)
        slots.append(jnp.concatenate([wh[:, j], z] if g == 0 else [z, wh[:, j]], axis=1))
    return jnp.concatenate(slots, axis=1)


def _proj_weight(w_in):
    q = _pad_heads_cols(w_in[:, 0:ATTN_DIM])
    kv = w_in[:, ATTN_DIM:ATTN_DIM + 6 * KV_DIM]
    gates = jnp.pad(w_in[:, ATTN_DIM + 6 * KV_DIM:NSA_DIM], ((0, 0), (0, LANES - NSA_GATES)))
    rest = w_in[:, NSA_DIM:]
    return jnp.concatenate([q, kv, gates, rest], axis=1).astype(BF16)


def _values_t(v, tile):
    b, s, _ = v.shape
    vt = v.reshape(b, s // tile, tile, KV_DIM).transpose(0, 1, 3, 2)
    row = jnp.arange(KV_DIM)[:, None] // HEAD_DIM
    one = jnp.ones((), v.dtype)
    return jnp.stack([jnp.where(row == g, vt, one) for g in range(KV_GROUPS)], axis=1)


def _compress_weights(w1, w2):
    w = w1.reshape(2, 2, CMP_STRIDE, HEAD_DIM, CMP_HIDDEN).transpose(0, 2, 3, 1, 4)
    zw = jnp.zeros_like(w)
    w1g = jnp.stack([jnp.stack([w if gg == g else zw for gg in range(KV_GROUPS)], axis=2)
                     for g in range(KV_GROUPS)], axis=1)
    w1g = w1g.reshape(2, KV_GROUPS, CMP_STRIDE * KV_DIM, 2 * CMP_HIDDEN)
    z2 = jnp.zeros_like(w2)
    w2g = jnp.stack([jnp.concatenate([w2 if gg == g else z2 for gg in range(KV_GROUPS)], axis=2)
                     for g in range(KV_GROUPS)], axis=1)
    return w1g.astype(BF16), w2g.astype(BF16)


def kernel(x, mix_norm, w_in, q_gain, k_gain, cmp_pos, cmp_w1, cmp_w2, w_attn_branch, tok_mix, w0,
           w_lora_up, a0, a_lora_up, g_lora_up, k_k, k_a, r_k, ln_x_w, ln_x_b, w_rwkv_branch, w_out,
           ffn_norm, w_ffn_up, w_ffn_down):
    b, s, _ = x.shape
    depth = w_in.shape[0]
    row = lambda v: v.reshape(1, -1)
    for l in range(depth):
        qg_pad = _pad_heads_cols(jnp.tile(q_gain[l], ATTN_HEADS).reshape(1, ATTN_DIM))
        kg2 = jnp.tile(k_gain[l, 1:3], (1, KV_GROUPS))
        q, kc, vc, ks, vs, kw, vw, gates, rw, mg = _proj_call(
            x, row(mix_norm[l]), _proj_weight(w_in[l]), qg_pad, kg2)

        w1g, w2g = _compress_weights(cmp_w1[l], cmp_w2[l])
        k_cmp, v_cmp = _compress_call(kc, vc, w1g, cmp_w1[l].astype(BF16), w2g,
                                      cmp_pos[l].reshape(2, 1, CMP_BLOCK * HEAD_DIM),
                                      jnp.tile(k_gain[l, 0], KV_GROUPS).reshape(1, KV_DIM))

        ya = _nsa_call(q, gates, k_cmp, v_cmp.transpose(0, 2, 1), ks,
                       _values_t(vs, NSA_KEY_TILE), kw, _values_t(vw, NSA_QUERY_TILE))

        zero = jnp.zeros((W_LORA, RWKV_DIM), F32)
        wup = jnp.concatenate([w_lora_up[l], zero], axis=0).astype(BF16)
        aup = jnp.concatenate([zero, a_lora_up[l]], axis=0).astype(BF16)
        yb = _rwkv_call(rw, row(tok_mix[l]), row(w0[l]), wup, row(a0[l]), aup,
                        g_lora_up[l].astype(BF16), row(k_k[l]), row(k_a[l]), row(r_k[l]),
                        row(ln_x_w[l]), row(ln_x_b[l]))

        wa_pad = _pad_heads_cols(w_attn_branch[l].T).T.astype(BF16)
        x, h = _merge_call(x, ya, yb, mg, wa_pad, w_rwkv_branch[l].astype(BF16),
                           w_out[l].astype(BF16), row(ffn_norm[l]))
        x = _ffn_call(x, h, w_ffn_up[l].astype(BF16), w_ffn_down[l].astype(BF16))
    return x
```

```python
import functools

import jax
import jax.numpy as jnp
import numpy as np
from jax import lax
from jax.experimental import pallas as pl
from jax.experimental.pallas import tpu as pltpu

F32 = jnp.float32
BF16 = jnp.bfloat16
I32 = jnp.int32

D_MODEL = 1024
ATTN_HEADS = 8
HEAD_DIM = 64
KV_GROUPS = 2
HEADS_PER_GROUP = ATTN_HEADS // KV_GROUPS
ATTN_DIM = ATTN_HEADS * HEAD_DIM
KV_DIM = KV_GROUPS * HEAD_DIM
N_NSA_BRANCHES = 3
CMP_BLOCK = 32
CMP_STRIDE = 16
CMP_HIDDEN = 2 * HEAD_DIM
SLC_BLOCK = 64
N_SELECT = 16
WINDOW = 512
FORCE_BONUS = 1000.0
RWKV_HEADS = 8
RWKV_HEAD_DIM = 64
RWKV_DIM = RWKV_HEADS * RWKV_HEAD_DIM
W_LORA = 64
A_LORA = 64
G_LORA = 128
RWKV_MIX_DIM = 3 * RWKV_DIM + W_LORA + A_LORA + G_LORA
D_FF = 4 * D_MODEL
NSA_GATES = N_NSA_BRANCHES * ATTN_HEADS
NSA_DIM = ATTN_DIM + 6 * KV_DIM + NSA_GATES
RMS_EPS = 1e-6
GN_EPS = 64e-5
NEG_INF = -1e30

LOG2_64 = 6
LANES = 128
SUBLANES = 8
MXU_COLS = 256
Q_SCALE = HEAD_DIM ** -0.5 * 1.4426950408889634
Q_PAD = ATTN_HEADS * LANES
RWKV_PAIRS = RWKV_HEADS // 2
RWKV_CHUNK = 64
ROW_TILE = 512
NSA_QUERY_TILE = 256
NSA_KEY_TILE = 512
VMEM_LIMIT = 48 * 1024 * 1024

C_Q = 0
C_KV = C_Q + Q_PAD
C_GATE = C_KV + 6 * KV_DIM
C_RWKV = C_GATE + LANES
C_MERGE = C_RWKV + RWKV_MIX_DIM
C_TOTAL = C_MERGE + 2 * D_MODEL


def _nn(a, b):
    return lax.dot_general(a, b, (((1,), (0,)), ((), ())), preferred_element_type=F32)


def _nt(a, b):
    return lax.dot_general(a, b, (((1,), (1,)), ((), ())), preferred_element_type=F32)


def _bnn(a, b):
    return lax.dot_general(a, b, (((2,), (1,)), ((0,), (0,))), preferred_element_type=F32)


def _bnt(a, b):
    return lax.dot_general(a, b, (((2,), (2,)), ((0,), (0,))), preferred_element_type=F32)


def _btn(a, b):
    return lax.dot_general(a, b, (((1,), (1,)), ((0,), (0,))), preferred_element_type=F32)


def _split2(x):
    hi = x.astype(BF16)
    lo = (x - hi.astype(F32)).astype(BF16)
    return hi, lo


def _iota(shape, dim):
    return lax.broadcasted_iota(I32, shape, dim)


def _params(*sem):
    return pltpu.CompilerParams(dimension_semantics=sem, vmem_limit_bytes=VMEM_LIMIT)


def _group_rms(x, gain):
    lane = _iota((1, LANES), 1)
    sq = x * x
    lo = lane < HEAD_DIM
    s0 = jnp.sum(jnp.where(lo, sq, 0.0), axis=-1, keepdims=True)
    s1 = jnp.sum(jnp.where(lo, 0.0, sq), axis=-1, keepdims=True)
    ms = jnp.where(lo, s0, s1) * (1.0 / HEAD_DIM)
    return x * lax.rsqrt(ms + RMS_EPS) * gain


def _proj_kernel(x_ref, g_ref, w_ref, qg_ref, kg_ref,
                 q_ref, kc_ref, vc_ref, ks_ref, vs_ref, kw_ref, vw_ref, gt_ref, rw_ref, mg_ref):
    x = x_ref[...]
    ms = jnp.mean(x * x, axis=-1, keepdims=True)
    u = (x * lax.rsqrt(ms + RMS_EPS) * g_ref[...]).astype(BF16)

    def col(c0, width):
        return _nn(u, w_ref[:, c0:c0 + width])

    for jp in range(ATTN_HEADS // 2):
        qq = col(C_Q + jp * MXU_COLS, MXU_COLS)
        for half in range(2):
            j = 2 * jp + half
            qj = qq[:, half * LANES:(half + 1) * LANES]
            msq = jnp.sum(qj * qj, axis=-1, keepdims=True) * (1.0 / HEAD_DIM)
            qn = qj * lax.rsqrt(msq + RMS_EPS) * qg_ref[:, j * LANES:(j + 1) * LANES]
            q_ref[:, j * LANES:(j + 1) * LANES] = (qn * Q_SCALE).astype(BF16)
    kvc = col(C_KV, MXU_COLS)
    kc_ref[...] = kvc[:, 0:KV_DIM]
    vc_ref[...] = kvc[:, KV_DIM:2 * KV_DIM]
    kvs = col(C_KV + MXU_COLS, MXU_COLS)
    ks_ref[...] = _group_rms(kvs[:, 0:KV_DIM], kg_ref[0:1, :]).astype(BF16)
    vs_ref[...] = kvs[:, KV_DIM:2 * KV_DIM].astype(BF16)
    kvw = col(C_KV + 2 * MXU_COLS, MXU_COLS)
    kw_ref[...] = _group_rms(kvw[:, 0:KV_DIM], kg_ref[1:2, :]).astype(BF16)
    vw_ref[...] = kvw[:, KV_DIM:2 * KV_DIM].astype(BF16)
    gt_ref[...] = col(C_GATE, LANES)
    for c in range(RWKV_MIX_DIM // MXU_COLS):
        rw_ref[:, c * MXU_COLS:(c + 1) * MXU_COLS] = col(C_RWKV + c * MXU_COLS, MXU_COLS)
    for c in range(2 * D_MODEL // 512):
        mg_ref[:, c * 512:(c + 1) * 512] = col(C_MERGE + c * 512, 512).astype(BF16)


def _row_spec(width):
    return pl.BlockSpec((None, ROW_TILE, width), lambda i, j: (i, j, 0))


def _const_spec(a, buffers=2):
    return pl.BlockSpec(a.shape, lambda i, j: (0,) * a.ndim, pipeline_mode=pl.Buffered(buffers))


def _proj_call(x, gain, w_all, qg_pad, kg2):
    b, s, _ = x.shape
    widths = [(Q_PAD, BF16), (KV_DIM, F32), (KV_DIM, F32), (KV_DIM, BF16), (KV_DIM, BF16),
              (KV_DIM, BF16), (KV_DIM, BF16), (LANES, F32), (RWKV_MIX_DIM, F32), (2 * D_MODEL, BF16)]
    return pl.pallas_call(
        _proj_kernel,
        grid=(b, s // ROW_TILE),
        in_specs=[_row_spec(D_MODEL), _const_spec(gain), _const_spec(w_all, buffers=1),
                  _const_spec(qg_pad), _const_spec(kg2)],
        out_specs=[_row_spec(w) for w, _ in widths],
        out_shape=[jax.ShapeDtypeStruct((b, s, w), dt) for w, dt in widths],
        compiler_params=_params("parallel", "parallel"),
        name="proj",
    )(x, gain, w_all, qg_pad, kg2)


def _compress_kernel(kc_ref, vc_ref, w1g_ref, w1_ref, w2g_ref, pos_ref, kg_ref, ko_ref, vo_ref, *, nc):
    rowi = _iota((nc, 1), 0)
    for which, (src, dst) in enumerate(((kc_ref, ko_ref), (vc_ref, vo_ref))):
        xcat = jnp.concatenate([src[0, pl.ds(l, nc, stride=CMP_STRIDE), :] for l in range(CMP_STRIDE)],
                               axis=1).astype(BF16)
        pos8 = jnp.broadcast_to(pos_ref[which], (SUBLANES, CMP_BLOCK * HEAD_DIM)).astype(BF16)
        bias = _nn(pos8, w1_ref[which])[0:1, :]
        out = jnp.zeros((nc, KV_DIM), F32)
        for g in range(KV_GROUPS):
            hab = _nn(xcat, w1g_ref[which, g])
            hid = hab[:, 0:CMP_HIDDEN] + pltpu.roll(hab[:, CMP_HIDDEN:2 * CMP_HIDDEN], nc - 1, axis=0) + bias
            out = out + _nn(jax.nn.gelu(hid).astype(BF16), w2g_ref[which, g])
        if which == 0:
            out = _group_rms(out, kg_ref[...])
        dst[0] = jnp.where(rowi < nc - 1, out, 0.0).astype(BF16)


def _compress_call(kc, vc, w1g, w1, w2g, pos_flat, kgain2):
    b, s, _ = kc.shape
    nc = s // CMP_STRIDE
    seq = pl.BlockSpec((1, s, KV_DIM), lambda i: (i, 0, 0))
    full = lambda a: pl.BlockSpec(a.shape, lambda i: (0,) * a.ndim)
    out = pl.BlockSpec((1, nc, KV_DIM), lambda i: (i, 0, 0))
    return pl.pallas_call(
        functools.partial(_compress_kernel, nc=nc),
        grid=(b,),
        in_specs=[seq, seq, full(w1g), full(w1), full(w2g), full(pos_flat), full(kgain2)],
        out_specs=[out, out],
        out_shape=[jax.ShapeDtypeStruct((b, nc, KV_DIM), BF16)] * 2,
        compiler_params=_params("parallel"),
        name="compress",
    )(kc, vc, w1g, w1, w2g, pos_flat, kgain2)


def _nsa_kernel(q_ref, gt_ref, kc_ref, vct_ref, ks_ref, vst_ref, kw_ref, vwt_ref, y_ref,
                m_s, acc_s, sb_s, rk_s, *, tq, tk, nb, nc):
    nh = ATTN_HEADS
    hp = HEADS_PER_GROUP
    cols = hp * tq
    wk = WINDOW + tq
    bpt = tk // SLC_BLOCK
    qi = pl.program_id(1)
    q0 = qi * tq
    tlane = q0 + _iota((1, tq), 1)
    tile_h = lambda a, reps: jnp.concatenate([a] * reps, axis=1)
    eye_d = jnp.where(_iota((LANES, LANES), 0) == _iota((LANES, LANES), 1), 1.0, 0.0).astype(BF16)
    eye_q = jnp.where(_iota((tq, tq), 0) == _iota((tq, tq), 1), 1.0, 0.0).astype(BF16)
    drow = _iota((LANES, 1), 0)
    own = [(drow >= g * HEAD_DIM) & (drow < (g + 1) * HEAD_DIM) for g in range(KV_GROUPS)]

    qall = jnp.concatenate([q_ref[0, :, j * LANES:(j + 1) * LANES] for j in range(nh)], axis=0)
    q_t = _nt(eye_d, qall).astype(BF16)

    ncol = _iota((nc, 1), 0)
    cvalid = (ncol * CMP_STRIDE + (CMP_BLOCK - 1) <= tlane) & (ncol < nc - 1)
    s_c = _nn(kc_ref[0], q_t) + tile_h(jnp.where(cvalid, 0.0, NEG_INF), nh)
    e_c = jnp.exp2(s_c - jnp.max(s_c, axis=0, keepdims=True))
    p_c = e_c * (1.0 / jnp.sum(e_c, axis=0, keepdims=True)) * tile_h(jnp.where(cvalid, 1.0, 0.0), nh)
    p_cb = p_c.astype(BF16)
    o_c = [_nn(vct_ref[0], p_cb[:, g * cols:(g + 1) * cols]) for g in range(KV_GROUPS)]

    jcol = _iota((nb, 1), 0)
    ncmp = _iota((nb, nc), 1)
    ov = ((ncmp * CMP_STRIDE <= jcol * SLC_BLOCK + (SLC_BLOCK - 1))
          & (ncmp * CMP_STRIDE + (CMP_BLOCK - 1) >= jcol * SLC_BLOCK))
    ov = jnp.where(ov, 1.0, 0.0).astype(BF16)
    cur = jnp.right_shift(tlane, LOG2_64)
    forced = (jcol == 0) | (jcol == cur) | (jcol == cur - 1)
    causal_blk = jcol * SLC_BLOCK <= tlane
    imps = []
    for g in range(KV_GROUPS):
        c0 = g * cols
        psum = (p_c[:, c0:c0 + tq] + p_c[:, c0 + tq:c0 + 2 * tq]
                + p_c[:, c0 + 2 * tq:c0 + 3 * tq] + p_c[:, c0 + 3 * tq:c0 + 4 * tq])
        hi, lo = _split2(psum)
        imp = _nn(ov, hi) + _nn(ov, lo)
        imps.append(jnp.where(causal_blk, imp + jnp.where(forced, FORCE_BONUS, 0.0), -1.0))

    rk_s[...] = jnp.zeros(rk_s.shape, F32)
    jsub = _iota((SUBLANES, 1), 0)
    last_blk = jnp.right_shift(q0 + tq - 1, LOG2_64)
    for sg in range(nb // SUBLANES):
        @pl.when(sg * SUBLANES <= last_blk)
        def _(sg=sg):
            for g, imp in enumerate(imps):
                groups = [imp[r:r + SUBLANES, :] for r in range(0, nb, SUBLANES)]
                ranks = [jnp.zeros((SUBLANES, tq), F32) for _ in groups]
                for i in range(sg * SUBLANES, (sg + 1) * SUBLANES):
                    ri = imp[i:i + 1, :]
                    for gi, grp in enumerate(groups):
                        r0 = gi * SUBLANES
                        if r0 + SUBLANES - 1 < i:
                            ahead = jnp.where(ri > grp, 1.0, 0.0)
                        elif r0 > i:
                            ahead = jnp.where(ri >= grp, 1.0, 0.0)
                        else:
                            ahead = jnp.where(jsub > i - r0, jnp.where(ri >= grp, 1.0, 0.0),
                                              jnp.where(ri > grp, 1.0, 0.0))
                        ranks[gi] = ranks[gi] + ahead
                rk_s[g] = rk_s[g] + jnp.concatenate(ranks, axis=0)
    sb_s[...] = jnp.where(rk_s[...] < float(min(N_SELECT, nb)), 0.0, NEG_INF)

    m_s[...] = jnp.full(m_s.shape, NEG_INF, F32)
    acc_s[...] = jnp.zeros(acc_s.shape, F32)

    def sweep(tiles):
        scores = [_nn(ks_ref[0, pl.ds(pl.multiple_of(kt * tk, tk), tk), :], q_t).astype(BF16)
                  for kt in tiles]
        m_run = m_s[0:1, :]
        acc_run = [acc_s[:, g * cols:(g + 1) * cols] for g in range(KV_GROUPS)]
        m_out = [None] * KV_GROUPS
        for kt, s in zip(tiles, scores):
            causal = (kt * tk + _iota((tk, 1), 0)) <= tlane
            for g in range(KV_GROUPS):
                cs = slice(g * cols, (g + 1) * cols)
                bias = jnp.concatenate(
                    [jnp.broadcast_to(sb_s[g, pl.ds(kt * bpt + i, 1), :], (SLC_BLOCK, tq))
                     for i in range(bpt)], axis=0)
                sm = s[:, cs] + tile_h(jnp.where(causal, bias, NEG_INF).astype(BF16), hp)
                m_prev = m_run[:, cs] if m_out[g] is None else m_out[g]
                m_new = jnp.maximum(m_prev, jnp.max(sm, axis=0, keepdims=True).astype(F32))
                p = jnp.exp2(sm - m_new.astype(BF16))
                acc_run[g] = jnp.exp2(m_prev - m_new) * acc_run[g] + _nn(vst_ref[0, g, kt], p)
                m_out[g] = m_new
        for g in range(KV_GROUPS):
            cs = slice(g * cols, (g + 1) * cols)
            m_s[:, cs] = jnp.broadcast_to(m_out[g], (SUBLANES, cols))
            acc_s[:, cs] = acc_run[g]

    n_tiles = qi // (tk // tq) + 1

    def pair(i, _):
        sweep([2 * i, 2 * i + 1])
        return 0

    lax.fori_loop(0, n_tiles // 2, pair, 0)

    @pl.when(n_tiles % 2 == 1)
    def _():
        sweep([n_tiles - 1])

    w0 = pl.multiple_of(jnp.maximum(q0 - WINDOW, 0), tq)
    dist = tlane - (w0 + _iota((wk, 1), 0))
    s_w = (_nn(kw_ref[0, pl.ds(w0, wk), :], q_t).astype(BF16)
           + tile_h(jnp.where((dist >= 0) & (dist < WINDOW), 0.0, NEG_INF).astype(BF16), nh))
    p_w = jnp.exp2(s_w - jnp.max(s_w, axis=0, keepdims=True))
    wb = w0 // tq

    g_hi, g_lo = _split2(jax.nn.sigmoid(gt_ref[0]))
    gate_t = _nt(eye_d, g_hi) + _nt(eye_d, g_lo)
    for g in range(KV_GROUPS):
        cs = slice(g * cols, (g + 1) * cols)
        den_row = (1 - g) * HEAD_DIM
        acc = acc_s[:, cs]
        o_s = acc * (1.0 / acc[den_row:den_row + 1, :])
        v_w = jnp.concatenate([vwt_ref[0, g, wb + i] for i in range(wk // tq)], axis=1)
        acc_w = _nn(v_w, p_w[:, cs])
        o_w = acc_w * (1.0 / acc_w[den_row:den_row + 1, :])
        for h in range(hp):
            j = g * hp + h
            r0 = j * N_NSA_BRANCHES
            hs = slice(h * tq, (h + 1) * tq)
            o = (gate_t[r0:r0 + 1, :] * o_c[g][:, hs] + gate_t[r0 + 1:r0 + 2, :] * o_s[:, hs]
                 + gate_t[r0 + 2:r0 + 3, :] * o_w[:, hs])
            o = jnp.where(own[g], o, 0.0).astype(BF16)
            y_ref[0, :, j * LANES:(j + 1) * LANES] = _nt(eye_q, o).astype(BF16)


def _nsa_call(q, gates, kc, vct, ks, vst, kw, vwt):
    b, s, _ = q.shape
    tq = NSA_QUERY_TILE
    nb = s // SLC_BLOCK
    nc = kc.shape[1]
    tk = vst.shape[-1]
    tile = lambda w: pl.BlockSpec((1, tq, w), lambda i, j: (i, j, 0))
    seq = lambda a: pl.BlockSpec((1,) + a.shape[1:], lambda i, j: (i,) + (0,) * (a.ndim - 1))
    return pl.pallas_call(
        functools.partial(_nsa_kernel, tq=tq, tk=tk, nb=nb, nc=nc),
        grid=(b, s // tq),
        in_specs=[tile(Q_PAD), tile(LANES), seq(kc), seq(vct), seq(ks), seq(vst), seq(kw), seq(vwt)],
        out_specs=tile(Q_PAD),
        out_shape=jax.ShapeDtypeStruct((b, s, Q_PAD), BF16),
        scratch_shapes=[pltpu.VMEM((8, ATTN_HEADS * tq), F32),
                        pltpu.VMEM((LANES, ATTN_HEADS * tq), F32),
                        pltpu.VMEM((KV_GROUPS, nb, tq), F32),
                        pltpu.VMEM((KV_GROUPS, nb, tq), F32)],
        compiler_params=_params("parallel", "arbitrary"),
        name="nsa",
    )(q, gates, kc, vct, ks, vst, kw, vwt)


def _seg_sum(x, bd):
    hi, lo = _split2(x)
    w = bd.shape[0]
    return jnp.concatenate([_nn(hi[:, c:c + w], bd) + _nn(lo[:, c:c + w], bd)
                            for c in range(0, x.shape[1], w)], axis=1)


def _stack_heads(x):
    lane = _iota((1, LANES), 1)
    lo = lane < RWKV_HEAD_DIM
    return jnp.concatenate([jnp.where(lo, x, 0.0), jnp.where(lo, 0.0, x)], axis=0)


def _rwkv_kernel(p_ref, mu_ref, w0_ref, wup_ref, a0_ref, aup_ref, gup_ref, kk_ref, ka_ref, rk_ref,
                 lnw_ref, lnb_ref, o_ref, carry_ref, state_ref, *, tt):
    ch = RWKV_CHUNK
    c2 = 2 * ch
    d = RWKV_DIM
    nch = tt // ch
    npr = RWKV_PAIRS

    @pl.when(pl.program_id(1) == 0)
    def _():
        carry_ref[...] = jnp.zeros_like(carry_ref)
        state_ref[...] = jnp.zeros_like(state_ref)

    p = p_ref[0]
    prev = pltpu.roll(p, 1, axis=0)
    prev = jnp.where(_iota((tt, 1), 0) == 0, carry_ref[0:1, :], prev)
    carry_ref[0:1, :] = p[tt - 1:tt, :]
    pm = p + (prev - p) * mu_ref[...]

    r = pm[:, 0:d]
    k = pm[:, d:2 * d]
    v = pm[:, 2 * d:3 * d]
    wa = pm[:, 3 * d:3 * d + LANES]
    gl = pm[:, 3 * d + LANES:3 * d + 2 * LANES]

    z = w0_ref[...] + _nn(jnp.tanh(wa).astype(BF16), wup_ref[...])
    softplus = jnp.maximum(-z, 0.0) + jnp.log(1.0 + jnp.exp(-jnp.abs(z)))
    lw = -jnp.exp(-softplus - 0.5)
    a = jax.nn.sigmoid(a0_ref[...] + _nn(wa.astype(BF16), aup_ref[...]))
    gate = _nn(jax.nn.sigmoid(gl).astype(BF16), gup_ref[...])

    seg = (jnp.right_shift(_iota((MXU_COLS, MXU_COLS), 0), LOG2_64)
           == jnp.right_shift(_iota((MXU_COLS, MXU_COLS), 1), LOG2_64))
    bd = jnp.where(seg, 1.0, 0.0).astype(BF16)
    kk = k * kk_ref[...]
    kk = kk * lax.rsqrt(jnp.maximum(_seg_sum(kk * kk, bd), 1e-24))
    k2 = k * (1.0 + (a - 1.0) * ka_ref[...])

    ti = _iota((tt, tt), 0)
    tj = _iota((tt, tt), 1)
    same_chunk = jnp.right_shift(ti, LOG2_64) == jnp.right_shift(tj, LOG2_64)
    tri = jnp.where(same_chunk & (tj <= ti), 1.0, 0.0).astype(BF16)
    hi, lo = _split2(lw)
    cum = _nn(tri, hi) + _nn(tri, lo)
    tot = jnp.concatenate([jnp.broadcast_to(cum[(c + 1) * ch - 1:(c + 1) * ch, :], (ch, d))
                           for c in range(nch)], axis=0)
    e_inv = jnp.exp(-cum)
    e_end = jnp.exp(tot - cum)
    dec = jnp.exp(tot)
    bb = kk * a

    def tiles(x):
        return jnp.stack([_stack_heads(x[c * ch:(c + 1) * ch, pr * LANES:(pr + 1) * LANES])
                          for c in range(nch) for pr in range(npr)]).astype(BF16)

    a_t = tiles(-kk * jnp.exp(cum - lw))
    r_t = tiles(r * jnp.exp(cum))
    b_h = tiles(bb * e_inv)
    k_h = tiles(k2 * e_inv)
    bke = jnp.concatenate([tiles(bb * e_end), tiles(k2 * e_end)], axis=1)
    v_t = tiles(v)

    ri = _iota((c2, c2), 0)
    ci = _iota((c2, c2), 1)
    same = jnp.right_shift(ri, LOG2_64) == jnp.right_shift(ci, LOG2_64)
    strict = jnp.where(same & (ci < ri), 1.0, 0.0)
    incl = jnp.where(same & (ci <= ri), 1.0, 0.0)
    ident = jnp.where(ri == ci, 1.0, 0.0)

    scores = _bnt(jnp.concatenate([a_t, r_t], axis=1), jnp.concatenate([b_h, k_h], axis=1))
    m_ab = scores[:, 0:c2, 0:c2] * strict
    m_ak = (scores[:, 0:c2, c2:2 * c2] * strict).astype(BF16)
    m_rbk = jnp.concatenate([scores[:, c2:2 * c2, 0:c2] * incl,
                             scores[:, c2:2 * c2, c2:2 * c2] * incl], axis=2).astype(BF16)
    tinv = ident + m_ab
    mp = m_ab
    for _ in range(5):
        mpb = mp.astype(BF16)
        mp = _bnn(mpb, mpb)
        tinv = tinv + _bnn(tinv.astype(BF16), mp.astype(BF16))
    rhs = jnp.concatenate([a_t, _bnn(m_ak, v_t).astype(BF16)], axis=2)
    wu = _bnn(tinv.astype(BF16), rhs)
    w_t = wu[:, :, 0:c2].astype(BF16)
    u0 = wu[:, :, c2:2 * c2]

    g_st = state_ref[...]
    y_rows = []
    for c in range(nch):
        sl = slice(c * npr, (c + 1) * npr)
        g_b = g_st.astype(BF16)
        u = _bnt(w_t[sl], g_b) + u0[sl]
        uv = jnp.concatenate([u.astype(BF16), v_t[sl]], axis=1)
        y = _bnt(r_t[sl], g_b) + _bnn(m_rbk[sl], uv)
        y = y[:, 0:ch, :] + y[:, ch:c2, :]
        y_rows.append(jnp.concatenate([y[pr] for pr in range(npr)], axis=1))
        dec_c = jnp.stack([dec[c * ch:c * ch + 1, pr * LANES:(pr + 1) * LANES] for pr in range(npr)])
        g_st = g_st * dec_c + _btn(uv, bke[sl])
    state_ref[...] = g_st
    y = jnp.concatenate(y_rows, axis=0)

    inv = 1.0 / RWKV_HEAD_DIM
    mean = _seg_sum(y, bd) * inv
    yc = y - mean
    var = _seg_sum(yc * yc, bd) * inv
    yn = yc * lax.rsqrt(var + GN_EPS) * lnw_ref[...] + lnb_ref[...]
    bonus = _seg_sum(r * k2 * rk_ref[...], bd) * v
    o_ref[0] = ((yn + bonus) * gate).astype(BF16)


def _rwkv_call(p, mu, w0, wup, a0, aup, gup, k_k, k_a, r_k, ln_w, ln_b, tt=256):
    b, s, _ = p.shape
    full = lambda a: pl.BlockSpec(a.shape, lambda i, j: (0,) * a.ndim)
    consts = (mu, w0, wup, a0, aup, gup, k_k, k_a, r_k, ln_w, ln_b)
    return pl.pallas_call(
        functools.partial(_rwkv_kernel, tt=tt),
        grid=(b, s // tt),
        in_specs=[pl.BlockSpec((1, tt, RWKV_MIX_DIM), lambda i, j: (i, j, 0))]
                 + [full(c) for c in consts],
        out_specs=pl.BlockSpec((1, tt, RWKV_DIM), lambda i, j: (i, j, 0)),
        out_shape=jax.ShapeDtypeStruct((b, s, RWKV_DIM), BF16),
        scratch_shapes=[pltpu.VMEM((8, RWKV_MIX_DIM), F32),
                        pltpu.VMEM((RWKV_PAIRS, LANES, LANES), F32)],
        compiler_params=_params("arbitrary", "arbitrary"),
        name="rwkv",
    )(p, *consts)


def _merge_kernel(x_ref, ya_ref, yb_ref, mg_ref, wa_ref, wb_ref, wo_ref, fg_ref, xo_ref, h_ref):
    ga = jax.nn.sigmoid(mg_ref[:, 0:D_MODEL].astype(F32))
    gb = jax.nn.sigmoid(mg_ref[:, D_MODEL:2 * D_MODEL].astype(F32))
    merged = ga * _nn(ya_ref[...], wa_ref[...]) + gb * _nn(yb_ref[...], wb_ref[...])
    xn = x_ref[...] + _nn(merged.astype(BF16), wo_ref[...])
    xo_ref[...] = xn
    ms = jnp.mean(xn * xn, axis=-1, keepdims=True)
    h_ref[...] = (xn * lax.rsqrt(ms + RMS_EPS) * fg_ref[...]).astype(BF16)


def _merge_call(x, ya, yb, mg, wa_pad, wb, wo, fgain):
    b, s, _ = x.shape
    return pl.pallas_call(
        _merge_kernel,
        grid=(b, s // ROW_TILE),
        in_specs=[_row_spec(D_MODEL), _row_spec(Q_PAD), _row_spec(RWKV_DIM), _row_spec(2 * D_MODEL),
                  _const_spec(wa_pad), _const_spec(wb), _const_spec(wo), _const_spec(fgain)],
        out_specs=[_row_spec(D_MODEL), _row_spec(D_MODEL)],
        out_shape=[jax.ShapeDtypeStruct((b, s, D_MODEL), F32),
                   jax.ShapeDtypeStruct((b, s, D_MODEL), BF16)],
        compiler_params=_params("parallel", "parallel"),
        name="merge",
    )(x, ya, yb, mg, wa_pad, wb, wo, fgain)


def _ffn_kernel(x_ref, h_ref, wu_ref, wd_ref, o_ref, *, fc):
    h = h_ref[...]
    acc = x_ref[...]
    for c in range(D_FF // fc):
        up = jnp.maximum(_nn(h, wu_ref[:, c * fc:(c + 1) * fc]), 0.0)
        acc = acc + _nn((up * up).astype(BF16), wd_ref[c * fc:(c + 1) * fc, :])
    o_ref[...] = acc


def _ffn_call(x, h, wu, wd, fc=1024):
    b, s, _ = x.shape
    return pl.pallas_call(
        functools.partial(_ffn_kernel, fc=fc),
        grid=(b, s // ROW_TILE),
        in_specs=[_row_spec(D_MODEL), _row_spec(D_MODEL), _const_spec(wu), _const_spec(wd)],
        out_specs=_row_spec(D_MODEL),
        out_shape=jax.ShapeDtypeStruct((b, s, D_MODEL), F32),
        compiler_params=_params("parallel", "parallel"),
        name="ffn",
    )(x, h, wu, wd)


def _pad_heads_cols(w):
    rows = w.shape[0]
    wh = w.reshape(rows, ATTN_HEADS, HEAD_DIM)
    slots = []
    for j in range(ATTN_HEADS):
        g = j // HEADS_PER_GROUP
        z = jnp.zeros((rows, HEAD_DIM), w.dtype)
        slots.append(jnp.concatenate([wh[:, j], z] if g == 0 else [z, wh[:, j]], axis=1))
    return jnp.concatenate(slots, axis=1)


def _proj_weight(w_in):
    q = _pad_heads_cols(w_in[:, 0:ATTN_DIM])
    kv = w_in[:, ATTN_DIM:ATTN_DIM + 6 * KV_DIM]
    gates = jnp.pad(w_in[:, ATTN_DIM + 6 * KV_DIM:NSA_DIM], ((0, 0), (0, LANES - NSA_GATES)))
    rest = w_in[:, NSA_DIM:]
    return jnp.concatenate([q, kv, gates, rest], axis=1).astype(BF16)


def _values_t(v, tile):
    b, s, _ = v.shape
    vt = v.reshape(b, s // tile, tile, KV_DIM).transpose(0, 1, 3, 2)
    row = jnp.arange(KV_DIM)[:, None] // HEAD_DIM
    one = jnp.ones((), v.dtype)
    return jnp.stack([jnp.where(row == g, vt, one) for g in range(KV_GROUPS)], axis=1)


def _compress_weights(w1, w2):
    w = w1.reshape(2, 2, CMP_STRIDE, HEAD_DIM, CMP_HIDDEN).transpose(0, 2, 3, 1, 4)
    zw = jnp.zeros_like(w)
    w1g = jnp.stack([jnp.stack([w if gg == g else zw for gg in range(KV_GROUPS)], axis=2)
                     for g in range(KV_GROUPS)], axis=1)
    w1g = w1g.reshape(2, KV_GROUPS, CMP_STRIDE * KV_DIM, 2 * CMP_HIDDEN)
    z2 = jnp.zeros_like(w2)
    w2g = jnp.stack([jnp.concatenate([w2 if gg == g else z2 for gg in range(KV_GROUPS)], axis=2)
                     for g in range(KV_GROUPS)], axis=1)
    return w1g.astype(BF16), w2g.astype(BF16)


def kernel(x, mix_norm, w_in, q_gain, k_gain, cmp_pos, cmp_w1, cmp_w2, w_attn_branch, tok_mix, w0,
           w_lora_up, a0, a_lora_up, g_lora_up, k_k, k_a, r_k, ln_x_w, ln_x_b, w_rwkv_branch, w_out,
           ffn_norm, w_ffn_up, w_ffn_down):
    b, s, _ = x.shape
    depth = w_in.shape[0]
    row = lambda v: v.reshape(1, -1)
    for l in range(depth):
        qg_pad = _pad_heads_cols(jnp.tile(q_gain[l], ATTN_HEADS).reshape(1, ATTN_DIM))
        kg2 = jnp.tile(k_gain[l, 1:3], (1, KV_GROUPS))
        q, kc, vc, ks, vs, kw, vw, gates, rw, mg = _proj_call(
            x, row(mix_norm[l]), _proj_weight(w_in[l]), qg_pad, kg2)

        w1g, w2g = _compress_weights(cmp_w1[l], cmp_w2[l])
        k_cmp, v_cmp = _compress_call(kc, vc, w1g, cmp_w1[l].astype(BF16), w2g,
                                      cmp_pos[l].reshape(2, 1, CMP_BLOCK * HEAD_DIM),
                                      jnp.tile(k_gain[l, 0], KV_GROUPS).reshape(1, KV_DIM))

        ya = _nsa_call(q, gates, k_cmp, v_cmp.transpose(0, 2, 1), ks,
                       _values_t(vs, NSA_KEY_TILE), kw, _values_t(vw, NSA_QUERY_TILE))

        zero = jnp.zeros((W_LORA, RWKV_DIM), F32)
        wup = jnp.concatenate([w_lora_up[l], zero], axis=0).astype(BF16)
        aup = jnp.concatenate([zero, a_lora_up[l]], axis=0).astype(BF16)
        yb = _rwkv_call(rw, row(tok_mix[l]), row(w0[l]), wup, row(a0[l]), aup,
                        g_lora_up[l].astype(BF16), row(k_k[l]), row(k_a[l]), row(r_k[l]),
                        row(ln_x_w[l]), row(ln_x_b[l]))

        wa_pad = _pad_heads_cols(w_attn_branch[l].T).T.astype(BF16)
        x, h = _merge_call(x, ya, yb, mg, wa_pad, w_rwkv_branch[l].astype(BF16),
                           w_out[l].astype(BF16), row(ffn_norm[l]))
        x = _ffn_call(x, h, w_ffn_up[l].astype(BF16), w_ffn_down[l].astype(BF16))
    return x
```

```python
import functools

import jax
import jax.numpy as jnp
import numpy as np
from jax import lax
from jax.experimental import pallas as pl
from jax.experimental.pallas import tpu as pltpu

F32 = jnp.float32
BF16 = jnp.bfloat16
I32 = jnp.int32

D_MODEL = 1024
ATTN_HEADS = 8
HEAD_DIM = 64
KV_GROUPS = 2
HEADS_PER_GROUP = ATTN_HEADS // KV_GROUPS
ATTN_DIM = ATTN_HEADS * HEAD_DIM
KV_DIM = KV_GROUPS * HEAD_DIM
N_NSA_BRANCHES = 3
CMP_BLOCK = 32
CMP_STRIDE = 16
CMP_HIDDEN = 2 * HEAD_DIM
SLC_BLOCK = 64
N_SELECT = 16
WINDOW = 512
FORCE_BONUS = 1000.0
RWKV_HEADS = 8
RWKV_HEAD_DIM = 64
RWKV_DIM = RWKV_HEADS * RWKV_HEAD_DIM
W_LORA = 64
A_LORA = 64
G_LORA = 128
RWKV_MIX_DIM = 3 * RWKV_DIM + W_LORA + A_LORA + G_LORA
D_FF = 4 * D_MODEL
NSA_GATES = N_NSA_BRANCHES * ATTN_HEADS
NSA_DIM = ATTN_DIM + 6 * KV_DIM + NSA_GATES
RMS_EPS = 1e-6
GN_EPS = 64e-5
NEG_INF = -1e30

LOG2_64 = 6
LANES = 128
SUBLANES = 8
MXU_COLS = 256
Q_SCALE = HEAD_DIM ** -0.5 * 1.4426950408889634
HEAD_SLOTS = HEADS_PER_GROUP
RWKV_PAIRS = RWKV_HEADS // 2
RWKV_CHUNK = 64
ROW_TILE = 512
NSA_QUERY_TILE = 256
NSA_KEY_TILE = 512
VMEM_LIMIT = 48 * 1024 * 1024

C_Q = 0
C_KV = C_Q + ATTN_DIM
C_GATE = C_KV + 6 * KV_DIM
C_RWKV = C_GATE + LANES
C_MERGE = C_RWKV + RWKV_MIX_DIM
C_TOTAL = C_MERGE + 2 * D_MODEL


def _nn(a, b):
    return lax.dot_general(a, b, (((1,), (0,)), ((), ())), preferred_element_type=F32)


def _nt(a, b):
    return lax.dot_general(a, b, (((1,), (1,)), ((), ())), preferred_element_type=F32)


def _bnn(a, b):
    return lax.dot_general(a, b, (((2,), (1,)), ((0,), (0,))), preferred_element_type=F32)


def _bnt(a, b):
    return lax.dot_general(a, b, (((2,), (2,)), ((0,), (0,))), preferred_element_type=F32)


def _btn(a, b):
    return lax.dot_general(a, b, (((1,), (1,)), ((0,), (0,))), preferred_element_type=F32)


def _split2(x):
    hi = x.astype(BF16)
    lo = (x - hi.astype(F32)).astype(BF16)
    return hi, lo


def _iota(shape, dim):
    return lax.broadcasted_iota(I32, shape, dim)


def _params(*sem):
    return pltpu.CompilerParams(dimension_semantics=sem, vmem_limit_bytes=VMEM_LIMIT)


def _group_rms(x, gain):
    lane = _iota((1, LANES), 1)
    sq = x * x
    lo = lane < HEAD_DIM
    s0 = jnp.sum(jnp.where(lo, sq, 0.0), axis=-1, keepdims=True)
    s1 = jnp.sum(jnp.where(lo, 0.0, sq), axis=-1, keepdims=True)
    ms = jnp.where(lo, s0, s1) * (1.0 / HEAD_DIM)
    return x * lax.rsqrt(ms + RMS_EPS) * gain


def _proj_kernel(x_ref, g_ref, w_ref, qg_ref, kg_ref,
                 q_ref, kc_ref, vc_ref, ks_ref, vs_ref, kw_ref, vw_ref, gt_ref, rw_ref, mg_ref):
    x = x_ref[...]
    ms = jnp.mean(x * x, axis=-1, keepdims=True)
    u = (x * lax.rsqrt(ms + RMS_EPS) * g_ref[...]).astype(BF16)

    def col(c0, width):
        return _nn(u, w_ref[:, c0:c0 + width])

    for piece in range(ATTN_DIM // MXU_COLS):
        qq = col(C_Q + piece * MXU_COLS, MXU_COLS)
        for half in range(MXU_COLS // LANES):
            slot = slice(piece * MXU_COLS + half * LANES, piece * MXU_COLS + (half + 1) * LANES)
            qn = _group_rms(qq[:, half * LANES:(half + 1) * LANES], qg_ref[...])
            q_ref[:, slot] = (qn * Q_SCALE).astype(BF16)
    kvc = col(C_KV, MXU_COLS)
    kc_ref[...] = kvc[:, 0:KV_DIM]
    vc_ref[...] = kvc[:, KV_DIM:2 * KV_DIM]
    kvs = col(C_KV + MXU_COLS, MXU_COLS)
    ks_ref[...] = _group_rms(kvs[:, 0:KV_DIM], kg_ref[0:1, :]).astype(BF16)
    vs_ref[...] = kvs[:, KV_DIM:2 * KV_DIM].astype(BF16)
    kvw = col(C_KV + 2 * MXU_COLS, MXU_COLS)
    kw_ref[...] = _group_rms(kvw[:, 0:KV_DIM], kg_ref[1:2, :]).astype(BF16)
    vw_ref[...] = kvw[:, KV_DIM:2 * KV_DIM].astype(BF16)
    gt_ref[...] = col(C_GATE, LANES)
    for c in range(RWKV_MIX_DIM // MXU_COLS):
        rw_ref[:, c * MXU_COLS:(c + 1) * MXU_COLS] = col(C_RWKV + c * MXU_COLS, MXU_COLS)
    for c in range(2 * D_MODEL // 512):
        mg_ref[:, c * 512:(c + 1) * 512] = col(C_MERGE + c * 512, 512).astype(BF16)


def _row_spec(width):
    return pl.BlockSpec((None, ROW_TILE, width), lambda i, j: (i, j, 0))


def _const_spec(a, buffers=2):
    return pl.BlockSpec(a.shape, lambda i, j: (0,) * a.ndim, pipeline_mode=pl.Buffered(buffers))


def _proj_call(x, gain, w_all, qg2, kg2):
    b, s, _ = x.shape
    widths = [(ATTN_DIM, BF16), (KV_DIM, F32), (KV_DIM, F32), (KV_DIM, BF16), (KV_DIM, BF16),
              (KV_DIM, BF16), (KV_DIM, BF16), (LANES, F32), (RWKV_MIX_DIM, F32), (2 * D_MODEL, BF16)]
    return pl.pallas_call(
        _proj_kernel,
        grid=(b, s // ROW_TILE),
        in_specs=[_row_spec(D_MODEL), _const_spec(gain), _const_spec(w_all, buffers=1),
                  _const_spec(qg2), _const_spec(kg2)],
        out_specs=[_row_spec(w) for w, _ in widths],
        out_shape=[jax.ShapeDtypeStruct((b, s, w), dt) for w, dt in widths],
        compiler_params=_params("parallel", "parallel"),
        name="proj",
    )(x, gain, w_all, qg2, kg2)


def _compress_kernel(kc_ref, vc_ref, w1g_ref, w1_ref, w2g_ref, pos_ref, kg_ref, ko_ref, vo_ref, *, nc):
    rowi = _iota((nc, 1), 0)
    for which, (src, dst) in enumerate(((kc_ref, ko_ref), (vc_ref, vo_ref))):
        xcat = jnp.concatenate([src[0, pl.ds(l, nc, stride=CMP_STRIDE), :] for l in range(CMP_STRIDE)],
                               axis=1).astype(BF16)
        pos8 = jnp.broadcast_to(pos_ref[which], (SUBLANES, CMP_BLOCK * HEAD_DIM)).astype(BF16)
        bias = _nn(pos8, w1_ref[which])[0:1, :]
        out = jnp.zeros((nc, KV_DIM), F32)
        for g in range(KV_GROUPS):
            hab = _nn(xcat, w1g_ref[which, g])
            hid = hab[:, 0:CMP_HIDDEN] + pltpu.roll(hab[:, CMP_HIDDEN:2 * CMP_HIDDEN], nc - 1, axis=0) + bias
            out = out + _nn(jax.nn.gelu(hid).astype(BF16), w2g_ref[which, g])
        if which == 0:
            out = _group_rms(out, kg_ref[...])
        dst[0] = jnp.where(rowi < nc - 1, out, 0.0).astype(BF16)


def _compress_call(kc, vc, w1g, w1, w2g, pos_flat, kgain2):
    b, s, _ = kc.shape
    nc = s // CMP_STRIDE
    seq = pl.BlockSpec((1, s, KV_DIM), lambda i: (i, 0, 0))
    full = lambda a: pl.BlockSpec(a.shape, lambda i: (0,) * a.ndim)
    out = pl.BlockSpec((1, nc, KV_DIM), lambda i: (i, 0, 0))
    return pl.pallas_call(
        functools.partial(_compress_kernel, nc=nc),
        grid=(b,),
        in_specs=[seq, seq, full(w1g), full(w1), full(w2g), full(pos_flat), full(kgain2)],
        out_specs=[out, out],
        out_shape=[jax.ShapeDtypeStruct((b, nc, KV_DIM), BF16)] * 2,
        compiler_params=_params("parallel"),
        name="compress",
    )(kc, vc, w1g, w1, w2g, pos_flat, kgain2)


def _nsa_kernel(q_ref, gt_ref, kc_ref, vct_ref, ks_ref, vst_ref, kw_ref, vwt_ref, y_ref,
                m_s, acc_s, sb_s, rk_s, *, tq, tk, nb, nc):
    nh = ATTN_HEADS
    hp = HEADS_PER_GROUP
    cols = hp * tq
    wk = WINDOW + tq
    bpt = tk // SLC_BLOCK
    qi = pl.program_id(1)
    q0 = qi * tq
    tlane = q0 + _iota((1, tq), 1)
    tile_h = lambda a, reps: jnp.concatenate([a] * reps, axis=1)
    eye_d = jnp.where(_iota((LANES, LANES), 0) == _iota((LANES, LANES), 1), 1.0, 0.0).astype(BF16)
    eye_q = jnp.where(_iota((tq, tq), 0) == _iota((tq, tq), 1), 1.0, 0.0).astype(BF16)
    drow = _iota((LANES, 1), 0)
    own = [(drow >= g * HEAD_DIM) & (drow < (g + 1) * HEAD_DIM) for g in range(KV_GROUPS)]

    lane = _iota((1, LANES), 1)
    glanes = [jnp.where((lane >= g * HEAD_DIM) & (lane < (g + 1) * HEAD_DIM), 1.0, 0.0).astype(BF16)
              for g in range(KV_GROUPS)]
    qall = jnp.concatenate([q_ref[0, :, h * LANES:(h + 1) * LANES] * glanes[g]
                            for g in range(KV_GROUPS) for h in range(hp)], axis=0)
    q_t = _nt(eye_d, qall).astype(BF16)

    ncol = _iota((nc, 1), 0)
    cvalid = (ncol * CMP_STRIDE + (CMP_BLOCK - 1) <= tlane) & (ncol < nc - 1)
    s_c = _nn(kc_ref[0], q_t) + tile_h(jnp.where(cvalid, 0.0, NEG_INF), nh)
    e_c = jnp.exp2(s_c - jnp.max(s_c, axis=0, keepdims=True))
    p_c = e_c * (1.0 / jnp.sum(e_c, axis=0, keepdims=True)) * tile_h(jnp.where(cvalid, 1.0, 0.0), nh)
    p_cb = p_c.astype(BF16)
    o_c = [_nn(vct_ref[0], p_cb[:, g * cols:(g + 1) * cols]) for g in range(KV_GROUPS)]

    jcol = _iota((nb, 1), 0)
    ncmp = _iota((nb, nc), 1)
    ov = ((ncmp * CMP_STRIDE <= jcol * SLC_BLOCK + (SLC_BLOCK - 1))
          & (ncmp * CMP_STRIDE + (CMP_BLOCK - 1) >= jcol * SLC_BLOCK))
    ov = jnp.where(ov, 1.0, 0.0).astype(BF16)
    cur = jnp.right_shift(tlane, LOG2_64)
    forced = (jcol == 0) | (jcol == cur) | (jcol == cur - 1)
    causal_blk = jcol * SLC_BLOCK <= tlane
    imps = []
    for g in range(KV_GROUPS):
        c0 = g * cols
        psum = (p_c[:, c0:c0 + tq] + p_c[:, c0 + tq:c0 + 2 * tq]
                + p_c[:, c0 + 2 * tq:c0 + 3 * tq] + p_c[:, c0 + 3 * tq:c0 + 4 * tq])
        hi, lo = _split2(psum)
        imp = _nn(ov, hi) + _nn(ov, lo)
        imps.append(jnp.where(causal_blk, imp + jnp.where(forced, FORCE_BONUS, 0.0), -1.0))

    rk_s[...] = jnp.zeros(rk_s.shape, F32)
    jsub = _iota((SUBLANES, 1), 0)
    last_blk = jnp.right_shift(q0 + tq - 1, LOG2_64)
    for sg in range(nb // SUBLANES):
        @pl.when(sg * SUBLANES <= last_blk)
        def _(sg=sg):
            for g, imp in enumerate(imps):
                groups = [imp[r:r + SUBLANES, :] for r in range(0, nb, SUBLANES)]
                ranks = [jnp.zeros((SUBLANES, tq), F32) for _ in groups]
                for i in range(sg * SUBLANES, (sg + 1) * SUBLANES):
                    ri = imp[i:i + 1, :]
                    for gi, grp in enumerate(groups):
                        r0 = gi * SUBLANES
                        if r0 + SUBLANES - 1 < i:
                            ahead = jnp.where(ri > grp, 1.0, 0.0)
                        elif r0 > i:
                            ahead = jnp.where(ri >= grp, 1.0, 0.0)
                        else:
                            ahead = jnp.where(jsub > i - r0, jnp.where(ri >= grp, 1.0, 0.0),
                                              jnp.where(ri > grp, 1.0, 0.0))
                        ranks[gi] = ranks[gi] + ahead
                rk_s[g] = rk_s[g] + jnp.concatenate(ranks, axis=0)
    sb_s[...] = jnp.where(rk_s[...] < float(min(N_SELECT, nb)), 0.0, NEG_INF)

    m_s[...] = jnp.full(m_s.shape, NEG_INF, F32)
    acc_s[...] = jnp.zeros(acc_s.shape, F32)

    def sweep(tiles):
        scores = [_nn(ks_ref[0, pl.ds(pl.multiple_of(kt * tk, tk), tk), :], q_t).astype(BF16)
                  for kt in tiles]
        m_run = m_s[0:1, :]
        acc_run = [acc_s[:, g * cols:(g + 1) * cols] for g in range(KV_GROUPS)]
        m_out = [None] * KV_GROUPS
        for kt, s in zip(tiles, scores):
            causal = (kt * tk + _iota((tk, 1), 0)) <= tlane
            for g in range(KV_GROUPS):
                cs = slice(g * cols, (g + 1) * cols)
                bias = jnp.concatenate(
                    [jnp.broadcast_to(sb_s[g, pl.ds(kt * bpt + i, 1), :], (SLC_BLOCK, tq))
                     for i in range(bpt)], axis=0)
                sm = s[:, cs] + tile_h(jnp.where(causal, bias, NEG_INF).astype(BF16), hp)
                m_prev = m_run[:, cs] if m_out[g] is None else m_out[g]
                m_new = jnp.maximum(m_prev, jnp.max(sm, axis=0, keepdims=True).astype(F32))
                p = jnp.exp2(sm - m_new.astype(BF16))
                acc_run[g] = jnp.exp2(m_prev - m_new) * acc_run[g] + _nn(vst_ref[0, g, kt], p)
                m_out[g] = m_new
        for g in range(KV_GROUPS):
            cs = slice(g * cols, (g + 1) * cols)
            m_s[:, cs] = jnp.broadcast_to(m_out[g], (SUBLANES, cols))
            acc_s[:, cs] = acc_run[g]

    n_tiles = qi // (tk // tq) + 1

    def pair(i, _):
        sweep([2 * i, 2 * i + 1])
        return 0

    lax.fori_loop(0, n_tiles // 2, pair, 0)

    @pl.when(n_tiles % 2 == 1)
    def _():
        sweep([n_tiles - 1])

    w0 = pl.multiple_of(jnp.maximum(q0 - WINDOW, 0), tq)
    dist = tlane - (w0 + _iota((wk, 1), 0))
    s_w = (_nn(kw_ref[0, pl.ds(w0, wk), :], q_t).astype(BF16)
           + tile_h(jnp.where((dist >= 0) & (dist < WINDOW), 0.0, NEG_INF).astype(BF16), nh))
    p_w = jnp.exp2(s_w - jnp.max(s_w, axis=0, keepdims=True))
    wb = w0 // tq

    g_hi, g_lo = _split2(jax.nn.sigmoid(gt_ref[0]))
    gate_t = _nt(eye_d, g_hi) + _nt(eye_d, g_lo)
    gated = []
    for g in range(KV_GROUPS):
        cs = slice(g * cols, (g + 1) * cols)
        den_row = (1 - g) * HEAD_DIM
        acc = acc_s[:, cs]
        o_s = acc * (1.0 / acc[den_row:den_row + 1, :])
        v_w = jnp.concatenate([vwt_ref[0, g, wb + i] for i in range(wk // tq)], axis=1)
        acc_w = _nn(v_w, p_w[:, cs])
        o_w = acc_w * (1.0 / acc_w[den_row:den_row + 1, :])
        for h in range(hp):
            r0 = (g * hp + h) * N_NSA_BRANCHES
            hs = slice(h * tq, (h + 1) * tq)
            gated.append(gate_t[r0:r0 + 1, :] * o_c[g][:, hs] + gate_t[r0 + 1:r0 + 2, :] * o_s[:, hs]
                         + gate_t[r0 + 2:r0 + 3, :] * o_w[:, hs])
    for h in range(hp):
        o = jnp.where(own[0], gated[h], gated[hp + h]).astype(BF16)
        y_ref[0, :, h * LANES:(h + 1) * LANES] = _nt(eye_q, o).astype(BF16)


def _nsa_call(q, gates, kc, vct, ks, vst, kw, vwt):
    b, s, _ = q.shape
    tq = NSA_QUERY_TILE
    nb = s // SLC_BLOCK
    nc = kc.shape[1]
    tk = vst.shape[-1]
    tile = lambda w: pl.BlockSpec((1, tq, w), lambda i, j: (i, j, 0))
    seq = lambda a: pl.BlockSpec((1,) + a.shape[1:], lambda i, j: (i,) + (0,) * (a.ndim - 1))
    return pl.pallas_call(
        functools.partial(_nsa_kernel, tq=tq, tk=tk, nb=nb, nc=nc),
        grid=(b, s // tq),
        in_specs=[tile(ATTN_DIM), tile(LANES), seq(kc), seq(vct), seq(ks), seq(vst), seq(kw), seq(vwt)],
        out_specs=tile(ATTN_DIM),
        out_shape=jax.ShapeDtypeStruct((b, s, ATTN_DIM), BF16),
        scratch_shapes=[pltpu.VMEM((8, ATTN_HEADS * tq), F32),
                        pltpu.VMEM((LANES, ATTN_HEADS * tq), F32),
                        pltpu.VMEM((KV_GROUPS, nb, tq), F32),
                        pltpu.VMEM((KV_GROUPS, nb, tq), F32)],
        compiler_params=_params("parallel", "arbitrary"),
        name="nsa",
    )(q, gates, kc, vct, ks, vst, kw, vwt)


def _seg_sum(x, bd):
    hi, lo = _split2(x)
    w = bd.shape[0]
    return jnp.concatenate([_nn(hi[:, c:c + w], bd) + _nn(lo[:, c:c + w], bd)
                            for c in range(0, x.shape[1], w)], axis=1)


def _stack_heads(x):
    lane = _iota((1, LANES), 1)
    lo = lane < RWKV_HEAD_DIM
    return jnp.concatenate([jnp.where(lo, x, 0.0), jnp.where(lo, 0.0, x)], axis=0)


def _rwkv_kernel(p_ref, mu_ref, w0_ref, wup_ref, a0_ref, aup_ref, gup_ref, kk_ref, ka_ref, rk_ref,
                 lnw_ref, lnb_ref, o_ref, carry_ref, state_ref, *, tt):
    ch = RWKV_CHUNK
    c2 = 2 * ch
    d = RWKV_DIM
    nch = tt // ch
    npr = RWKV_PAIRS

    @pl.when(pl.program_id(1) == 0)
    def _():
        carry_ref[...] = jnp.zeros_like(carry_ref)
        state_ref[...] = jnp.zeros_like(state_ref)

    p = p_ref[0]
    prev = pltpu.roll(p, 1, axis=0)
    prev = jnp.where(_iota((tt, 1), 0) == 0, carry_ref[0:1, :], prev)
    carry_ref[0:1, :] = p[tt - 1:tt, :]
    pm = p + (prev - p) * mu_ref[...]

    r = pm[:, 0:d]
    k = pm[:, d:2 * d]
    v = pm[:, 2 * d:3 * d]
    wa = pm[:, 3 * d:3 * d + LANES]
    gl = pm[:, 3 * d + LANES:3 * d + 2 * LANES]

    z = w0_ref[...] + _nn(jnp.tanh(wa).astype(BF16), wup_ref[...])
    softplus = jnp.maximum(-z, 0.0) + jnp.log(1.0 + jnp.exp(-jnp.abs(z)))
    lw = -jnp.exp(-softplus - 0.5)
    a = jax.nn.sigmoid(a0_ref[...] + _nn(wa.astype(BF16), aup_ref[...]))
    gate = _nn(jax.nn.sigmoid(gl).astype(BF16), gup_ref[...])

    seg = (jnp.right_shift(_iota((MXU_COLS, MXU_COLS), 0), LOG2_64)
           == jnp.right_shift(_iota((MXU_COLS, MXU_COLS), 1), LOG2_64))
    bd = jnp.where(seg, 1.0, 0.0).astype(BF16)
    kk = k * kk_ref[...]
    kk = kk * lax.rsqrt(jnp.maximum(_seg_sum(kk * kk, bd), 1e-24))
    k2 = k * (1.0 + (a - 1.0) * ka_ref[...])

    ti = _iota((tt, tt), 0)
    tj = _iota((tt, tt), 1)
    same_chunk = jnp.right_shift(ti, LOG2_64) == jnp.right_shift(tj, LOG2_64)
    tri = jnp.where(same_chunk & (tj <= ti), 1.0, 0.0).astype(BF16)
    hi, lo = _split2(lw)
    cum = _nn(tri, hi) + _nn(tri, lo)
    tot = jnp.concatenate([jnp.broadcast_to(cum[(c + 1) * ch - 1:(c + 1) * ch, :], (ch, d))
                           for c in range(nch)], axis=0)
    e_inv = jnp.exp(-cum)
    e_end = jnp.exp(tot - cum)
    dec = jnp.exp(tot)
    bb = kk * a

    def tiles(x):
        return jnp.stack([_stack_heads(x[c * ch:(c + 1) * ch, pr * LANES:(pr + 1) * LANES])
                          for c in range(nch) for pr in range(npr)]).astype(BF16)

    a_t = tiles(-kk * jnp.exp(cum - lw))
    r_t = tiles(r * jnp.exp(cum))
    b_h = tiles(bb * e_inv)
    k_h = tiles(k2 * e_inv)
    bke = jnp.concatenate([tiles(bb * e_end), tiles(k2 * e_end)], axis=1)
    v_t = tiles(v)

    ri = _iota((c2, c2), 0)
    ci = _iota((c2, c2), 1)
    same = jnp.right_shift(ri, LOG2_64) == jnp.right_shift(ci, LOG2_64)
    strict = jnp.where(same & (ci < ri), 1.0, 0.0)
    incl = jnp.where(same & (ci <= ri), 1.0, 0.0)
    ident = jnp.where(ri == ci, 1.0, 0.0)

    scores = _bnt(jnp.concatenate([a_t, r_t], axis=1), jnp.concatenate([b_h, k_h], axis=1))
    m_ab = scores[:, 0:c2, 0:c2] * strict
    m_ak = (scores[:, 0:c2, c2:2 * c2] * strict).astype(BF16)
    m_rbk = jnp.concatenate([scores[:, c2:2 * c2, 0:c2] * incl,
                             scores[:, c2:2 * c2, c2:2 * c2] * incl], axis=2).astype(BF16)
    tinv = ident + m_ab
    mp = m_ab
    for _ in range(5):
        mpb = mp.astype(BF16)
        mp = _bnn(mpb, mpb)
        tinv = tinv + _bnn(tinv.astype(BF16), mp.astype(BF16))
    rhs = jnp.concatenate([a_t, _bnn(m_ak, v_t).astype(BF16)], axis=2)
    wu = _bnn(tinv.astype(BF16), rhs)
    w_t = wu[:, :, 0:c2].astype(BF16)
    u0 = wu[:, :, c2:2 * c2]

    g_st = state_ref[...]
    y_rows = []
    for c in range(nch):
        sl = slice(c * npr, (c + 1) * npr)
        g_b = g_st.astype(BF16)
        u = _bnt(w_t[sl], g_b) + u0[sl]
        uv = jnp.concatenate([u.astype(BF16), v_t[sl]], axis=1)
        y = _bnt(r_t[sl], g_b) + _bnn(m_rbk[sl], uv)
        y = y[:, 0:ch, :] + y[:, ch:c2, :]
        y_rows.append(jnp.concatenate([y[pr] for pr in range(npr)], axis=1))
        dec_c = jnp.stack([dec[c * ch:c * ch + 1, pr * LANES:(pr + 1) * LANES] for pr in range(npr)])
        g_st = g_st * dec_c + _btn(uv, bke[sl])
    state_ref[...] = g_st
    y = jnp.concatenate(y_rows, axis=0)

    inv = 1.0 / RWKV_HEAD_DIM
    mean = _seg_sum(y, bd) * inv
    yc = y - mean
    var = _seg_sum(yc * yc, bd) * inv
    yn = yc * lax.rsqrt(var + GN_EPS) * lnw_ref[...] + lnb_ref[...]
    bonus = _seg_sum(r * k2 * rk_ref[...], bd) * v
    o_ref[0] = ((yn + bonus) * gate).astype(BF16)


def _rwkv_call(p, mu, w0, wup, a0, aup, gup, k_k, k_a, r_k, ln_w, ln_b, tt=256):
    b, s, _ = p.shape
    full = lambda a: pl.BlockSpec(a.shape, lambda i, j: (0,) * a.ndim)
    consts = (mu, w0, wup, a0, aup, gup, k_k, k_a, r_k, ln_w, ln_b)
    return pl.pallas_call(
        functools.partial(_rwkv_kernel, tt=tt),
        grid=(b, s // tt),
        in_specs=[pl.BlockSpec((1, tt, RWKV_MIX_DIM), lambda i, j: (i, j, 0))]
                 + [full(c) for c in consts],
        out_specs=pl.BlockSpec((1, tt, RWKV_DIM), lambda i, j: (i, j, 0)),
        out_shape=jax.ShapeDtypeStruct((b, s, RWKV_DIM), BF16),
        scratch_shapes=[pltpu.VMEM((8, RWKV_MIX_DIM), F32),
                        pltpu.VMEM((RWKV_PAIRS, LANES, LANES), F32)],
        compiler_params=_params("arbitrary", "arbitrary"),
        name="rwkv",
    )(p, *consts)


def _merge_kernel(x_ref, ya_ref, yb_ref, mg_ref, wa_ref, wb_ref, wo_ref, fg_ref, xo_ref, h_ref):
    ga = jax.nn.sigmoid(mg_ref[:, 0:D_MODEL].astype(F32))
    gb = jax.nn.sigmoid(mg_ref[:, D_MODEL:2 * D_MODEL].astype(F32))
    merged = ga * _nn(ya_ref[...], wa_ref[...]) + gb * _nn(yb_ref[...], wb_ref[...])
    xn = x_ref[...] + _nn(merged.astype(BF16), wo_ref[...])
    xo_ref[...] = xn
    ms = jnp.mean(xn * xn, axis=-1, keepdims=True)
    h_ref[...] = (xn * lax.rsqrt(ms + RMS_EPS) * fg_ref[...]).astype(BF16)


def _merge_call(x, ya, yb, mg, wa_pad, wb, wo, fgain):
    b, s, _ = x.shape
    return pl.pallas_call(
        _merge_kernel,
        grid=(b, s // ROW_TILE),
        in_specs=[_row_spec(D_MODEL), _row_spec(ATTN_DIM), _row_spec(RWKV_DIM), _row_spec(2 * D_MODEL),
                  _const_spec(wa_pad), _const_spec(wb), _const_spec(wo), _const_spec(fgain)],
        out_specs=[_row_spec(D_MODEL), _row_spec(D_MODEL)],
        out_shape=[jax.ShapeDtypeStruct((b, s, D_MODEL), F32),
                   jax.ShapeDtypeStruct((b, s, D_MODEL), BF16)],
        compiler_params=_params("parallel", "parallel"),
        name="merge",
    )(x, ya, yb, mg, wa_pad, wb, wo, fgain)


def _ffn_kernel(x_ref, h_ref, wu_ref, wd_ref, o_ref, *, fc):
    h = h_ref[...]
    acc = x_ref[...]
    for c in range(D_FF // fc):
        up = jnp.maximum(_nn(h, wu_ref[:, c * fc:(c + 1) * fc]), 0.0)
        acc = acc + _nn((up * up).astype(BF16), wd_ref[c * fc:(c + 1) * fc, :])
    o_ref[...] = acc


def _ffn_call(x, h, wu, wd, fc=1024):
    b, s, _ = x.shape
    return pl.pallas_call(
        functools.partial(_ffn_kernel, fc=fc),
        grid=(b, s // ROW_TILE),
        in_specs=[_row_spec(D_MODEL), _row_spec(D_MODEL), _const_spec(wu), _const_spec(wd)],
        out_specs=_row_spec(D_MODEL),
        out_shape=jax.ShapeDtypeStruct((b, s, D_MODEL), F32),
        compiler_params=_params("parallel", "parallel"),
        name="ffn",
    )(x, h, wu, wd)


def _pair_heads(w):
    rows = w.shape[0]
    w = w.reshape(rows, KV_GROUPS, HEADS_PER_GROUP, HEAD_DIM).transpose(0, 2, 1, 3)
    return w.reshape(rows, ATTN_DIM)


def _proj_weight(w_in):
    q = _pair_heads(w_in[:, 0:ATTN_DIM])
    kv = w_in[:, ATTN_DIM:ATTN_DIM + 6 * KV_DIM]
    gates = jnp.pad(w_in[:, ATTN_DIM + 6 * KV_DIM:NSA_DIM], ((0, 0), (0, LANES - NSA_GATES)))
    rest = w_in[:, NSA_DIM:]
    return jnp.concatenate([q, kv, gates, rest], axis=1).astype(BF16)


def _values_t(v, tile):
    b, s, _ = v.shape
    vt = v.reshape(b, s // tile, tile, KV_DIM).transpose(0, 1, 3, 2)
    row = jnp.arange(KV_DIM)[:, None] // HEAD_DIM
    one = jnp.ones((), v.dtype)
    return jnp.stack([jnp.where(row == g, vt, one) for g in range(KV_GROUPS)], axis=1)


def _compress_weights(w1, w2):
    w = w1.reshape(2, 2, CMP_STRIDE, HEAD_DIM, CMP_HIDDEN).transpose(0, 2, 3, 1, 4)
    zw = jnp.zeros_like(w)
    w1g = jnp.stack([jnp.stack([w if gg == g else zw for gg in range(KV_GROUPS)], axis=2)
                     for g in range(KV_GROUPS)], axis=1)
    w1g = w1g.reshape(2, KV_GROUPS, CMP_STRIDE * KV_DIM, 2 * CMP_HIDDEN)
    z2 = jnp.zeros_like(w2)
    w2g = jnp.stack([jnp.concatenate([w2 if gg == g else z2 for gg in range(KV_GROUPS)], axis=2)
                     for g in range(KV_GROUPS)], axis=1)
    return w1g.astype(BF16), w2g.astype(BF16)


def kernel(x, mix_norm, w_in, q_gain, k_gain, cmp_pos, cmp_w1, cmp_w2, w_attn_branch, tok_mix, w0,
           w_lora_up, a0, a_lora_up, g_lora_up, k_k, k_a, r_k, ln_x_w, ln_x_b, w_rwkv_branch, w_out,
           ffn_norm, w_ffn_up, w_ffn_down):
    b, s, _ = x.shape
    depth = w_in.shape[0]
    row = lambda v: v.reshape(1, -1)
    for l in range(depth):
        qg2 = jnp.tile(q_gain[l], KV_GROUPS).reshape(1, KV_DIM)
        kg2 = jnp.tile(k_gain[l, 1:3], (1, KV_GROUPS))
        q, kc, vc, ks, vs, kw, vw, gates, rw, mg = _proj_call(
            x, row(mix_norm[l]), _proj_weight(w_in[l]), qg2, kg2)

        w1g, w2g = _compress_weights(cmp_w1[l], cmp_w2[l])
        k_cmp, v_cmp = _compress_call(kc, vc, w1g, cmp_w1[l].astype(BF16), w2g,
                                      cmp_pos[l].reshape(2, 1, CMP_BLOCK * HEAD_DIM),
                                      jnp.tile(k_gain[l, 0], KV_GROUPS).reshape(1, KV_DIM))

        ya = _nsa_call(q, gates, k_cmp, v_cmp.transpose(0, 2, 1), ks,
                       _values_t(vs, NSA_KEY_TILE), kw, _values_t(vw, NSA_QUERY_TILE))

        zero = jnp.zeros((W_LORA, RWKV_DIM), F32)
        wup = jnp.concatenate([w_lora_up[l], zero], axis=0).astype(BF16)
        aup = jnp.concatenate([zero, a_lora_up[l]], axis=0).astype(BF16)
        yb = _rwkv_call(rw, row(tok_mix[l]), row(w0[l]), wup, row(a0[l]), aup,
                        g_lora_up[l].astype(BF16), row(k_k[l]), row(k_a[l]), row(r_k[l]),
                        row(ln_x_w[l]), row(ln_x_b[l]))

        wa_pad = _pair_heads(w_attn_branch[l].T).T.astype(BF16)
        x, h = _merge_call(x, ya, yb, mg, wa_pad, w_rwkv_branch[l].astype(BF16),
                           w_out[l].astype(BF16), row(ffn_norm[l]))
        x = _ffn_call(x, h, w_ffn_up[l].astype(BF16), w_ffn_down[l].astype(BF16))
    return x
```

```python
import functools

import jax
import jax.numpy as jnp
from jax import lax
from jax.experimental import pallas as pl
from jax.experimental.pallas import tpu as pltpu

F32 = jnp.float32
BF16 = jnp.bfloat16
I32 = jnp.int32

D_MODEL = 1024
ATTN_HEADS = 8
HEAD_DIM = 64
KV_GROUPS = 2
HEADS_PER_GROUP = ATTN_HEADS // KV_GROUPS
ATTN_DIM = ATTN_HEADS * HEAD_DIM
KV_DIM = KV_GROUPS * HEAD_DIM
N_NSA_BRANCHES = 3
CMP_BLOCK = 32
CMP_STRIDE = 16
CMP_HIDDEN = 2 * HEAD_DIM
SLC_BLOCK = 64
N_SELECT = 16
WINDOW = 512
FORCE_BONUS = 1000.0
RWKV_HEADS = 8
RWKV_HEAD_DIM = 64
RWKV_DIM = RWKV_HEADS * RWKV_HEAD_DIM
W_LORA = 64
A_LORA = 64
G_LORA = 128
RWKV_MIX_DIM = 3 * RWKV_DIM + W_LORA + A_LORA + G_LORA
D_FF = 4 * D_MODEL
NSA_GATES = N_NSA_BRANCHES * ATTN_HEADS
NSA_DIM = ATTN_DIM + 6 * KV_DIM + NSA_GATES
RMS_EPS = 1e-6
GN_EPS = 64e-5
NEG_INF = -1e30

LOG2_64 = 6
LANES = 128
SUBLANES = 8
MXU_COLS = 256
Q_SCALE = HEAD_DIM ** -0.5 * 1.4426950408889634
DECAY_SCALE = 0.6065306597126334
RWKV_PAIRS = RWKV_HEADS // 2
RWKV_CHUNK = 64
ROW_TILE = 512
NSA_QUERY_TILE = 256
NSA_KEY_TILE = 512
VMEM_LIMIT = 48 * 1024 * 1024

C_Q = 0
C_KV = C_Q + ATTN_DIM
C_GATE = C_KV + 6 * KV_DIM
C_RWKV = C_GATE + LANES
C_MERGE = C_RWKV + RWKV_MIX_DIM


def _nn(a, b):
    return lax.dot_general(a, b, (((1,), (0,)), ((), ())), preferred_element_type=F32)


def _nt(a, b):
    return lax.dot_general(a, b, (((1,), (1,)), ((), ())), preferred_element_type=F32)


def _bnn(a, b):
    return lax.dot_general(a, b, (((2,), (1,)), ((0,), (0,))), preferred_element_type=F32)


def _bnt(a, b):
    return lax.dot_general(a, b, (((2,), (2,)), ((0,), (0,))), preferred_element_type=F32)


def _btn(a, b):
    return lax.dot_general(a, b, (((1,), (1,)), ((0,), (0,))), preferred_element_type=F32)


def _split2(x):
    hi = x.astype(BF16)
    lo = (x - hi.astype(F32)).astype(BF16)
    return hi, lo


def _iota(shape, dim):
    return lax.broadcasted_iota(I32, shape, dim)


def _params(*sem):
    return pltpu.CompilerParams(dimension_semantics=sem, vmem_limit_bytes=VMEM_LIMIT)


def _group_rms(x, gain):
    lane = _iota((1, LANES), 1)
    sq = x * x
    lo = lane < HEAD_DIM
    s0 = jnp.sum(jnp.where(lo, sq, 0.0), axis=-1, keepdims=True)
    s1 = jnp.sum(jnp.where(lo, 0.0, sq), axis=-1, keepdims=True)
    ms = jnp.where(lo, s0, s1) * (1.0 / HEAD_DIM)
    return x * lax.rsqrt(ms + RMS_EPS) * gain


def _proj_kernel(x_ref, g_ref, w_ref, qg_ref, kg_ref,
                 q_ref, kc_ref, vc_ref, ks_ref, vs_ref, kw_ref, vw_ref, gt_ref, rw_ref, mg_ref):
    x = x_ref[...]
    ms = jnp.mean(x * x, axis=-1, keepdims=True)
    u = (x * lax.rsqrt(ms + RMS_EPS) * g_ref[...]).astype(BF16)

    def col(c0, width):
        return _nn(u, w_ref[:, c0:c0 + width])

    for piece in range(ATTN_DIM // MXU_COLS):
        qq = col(C_Q + piece * MXU_COLS, MXU_COLS)
        for half in range(MXU_COLS // LANES):
            slot = slice(piece * MXU_COLS + half * LANES, piece * MXU_COLS + (half + 1) * LANES)
            qn = _group_rms(qq[:, half * LANES:(half + 1) * LANES], qg_ref[...])
            q_ref[:, slot] = (qn * Q_SCALE).astype(BF16)
    kvc = col(C_KV, MXU_COLS)
    kc_ref[...] = kvc[:, 0:KV_DIM]
    vc_ref[...] = kvc[:, KV_DIM:2 * KV_DIM]
    kvs = col(C_KV + MXU_COLS, MXU_COLS)
    ks_ref[...] = _group_rms(kvs[:, 0:KV_DIM], kg_ref[0:1, :]).astype(BF16)
    vs_ref[...] = kvs[:, KV_DIM:2 * KV_DIM].astype(BF16)
    kvw = col(C_KV + 2 * MXU_COLS, MXU_COLS)
    kw_ref[...] = _group_rms(kvw[:, 0:KV_DIM], kg_ref[1:2, :]).astype(BF16)
    vw_ref[...] = kvw[:, KV_DIM:2 * KV_DIM].astype(BF16)
    gt_ref[...] = col(C_GATE, LANES)
    for c in range(RWKV_MIX_DIM // MXU_COLS):
        rw_ref[:, c * MXU_COLS:(c + 1) * MXU_COLS] = col(C_RWKV + c * MXU_COLS, MXU_COLS)
    for c in range(2 * D_MODEL // 512):
        mg_ref[:, c * 512:(c + 1) * 512] = col(C_MERGE + c * 512, 512).astype(BF16)


def _row_spec(width):
    return pl.BlockSpec((None, ROW_TILE, width), lambda i, j: (i, j, 0))


def _const_spec(a, buffers=2):
    return pl.BlockSpec(a.shape, lambda i, j: (0,) * a.ndim, pipeline_mode=pl.Buffered(buffers))


def _proj_call(x, gain, w_all, qg2, kg2):
    b, s, _ = x.shape
    widths = [(ATTN_DIM, BF16), (KV_DIM, F32), (KV_DIM, F32), (KV_DIM, BF16), (KV_DIM, BF16),
              (KV_DIM, BF16), (KV_DIM, BF16), (LANES, F32), (RWKV_MIX_DIM, F32), (2 * D_MODEL, BF16)]
    return pl.pallas_call(
        _proj_kernel,
        grid=(b, s // ROW_TILE),
        in_specs=[_row_spec(D_MODEL), _const_spec(gain), _const_spec(w_all, buffers=1),
                  _const_spec(qg2), _const_spec(kg2)],
        out_specs=[_row_spec(w) for w, _ in widths],
        out_shape=[jax.ShapeDtypeStruct((b, s, w), dt) for w, dt in widths],
        compiler_params=_params("parallel", "parallel"),
        name="proj",
    )(x, gain, w_all, qg2, kg2)


def _compress_kernel(kc_ref, vc_ref, w1g_ref, w1_ref, w2g_ref, pos_ref, kg_ref, ko_ref, vo_ref, *, nc):
    rowi = _iota((nc, 1), 0)
    for which, (src, dst) in enumerate(((kc_ref, ko_ref), (vc_ref, vo_ref))):
        xcat = jnp.concatenate([src[0, pl.ds(l, nc, stride=CMP_STRIDE), :] for l in range(CMP_STRIDE)],
                               axis=1).astype(BF16)
        pos8 = jnp.broadcast_to(pos_ref[which], (SUBLANES, CMP_BLOCK * HEAD_DIM)).astype(BF16)
        bias = _nn(pos8, w1_ref[which])[0:1, :]
        out = jnp.zeros((nc, KV_DIM), F32)
        for g in range(KV_GROUPS):
            hab = _nn(xcat, w1g_ref[which, g])
            hid = hab[:, 0:CMP_HIDDEN] + pltpu.roll(hab[:, CMP_HIDDEN:2 * CMP_HIDDEN], nc - 1, axis=0) + bias
            out = out + _nn(jax.nn.gelu(hid).astype(BF16), w2g_ref[which, g])
        if which == 0:
            out = _group_rms(out, kg_ref[...])
        dst[0] = jnp.where(rowi < nc - 1, out, 0.0).astype(BF16)


def _compress_call(kc, vc, w1g, w1, w2g, pos_flat, kgain2):
    b, s, _ = kc.shape
    nc = s // CMP_STRIDE
    seq = pl.BlockSpec((1, s, KV_DIM), lambda i: (i, 0, 0))
    full = lambda a: pl.BlockSpec(a.shape, lambda i: (0,) * a.ndim)
    out = pl.BlockSpec((1, nc, KV_DIM), lambda i: (i, 0, 0))
    return pl.pallas_call(
        functools.partial(_compress_kernel, nc=nc),
        grid=(b,),
        in_specs=[seq, seq, full(w1g), full(w1), full(w2g), full(pos_flat), full(kgain2)],
        out_specs=[out, out],
        out_shape=[jax.ShapeDtypeStruct((b, nc, KV_DIM), BF16)] * 2,
        compiler_params=_params("parallel"),
        name="compress",
    )(kc, vc, w1g, w1, w2g, pos_flat, kgain2)


def _nsa_kernel(q_ref, gt_ref, kc_ref, vct_ref, ks_ref, vst_ref, kw_ref, vwt_ref, y_ref,
                m_s, acc_s, sb_s, rk_s, *, tq, tk, nb, nc):
    nh = ATTN_HEADS
    hp = HEADS_PER_GROUP
    cols = hp * tq
    wk = WINDOW + tq
    bpt = tk // SLC_BLOCK
    qi = pl.program_id(1)
    q0 = qi * tq
    tlane = q0 + _iota((1, tq), 1)
    tile_h = lambda a, reps: jnp.concatenate([a] * reps, axis=1)
    eye_d = jnp.where(_iota((LANES, LANES), 0) == _iota((LANES, LANES), 1), 1.0, 0.0).astype(BF16)
    eye_q = jnp.where(_iota((tq, tq), 0) == _iota((tq, tq), 1), 1.0, 0.0).astype(BF16)
    drow = _iota((LANES, 1), 0)
    own = [(drow >= g * HEAD_DIM) & (drow < (g + 1) * HEAD_DIM) for g in range(KV_GROUPS)]

    lane = _iota((1, LANES), 1)
    glanes = [jnp.where((lane >= g * HEAD_DIM) & (lane < (g + 1) * HEAD_DIM), 1.0, 0.0).astype(BF16)
              for g in range(KV_GROUPS)]
    qall = jnp.concatenate([q_ref[0, :, h * LANES:(h + 1) * LANES] * glanes[g]
                            for g in range(KV_GROUPS) for h in range(hp)], axis=0)
    q_t = _nt(eye_d, qall).astype(BF16)

    ncol = _iota((nc, 1), 0)
    cvalid = (ncol * CMP_STRIDE + (CMP_BLOCK - 1) <= tlane) & (ncol < nc - 1)
    s_c = _nn(kc_ref[0], q_t) + tile_h(jnp.where(cvalid, 0.0, NEG_INF), nh)
    e_c = jnp.exp2(s_c - jnp.max(s_c, axis=0, keepdims=True))
    p_c = e_c * (1.0 / jnp.sum(e_c, axis=0, keepdims=True)) * tile_h(jnp.where(cvalid, 1.0, 0.0), nh)
    p_cb = p_c.astype(BF16)
    o_c = [_nn(vct_ref[0], p_cb[:, g * cols:(g + 1) * cols]) for g in range(KV_GROUPS)]

    jcol = _iota((nb, 1), 0)
    ncmp = _iota((nb, nc), 1)
    ov = ((ncmp * CMP_STRIDE <= jcol * SLC_BLOCK + (SLC_BLOCK - 1))
          & (ncmp * CMP_STRIDE + (CMP_BLOCK - 1) >= jcol * SLC_BLOCK))
    ov = jnp.where(ov, 1.0, 0.0).astype(BF16)
    cur = jnp.right_shift(tlane, LOG2_64)
    forced = (jcol == 0) | (jcol == cur) | (jcol == cur - 1)
    causal_blk = jcol * SLC_BLOCK <= tlane
    imps = []
    for g in range(KV_GROUPS):
        c0 = g * cols
        psum = (p_c[:, c0:c0 + tq] + p_c[:, c0 + tq:c0 + 2 * tq]
                + p_c[:, c0 + 2 * tq:c0 + 3 * tq] + p_c[:, c0 + 3 * tq:c0 + 4 * tq])
        hi, lo = _split2(psum)
        imp = _nn(ov, hi) + _nn(ov, lo)
        imps.append(jnp.where(causal_blk, imp + jnp.where(forced, FORCE_BONUS, 0.0), -1.0))

    rk_s[...] = jnp.zeros(rk_s.shape, F32)
    jsub = _iota((SUBLANES, 1), 0)
    last_blk = jnp.right_shift(q0 + tq - 1, LOG2_64)
    for sg in range(nb // SUBLANES):
        @pl.when(sg * SUBLANES <= last_blk)
        def _(sg=sg):
            for g, imp in enumerate(imps):
                groups = [imp[r:r + SUBLANES, :] for r in range(0, nb, SUBLANES)]
                ranks = [jnp.zeros((SUBLANES, tq), F32) for _ in groups]
                for i in range(sg * SUBLANES, (sg + 1) * SUBLANES):
                    ri = imp[i:i + 1, :]
                    for gi, grp in enumerate(groups):
                        r0 = gi * SUBLANES
                        if r0 + SUBLANES - 1 < i:
                            ahead = jnp.where(ri > grp, 1.0, 0.0)
                        elif r0 > i:
                            ahead = jnp.where(ri >= grp, 1.0, 0.0)
                        else:
                            ahead = jnp.where(jsub > i - r0, jnp.where(ri >= grp, 1.0, 0.0),
                                              jnp.where(ri > grp, 1.0, 0.0))
                        ranks[gi] = ranks[gi] + ahead
                rk_s[g] = rk_s[g] + jnp.concatenate(ranks, axis=0)
    sb_s[...] = jnp.where(rk_s[...] < float(min(N_SELECT, nb)), 0.0, NEG_INF)

    m_s[...] = jnp.full(m_s.shape, NEG_INF, F32)
    acc_s[...] = jnp.zeros(acc_s.shape, F32)

    def sweep(tiles):
        scores = [_nn(ks_ref[0, pl.ds(pl.multiple_of(kt * tk, tk), tk), :], q_t).astype(BF16)
                  for kt in tiles]
        m_run = m_s[0:1, :]
        acc_run = [acc_s[:, g * cols:(g + 1) * cols] for g in range(KV_GROUPS)]
        m_out = [None] * KV_GROUPS
        for kt, s in zip(tiles, scores):
            causal = (kt * tk + _iota((tk, 1), 0)) <= tlane
            for g in range(KV_GROUPS):
                cs = slice(g * cols, (g + 1) * cols)
                bias = jnp.concatenate(
                    [jnp.broadcast_to(sb_s[g, pl.ds(kt * bpt + i, 1), :], (SLC_BLOCK, tq))
                     for i in range(bpt)], axis=0)
                sm = s[:, cs] + tile_h(jnp.where(causal, bias, NEG_INF).astype(BF16), hp)
                m_prev = m_run[:, cs] if m_out[g] is None else m_out[g]
                m_new = jnp.maximum(m_prev, jnp.max(sm, axis=0, keepdims=True).astype(F32))
                p = jnp.exp2(sm - m_new.astype(BF16))
                acc_run[g] = jnp.exp2(m_prev - m_new) * acc_run[g] + _nn(vst_ref[0, g, kt], p)
                m_out[g] = m_new
        for g in range(KV_GROUPS):
            cs = slice(g * cols, (g + 1) * cols)
            m_s[:, cs] = jnp.broadcast_to(m_out[g], (SUBLANES, cols))
            acc_s[:, cs] = acc_run[g]

    n_tiles = qi // (tk // tq) + 1

    def pair(i, _):
        sweep([2 * i, 2 * i + 1])
        return 0

    lax.fori_loop(0, n_tiles // 2, pair, 0)

    @pl.when(n_tiles % 2 == 1)
    def _():
        sweep([n_tiles - 1])

    w0 = pl.multiple_of(jnp.maximum(q0 - WINDOW, 0), tq)
    dist = tlane - (w0 + _iota((wk, 1), 0))
    s_w = (_nn(kw_ref[0, pl.ds(w0, wk), :], q_t).astype(BF16)
           + tile_h(jnp.where((dist >= 0) & (dist < WINDOW), 0.0, NEG_INF).astype(BF16), nh))
    p_w = jnp.exp2(s_w - jnp.max(s_w, axis=0, keepdims=True))
    wb = w0 // tq

    g_hi, g_lo = _split2(jax.nn.sigmoid(gt_ref[0]))
    gate_t = _nt(eye_d, g_hi) + _nt(eye_d, g_lo)
    gated = []
    for g in range(KV_GROUPS):
        cs = slice(g * cols, (g + 1) * cols)
        den_row = (1 - g) * HEAD_DIM
        acc = acc_s[:, cs]
        o_s = acc * (1.0 / acc[den_row:den_row + 1, :])
        v_w = jnp.concatenate([vwt_ref[0, g, wb + i] for i in range(wk // tq)], axis=1)
        acc_w = _nn(v_w, p_w[:, cs])
        o_w = acc_w * (1.0 / acc_w[den_row:den_row + 1, :])
        for h in range(hp):
            r0 = (g * hp + h) * N_NSA_BRANCHES
            hs = slice(h * tq, (h + 1) * tq)
            gated.append(gate_t[r0:r0 + 1, :] * o_c[g][:, hs] + gate_t[r0 + 1:r0 + 2, :] * o_s[:, hs]
                         + gate_t[r0 + 2:r0 + 3, :] * o_w[:, hs])
    for h in range(hp):
        o = jnp.where(own[0], gated[h], gated[hp + h]).astype(BF16)
        y_ref[0, :, h * LANES:(h + 1) * LANES] = _nt(eye_q, o).astype(BF16)


def _nsa_call(q, gates, kc, vct, ks, vst, kw, vwt):
    b, s, _ = q.shape
    tq = NSA_QUERY_TILE
    nb = s // SLC_BLOCK
    nc = kc.shape[1]
    tk = vst.shape[-1]
    tile = lambda w: pl.BlockSpec((1, tq, w), lambda i, j: (i, j, 0))
    seq = lambda a: pl.BlockSpec((1,) + a.shape[1:], lambda i, j: (i,) + (0,) * (a.ndim - 1))
    return pl.pallas_call(
        functools.partial(_nsa_kernel, tq=tq, tk=tk, nb=nb, nc=nc),
        grid=(b, s // tq),
        in_specs=[tile(ATTN_DIM), tile(LANES), seq(kc), seq(vct), seq(ks), seq(vst), seq(kw), seq(vwt)],
        out_specs=tile(ATTN_DIM),
        out_shape=jax.ShapeDtypeStruct((b, s, ATTN_DIM), BF16),
        scratch_shapes=[pltpu.VMEM((SUBLANES, ATTN_HEADS * tq), F32),
                        pltpu.VMEM((LANES, ATTN_HEADS * tq), F32),
                        pltpu.VMEM((KV_GROUPS, nb, tq), F32),
                        pltpu.VMEM((KV_GROUPS, nb, tq), F32)],
        compiler_params=_params("parallel", "arbitrary"),
        name="nsa",
    )(q, gates, kc, vct, ks, vst, kw, vwt)


def _seg_sum(x, bd):
    hi, lo = _split2(x)
    w = bd.shape[0]
    return jnp.concatenate([_nn(hi[:, c:c + w], bd) + _nn(lo[:, c:c + w], bd)
                            for c in range(0, x.shape[1], w)], axis=1)


def _stack_heads(x):
    lane = _iota((1, LANES), 1)
    lo = lane < RWKV_HEAD_DIM
    return jnp.concatenate([jnp.where(lo, x, 0.0), jnp.where(lo, 0.0, x)], axis=0)


def _rwkv_kernel(p_ref, mu_ref, w0_ref, wup_ref, a0_ref, aup_ref, gup_ref, kk_ref, ka_ref, rk_ref,
                 lnw_ref, lnb_ref, o_ref, carry_ref, state_ref, *, tt):
    ch = RWKV_CHUNK
    c2 = 2 * ch
    d = RWKV_DIM
    nch = tt // ch
    npr = RWKV_PAIRS

    @pl.when(pl.program_id(1) == 0)
    def _():
        carry_ref[...] = jnp.zeros_like(carry_ref)
        state_ref[...] = jnp.zeros_like(state_ref)

    p = p_ref[0]
    prev = pltpu.roll(p, 1, axis=0)
    prev = jnp.where(_iota((tt, 1), 0) == 0, carry_ref[0:1, :], prev)
    carry_ref[0:1, :] = p[tt - 1:tt, :]
    pm = p + (prev - p) * mu_ref[...]

    r = pm[:, 0:d]
    k = pm[:, d:2 * d]
    v = pm[:, 2 * d:3 * d]
    wa = pm[:, 3 * d:3 * d + LANES]
    gl = pm[:, 3 * d + LANES:3 * d + 2 * LANES]

    z = w0_ref[...] + _nn(jnp.tanh(wa).astype(BF16), wup_ref[...])
    lw = -DECAY_SCALE * jax.nn.sigmoid(z)
    a = jax.nn.sigmoid(a0_ref[...] + _nn(wa.astype(BF16), aup_ref[...]))
    gate = _nn(jax.nn.sigmoid(gl).astype(BF16), gup_ref[...])

    seg = (jnp.right_shift(_iota((MXU_COLS, MXU_COLS), 0), LOG2_64)
           == jnp.right_shift(_iota((MXU_COLS, MXU_COLS), 1), LOG2_64))
    bd = jnp.where(seg, 1.0, 0.0).astype(BF16)
    kk = k * kk_ref[...]
    kk = kk * lax.rsqrt(jnp.maximum(_seg_sum(kk * kk, bd), 1e-24))
    k2 = k * (1.0 + (a - 1.0) * ka_ref[...])

    ti = _iota((tt, tt), 0)
    tj = _iota((tt, tt), 1)
    same_chunk = jnp.right_shift(ti, LOG2_64) == jnp.right_shift(tj, LOG2_64)
    tri = jnp.where(same_chunk & (tj <= ti), 1.0, 0.0).astype(BF16)
    hi, lo = _split2(lw)
    cum = _nn(tri, hi) + _nn(tri, lo)
    tot = jnp.concatenate([jnp.broadcast_to(cum[(c + 1) * ch - 1:(c + 1) * ch, :], (ch, d))
                           for c in range(nch)], axis=0)
    e_inv = jnp.exp(-cum)
    e_end = jnp.exp(tot - cum)
    dec = jnp.exp(tot)
    bb = kk * a

    def tiles(x):
        return jnp.stack([_stack_heads(x[c * ch:(c + 1) * ch, pr * LANES:(pr + 1) * LANES])
                          for c in range(nch) for pr in range(npr)]).astype(BF16)

    a_t = tiles(-kk * jnp.exp(cum - lw))
    r_t = tiles(r * jnp.exp(cum))
    b_h = tiles(bb * e_inv)
    k_h = tiles(k2 * e_inv)
    bke = jnp.concatenate([tiles(bb * e_end), tiles(k2 * e_end)], axis=1)
    v_t = tiles(v)

    ri = _iota((c2, c2), 0)
    ci = _iota((c2, c2), 1)
    same = jnp.right_shift(ri, LOG2_64) == jnp.right_shift(ci, LOG2_64)
    strict = jnp.where(same & (ci < ri), 1.0, 0.0)
    incl = jnp.where(same & (ci <= ri), 1.0, 0.0)
    ident = jnp.where(ri == ci, 1.0, 0.0)

    scores = _bnt(jnp.concatenate([a_t, r_t], axis=1), jnp.concatenate([b_h, k_h], axis=1))
    m_ab = scores[:, 0:c2, 0:c2] * strict
    m_ak = (scores[:, 0:c2, c2:2 * c2] * strict).astype(BF16)
    m_rbk = jnp.concatenate([scores[:, c2:2 * c2, 0:c2] * incl,
                             scores[:, c2:2 * c2, c2:2 * c2] * incl], axis=2).astype(BF16)
    tinv = ident + m_ab
    mp = m_ab
    for _ in range(5):
        mpb = mp.astype(BF16)
        mp = _bnn(mpb, mpb)
        tinv = tinv + _bnn(tinv.astype(BF16), mp.astype(BF16))
    rhs = jnp.concatenate([a_t, _bnn(m_ak, v_t).astype(BF16)], axis=2)
    wu = _bnn(tinv.astype(BF16), rhs)
    w_t = wu[:, :, 0:c2].astype(BF16)
    u0 = wu[:, :, c2:2 * c2]

    g_st = state_ref[...]
    y_rows = []
    for c in range(nch):
        sl = slice(c * npr, (c + 1) * npr)
        g_b = g_st.astype(BF16)
        u = _bnt(w_t[sl], g_b) + u0[sl]
        uv = jnp.concatenate([u.astype(BF16), v_t[sl]], axis=1)
        y = _bnt(r_t[sl], g_b) + _bnn(m_rbk[sl], uv)
        y = y[:, 0:ch, :] + y[:, ch:c2, :]
        y_rows.append(jnp.concatenate([y[pr] for pr in range(npr)], axis=1))
        dec_c = jnp.stack([dec[c * ch:c * ch + 1, pr * LANES:(pr + 1) * LANES] for pr in range(npr)])
        g_st = g_st * dec_c + _btn(uv, bke[sl])
    state_ref[...] = g_st
    y = jnp.concatenate(y_rows, axis=0)

    inv = 1.0 / RWKV_HEAD_DIM
    mean = _seg_sum(y, bd) * inv
    yc = y - mean
    var = _seg_sum(yc * yc, bd) * inv
    yn = yc * lax.rsqrt(var + GN_EPS) * lnw_ref[...] + lnb_ref[...]
    bonus = _seg_sum(r * k2 * rk_ref[...], bd) * v
    o_ref[0] = ((yn + bonus) * gate).astype(BF16)


def _rwkv_call(p, mu, w0, wup, a0, aup, gup, k_k, k_a, r_k, ln_w, ln_b, tt=256):
    b, s, _ = p.shape
    full = lambda a: pl.BlockSpec(a.shape, lambda i, j: (0,) * a.ndim)
    consts = (mu, w0, wup, a0, aup, gup, k_k, k_a, r_k, ln_w, ln_b)
    return pl.pallas_call(
        functools.partial(_rwkv_kernel, tt=tt),
        grid=(b, s // tt),
        in_specs=[pl.BlockSpec((1, tt, RWKV_MIX_DIM), lambda i, j: (i, j, 0))]
                 + [full(c) for c in consts],
        out_specs=pl.BlockSpec((1, tt, RWKV_DIM), lambda i, j: (i, j, 0)),
        out_shape=jax.ShapeDtypeStruct((b, s, RWKV_DIM), BF16),
        scratch_shapes=[pltpu.VMEM((SUBLANES, RWKV_MIX_DIM), F32),
                        pltpu.VMEM((RWKV_PAIRS, LANES, LANES), F32)],
        compiler_params=_params("arbitrary", "arbitrary"),
        name="rwkv",
    )(p, *consts)


def _merge_kernel(x_ref, ya_ref, yb_ref, mg_ref, wa_ref, wb_ref, wo_ref, fg_ref, xo_ref, h_ref):
    ga = jax.nn.sigmoid(mg_ref[:, 0:D_MODEL].astype(F32))
    gb = jax.nn.sigmoid(mg_ref[:, D_MODEL:2 * D_MODEL].astype(F32))
    merged = ga * _nn(ya_ref[...], wa_ref[...]) + gb * _nn(yb_ref[...], wb_ref[...])
    xn = x_ref[...] + _nn(merged.astype(BF16), wo_ref[...])
    xo_ref[...] = xn
    ms = jnp.mean(xn * xn, axis=-1, keepdims=True)
    h_ref[...] = (xn * lax.rsqrt(ms + RMS_EPS) * fg_ref[...]).astype(BF16)


def _merge_call(x, ya, yb, mg, wa_pad, wb, wo, fgain):
    b, s, _ = x.shape
    return pl.pallas_call(
        _merge_kernel,
        grid=(b, s // ROW_TILE),
        in_specs=[_row_spec(D_MODEL), _row_spec(ATTN_DIM), _row_spec(RWKV_DIM), _row_spec(2 * D_MODEL),
                  _const_spec(wa_pad), _const_spec(wb), _const_spec(wo), _const_spec(fgain)],
        out_specs=[_row_spec(D_MODEL), _row_spec(D_MODEL)],
        out_shape=[jax.ShapeDtypeStruct((b, s, D_MODEL), F32),
                   jax.ShapeDtypeStruct((b, s, D_MODEL), BF16)],
        compiler_params=_params("parallel", "parallel"),
        name="merge",
    )(x, ya, yb, mg, wa_pad, wb, wo, fgain)


def _ffn_kernel(x_ref, h_ref, wu_ref, wd_ref, o_ref, *, fc):
    h = h_ref[...]
    acc = x_ref[...]
    for c in range(D_FF // fc):
        up = jnp.maximum(_nn(h, wu_ref[:, c * fc:(c + 1) * fc]), 0.0)
        acc = acc + _nn((up * up).astype(BF16), wd_ref[c * fc:(c + 1) * fc, :])
    o_ref[...] = acc


def _ffn_call(x, h, wu, wd, fc=1024):
    b, s, _ = x.shape
    return pl.pallas_call(
        functools.partial(_ffn_kernel, fc=fc),
        grid=(b, s // ROW_TILE),
        in_specs=[_row_spec(D_MODEL), _row_spec(D_MODEL), _const_spec(wu), _const_spec(wd)],
        out_specs=_row_spec(D_MODEL),
        out_shape=jax.ShapeDtypeStruct((b, s, D_MODEL), F32),
        compiler_params=_params("parallel", "parallel"),
        name="ffn",
    )(x, h, wu, wd)


def _pair_heads(w):
    rows = w.shape[0]
    w = w.reshape(rows, KV_GROUPS, HEADS_PER_GROUP, HEAD_DIM).transpose(0, 2, 1, 3)
    return w.reshape(rows, ATTN_DIM)


def _proj_weight(w_in):
    q = _pair_heads(w_in[:, 0:ATTN_DIM])
    kv = w_in[:, ATTN_DIM:ATTN_DIM + 6 * KV_DIM]
    gates = jnp.pad(w_in[:, ATTN_DIM + 6 * KV_DIM:NSA_DIM], ((0, 0), (0, LANES - NSA_GATES)))
    rest = w_in[:, NSA_DIM:]
    return jnp.concatenate([q, kv, gates, rest], axis=1).astype(BF16)


def _values_t(v, tile):
    b, s, _ = v.shape
    vt = v.reshape(b, s // tile, tile, KV_DIM).transpose(0, 1, 3, 2)
    row = jnp.arange(KV_DIM)[:, None] // HEAD_DIM
    one = jnp.ones((), v.dtype)
    return jnp.stack([jnp.where(row == g, vt, one) for g in range(KV_GROUPS)], axis=1)


def _compress_weights(w1, w2):
    w = w1.reshape(2, 2, CMP_STRIDE, HEAD_DIM, CMP_HIDDEN).transpose(0, 2, 3, 1, 4)
    zw = jnp.zeros_like(w)
    w1g = jnp.stack([jnp.stack([w if gg == g else zw for gg in range(KV_GROUPS)], axis=2)
                     for g in range(KV_GROUPS)], axis=1)
    w1g = w1g.reshape(2, KV_GROUPS, CMP_STRIDE * KV_DIM, 2 * CMP_HIDDEN)
    z2 = jnp.zeros_like(w2)
    w2g = jnp.stack([jnp.concatenate([w2 if gg == g else z2 for gg in range(KV_GROUPS)], axis=2)
                     for g in range(KV_GROUPS)], axis=1)
    return w1g.astype(BF16), w2g.astype(BF16)


def kernel(x, mix_norm, w_in, q_gain, k_gain, cmp_pos, cmp_w1, cmp_w2, w_attn_branch, tok_mix, w0,
           w_lora_up, a0, a_lora_up, g_lora_up, k_k, k_a, r_k, ln_x_w, ln_x_b, w_rwkv_branch, w_out,
           ffn_norm, w_ffn_up, w_ffn_down):
    b, s, _ = x.shape
    depth = w_in.shape[0]
    row = lambda v: v.reshape(1, -1)
    for l in range(depth):
        qg2 = jnp.tile(q_gain[l], KV_GROUPS).reshape(1, KV_DIM)
        kg2 = jnp.tile(k_gain[l, 1:3], (1, KV_GROUPS))
        q, kc, vc, ks, vs, kw, vw, gates, rw, mg = _proj_call(
            x, row(mix_norm[l]), _proj_weight(w_in[l]), qg2, kg2)

        w1g, w2g = _compress_weights(cmp_w1[l], cmp_w2[l])
        k_cmp, v_cmp = _compress_call(kc, vc, w1g, cmp_w1[l].astype(BF16), w2g,
                                      cmp_pos[l].reshape(2, 1, CMP_BLOCK * HEAD_DIM),
                                      jnp.tile(k_gain[l, 0], KV_GROUPS).reshape(1, KV_DIM))

        ya = _nsa_call(q, gates, k_cmp, v_cmp.transpose(0, 2, 1), ks,
                       _values_t(vs, NSA_KEY_TILE), kw, _values_t(vw, NSA_QUERY_TILE))

        zero = jnp.zeros((W_LORA, RWKV_DIM), F32)
        wup = jnp.concatenate([w_lora_up[l], zero], axis=0).astype(BF16)
        aup = jnp.concatenate([zero, a_lora_up[l]], axis=0).astype(BF16)
        yb = _rwkv_call(rw, row(tok_mix[l]), row(w0[l]), wup, row(a0[l]), aup,
                        g_lora_up[l].astype(BF16), row(k_k[l]), row(k_a[l]), row(r_k[l]),
                        row(ln_x_w[l]), row(ln_x_b[l]))

        wa_pad = _pair_heads(w_attn_branch[l].T).T.astype(BF16)
        x, h = _merge_call(x, ya, yb, mg, wa_pad, w_rwkv_branch[l].astype(BF16),
                           w_out[l].astype(BF16), row(ffn_norm[l]))
        x = _ffn_call(x, h, w_ffn_up[l].astype(BF16), w_ffn_down[l].astype(BF16))
    return x
```

```python
import functools

import jax
import jax.numpy as jnp
from jax import lax
from jax.experimental import pallas as pl
from jax.experimental.pallas import tpu as pltpu

F32 = jnp.float32
BF16 = jnp.bfloat16
I32 = jnp.int32

D_MODEL = 1024
ATTN_HEADS = 8
HEAD_DIM = 64
KV_GROUPS = 2
HEADS_PER_GROUP = ATTN_HEADS // KV_GROUPS
ATTN_DIM = ATTN_HEADS * HEAD_DIM
KV_DIM = KV_GROUPS * HEAD_DIM
N_NSA_BRANCHES = 3
CMP_BLOCK = 32
CMP_STRIDE = 16
CMP_HIDDEN = 2 * HEAD_DIM
SLC_BLOCK = 64
N_SELECT = 16
WINDOW = 512
FORCE_BONUS = 1000.0
RWKV_HEADS = 8
RWKV_HEAD_DIM = 64
RWKV_DIM = RWKV_HEADS * RWKV_HEAD_DIM
W_LORA = 64
A_LORA = 64
G_LORA = 128
RWKV_MIX_DIM = 3 * RWKV_DIM + W_LORA + A_LORA + G_LORA
D_FF = 4 * D_MODEL
NSA_GATES = N_NSA_BRANCHES * ATTN_HEADS
NSA_DIM = ATTN_DIM + 6 * KV_DIM + NSA_GATES
RMS_EPS = 1e-6
GN_EPS = 64e-5
NEG_INF = -1e30

LOG2_64 = 6
LANES = 128
SUBLANES = 8
MXU_COLS = 256
Q_SCALE = HEAD_DIM ** -0.5 * 1.4426950408889634
DECAY_SCALE = 0.6065306597126334
RWKV_PAIRS = RWKV_HEADS // 2
RWKV_CHUNK = 64
ROW_TILE = 512
NSA_QUERY_TILE = 256
NSA_KEY_TILE = 512
VMEM_LIMIT = 48 * 1024 * 1024

C_Q = 0
C_KV = C_Q + ATTN_DIM
C_GATE = C_KV + 6 * KV_DIM
C_RWKV = C_GATE + LANES
C_MERGE = C_RWKV + RWKV_MIX_DIM


def _nn(a, b):
    return lax.dot_general(a, b, (((1,), (0,)), ((), ())), preferred_element_type=F32)


def _nt(a, b):
    return lax.dot_general(a, b, (((1,), (1,)), ((), ())), preferred_element_type=F32)


def _bnn(a, b):
    return lax.dot_general(a, b, (((2,), (1,)), ((0,), (0,))), preferred_element_type=F32)


def _bnt(a, b):
    return lax.dot_general(a, b, (((2,), (2,)), ((0,), (0,))), preferred_element_type=F32)


def _btn(a, b):
    return lax.dot_general(a, b, (((1,), (1,)), ((0,), (0,))), preferred_element_type=F32)


def _split2(x):
    hi = x.astype(BF16)
    lo = (x - hi.astype(F32)).astype(BF16)
    return hi, lo


def _iota(shape, dim):
    return lax.broadcasted_iota(I32, shape, dim)


def _params(*sem):
    return pltpu.CompilerParams(dimension_semantics=sem, vmem_limit_bytes=VMEM_LIMIT)


def _group_rms(x, gain):
    lane = _iota((1, LANES), 1)
    sq = x * x
    lo = lane < HEAD_DIM
    s0 = jnp.sum(jnp.where(lo, sq, 0.0), axis=-1, keepdims=True)
    s1 = jnp.sum(jnp.where(lo, 0.0, sq), axis=-1, keepdims=True)
    ms = jnp.where(lo, s0, s1) * (1.0 / HEAD_DIM)
    return x * lax.rsqrt(ms + RMS_EPS) * gain


def _proj_kernel(x_ref, g_ref, w_ref, qg_ref, kg_ref,
                 q_ref, kc_ref, vc_ref, ks_ref, vs_ref, kw_ref, vw_ref, gt_ref, rw_ref, mg_ref):
    x = x_ref[...]
    ms = jnp.mean(x * x, axis=-1, keepdims=True)
    u = (x * lax.rsqrt(ms + RMS_EPS) * g_ref[...]).astype(BF16)

    def col(c0, width):
        return _nn(u, w_ref[:, c0:c0 + width])

    for piece in range(ATTN_DIM // MXU_COLS):
        qq = col(C_Q + piece * MXU_COLS, MXU_COLS)
        for half in range(MXU_COLS // LANES):
            slot = slice(piece * MXU_COLS + half * LANES, piece * MXU_COLS + (half + 1) * LANES)
            qn = _group_rms(qq[:, half * LANES:(half + 1) * LANES], qg_ref[...])
            q_ref[:, slot] = (qn * Q_SCALE).astype(BF16)
    kvc = col(C_KV, MXU_COLS)
    kc_ref[...] = kvc[:, 0:KV_DIM]
    vc_ref[...] = kvc[:, KV_DIM:2 * KV_DIM]
    kvs = col(C_KV + MXU_COLS, MXU_COLS)
    ks_ref[...] = _group_rms(kvs[:, 0:KV_DIM], kg_ref[0:1, :]).astype(BF16)
    vs_ref[...] = kvs[:, KV_DIM:2 * KV_DIM].astype(BF16)
    kvw = col(C_KV + 2 * MXU_COLS, MXU_COLS)
    kw_ref[...] = _group_rms(kvw[:, 0:KV_DIM], kg_ref[1:2, :]).astype(BF16)
    vw_ref[...] = kvw[:, KV_DIM:2 * KV_DIM].astype(BF16)
    gt_ref[...] = col(C_GATE, LANES)
    for c in range(RWKV_MIX_DIM // MXU_COLS):
        rw_ref[:, c * MXU_COLS:(c + 1) * MXU_COLS] = col(C_RWKV + c * MXU_COLS, MXU_COLS)
    for c in range(2 * D_MODEL // 512):
        mg_ref[:, c * 512:(c + 1) * 512] = col(C_MERGE + c * 512, 512).astype(BF16)


def _row_spec(width):
    return pl.BlockSpec((None, ROW_TILE, width), lambda i, j: (i, j, 0))


def _const_spec(a, buffers=2):
    return pl.BlockSpec(a.shape, lambda i, j: (0,) * a.ndim, pipeline_mode=pl.Buffered(buffers))


def _proj_call(x, gain, w_all, qg2, kg2):
    b, s, _ = x.shape
    widths = [(ATTN_DIM, BF16), (KV_DIM, F32), (KV_DIM, F32), (KV_DIM, BF16), (KV_DIM, BF16),
              (KV_DIM, BF16), (KV_DIM, BF16), (LANES, F32), (RWKV_MIX_DIM, F32), (2 * D_MODEL, BF16)]
    return pl.pallas_call(
        _proj_kernel,
        grid=(b, s // ROW_TILE),
        in_specs=[_row_spec(D_MODEL), _const_spec(gain), _const_spec(w_all, buffers=1),
                  _const_spec(qg2), _const_spec(kg2)],
        out_specs=[_row_spec(w) for w, _ in widths],
        out_shape=[jax.ShapeDtypeStruct((b, s, w), dt) for w, dt in widths],
        compiler_params=_params("parallel", "parallel"),
        name="proj",
    )(x, gain, w_all, qg2, kg2)


def _compress_kernel(kc_ref, vc_ref, w1g_ref, w1_ref, w2g_ref, pos_ref, kg_ref, ko_ref, vo_ref, *, nc):
    rowi = _iota((nc, 1), 0)
    for which, (src, dst) in enumerate(((kc_ref, ko_ref), (vc_ref, vo_ref))):
        xcat = jnp.concatenate([src[0, pl.ds(l, nc, stride=CMP_STRIDE), :] for l in range(CMP_STRIDE)],
                               axis=1).astype(BF16)
        pos8 = jnp.broadcast_to(pos_ref[which], (SUBLANES, CMP_BLOCK * HEAD_DIM)).astype(BF16)
        bias = _nn(pos8, w1_ref[which])[0:1, :]
        out = jnp.zeros((nc, KV_DIM), F32)
        for g in range(KV_GROUPS):
            hab = _nn(xcat, w1g_ref[which, g])
            hid = hab[:, 0:CMP_HIDDEN] + pltpu.roll(hab[:, CMP_HIDDEN:2 * CMP_HIDDEN], nc - 1, axis=0) + bias
            out = out + _nn(jax.nn.gelu(hid).astype(BF16), w2g_ref[which, g])
        if which == 0:
            out = _group_rms(out, kg_ref[...])
        dst[0] = jnp.where(rowi < nc - 1, out, 0.0).astype(BF16)


def _compress_call(kc, vc, w1g, w1, w2g, pos_flat, kgain2):
    b, s, _ = kc.shape
    nc = s // CMP_STRIDE
    seq = pl.BlockSpec((1, s, KV_DIM), lambda i: (i, 0, 0))
    full = lambda a: pl.BlockSpec(a.shape, lambda i: (0,) * a.ndim)
    out = pl.BlockSpec((1, nc, KV_DIM), lambda i: (i, 0, 0))
    return pl.pallas_call(
        functools.partial(_compress_kernel, nc=nc),
        grid=(b,),
        in_specs=[seq, seq, full(w1g), full(w1), full(w2g), full(pos_flat), full(kgain2)],
        out_specs=[out, out],
        out_shape=[jax.ShapeDtypeStruct((b, nc, KV_DIM), BF16)] * 2,
        compiler_params=_params("parallel"),
        name="compress",
    )(kc, vc, w1g, w1, w2g, pos_flat, kgain2)


def _nsa_kernel(q_ref, gt_ref, kc_ref, vct_ref, ks_ref, vst_ref, kw_ref, vwt_ref, y_ref,
                m_s, acc_s, sb_s, rk_s, *, tq, tk, nb, nc):
    nh = ATTN_HEADS
    hp = HEADS_PER_GROUP
    cols = hp * tq
    wk = WINDOW + tq
    bpt = tk // SLC_BLOCK
    qi = pl.program_id(1)
    q0 = qi * tq
    tlane = q0 + _iota((1, tq), 1)
    tile_h = lambda a, reps: jnp.concatenate([a] * reps, axis=1)
    eye_d = jnp.where(_iota((LANES, LANES), 0) == _iota((LANES, LANES), 1), 1.0, 0.0).astype(BF16)
    eye_q = jnp.where(_iota((tq, tq), 0) == _iota((tq, tq), 1), 1.0, 0.0).astype(BF16)
    drow = _iota((LANES, 1), 0)
    own = [(drow >= g * HEAD_DIM) & (drow < (g + 1) * HEAD_DIM) for g in range(KV_GROUPS)]

    lane = _iota((1, LANES), 1)
    glanes = [jnp.where((lane >= g * HEAD_DIM) & (lane < (g + 1) * HEAD_DIM), 1.0, 0.0).astype(BF16)
              for g in range(KV_GROUPS)]
    qall = jnp.concatenate([q_ref[0, :, h * LANES:(h + 1) * LANES] * glanes[g]
                            for g in range(KV_GROUPS) for h in range(hp)], axis=0)
    q_t = _nt(eye_d, qall).astype(BF16)

    ncol = _iota((nc, 1), 0)
    cvalid = (ncol * CMP_STRIDE + (CMP_BLOCK - 1) <= tlane) & (ncol < nc - 1)
    s_c = _nn(kc_ref[0], q_t) + tile_h(jnp.where(cvalid, 0.0, NEG_INF), nh)
    e_c = jnp.exp2(s_c - jnp.max(s_c, axis=0, keepdims=True))
    p_c = e_c * (1.0 / jnp.sum(e_c, axis=0, keepdims=True)) * tile_h(jnp.where(cvalid, 1.0, 0.0), nh)
    p_cb = p_c.astype(BF16)
    o_c = [_nn(vct_ref[0], p_cb[:, g * cols:(g + 1) * cols]) for g in range(KV_GROUPS)]

    jcol = _iota((nb, 1), 0)
    ncmp = _iota((nb, nc), 1)
    ov = ((ncmp * CMP_STRIDE <= jcol * SLC_BLOCK + (SLC_BLOCK - 1))
          & (ncmp * CMP_STRIDE + (CMP_BLOCK - 1) >= jcol * SLC_BLOCK))
    ov = jnp.where(ov, 1.0, 0.0).astype(BF16)
    cur = jnp.right_shift(tlane, LOG2_64)
    forced = (jcol == 0) | (jcol == cur) | (jcol == cur - 1)
    causal_blk = jcol * SLC_BLOCK <= tlane
    imps = []
    for g in range(KV_GROUPS):
        c0 = g * cols
        psum = (p_c[:, c0:c0 + tq] + p_c[:, c0 + tq:c0 + 2 * tq]
                + p_c[:, c0 + 2 * tq:c0 + 3 * tq] + p_c[:, c0 + 3 * tq:c0 + 4 * tq])
        hi, lo = _split2(psum)
        imp = _nn(ov, hi) + _nn(ov, lo)
        imps.append(jnp.where(causal_blk, imp + jnp.where(forced, FORCE_BONUS, 0.0), -1.0))

    rk_s[...] = jnp.zeros(rk_s.shape, F32)
    jsub = _iota((SUBLANES, 1), 0)
    last_blk = jnp.right_shift(q0 + tq - 1, LOG2_64)
    for sg in range(nb // SUBLANES):
        @pl.when(sg * SUBLANES <= last_blk)
        def _(sg=sg):
            for g, imp in enumerate(imps):
                groups = [imp[r:r + SUBLANES, :] for r in range(0, nb, SUBLANES)]
                ranks = [jnp.zeros((SUBLANES, tq), F32) for _ in groups]
                for i in range(sg * SUBLANES, (sg + 1) * SUBLANES):
                    ri = imp[i:i + 1, :]
                    for gi, grp in enumerate(groups):
                        r0 = gi * SUBLANES
                        if r0 + SUBLANES - 1 < i:
                            ahead = jnp.where(ri > grp, 1.0, 0.0)
                        elif r0 > i:
                            ahead = jnp.where(ri >= grp, 1.0, 0.0)
                        else:
                            ahead = jnp.where(jsub > i - r0, jnp.where(ri >= grp, 1.0, 0.0),
                                              jnp.where(ri > grp, 1.0, 0.0))
                        ranks[gi] = ranks[gi] + ahead
                rk_s[g] = rk_s[g] + jnp.concatenate(ranks, axis=0)
    sb_s[...] = jnp.where(rk_s[...] < float(min(N_SELECT, nb)), 0.0, NEG_INF)

    m_s[...] = jnp.full(m_s.shape, NEG_INF, F32)
    acc_s[...] = jnp.zeros(acc_s.shape, F32)

    def sweep(tiles):
        scores = [_nn(ks_ref[0, pl.ds(pl.multiple_of(kt * tk, tk), tk), :], q_t).astype(BF16)
                  for kt in tiles]
        m_run = m_s[0:1, :]
        acc_run = [acc_s[:, g * cols:(g + 1) * cols] for g in range(KV_GROUPS)]
        m_out = [None] * KV_GROUPS
        for kt, s in zip(tiles, scores):
            causal = (kt * tk + _iota((tk, 1), 0)) <= tlane
            for g in range(KV_GROUPS):
                cs = slice(g * cols, (g + 1) * cols)
                bias = jnp.concatenate(
                    [jnp.broadcast_to(sb_s[g, pl.ds(kt * bpt + i, 1), :], (SLC_BLOCK, tq))
                     for i in range(bpt)], axis=0)
                sm = s[:, cs] + tile_h(jnp.where(causal, bias, NEG_INF).astype(BF16), hp)
                m_prev = m_run[:, cs] if m_out[g] is None else m_out[g]
                m_new = jnp.maximum(m_prev, jnp.max(sm, axis=0, keepdims=True).astype(F32))
                p = jnp.exp2(sm - m_new.astype(BF16))
                acc_run[g] = jnp.exp2(m_prev - m_new) * acc_run[g] + _nn(vst_ref[0, g, kt], p)
                m_out[g] = m_new
        for g in range(KV_GROUPS):
            cs = slice(g * cols, (g + 1) * cols)
            m_s[:, cs] = jnp.broadcast_to(m_out[g], (SUBLANES, cols))
            acc_s[:, cs] = acc_run[g]

    n_tiles = qi // (tk // tq) + 1

    def pair(i, _):
        sweep([2 * i, 2 * i + 1])
        return 0

    lax.fori_loop(0, n_tiles // 2, pair, 0)

    @pl.when(n_tiles % 2 == 1)
    def _():
        sweep([n_tiles - 1])

    w0 = pl.multiple_of(jnp.maximum(q0 - WINDOW, 0), tq)
    dist = tlane - (w0 + _iota((wk, 1), 0))
    s_w = (_nn(kw_ref[0, pl.ds(w0, wk), :], q_t).astype(BF16)
           + tile_h(jnp.where((dist >= 0) & (dist < WINDOW), 0.0, NEG_INF).astype(BF16), nh))
    p_w = jnp.exp2(s_w - jnp.max(s_w, axis=0, keepdims=True))
    wb = w0 // tq

    g_hi, g_lo = _split2(jax.nn.sigmoid(gt_ref[0]))
    gate_t = _nt(eye_d, g_hi) + _nt(eye_d, g_lo)
    gated = []
    for g in range(KV_GROUPS):
        cs = slice(g * cols, (g + 1) * cols)
        den_row = (1 - g) * HEAD_DIM
        acc = acc_s[:, cs]
        o_s = acc * (1.0 / acc[den_row:den_row + 1, :])
        v_w = jnp.concatenate([vwt_ref[0, g, wb + i] for i in range(wk // tq)], axis=1)
        acc_w = _nn(v_w, p_w[:, cs])
        o_w = acc_w * (1.0 / acc_w[den_row:den_row + 1, :])
        for h in range(hp):
            r0 = (g * hp + h) * N_NSA_BRANCHES
            hs = slice(h * tq, (h + 1) * tq)
            gated.append(gate_t[r0:r0 + 1, :] * o_c[g][:, hs] + gate_t[r0 + 1:r0 + 2, :] * o_s[:, hs]
                         + gate_t[r0 + 2:r0 + 3, :] * o_w[:, hs])
    for h in range(hp):
        o = jnp.where(own[0], gated[h], gated[hp + h]).astype(BF16)
        y_ref[0, :, h * LANES:(h + 1) * LANES] = _nt(eye_q, o).astype(BF16)


def _nsa_call(q, gates, kc, vct, ks, vst, kw, vwt):
    b, s, _ = q.shape
    tq = NSA_QUERY_TILE
    nb = s // SLC_BLOCK
    nc = kc.shape[1]
    tk = vst.shape[-1]
    tile = lambda w: pl.BlockSpec((1, tq, w), lambda i, j: (i, j, 0))
    seq = lambda a: pl.BlockSpec((1,) + a.shape[1:], lambda i, j: (i,) + (0,) * (a.ndim - 1))
    return pl.pallas_call(
        functools.partial(_nsa_kernel, tq=tq, tk=tk, nb=nb, nc=nc),
        grid=(b, s // tq),
        in_specs=[tile(ATTN_DIM), tile(LANES), seq(kc), seq(vct), seq(ks), seq(vst), seq(kw), seq(vwt)],
        out_specs=tile(ATTN_DIM),
        out_shape=jax.ShapeDtypeStruct((b, s, ATTN_DIM), BF16),
        scratch_shapes=[pltpu.VMEM((SUBLANES, ATTN_HEADS * tq), F32),
                        pltpu.VMEM((LANES, ATTN_HEADS * tq), F32),
                        pltpu.VMEM((KV_GROUPS, nb, tq), F32),
                        pltpu.VMEM((KV_GROUPS, nb, tq), F32)],
        compiler_params=_params("parallel", "arbitrary"),
        name="nsa",
    )(q, gates, kc, vct, ks, vst, kw, vwt)


def _seg_sum(x, bd, split=False):
    parts = _split2(x) if split else (x.astype(BF16),)
    w = bd.shape[0]
    return jnp.concatenate([sum(_nn(part[:, c:c + w], bd) for part in parts)
                            for c in range(0, x.shape[1], w)], axis=1)


def _stack_heads(x):
    lane = _iota((1, LANES), 1)
    lo = lane < RWKV_HEAD_DIM
    return jnp.concatenate([jnp.where(lo, x, 0.0), jnp.where(lo, 0.0, x)], axis=0)


def _rwkv_kernel(p_ref, mu_ref, w0_ref, wup_ref, a0_ref, aup_ref, gup_ref, kk_ref, ka_ref, rk_ref,
                 lnw_ref, lnb_ref, o_ref, carry_ref, state_ref, *, tt):
    ch = RWKV_CHUNK
    c2 = 2 * ch
    d = RWKV_DIM
    nch = tt // ch
    npr = RWKV_PAIRS

    @pl.when(pl.program_id(1) == 0)
    def _():
        carry_ref[...] = jnp.zeros_like(carry_ref)
        state_ref[...] = jnp.zeros_like(state_ref)

    p = p_ref[0]
    prev = pltpu.roll(p, 1, axis=0)
    prev = jnp.where(_iota((tt, 1), 0) == 0, carry_ref[0:1, :], prev)
    carry_ref[0:1, :] = p[tt - 1:tt, :]
    pm = p + (prev - p) * mu_ref[...]

    r = pm[:, 0:d]
    k = pm[:, d:2 * d]
    v = pm[:, 2 * d:3 * d]
    wa = pm[:, 3 * d:3 * d + LANES]
    gl = pm[:, 3 * d + LANES:3 * d + 2 * LANES]

    z = w0_ref[...] + _nn(jnp.tanh(wa).astype(BF16), wup_ref[...])
    lw = -DECAY_SCALE * jax.nn.sigmoid(z)
    a = jax.nn.sigmoid(a0_ref[...] + _nn(wa.astype(BF16), aup_ref[...]))
    gate = _nn(jax.nn.sigmoid(gl).astype(BF16), gup_ref[...])

    seg = (jnp.right_shift(_iota((MXU_COLS, MXU_COLS), 0), LOG2_64)
           == jnp.right_shift(_iota((MXU_COLS, MXU_COLS), 1), LOG2_64))
    bd = jnp.where(seg, 1.0, 0.0).astype(BF16)
    kk = k * kk_ref[...]
    kk = kk * lax.rsqrt(jnp.maximum(_seg_sum(kk * kk, bd), 1e-24))
    k2 = k * (1.0 + (a - 1.0) * ka_ref[...])

    ti = _iota((tt, tt), 0)
    tj = _iota((tt, tt), 1)
    same_chunk = jnp.right_shift(ti, LOG2_64) == jnp.right_shift(tj, LOG2_64)
    tri = jnp.where(same_chunk & (tj <= ti), 1.0, 0.0).astype(BF16)
    hi, lo = _split2(lw)
    cum = _nn(tri, hi) + _nn(tri, lo)
    tot = jnp.concatenate([jnp.broadcast_to(cum[(c + 1) * ch - 1:(c + 1) * ch, :], (ch, d))
                           for c in range(nch)], axis=0)
    e_inv = jnp.exp(-cum)
    e_end = jnp.exp(tot - cum)
    dec = jnp.exp(tot)
    bb = kk * a

    def tiles(x):
        return jnp.stack([_stack_heads(x[c * ch:(c + 1) * ch, pr * LANES:(pr + 1) * LANES])
                          for c in range(nch) for pr in range(npr)]).astype(BF16)

    a_t = tiles(-kk * jnp.exp(cum - lw))
    r_t = tiles(r * jnp.exp(cum))
    b_h = tiles(bb * e_inv)
    k_h = tiles(k2 * e_inv)
    bke = jnp.concatenate([tiles(bb * e_end), tiles(k2 * e_end)], axis=1)
    v_t = tiles(v)

    ri = _iota((c2, c2), 0)
    ci = _iota((c2, c2), 1)
    same = jnp.right_shift(ri, LOG2_64) == jnp.right_shift(ci, LOG2_64)
    strict = jnp.where(same & (ci < ri), 1.0, 0.0)
    incl = jnp.where(same & (ci <= ri), 1.0, 0.0)
    ident = jnp.where(ri == ci, 1.0, 0.0)

    scores = _bnt(jnp.concatenate([a_t, r_t], axis=1), jnp.concatenate([b_h, k_h], axis=1))
    m_ab = scores[:, 0:c2, 0:c2] * strict
    m_ak = (scores[:, 0:c2, c2:2 * c2] * strict).astype(BF16)
    m_rbk = jnp.concatenate([scores[:, c2:2 * c2, 0:c2] * incl,
                             scores[:, c2:2 * c2, c2:2 * c2] * incl], axis=2).astype(BF16)
    tinv = ident + m_ab
    mp = m_ab
    for _ in range(5):
        mpb = mp.astype(BF16)
        mp = _bnn(mpb, mpb)
        tinv = tinv + _bnn(tinv.astype(BF16), mp.astype(BF16))
    rhs = jnp.concatenate([a_t, _bnn(m_ak, v_t).astype(BF16)], axis=2)
    wu = _bnn(tinv.astype(BF16), rhs)
    w_t = wu[:, :, 0:c2].astype(BF16)
    u0 = wu[:, :, c2:2 * c2]

    g_st = state_ref[...]
    y_rows = []
    for c in range(nch):
        sl = slice(c * npr, (c + 1) * npr)
        g_b = g_st.astype(BF16)
        u = _bnt(w_t[sl], g_b) + u0[sl]
        uv = jnp.concatenate([u.astype(BF16), v_t[sl]], axis=1)
        y = _bnt(r_t[sl], g_b) + _bnn(m_rbk[sl], uv)
        y = y[:, 0:ch, :] + y[:, ch:c2, :]
        y_rows.append(jnp.concatenate([y[pr] for pr in range(npr)], axis=1))
        dec_c = jnp.stack([dec[c * ch:c * ch + 1, pr * LANES:(pr + 1) * LANES] for pr in range(npr)])
        g_st = g_st * dec_c + _btn(uv, bke[sl])
    state_ref[...] = g_st
    y = jnp.concatenate(y_rows, axis=0)

    inv = 1.0 / RWKV_HEAD_DIM
    mean = _seg_sum(y, bd, split=True) * inv
    yc = y - mean
    var = _seg_sum(yc * yc, bd) * inv
    yn = yc * lax.rsqrt(var + GN_EPS) * lnw_ref[...] + lnb_ref[...]
    bonus = _seg_sum(r * k2 * rk_ref[...], bd) * v
    o_ref[0] = ((yn + bonus) * gate).astype(BF16)


def _rwkv_call(p, mu, w0, wup, a0, aup, gup, k_k, k_a, r_k, ln_w, ln_b, tt=256):
    b, s, _ = p.shape
    full = lambda a: pl.BlockSpec(a.shape, lambda i, j: (0,) * a.ndim)
    consts = (mu, w0, wup, a0, aup, gup, k_k, k_a, r_k, ln_w, ln_b)
    return pl.pallas_call(
        functools.partial(_rwkv_kernel, tt=tt),
        grid=(b, s // tt),
        in_specs=[pl.BlockSpec((1, tt, RWKV_MIX_DIM), lambda i, j: (i, j, 0))]
                 + [full(c) for c in consts],
        out_specs=pl.BlockSpec((1, tt, RWKV_DIM), lambda i, j: (i, j, 0)),
        out_shape=jax.ShapeDtypeStruct((b, s, RWKV_DIM), BF16),
        scratch_shapes=[pltpu.VMEM((SUBLANES, RWKV_MIX_DIM), F32),
                        pltpu.VMEM((RWKV_PAIRS, LANES, LANES), F32)],
        compiler_params=_params("arbitrary", "arbitrary"),
        name="rwkv",
    )(p, *consts)


def _merge_kernel(x_ref, ya_ref, yb_ref, mg_ref, wa_ref, wb_ref, wo_ref, fg_ref, xo_ref, h_ref):
    ga = jax.nn.sigmoid(mg_ref[:, 0:D_MODEL].astype(F32))
    gb = jax.nn.sigmoid(mg_ref[:, D_MODEL:2 * D_MODEL].astype(F32))
    merged = ga * _nn(ya_ref[...], wa_ref[...]) + gb * _nn(yb_ref[...], wb_ref[...])
    xn = x_ref[...] + _nn(merged.astype(BF16), wo_ref[...])
    xo_ref[...] = xn
    ms = jnp.mean(xn * xn, axis=-1, keepdims=True)
    h_ref[...] = (xn * lax.rsqrt(ms + RMS_EPS) * fg_ref[...]).astype(BF16)


def _merge_call(x, ya, yb, mg, wa_pad, wb, wo, fgain):
    b, s, _ = x.shape
    return pl.pallas_call(
        _merge_kernel,
        grid=(b, s // ROW_TILE),
        in_specs=[_row_spec(D_MODEL), _row_spec(ATTN_DIM), _row_spec(RWKV_DIM), _row_spec(2 * D_MODEL),
                  _const_spec(wa_pad), _const_spec(wb), _const_spec(wo), _const_spec(fgain)],
        out_specs=[_row_spec(D_MODEL), _row_spec(D_MODEL)],
        out_shape=[jax.ShapeDtypeStruct((b, s, D_MODEL), F32),
                   jax.ShapeDtypeStruct((b, s, D_MODEL), BF16)],
        compiler_params=_params("parallel", "parallel"),
        name="merge",
    )(x, ya, yb, mg, wa_pad, wb, wo, fgain)


def _ffn_kernel(x_ref, h_ref, wu_ref, wd_ref, o_ref, *, fc):
    h = h_ref[...]
    acc = x_ref[...]
    for c in range(D_FF // fc):
        up = jnp.maximum(_nn(h, wu_ref[:, c * fc:(c + 1) * fc]), 0.0)
        acc = acc + _nn((up * up).astype(BF16), wd_ref[c * fc:(c + 1) * fc, :])
    o_ref[...] = acc


def _ffn_call(x, h, wu, wd, fc=1024):
    b, s, _ = x.shape
    return pl.pallas_call(
        functools.partial(_ffn_kernel, fc=fc),
        grid=(b, s // ROW_TILE),
        in_specs=[_row_spec(D_MODEL), _row_spec(D_MODEL), _const_spec(wu), _const_spec(wd)],
        out_specs=_row_spec(D_MODEL),
        out_shape=jax.ShapeDtypeStruct((b, s, D_MODEL), F32),
        compiler_params=_params("parallel", "parallel"),
        name="ffn",
    )(x, h, wu, wd)


def _pair_heads(w):
    rows = w.shape[0]
    w = w.reshape(rows, KV_GROUPS, HEADS_PER_GROUP, HEAD_DIM).transpose(0, 2, 1, 3)
    return w.reshape(rows, ATTN_DIM)


def _proj_weight(w_in):
    q = _pair_heads(w_in[:, 0:ATTN_DIM])
    kv = w_in[:, ATTN_DIM:ATTN_DIM + 6 * KV_DIM]
    gates = jnp.pad(w_in[:, ATTN_DIM + 6 * KV_DIM:NSA_DIM], ((0, 0), (0, LANES - NSA_GATES)))
    rest = w_in[:, NSA_DIM:]
    return jnp.concatenate([q, kv, gates, rest], axis=1).astype(BF16)


def _values_t(v, tile):
    b, s, _ = v.shape
    vt = v.reshape(b, s // tile, tile, KV_DIM).transpose(0, 1, 3, 2)
    row = jnp.arange(KV_DIM)[:, None] // HEAD_DIM
    one = jnp.ones((), v.dtype)
    return jnp.stack([jnp.where(row == g, vt, one) for g in range(KV_GROUPS)], axis=1)


def _compress_weights(w1, w2):
    w = w1.reshape(2, 2, CMP_STRIDE, HEAD_DIM, CMP_HIDDEN).transpose(0, 2, 3, 1, 4)
    zw = jnp.zeros_like(w)
    w1g = jnp.stack([jnp.stack([w if gg == g else zw for gg in range(KV_GROUPS)], axis=2)
                     for g in range(KV_GROUPS)], axis=1)
    w1g = w1g.reshape(2, KV_GROUPS, CMP_STRIDE * KV_DIM, 2 * CMP_HIDDEN)
    z2 = jnp.zeros_like(w2)
    w2g = jnp.stack([jnp.concatenate([w2 if gg == g else z2 for gg in range(KV_GROUPS)], axis=2)
                     for g in range(KV_GROUPS)], axis=1)
    return w1g.astype(BF16), w2g.astype(BF16)


def kernel(x, mix_norm, w_in, q_gain, k_gain, cmp_pos, cmp_w1, cmp_w2, w_attn_branch, tok_mix, w0,
           w_lora_up, a0, a_lora_up, g_lora_up, k_k, k_a, r_k, ln_x_w, ln_x_b, w_rwkv_branch, w_out,
           ffn_norm, w_ffn_up, w_ffn_down):
    b, s, _ = x.shape
    depth = w_in.shape[0]
    row = lambda v: v.reshape(1, -1)
    for l in range(depth):
        qg2 = jnp.tile(q_gain[l], KV_GROUPS).reshape(1, KV_DIM)
        kg2 = jnp.tile(k_gain[l, 1:3], (1, KV_GROUPS))
        q, kc, vc, ks, vs, kw, vw, gates, rw, mg = _proj_call(
            x, row(mix_norm[l]), _proj_weight(w_in[l]), qg2, kg2)

        w1g, w2g = _compress_weights(cmp_w1[l], cmp_w2[l])
        k_cmp, v_cmp = _compress_call(kc, vc, w1g, cmp_w1[l].astype(BF16), w2g,
                                      cmp_pos[l].reshape(2, 1, CMP_BLOCK * HEAD_DIM),
                                      jnp.tile(k_gain[l, 0], KV_GROUPS).reshape(1, KV_DIM))

        ya = _nsa_call(q, gates, k_cmp, v_cmp.transpose(0, 2, 1), ks,
                       _values_t(vs, NSA_KEY_TILE), kw, _values_t(vw, NSA_QUERY_TILE))

        zero = jnp.zeros((W_LORA, RWKV_DIM), F32)
        wup = jnp.concatenate([w_lora_up[l], zero], axis=0).astype(BF16)
        aup = jnp.concatenate([zero, a_lora_up[l]], axis=0).astype(BF16)
        yb = _rwkv_call(rw, row(tok_mix[l]), row(w0[l]), wup, row(a0[l]), aup,
                        g_lora_up[l].astype(BF16), row(k_k[l]), row(k_a[l]), row(r_k[l]),
                        row(ln_x_w[l]), row(ln_x_b[l]))

        wa_pad = _pair_heads(w_attn_branch[l].T).T.astype(BF16)
        x, h = _merge_call(x, ya, yb, mg, wa_pad, w_rwkv_branch[l].astype(BF16),
                           w_out[l].astype(BF16), row(ffn_norm[l]))
        x = _ffn_call(x, h, w_ffn_up[l].astype(BF16), w_ffn_down[l].astype(BF16))
    return x
```

```python
import functools

import jax
import jax.numpy as jnp
from jax import lax
from jax.experimental import pallas as pl
from jax.experimental.pallas import tpu as pltpu

F32 = jnp.float32
BF16 = jnp.bfloat16
I32 = jnp.int32

D_MODEL = 1024
ATTN_HEADS = 8
HEAD_DIM = 64
KV_GROUPS = 2
HEADS_PER_GROUP = ATTN_HEADS // KV_GROUPS
ATTN_DIM = ATTN_HEADS * HEAD_DIM
KV_DIM = KV_GROUPS * HEAD_DIM
N_NSA_BRANCHES = 3
CMP_BLOCK = 32
CMP_STRIDE = 16
CMP_HIDDEN = 2 * HEAD_DIM
SLC_BLOCK = 64
N_SELECT = 16
WINDOW = 512
FORCE_BONUS = 1000.0
RWKV_HEADS = 8
RWKV_HEAD_DIM = 64
RWKV_DIM = RWKV_HEADS * RWKV_HEAD_DIM
W_LORA = 64
A_LORA = 64
G_LORA = 128
RWKV_MIX_DIM = 3 * RWKV_DIM + W_LORA + A_LORA + G_LORA
D_FF = 4 * D_MODEL
NSA_GATES = N_NSA_BRANCHES * ATTN_HEADS
NSA_DIM = ATTN_DIM + 6 * KV_DIM + NSA_GATES
RMS_EPS = 1e-6
GN_EPS = 64e-5
NEG_INF = -1e30

LOG2_64 = 6
LANES = 128
SUBLANES = 8
MXU_COLS = 256
Q_SCALE = HEAD_DIM ** -0.5 * 1.4426950408889634
DECAY_SCALE = 0.6065306597126334
RWKV_PAIRS = RWKV_HEADS // 2
RWKV_CHUNK = 64
RWKV_SEQS = 2
ROW_TILE = 512
NSA_QUERY_TILE = 256
NSA_KEY_TILE = 512
VMEM_LIMIT = 48 * 1024 * 1024

C_Q = 0
C_KV = C_Q + ATTN_DIM
C_GATE = C_KV + 6 * KV_DIM
C_RWKV = C_GATE + LANES
C_MERGE = C_RWKV + RWKV_MIX_DIM


def _nn(a, b):
    return lax.dot_general(a, b, (((1,), (0,)), ((), ())), preferred_element_type=F32)


def _nt(a, b):
    return lax.dot_general(a, b, (((1,), (1,)), ((), ())), preferred_element_type=F32)


def _bnn(a, b):
    return lax.dot_general(a, b, (((2,), (1,)), ((0,), (0,))), preferred_element_type=F32)


def _bnt(a, b):
    return lax.dot_general(a, b, (((2,), (2,)), ((0,), (0,))), preferred_element_type=F32)


def _btn(a, b):
    return lax.dot_general(a, b, (((1,), (1,)), ((0,), (0,))), preferred_element_type=F32)


def _split2(x):
    hi = x.astype(BF16)
    lo = (x - hi.astype(F32)).astype(BF16)
    return hi, lo


def _iota(shape, dim):
    return lax.broadcasted_iota(I32, shape, dim)


def _params(*sem):
    return pltpu.CompilerParams(dimension_semantics=sem, vmem_limit_bytes=VMEM_LIMIT)


def _group_rms(x, gain):
    lane = _iota((1, LANES), 1)
    sq = x * x
    lo = lane < HEAD_DIM
    s0 = jnp.sum(jnp.where(lo, sq, 0.0), axis=-1, keepdims=True)
    s1 = jnp.sum(jnp.where(lo, 0.0, sq), axis=-1, keepdims=True)
    ms = jnp.where(lo, s0, s1) * (1.0 / HEAD_DIM)
    return x * lax.rsqrt(ms + RMS_EPS) * gain


def _proj_kernel(x_ref, g_ref, w_ref, qg_ref, kg_ref,
                 q_ref, kc_ref, vc_ref, ks_ref, vs_ref, kw_ref, vw_ref, gt_ref, rw_ref, mg_ref):
    x = x_ref[...]
    ms = jnp.mean(x * x, axis=-1, keepdims=True)
    u = (x * lax.rsqrt(ms + RMS_EPS) * g_ref[...]).astype(BF16)

    def col(c0, width):
        return _nn(u, w_ref[:, c0:c0 + width])

    for piece in range(ATTN_DIM // MXU_COLS):
        qq = col(C_Q + piece * MXU_COLS, MXU_COLS)
        for half in range(MXU_COLS // LANES):
            slot = slice(piece * MXU_COLS + half * LANES, piece * MXU_COLS + (half + 1) * LANES)
            qn = _group_rms(qq[:, half * LANES:(half + 1) * LANES], qg_ref[...])
            q_ref[:, slot] = (qn * Q_SCALE).astype(BF16)
    kvc = col(C_KV, MXU_COLS)
    kc_ref[...] = kvc[:, 0:KV_DIM]
    vc_ref[...] = kvc[:, KV_DIM:2 * KV_DIM]
    kvs = col(C_KV + MXU_COLS, MXU_COLS)
    ks_ref[...] = _group_rms(kvs[:, 0:KV_DIM], kg_ref[0:1, :]).astype(BF16)
    vs_ref[...] = kvs[:, KV_DIM:2 * KV_DIM].astype(BF16)
    kvw = col(C_KV + 2 * MXU_COLS, MXU_COLS)
    kw_ref[...] = _group_rms(kvw[:, 0:KV_DIM], kg_ref[1:2, :]).astype(BF16)
    vw_ref[...] = kvw[:, KV_DIM:2 * KV_DIM].astype(BF16)
    gt_ref[...] = col(C_GATE, LANES)
    for c in range(RWKV_MIX_DIM // MXU_COLS):
        rw_ref[:, c * MXU_COLS:(c + 1) * MXU_COLS] = col(C_RWKV + c * MXU_COLS, MXU_COLS)
    for c in range(2 * D_MODEL // 512):
        mg_ref[:, c * 512:(c + 1) * 512] = col(C_MERGE + c * 512, 512).astype(BF16)


def _row_spec(width):
    return pl.BlockSpec((None, ROW_TILE, width), lambda i, j: (i, j, 0))


def _const_spec(a, buffers=2):
    return pl.BlockSpec(a.shape, lambda i, j: (0,) * a.ndim, pipeline_mode=pl.Buffered(buffers))


def _proj_call(x, gain, w_all, qg2, kg2):
    b, s, _ = x.shape
    widths = [(ATTN_DIM, BF16), (KV_DIM, F32), (KV_DIM, F32), (KV_DIM, BF16), (KV_DIM, BF16),
              (KV_DIM, BF16), (KV_DIM, BF16), (LANES, F32), (RWKV_MIX_DIM, F32), (2 * D_MODEL, BF16)]
    return pl.pallas_call(
        _proj_kernel,
        grid=(b, s // ROW_TILE),
        in_specs=[_row_spec(D_MODEL), _const_spec(gain), _const_spec(w_all, buffers=1),
                  _const_spec(qg2), _const_spec(kg2)],
        out_specs=[_row_spec(w) for w, _ in widths],
        out_shape=[jax.ShapeDtypeStruct((b, s, w), dt) for w, dt in widths],
        compiler_params=_params("parallel", "parallel"),
        name="proj",
    )(x, gain, w_all, qg2, kg2)


def _compress_kernel(kc_ref, vc_ref, w1g_ref, w1_ref, w2g_ref, pos_ref, kg_ref, ko_ref, vo_ref, *, nc):
    rowi = _iota((nc, 1), 0)
    for which, (src, dst) in enumerate(((kc_ref, ko_ref), (vc_ref, vo_ref))):
        xcat = jnp.concatenate([src[0, pl.ds(l, nc, stride=CMP_STRIDE), :] for l in range(CMP_STRIDE)],
                               axis=1).astype(BF16)
        pos8 = jnp.broadcast_to(pos_ref[which], (SUBLANES, CMP_BLOCK * HEAD_DIM)).astype(BF16)
        bias = _nn(pos8, w1_ref[which])[0:1, :]
        out = jnp.zeros((nc, KV_DIM), F32)
        for g in range(KV_GROUPS):
            hab = _nn(xcat, w1g_ref[which, g])
            hid = hab[:, 0:CMP_HIDDEN] + pltpu.roll(hab[:, CMP_HIDDEN:2 * CMP_HIDDEN], nc - 1, axis=0) + bias
            out = out + _nn(jax.nn.gelu(hid).astype(BF16), w2g_ref[which, g])
        if which == 0:
            out = _group_rms(out, kg_ref[...])
        dst[0] = jnp.where(rowi < nc - 1, out, 0.0).astype(BF16)


def _compress_call(kc, vc, w1g, w1, w2g, pos_flat, kgain2):
    b, s, _ = kc.shape
    nc = s // CMP_STRIDE
    seq = pl.BlockSpec((1, s, KV_DIM), lambda i: (i, 0, 0))
    full = lambda a: pl.BlockSpec(a.shape, lambda i: (0,) * a.ndim)
    out = pl.BlockSpec((1, nc, KV_DIM), lambda i: (i, 0, 0))
    return pl.pallas_call(
        functools.partial(_compress_kernel, nc=nc),
        grid=(b,),
        in_specs=[seq, seq, full(w1g), full(w1), full(w2g), full(pos_flat), full(kgain2)],
        out_specs=[out, out],
        out_shape=[jax.ShapeDtypeStruct((b, nc, KV_DIM), BF16)] * 2,
        compiler_params=_params("parallel"),
        name="compress",
    )(kc, vc, w1g, w1, w2g, pos_flat, kgain2)


def _nsa_kernel(q_ref, gt_ref, kc_ref, vct_ref, ks_ref, vst_ref, kw_ref, vwt_ref, y_ref,
                m_s, acc_s, sb_s, rk_s, *, tq, tk, nb, nc):
    nh = ATTN_HEADS
    hp = HEADS_PER_GROUP
    cols = hp * tq
    wk = WINDOW + tq
    bpt = tk // SLC_BLOCK
    qi = pl.program_id(1)
    q0 = qi * tq
    tlane = q0 + _iota((1, tq), 1)
    tile_h = lambda a, reps: jnp.concatenate([a] * reps, axis=1)
    eye_d = jnp.where(_iota((LANES, LANES), 0) == _iota((LANES, LANES), 1), 1.0, 0.0).astype(BF16)
    eye_q = jnp.where(_iota((tq, tq), 0) == _iota((tq, tq), 1), 1.0, 0.0).astype(BF16)
    drow = _iota((LANES, 1), 0)
    own = [(drow >= g * HEAD_DIM) & (drow < (g + 1) * HEAD_DIM) for g in range(KV_GROUPS)]

    lane = _iota((1, LANES), 1)
    glanes = [jnp.where((lane >= g * HEAD_DIM) & (lane < (g + 1) * HEAD_DIM), 1.0, 0.0).astype(BF16)
              for g in range(KV_GROUPS)]
    qall = jnp.concatenate([q_ref[0, :, h * LANES:(h + 1) * LANES] * glanes[g]
                            for g in range(KV_GROUPS) for h in range(hp)], axis=0)
    q_t = _nt(eye_d, qall).astype(BF16)

    ncol = _iota((nc, 1), 0)
    cvalid = (ncol * CMP_STRIDE + (CMP_BLOCK - 1) <= tlane) & (ncol < nc - 1)
    s_c = _nn(kc_ref[0], q_t) + tile_h(jnp.where(cvalid, 0.0, NEG_INF), nh)
    e_c = jnp.exp2(s_c - jnp.max(s_c, axis=0, keepdims=True))
    p_c = e_c * (1.0 / jnp.sum(e_c, axis=0, keepdims=True)) * tile_h(jnp.where(cvalid, 1.0, 0.0), nh)
    p_cb = p_c.astype(BF16)
    o_c = [_nn(vct_ref[0], p_cb[:, g * cols:(g + 1) * cols]) for g in range(KV_GROUPS)]

    jcol = _iota((nb, 1), 0)
    ncmp = _iota((nb, nc), 1)
    ov = ((ncmp * CMP_STRIDE <= jcol * SLC_BLOCK + (SLC_BLOCK - 1))
          & (ncmp * CMP_STRIDE + (CMP_BLOCK - 1) >= jcol * SLC_BLOCK))
    ov = jnp.where(ov, 1.0, 0.0).astype(BF16)
    cur = jnp.right_shift(tlane, LOG2_64)
    forced = (jcol == 0) | (jcol == cur) | (jcol == cur - 1)
    causal_blk = jcol * SLC_BLOCK <= tlane
    imps = []
    for g in range(KV_GROUPS):
        c0 = g * cols
        psum = (p_c[:, c0:c0 + tq] + p_c[:, c0 + tq:c0 + 2 * tq]
                + p_c[:, c0 + 2 * tq:c0 + 3 * tq] + p_c[:, c0 + 3 * tq:c0 + 4 * tq])
        hi, lo = _split2(psum)
        imp = _nn(ov, hi) + _nn(ov, lo)
        imps.append(jnp.where(causal_blk, imp + jnp.where(forced, FORCE_BONUS, 0.0), -1.0))

    rk_s[...] = jnp.zeros(rk_s.shape, F32)
    jsub = _iota((SUBLANES, 1), 0)
    last_blk = jnp.right_shift(q0 + tq - 1, LOG2_64)
    for sg in range(nb // SUBLANES):
        @pl.when(sg * SUBLANES <= last_blk)
        def _(sg=sg):
            for g, imp in enumerate(imps):
                groups = [imp[r:r + SUBLANES, :] for r in range(0, nb, SUBLANES)]
                ranks = [jnp.zeros((SUBLANES, tq), F32) for _ in groups]
                for i in range(sg * SUBLANES, (sg + 1) * SUBLANES):
                    ri = imp[i:i + 1, :]
                    for gi, grp in enumerate(groups):
                        r0 = gi * SUBLANES
                        if r0 + SUBLANES - 1 < i:
                            ahead = jnp.where(ri > grp, 1.0, 0.0)
                        elif r0 > i:
                            ahead = jnp.where(ri >= grp, 1.0, 0.0)
                        else:
                            ahead = jnp.where(jsub > i - r0, jnp.where(ri >= grp, 1.0, 0.0),
                                              jnp.where(ri > grp, 1.0, 0.0))
                        ranks[gi] = ranks[gi] + ahead
                rk_s[g] = rk_s[g] + jnp.concatenate(ranks, axis=0)
    sb_s[...] = jnp.where(rk_s[...] < float(min(N_SELECT, nb)), 0.0, NEG_INF)

    m_s[...] = jnp.full(m_s.shape, NEG_INF, F32)
    acc_s[...] = jnp.zeros(acc_s.shape, F32)

    def sweep(tiles):
        scores = [_nn(ks_ref[0, pl.ds(pl.multiple_of(kt * tk, tk), tk), :], q_t).astype(BF16)
                  for kt in tiles]
        m_run = m_s[0:1, :]
        acc_run = [acc_s[:, g * cols:(g + 1) * cols] for g in range(KV_GROUPS)]
        m_out = [None] * KV_GROUPS
        for kt, s in zip(tiles, scores):
            causal = (kt * tk + _iota((tk, 1), 0)) <= tlane
            for g in range(KV_GROUPS):
                cs = slice(g * cols, (g + 1) * cols)
                bias = jnp.concatenate(
                    [jnp.broadcast_to(sb_s[g, pl.ds(kt * bpt + i, 1), :], (SLC_BLOCK, tq))
                     for i in range(bpt)], axis=0)
                sm = s[:, cs] + tile_h(jnp.where(causal, bias, NEG_INF).astype(BF16), hp)
                m_prev = m_run[:, cs] if m_out[g] is None else m_out[g]
                m_new = jnp.maximum(m_prev, jnp.max(sm, axis=0, keepdims=True).astype(F32))
                p = jnp.exp2(sm - m_new.astype(BF16))
                acc_run[g] = jnp.exp2(m_prev - m_new) * acc_run[g] + _nn(vst_ref[0, g, kt], p)
                m_out[g] = m_new
        for g in range(KV_GROUPS):
            cs = slice(g * cols, (g + 1) * cols)
            m_s[:, cs] = jnp.broadcast_to(m_out[g], (SUBLANES, cols))
            acc_s[:, cs] = acc_run[g]

    n_tiles = qi // (tk // tq) + 1

    def pair(i, _):
        sweep([2 * i, 2 * i + 1])
        return 0

    lax.fori_loop(0, n_tiles // 2, pair, 0)

    @pl.when(n_tiles % 2 == 1)
    def _():
        sweep([n_tiles - 1])

    w0 = pl.multiple_of(jnp.maximum(q0 - WINDOW, 0), tq)
    dist = tlane - (w0 + _iota((wk, 1), 0))
    s_w = (_nn(kw_ref[0, pl.ds(w0, wk), :], q_t).astype(BF16)
           + tile_h(jnp.where((dist >= 0) & (dist < WINDOW), 0.0, NEG_INF).astype(BF16), nh))
    p_w = jnp.exp2(s_w - jnp.max(s_w, axis=0, keepdims=True))
    wb = w0 // tq

    g_hi, g_lo = _split2(jax.nn.sigmoid(gt_ref[0]))
    gate_t = _nt(eye_d, g_hi) + _nt(eye_d, g_lo)
    gated = []
    for g in range(KV_GROUPS):
        cs = slice(g * cols, (g + 1) * cols)
        den_row = (1 - g) * HEAD_DIM
        acc = acc_s[:, cs]
        o_s = acc * (1.0 / acc[den_row:den_row + 1, :])
        v_w = jnp.concatenate([vwt_ref[0, g, wb + i] for i in range(wk // tq)], axis=1)
        acc_w = _nn(v_w, p_w[:, cs])
        o_w = acc_w * (1.0 / acc_w[den_row:den_row + 1, :])
        for h in range(hp):
            r0 = (g * hp + h) * N_NSA_BRANCHES
            hs = slice(h * tq, (h + 1) * tq)
            gated.append(gate_t[r0:r0 + 1, :] * o_c[g][:, hs] + gate_t[r0 + 1:r0 + 2, :] * o_s[:, hs]
                         + gate_t[r0 + 2:r0 + 3, :] * o_w[:, hs])
    for h in range(hp):
        o = jnp.where(own[0], gated[h], gated[hp + h]).astype(BF16)
        y_ref[0, :, h * LANES:(h + 1) * LANES] = _nt(eye_q, o).astype(BF16)


def _nsa_call(q, gates, kc, vct, ks, vst, kw, vwt):
    b, s, _ = q.shape
    tq = NSA_QUERY_TILE
    nb = s // SLC_BLOCK
    nc = kc.shape[1]
    tk = vst.shape[-1]
    tile = lambda w: pl.BlockSpec((1, tq, w), lambda i, j: (i, j, 0))
    seq = lambda a: pl.BlockSpec((1,) + a.shape[1:], lambda i, j: (i,) + (0,) * (a.ndim - 1))
    return pl.pallas_call(
        functools.partial(_nsa_kernel, tq=tq, tk=tk, nb=nb, nc=nc),
        grid=(b, s // tq),
        in_specs=[tile(ATTN_DIM), tile(LANES), seq(kc), seq(vct), seq(ks), seq(vst), seq(kw), seq(vwt)],
        out_specs=tile(ATTN_DIM),
        out_shape=jax.ShapeDtypeStruct((b, s, ATTN_DIM), BF16),
        scratch_shapes=[pltpu.VMEM((SUBLANES, ATTN_HEADS * tq), F32),
                        pltpu.VMEM((LANES, ATTN_HEADS * tq), F32),
                        pltpu.VMEM((KV_GROUPS, nb, tq), F32),
                        pltpu.VMEM((KV_GROUPS, nb, tq), F32)],
        compiler_params=_params("parallel", "arbitrary"),
        name="nsa",
    )(q, gates, kc, vct, ks, vst, kw, vwt)


def _seg_sum(x, bd, split=False):
    parts = _split2(x) if split else (x.astype(BF16),)
    w = bd.shape[0]
    return jnp.concatenate([sum(_nn(part[:, c:c + w], bd) for part in parts)
                            for c in range(0, x.shape[1], w)], axis=1)


def _stack_heads(x):
    lane = _iota((1, LANES), 1)
    lo = lane < RWKV_HEAD_DIM
    return jnp.concatenate([jnp.where(lo, x, 0.0), jnp.where(lo, 0.0, x)], axis=0)


def _rwkv_kernel(p_ref, mu_ref, w0_ref, wup_ref, a0_ref, aup_ref, gup_ref, kk_ref, ka_ref, rk_ref,
                 lnw_ref, lnb_ref, o_ref, carry_ref, state_ref, *, tt):
    ch = RWKV_CHUNK
    c2 = 2 * ch
    d = RWKV_DIM
    nseq = RWKV_SEQS
    nch = nseq * tt // ch
    npr = RWKV_PAIRS

    @pl.when(pl.program_id(1) == 0)
    def _():
        carry_ref[...] = jnp.zeros_like(carry_ref)
        state_ref[...] = jnp.zeros_like(state_ref)

    p = jnp.concatenate([p_ref[q] for q in range(nseq)], axis=0)
    rowi = _iota((nseq * tt, 1), 0)
    prev = pltpu.roll(p, 1, axis=0)
    for q in range(nseq):
        prev = jnp.where(rowi == q * tt, carry_ref[q:q + 1, :], prev)
        carry_ref[q:q + 1, :] = p[(q + 1) * tt - 1:(q + 1) * tt, :]
    pm = p + (prev - p) * mu_ref[...]

    r = pm[:, 0:d]
    k = pm[:, d:2 * d]
    v = pm[:, 2 * d:3 * d]
    wa = pm[:, 3 * d:3 * d + LANES]
    gl = pm[:, 3 * d + LANES:3 * d + 2 * LANES]

    z = w0_ref[...] + _nn(jnp.tanh(wa).astype(BF16), wup_ref[...])
    lw = -DECAY_SCALE * jax.nn.sigmoid(z)
    a = jax.nn.sigmoid(a0_ref[...] + _nn(wa.astype(BF16), aup_ref[...]))
    gate = _nn(jax.nn.sigmoid(gl).astype(BF16), gup_ref[...])

    seg = (jnp.right_shift(_iota((MXU_COLS, MXU_COLS), 0), LOG2_64)
           == jnp.right_shift(_iota((MXU_COLS, MXU_COLS), 1), LOG2_64))
    bd = jnp.where(seg, 1.0, 0.0).astype(BF16)
    kk = k * kk_ref[...]
    kk = kk * lax.rsqrt(jnp.maximum(_seg_sum(kk * kk, bd), 1e-24))
    k2 = k * (1.0 + (a - 1.0) * ka_ref[...])

    ti = _iota((nseq * tt, nseq * tt), 0)
    tj = _iota((nseq * tt, nseq * tt), 1)
    same_chunk = jnp.right_shift(ti, LOG2_64) == jnp.right_shift(tj, LOG2_64)
    tri = jnp.where(same_chunk & (tj <= ti), 1.0, 0.0).astype(BF16)
    hi, lo = _split2(lw)
    cum = _nn(tri, hi) + _nn(tri, lo)
    tot = jnp.concatenate([jnp.broadcast_to(cum[(c + 1) * ch - 1:(c + 1) * ch, :], (ch, d))
                           for c in range(nch)], axis=0)
    e_inv = jnp.exp(-cum)
    e_end = jnp.exp(tot - cum)
    dec = jnp.exp(tot)
    bb = kk * a

    def tiles(x):
        return jnp.stack([_stack_heads(x[c * ch:(c + 1) * ch, pr * LANES:(pr + 1) * LANES])
                          for c in range(nch) for pr in range(npr)]).astype(BF16)

    a_t = tiles(-kk * jnp.exp(cum - lw))
    r_t = tiles(r * jnp.exp(cum))
    b_h = tiles(bb * e_inv)
    k_h = tiles(k2 * e_inv)
    bke = jnp.concatenate([tiles(bb * e_end), tiles(k2 * e_end)], axis=1)
    v_t = tiles(v)

    ri = _iota((c2, c2), 0)
    ci = _iota((c2, c2), 1)
    same = jnp.right_shift(ri, LOG2_64) == jnp.right_shift(ci, LOG2_64)
    strict = jnp.where(same & (ci < ri), 1.0, 0.0)
    incl = jnp.where(same & (ci <= ri), 1.0, 0.0)
    ident = jnp.where(ri == ci, 1.0, 0.0)

    scores = _bnt(jnp.concatenate([a_t, r_t], axis=1), jnp.concatenate([b_h, k_h], axis=1))
    m_ab = scores[:, 0:c2, 0:c2] * strict
    m_ak = (scores[:, 0:c2, c2:2 * c2] * strict).astype(BF16)
    m_rbk = jnp.concatenate([scores[:, c2:2 * c2, 0:c2] * incl,
                             scores[:, c2:2 * c2, c2:2 * c2] * incl], axis=2).astype(BF16)
    tinv = ident + m_ab
    mp = m_ab
    for _ in range(5):
        mpb = mp.astype(BF16)
        mp = _bnn(mpb, mpb)
        tinv = tinv + _bnn(tinv.astype(BF16), mp.astype(BF16))
    rhs = jnp.concatenate([a_t, _bnn(m_ak, v_t).astype(BF16)], axis=2)
    wu = _bnn(tinv.astype(BF16), rhs)
    w_t = wu[:, :, 0:c2].astype(BF16)
    u0 = wu[:, :, c2:2 * c2]

    g_st = state_ref[...]
    cps = tt // ch
    y_rows = [[None] * cps for _ in range(nseq)]
    for c in range(cps):
        pick = lambda x: jnp.concatenate(
            [x[(q * cps + c) * npr:(q * cps + c + 1) * npr] for q in range(nseq)], axis=0)
        g_b = g_st.astype(BF16)
        u = _bnt(pick(w_t), g_b) + pick(u0)
        uv = jnp.concatenate([u.astype(BF16), pick(v_t)], axis=1)
        y = _bnt(pick(r_t), g_b) + _bnn(pick(m_rbk), uv)
        y = y[:, 0:ch, :] + y[:, ch:c2, :]
        for q in range(nseq):
            y_rows[q][c] = jnp.concatenate([y[q * npr + pr] for pr in range(npr)], axis=1)
        dec_c = jnp.stack([dec[(q * cps + c) * ch:(q * cps + c) * ch + 1, pr * LANES:(pr + 1) * LANES]
                           for q in range(nseq) for pr in range(npr)])
        g_st = g_st * dec_c + _btn(uv, pick(bke))
    state_ref[...] = g_st
    y = jnp.concatenate([row for rows in y_rows for row in rows], axis=0)

    inv = 1.0 / RWKV_HEAD_DIM
    mean = _seg_sum(y, bd, split=True) * inv
    yc = y - mean
    var = _seg_sum(yc * yc, bd) * inv
    yn = yc * lax.rsqrt(var + GN_EPS) * lnw_ref[...] + lnb_ref[...]
    bonus = _seg_sum(r * k2 * rk_ref[...], bd) * v
    out = ((yn + bonus) * gate).astype(BF16)
    for q in range(nseq):
        o_ref[q] = out[q * tt:(q + 1) * tt, :]


def _rwkv_call(p, mu, w0, wup, a0, aup, gup, k_k, k_a, r_k, ln_w, ln_b, tt=256):
    b, s, _ = p.shape
    full = lambda a: pl.BlockSpec(a.shape, lambda i, j: (0,) * a.ndim)
    consts = (mu, w0, wup, a0, aup, gup, k_k, k_a, r_k, ln_w, ln_b)
    return pl.pallas_call(
        functools.partial(_rwkv_kernel, tt=tt),
        grid=(b // RWKV_SEQS, s // tt),
        in_specs=[pl.BlockSpec((RWKV_SEQS, tt, RWKV_MIX_DIM), lambda i, j: (i, j, 0))]
                 + [full(c) for c in consts],
        out_specs=pl.BlockSpec((RWKV_SEQS, tt, RWKV_DIM), lambda i, j: (i, j, 0)),
        out_shape=jax.ShapeDtypeStruct((b, s, RWKV_DIM), BF16),
        scratch_shapes=[pltpu.VMEM((SUBLANES, RWKV_MIX_DIM), F32),
                        pltpu.VMEM((RWKV_SEQS * RWKV_PAIRS, LANES, LANES), F32)],
        compiler_params=_params("arbitrary", "arbitrary"),
        name="rwkv",
    )(p, *consts)


def _merge_kernel(x_ref, ya_ref, yb_ref, mg_ref, wa_ref, wb_ref, wo_ref, fg_ref, xo_ref, h_ref):
    ga = jax.nn.sigmoid(mg_ref[:, 0:D_MODEL].astype(F32))
    gb = jax.nn.sigmoid(mg_ref[:, D_MODEL:2 * D_MODEL].astype(F32))
    merged = ga * _nn(ya_ref[...], wa_ref[...]) + gb * _nn(yb_ref[...], wb_ref[...])
    xn = x_ref[...] + _nn(merged.astype(BF16), wo_ref[...])
    xo_ref[...] = xn
    ms = jnp.mean(xn * xn, axis=-1, keepdims=True)
    h_ref[...] = (xn * lax.rsqrt(ms + RMS_EPS) * fg_ref[...]).astype(BF16)


def _merge_call(x, ya, yb, mg, wa_pad, wb, wo, fgain):
    b, s, _ = x.shape
    return pl.pallas_call(
        _merge_kernel,
        grid=(b, s // ROW_TILE),
        in_specs=[_row_spec(D_MODEL), _row_spec(ATTN_DIM), _row_spec(RWKV_DIM), _row_spec(2 * D_MODEL),
                  _const_spec(wa_pad), _const_spec(wb), _const_spec(wo), _const_spec(fgain)],
        out_specs=[_row_spec(D_MODEL), _row_spec(D_MODEL)],
        out_shape=[jax.ShapeDtypeStruct((b, s, D_MODEL), F32),
                   jax.ShapeDtypeStruct((b, s, D_MODEL), BF16)],
        compiler_params=_params("parallel", "parallel"),
        name="merge",
    )(x, ya, yb, mg, wa_pad, wb, wo, fgain)


def _ffn_kernel(x_ref, h_ref, wu_ref, wd_ref, o_ref, *, fc):
    h = h_ref[...]
    acc = x_ref[...]
    for c in range(D_FF // fc):
        up = jnp.maximum(_nn(h, wu_ref[:, c * fc:(c + 1) * fc]), 0.0)
        acc = acc + _nn((up * up).astype(BF16), wd_ref[c * fc:(c + 1) * fc, :])
    o_ref[...] = acc


def _ffn_call(x, h, wu, wd, fc=1024):
    b, s, _ = x.shape
    return pl.pallas_call(
        functools.partial(_ffn_kernel, fc=fc),
        grid=(b, s // ROW_TILE),
        in_specs=[_row_spec(D_MODEL), _row_spec(D_MODEL), _const_spec(wu), _const_spec(wd)],
        out_specs=_row_spec(D_MODEL),
        out_shape=jax.ShapeDtypeStruct((b, s, D_MODEL), F32),
        compiler_params=_params("parallel", "parallel"),
        name="ffn",
    )(x, h, wu, wd)


def _pair_heads(w):
    rows = w.shape[0]
    w = w.reshape(rows, KV_GROUPS, HEADS_PER_GROUP, HEAD_DIM).transpose(0, 2, 1, 3)
    return w.reshape(rows, ATTN_DIM)


def _proj_weight(w_in):
    q = _pair_heads(w_in[:, 0:ATTN_DIM])
    kv = w_in[:, ATTN_DIM:ATTN_DIM + 6 * KV_DIM]
    gates = jnp.pad(w_in[:, ATTN_DIM + 6 * KV_DIM:NSA_DIM], ((0, 0), (0, LANES - NSA_GATES)))
    rest = w_in[:, NSA_DIM:]
    return jnp.concatenate([q, kv, gates, rest], axis=1).astype(BF16)


def _values_t(v, tile):
    b, s, _ = v.shape
    vt = v.reshape(b, s // tile, tile, KV_DIM).transpose(0, 1, 3, 2)
    row = jnp.arange(KV_DIM)[:, None] // HEAD_DIM
    one = jnp.ones((), v.dtype)
    return jnp.stack([jnp.where(row == g, vt, one) for g in range(KV_GROUPS)], axis=1)


def _compress_weights(w1, w2):
    w = w1.reshape(2, 2, CMP_STRIDE, HEAD_DIM, CMP_HIDDEN).transpose(0, 2, 3, 1, 4)
    zw = jnp.zeros_like(w)
    w1g = jnp.stack([jnp.stack([w if gg == g else zw for gg in range(KV_GROUPS)], axis=2)
                     for g in range(KV_GROUPS)], axis=1)
    w1g = w1g.reshape(2, KV_GROUPS, CMP_STRIDE * KV_DIM, 2 * CMP_HIDDEN)
    z2 = jnp.zeros_like(w2)
    w2g = jnp.stack([jnp.concatenate([w2 if gg == g else z2 for gg in range(KV_GROUPS)], axis=2)
                     for g in range(KV_GROUPS)], axis=1)
    return w1g.astype(BF16), w2g.astype(BF16)


def kernel(x, mix_norm, w_in, q_gain, k_gain, cmp_pos, cmp_w1, cmp_w2, w_attn_branch, tok_mix, w0,
           w_lora_up, a0, a_lora_up, g_lora_up, k_k, k_a, r_k, ln_x_w, ln_x_b, w_rwkv_branch, w_out,
           ffn_norm, w_ffn_up, w_ffn_down):
    b, s, _ = x.shape
    depth = w_in.shape[0]
    row = lambda v: v.reshape(1, -1)
    for l in range(depth):
        qg2 = jnp.tile(q_gain[l], KV_GROUPS).reshape(1, KV_DIM)
        kg2 = jnp.tile(k_gain[l, 1:3], (1, KV_GROUPS))
        q, kc, vc, ks, vs, kw, vw, gates, rw, mg = _proj_call(
            x, row(mix_norm[l]), _proj_weight(w_in[l]), qg2, kg2)

        w1g, w2g = _compress_weights(cmp_w1[l], cmp_w2[l])
        k_cmp, v_cmp = _compress_call(kc, vc, w1g, cmp_w1[l].astype(BF16), w2g,
                                      cmp_pos[l].reshape(2, 1, CMP_BLOCK * HEAD_DIM),
                                      jnp.tile(k_gain[l, 0], KV_GROUPS).reshape(1, KV_DIM))

        ya = _nsa_call(q, gates, k_cmp, v_cmp.transpose(0, 2, 1), ks,
                       _values_t(vs, NSA_KEY_TILE), kw, _values_t(vw, NSA_QUERY_TILE))

        zero = jnp.zeros((W_LORA, RWKV_DIM), F32)
        wup = jnp.concatenate([w_lora_up[l], zero], axis=0).astype(BF16)
        aup = jnp.concatenate([zero, a_lora_up[l]], axis=0).astype(BF16)
        yb = _rwkv_call(rw, row(tok_mix[l]), row(w0[l]), wup, row(a0[l]), aup,
                        g_lora_up[l].astype(BF16), row(k_k[l]), row(k_a[l]), row(r_k[l]),
                        row(ln_x_w[l]), row(ln_x_b[l]))

        wa_pad = _pair_heads(w_attn_branch[l].T).T.astype(BF16)
        x, h = _merge_call(x, ya, yb, mg, wa_pad, w_rwkv_branch[l].astype(BF16),
                           w_out[l].astype(BF16), row(ffn_norm[l]))
        x = _ffn_call(x, h, w_ffn_up[l].astype(BF16), w_ffn_down[l].astype(BF16))
    return x
```

```python
import functools

import jax
import jax.numpy as jnp
from jax import lax
from jax.experimental import pallas as pl
from jax.experimental.pallas import tpu as pltpu

F32 = jnp.float32
BF16 = jnp.bfloat16
I32 = jnp.int32

D_MODEL = 1024
ATTN_HEADS = 8
HEAD_DIM = 64
KV_GROUPS = 2
HEADS_PER_GROUP = ATTN_HEADS // KV_GROUPS
ATTN_DIM = ATTN_HEADS * HEAD_DIM
KV_DIM = KV_GROUPS * HEAD_DIM
N_NSA_BRANCHES = 3
CMP_BLOCK = 32
CMP_STRIDE = 16
CMP_HIDDEN = 2 * HEAD_DIM
SLC_BLOCK = 64
N_SELECT = 16
WINDOW = 512
FORCE_BONUS = 1000.0
RWKV_HEADS = 8
RWKV_HEAD_DIM = 64
RWKV_DIM = RWKV_HEADS * RWKV_HEAD_DIM
W_LORA = 64
A_LORA = 64
G_LORA = 128
RWKV_MIX_DIM = 3 * RWKV_DIM + W_LORA + A_LORA + G_LORA
D_FF = 4 * D_MODEL
NSA_GATES = N_NSA_BRANCHES * ATTN_HEADS
NSA_DIM = ATTN_DIM + 6 * KV_DIM + NSA_GATES
RMS_EPS = 1e-6
GN_EPS = 64e-5
NEG_INF = -1e30

LOG2_64 = 6
LANES = 128
SUBLANES = 8
MXU_COLS = 256
Q_SCALE = HEAD_DIM ** -0.5 * 1.4426950408889634
DECAY_SCALE = 0.6065306597126334
RWKV_PAIRS = RWKV_HEADS // 2
RWKV_CHUNK = 64
RWKV_SEQS = 2
RWKV_TOKEN_TILE = 128
ROW_TILE = 512
NSA_QUERY_TILE = 256
NSA_KEY_TILE = 512
VMEM_LIMIT = 48 * 1024 * 1024

C_Q = 0
C_KV = C_Q + ATTN_DIM
C_GATE = C_KV + 6 * KV_DIM
C_RWKV = C_GATE + LANES
C_MERGE = C_RWKV + RWKV_MIX_DIM


def _nn(a, b):
    return lax.dot_general(a, b, (((1,), (0,)), ((), ())), preferred_element_type=F32)


def _nt(a, b):
    return lax.dot_general(a, b, (((1,), (1,)), ((), ())), preferred_element_type=F32)


def _bnn(a, b):
    return lax.dot_general(a, b, (((2,), (1,)), ((0,), (0,))), preferred_element_type=F32)


def _bnt(a, b):
    return lax.dot_general(a, b, (((2,), (2,)), ((0,), (0,))), preferred_element_type=F32)


def _btn(a, b):
    return lax.dot_general(a, b, (((1,), (1,)), ((0,), (0,))), preferred_element_type=F32)


def _split2(x):
    hi = x.astype(BF16)
    lo = (x - hi.astype(F32)).astype(BF16)
    return hi, lo


def _iota(shape, dim):
    return lax.broadcasted_iota(I32, shape, dim)


def _params(*sem):
    return pltpu.CompilerParams(dimension_semantics=sem, vmem_limit_bytes=VMEM_LIMIT)


def _group_rms(x, gain):
    lane = _iota((1, LANES), 1)
    sq = x * x
    lo = lane < HEAD_DIM
    s0 = jnp.sum(jnp.where(lo, sq, 0.0), axis=-1, keepdims=True)
    s1 = jnp.sum(jnp.where(lo, 0.0, sq), axis=-1, keepdims=True)
    ms = jnp.where(lo, s0, s1) * (1.0 / HEAD_DIM)
    return x * lax.rsqrt(ms + RMS_EPS) * gain


def _proj_kernel(x_ref, g_ref, w_ref, qg_ref, kg_ref,
                 q_ref, kc_ref, vc_ref, ks_ref, vs_ref, kw_ref, vw_ref, gt_ref, rw_ref, mg_ref):
    x = x_ref[...]
    ms = jnp.mean(x * x, axis=-1, keepdims=True)
    u = (x * lax.rsqrt(ms + RMS_EPS) * g_ref[...]).astype(BF16)

    def col(c0, width):
        return _nn(u, w_ref[:, c0:c0 + width])

    for piece in range(ATTN_DIM // MXU_COLS):
        qq = col(C_Q + piece * MXU_COLS, MXU_COLS)
        for half in range(MXU_COLS // LANES):
            slot = slice(piece * MXU_COLS + half * LANES, piece * MXU_COLS + (half + 1) * LANES)
            qn = _group_rms(qq[:, half * LANES:(half + 1) * LANES], qg_ref[...])
            q_ref[:, slot] = (qn * Q_SCALE).astype(BF16)
    kvc = col(C_KV, MXU_COLS)
    kc_ref[...] = kvc[:, 0:KV_DIM]
    vc_ref[...] = kvc[:, KV_DIM:2 * KV_DIM]
    kvs = col(C_KV + MXU_COLS, MXU_COLS)
    ks_ref[...] = _group_rms(kvs[:, 0:KV_DIM], kg_ref[0:1, :]).astype(BF16)
    vs_ref[...] = kvs[:, KV_DIM:2 * KV_DIM].astype(BF16)
    kvw = col(C_KV + 2 * MXU_COLS, MXU_COLS)
    kw_ref[...] = _group_rms(kvw[:, 0:KV_DIM], kg_ref[1:2, :]).astype(BF16)
    vw_ref[...] = kvw[:, KV_DIM:2 * KV_DIM].astype(BF16)
    gt_ref[...] = col(C_GATE, LANES)
    for c in range(RWKV_MIX_DIM // MXU_COLS):
        rw_ref[:, c * MXU_COLS:(c + 1) * MXU_COLS] = col(C_RWKV + c * MXU_COLS, MXU_COLS)
    for c in range(2 * D_MODEL // 512):
        mg_ref[:, c * 512:(c + 1) * 512] = col(C_MERGE + c * 512, 512).astype(BF16)


def _row_spec(width):
    return pl.BlockSpec((None, ROW_TILE, width), lambda i, j: (i, j, 0))


def _const_spec(a, buffers=2):
    return pl.BlockSpec(a.shape, lambda i, j: (0,) * a.ndim, pipeline_mode=pl.Buffered(buffers))


def _proj_call(x, gain, w_all, qg2, kg2):
    b, s, _ = x.shape
    widths = [(ATTN_DIM, BF16), (KV_DIM, F32), (KV_DIM, F32), (KV_DIM, BF16), (KV_DIM, BF16),
              (KV_DIM, BF16), (KV_DIM, BF16), (LANES, F32), (RWKV_MIX_DIM, F32), (2 * D_MODEL, BF16)]
    return pl.pallas_call(
        _proj_kernel,
        grid=(b, s // ROW_TILE),
        in_specs=[_row_spec(D_MODEL), _const_spec(gain), _const_spec(w_all, buffers=1),
                  _const_spec(qg2), _const_spec(kg2)],
        out_specs=[_row_spec(w) for w, _ in widths],
        out_shape=[jax.ShapeDtypeStruct((b, s, w), dt) for w, dt in widths],
        compiler_params=_params("parallel", "parallel"),
        name="proj",
    )(x, gain, w_all, qg2, kg2)


def _compress_kernel(kc_ref, vc_ref, w1g_ref, w1_ref, w2g_ref, pos_ref, kg_ref, ko_ref, vo_ref, *, nc):
    rowi = _iota((nc, 1), 0)
    for which, (src, dst) in enumerate(((kc_ref, ko_ref), (vc_ref, vo_ref))):
        xcat = jnp.concatenate([src[0, pl.ds(l, nc, stride=CMP_STRIDE), :] for l in range(CMP_STRIDE)],
                               axis=1).astype(BF16)
        pos8 = jnp.broadcast_to(pos_ref[which], (SUBLANES, CMP_BLOCK * HEAD_DIM)).astype(BF16)
        bias = _nn(pos8, w1_ref[which])[0:1, :]
        out = jnp.zeros((nc, KV_DIM), F32)
        for g in range(KV_GROUPS):
            hab = _nn(xcat, w1g_ref[which, g])
            hid = hab[:, 0:CMP_HIDDEN] + pltpu.roll(hab[:, CMP_HIDDEN:2 * CMP_HIDDEN], nc - 1, axis=0) + bias
            out = out + _nn(jax.nn.gelu(hid).astype(BF16), w2g_ref[which, g])
        if which == 0:
            out = _group_rms(out, kg_ref[...])
        dst[0] = jnp.where(rowi < nc - 1, out, 0.0).astype(BF16)


def _compress_call(kc, vc, w1g, w1, w2g, pos_flat, kgain2):
    b, s, _ = kc.shape
    nc = s // CMP_STRIDE
    seq = pl.BlockSpec((1, s, KV_DIM), lambda i: (i, 0, 0))
    full = lambda a: pl.BlockSpec(a.shape, lambda i: (0,) * a.ndim)
    out = pl.BlockSpec((1, nc, KV_DIM), lambda i: (i, 0, 0))
    return pl.pallas_call(
        functools.partial(_compress_kernel, nc=nc),
        grid=(b,),
        in_specs=[seq, seq, full(w1g), full(w1), full(w2g), full(pos_flat), full(kgain2)],
        out_specs=[out, out],
        out_shape=[jax.ShapeDtypeStruct((b, nc, KV_DIM), BF16)] * 2,
        compiler_params=_params("parallel"),
        name="compress",
    )(kc, vc, w1g, w1, w2g, pos_flat, kgain2)


def _nsa_kernel(q_ref, gt_ref, kc_ref, vct_ref, ks_ref, vst_ref, kw_ref, vwt_ref, y_ref,
                m_s, acc_s, sb_s, rk_s, *, tq, tk, nb, nc):
    nh = ATTN_HEADS
    hp = HEADS_PER_GROUP
    cols = hp * tq
    wk = WINDOW + tq
    bpt = tk // SLC_BLOCK
    qi = pl.program_id(1)
    q0 = qi * tq
    tlane = q0 + _iota((1, tq), 1)
    tile_h = lambda a, reps: jnp.concatenate([a] * reps, axis=1)
    eye_d = jnp.where(_iota((LANES, LANES), 0) == _iota((LANES, LANES), 1), 1.0, 0.0).astype(BF16)
    eye_q = jnp.where(_iota((tq, tq), 0) == _iota((tq, tq), 1), 1.0, 0.0).astype(BF16)
    drow = _iota((LANES, 1), 0)
    own = [(drow >= g * HEAD_DIM) & (drow < (g + 1) * HEAD_DIM) for g in range(KV_GROUPS)]

    lane = _iota((1, LANES), 1)
    glanes = [jnp.where((lane >= g * HEAD_DIM) & (lane < (g + 1) * HEAD_DIM), 1.0, 0.0).astype(BF16)
              for g in range(KV_GROUPS)]
    qall = jnp.concatenate([q_ref[0, :, h * LANES:(h + 1) * LANES] * glanes[g]
                            for g in range(KV_GROUPS) for h in range(hp)], axis=0)
    q_t = _nt(eye_d, qall).astype(BF16)

    ncol = _iota((nc, 1), 0)
    cvalid = (ncol * CMP_STRIDE + (CMP_BLOCK - 1) <= tlane) & (ncol < nc - 1)
    s_c = _nn(kc_ref[0], q_t) + tile_h(jnp.where(cvalid, 0.0, NEG_INF), nh)
    e_c = jnp.exp2(s_c - jnp.max(s_c, axis=0, keepdims=True))
    p_c = e_c * (1.0 / jnp.sum(e_c, axis=0, keepdims=True)) * tile_h(jnp.where(cvalid, 1.0, 0.0), nh)
    p_cb = p_c.astype(BF16)
    o_c = [_nn(vct_ref[0], p_cb[:, g * cols:(g + 1) * cols]) for g in range(KV_GROUPS)]

    jcol = _iota((nb, 1), 0)
    ncmp = _iota((nb, nc), 1)
    ov = ((ncmp * CMP_STRIDE <= jcol * SLC_BLOCK + (SLC_BLOCK - 1))
          & (ncmp * CMP_STRIDE + (CMP_BLOCK - 1) >= jcol * SLC_BLOCK))
    ov = jnp.where(ov, 1.0, 0.0).astype(BF16)
    cur = jnp.right_shift(tlane, LOG2_64)
    forced = (jcol == 0) | (jcol == cur) | (jcol == cur - 1)
    causal_blk = jcol * SLC_BLOCK <= tlane
    imps = []
    for g in range(KV_GROUPS):
        c0 = g * cols
        psum = (p_c[:, c0:c0 + tq] + p_c[:, c0 + tq:c0 + 2 * tq]
                + p_c[:, c0 + 2 * tq:c0 + 3 * tq] + p_c[:, c0 + 3 * tq:c0 + 4 * tq])
        hi, lo = _split2(psum)
        imp = _nn(ov, hi) + _nn(ov, lo)
        imps.append(jnp.where(causal_blk, imp + jnp.where(forced, FORCE_BONUS, 0.0), -1.0))

    rk_s[...] = jnp.zeros(rk_s.shape, F32)
    jsub = _iota((SUBLANES, 1), 0)
    last_blk = jnp.right_shift(q0 + tq - 1, LOG2_64)
    for sg in range(nb // SUBLANES):
        @pl.when(sg * SUBLANES <= last_blk)
        def _(sg=sg):
            for g, imp in enumerate(imps):
                groups = [imp[r:r + SUBLANES, :] for r in range(0, nb, SUBLANES)]
                ranks = [jnp.zeros((SUBLANES, tq), F32) for _ in groups]
                for i in range(sg * SUBLANES, (sg + 1) * SUBLANES):
                    ri = imp[i:i + 1, :]
                    for gi, grp in enumerate(groups):
                        r0 = gi * SUBLANES
                        if r0 + SUBLANES - 1 < i:
                            ahead = jnp.where(ri > grp, 1.0, 0.0)
                        elif r0 > i:
                            ahead = jnp.where(ri >= grp, 1.0, 0.0)
                        else:
                            ahead = jnp.where(jsub > i - r0, jnp.where(ri >= grp, 1.0, 0.0),
                                              jnp.where(ri > grp, 1.0, 0.0))
                        ranks[gi] = ranks[gi] + ahead
                rk_s[g] = rk_s[g] + jnp.concatenate(ranks, axis=0)
    sb_s[...] = jnp.where(rk_s[...] < float(min(N_SELECT, nb)), 0.0, NEG_INF)

    m_s[...] = jnp.full(m_s.shape, NEG_INF, F32)
    acc_s[...] = jnp.zeros(acc_s.shape, F32)

    def sweep(tiles):
        scores = [_nn(ks_ref[0, pl.ds(pl.multiple_of(kt * tk, tk), tk), :], q_t).astype(BF16)
                  for kt in tiles]
        m_run = m_s[0:1, :]
        acc_run = [acc_s[:, g * cols:(g + 1) * cols] for g in range(KV_GROUPS)]
        m_out = [None] * KV_GROUPS
        for kt, s in zip(tiles, scores):
            causal = (kt * tk + _iota((tk, 1), 0)) <= tlane
            for g in range(KV_GROUPS):
                cs = slice(g * cols, (g + 1) * cols)
                bias = jnp.concatenate(
                    [jnp.broadcast_to(sb_s[g, pl.ds(kt * bpt + i, 1), :], (SLC_BLOCK, tq))
                     for i in range(bpt)], axis=0)
                sm = s[:, cs] + tile_h(jnp.where(causal, bias, NEG_INF).astype(BF16), hp)
                m_prev = m_run[:, cs] if m_out[g] is None else m_out[g]
                m_new = jnp.maximum(m_prev, jnp.max(sm, axis=0, keepdims=True).astype(F32))
                p = jnp.exp2(sm - m_new.astype(BF16))
                acc_run[g] = jnp.exp2(m_prev - m_new) * acc_run[g] + _nn(vst_ref[0, g, kt], p)
                m_out[g] = m_new
        for g in range(KV_GROUPS):
            cs = slice(g * cols, (g + 1) * cols)
            m_s[:, cs] = jnp.broadcast_to(m_out[g], (SUBLANES, cols))
            acc_s[:, cs] = acc_run[g]

    n_tiles = qi // (tk // tq) + 1

    def pair(i, _):
        sweep([2 * i, 2 * i + 1])
        return 0

    lax.fori_loop(0, n_tiles // 2, pair, 0)

    @pl.when(n_tiles % 2 == 1)
    def _():
        sweep([n_tiles - 1])

    w0 = pl.multiple_of(jnp.maximum(q0 - WINDOW, 0), tq)
    dist = tlane - (w0 + _iota((wk, 1), 0))
    s_w = (_nn(kw_ref[0, pl.ds(w0, wk), :], q_t).astype(BF16)
           + tile_h(jnp.where((dist >= 0) & (dist < WINDOW), 0.0, NEG_INF).astype(BF16), nh))
    p_w = jnp.exp2(s_w - jnp.max(s_w, axis=0, keepdims=True))
    wb = w0 // tq

    g_hi, g_lo = _split2(jax.nn.sigmoid(gt_ref[0]))
    gate_t = _nt(eye_d, g_hi) + _nt(eye_d, g_lo)
    gated = []
    for g in range(KV_GROUPS):
        cs = slice(g * cols, (g + 1) * cols)
        den_row = (1 - g) * HEAD_DIM
        acc = acc_s[:, cs]
        o_s = acc * (1.0 / acc[den_row:den_row + 1, :])
        v_w = jnp.concatenate([vwt_ref[0, g, wb + i] for i in range(wk // tq)], axis=1)
        acc_w = _nn(v_w, p_w[:, cs])
        o_w = acc_w * (1.0 / acc_w[den_row:den_row + 1, :])
        for h in range(hp):
            r0 = (g * hp + h) * N_NSA_BRANCHES
            hs = slice(h * tq, (h + 1) * tq)
            gated.append(gate_t[r0:r0 + 1, :] * o_c[g][:, hs] + gate_t[r0 + 1:r0 + 2, :] * o_s[:, hs]
                         + gate_t[r0 + 2:r0 + 3, :] * o_w[:, hs])
    for h in range(hp):
        o = jnp.where(own[0], gated[h], gated[hp + h]).astype(BF16)
        y_ref[0, :, h * LANES:(h + 1) * LANES] = _nt(eye_q, o).astype(BF16)


def _nsa_call(q, gates, kc, vct, ks, vst, kw, vwt):
    b, s, _ = q.shape
    tq = NSA_QUERY_TILE
    nb = s // SLC_BLOCK
    nc = kc.shape[1]
    tk = vst.shape[-1]
    tile = lambda w: pl.BlockSpec((1, tq, w), lambda i, j: (i, j, 0))
    seq = lambda a: pl.BlockSpec((1,) + a.shape[1:], lambda i, j: (i,) + (0,) * (a.ndim - 1))
    return pl.pallas_call(
        functools.partial(_nsa_kernel, tq=tq, tk=tk, nb=nb, nc=nc),
        grid=(b, s // tq),
        in_specs=[tile(ATTN_DIM), tile(LANES), seq(kc), seq(vct), seq(ks), seq(vst), seq(kw), seq(vwt)],
        out_specs=tile(ATTN_DIM),
        out_shape=jax.ShapeDtypeStruct((b, s, ATTN_DIM), BF16),
        scratch_shapes=[pltpu.VMEM((SUBLANES, ATTN_HEADS * tq), F32),
                        pltpu.VMEM((LANES, ATTN_HEADS * tq), F32),
                        pltpu.VMEM((KV_GROUPS, nb, tq), F32),
                        pltpu.VMEM((KV_GROUPS, nb, tq), F32)],
        compiler_params=_params("parallel", "arbitrary"),
        name="nsa",
    )(q, gates, kc, vct, ks, vst, kw, vwt)


def _seg_sum(x, bd, split=False):
    parts = _split2(x) if split else (x.astype(BF16),)
    w = bd.shape[0]
    return jnp.concatenate([sum(_nn(part[:, c:c + w], bd) for part in parts)
                            for c in range(0, x.shape[1], w)], axis=1)


def _stack_heads(x):
    lane = _iota((1, LANES), 1)
    lo = lane < RWKV_HEAD_DIM
    return jnp.concatenate([jnp.where(lo, x, 0.0), jnp.where(lo, 0.0, x)], axis=0)


def _rwkv_kernel(p_ref, mu_ref, w0_ref, wup_ref, a0_ref, aup_ref, gup_ref, kk_ref, ka_ref, rk_ref,
                 lnw_ref, lnb_ref, o_ref, carry_ref, state_ref, *, tt):
    ch = RWKV_CHUNK
    c2 = 2 * ch
    d = RWKV_DIM
    nseq = RWKV_SEQS
    nch = nseq * tt // ch
    npr = RWKV_PAIRS

    @pl.when(pl.program_id(1) == 0)
    def _():
        carry_ref[...] = jnp.zeros_like(carry_ref)
        state_ref[...] = jnp.zeros_like(state_ref)

    p = jnp.concatenate([p_ref[q] for q in range(nseq)], axis=0)
    rowi = _iota((nseq * tt, 1), 0)
    prev = pltpu.roll(p, 1, axis=0)
    for q in range(nseq):
        prev = jnp.where(rowi == q * tt, carry_ref[q:q + 1, :], prev)
        carry_ref[q:q + 1, :] = p[(q + 1) * tt - 1:(q + 1) * tt, :]
    pm = p + (prev - p) * mu_ref[...]

    r = pm[:, 0:d]
    k = pm[:, d:2 * d]
    v = pm[:, 2 * d:3 * d]
    wa = pm[:, 3 * d:3 * d + LANES]
    gl = pm[:, 3 * d + LANES:3 * d + 2 * LANES]

    z = w0_ref[...] + _nn(jnp.tanh(wa).astype(BF16), wup_ref[...])
    lw = -DECAY_SCALE * jax.nn.sigmoid(z)
    a = jax.nn.sigmoid(a0_ref[...] + _nn(wa.astype(BF16), aup_ref[...]))
    gate = _nn(jax.nn.sigmoid(gl).astype(BF16), gup_ref[...])

    seg = (jnp.right_shift(_iota((MXU_COLS, MXU_COLS), 0), LOG2_64)
           == jnp.right_shift(_iota((MXU_COLS, MXU_COLS), 1), LOG2_64))
    bd = jnp.where(seg, 1.0, 0.0).astype(BF16)
    kk = k * kk_ref[...]
    kk = kk * lax.rsqrt(jnp.maximum(_seg_sum(kk * kk, bd), 1e-24))
    k2 = k * (1.0 + (a - 1.0) * ka_ref[...])

    ti = _iota((nseq * tt, nseq * tt), 0)
    tj = _iota((nseq * tt, nseq * tt), 1)
    same_chunk = jnp.right_shift(ti, LOG2_64) == jnp.right_shift(tj, LOG2_64)
    tri = jnp.where(same_chunk & (tj <= ti), 1.0, 0.0).astype(BF16)
    hi, lo = _split2(lw)
    cum = _nn(tri, hi) + _nn(tri, lo)
    tot = jnp.concatenate([jnp.broadcast_to(cum[(c + 1) * ch - 1:(c + 1) * ch, :], (ch, d))
                           for c in range(nch)], axis=0)
    e_inv = jnp.exp(-cum)
    e_end = jnp.exp(tot - cum)
    dec = jnp.exp(tot)
    bb = kk * a

    def tiles(x):
        return jnp.stack([_stack_heads(x[c * ch:(c + 1) * ch, pr * LANES:(pr + 1) * LANES])
                          for c in range(nch) for pr in range(npr)]).astype(BF16)

    a_t = tiles(-kk * jnp.exp(cum - lw))
    r_t = tiles(r * jnp.exp(cum))
    b_h = tiles(bb * e_inv)
    k_h = tiles(k2 * e_inv)
    bke = jnp.concatenate([tiles(bb * e_end), tiles(k2 * e_end)], axis=1)
    v_t = tiles(v)

    ri = _iota((c2, c2), 0)
    ci = _iota((c2, c2), 1)
    same = jnp.right_shift(ri, LOG2_64) == jnp.right_shift(ci, LOG2_64)
    strict = jnp.where(same & (ci < ri), 1.0, 0.0)
    incl = jnp.where(same & (ci <= ri), 1.0, 0.0)
    ident = jnp.where(ri == ci, 1.0, 0.0)

    scores = _bnt(jnp.concatenate([a_t, r_t], axis=1), jnp.concatenate([b_h, k_h], axis=1))
    m_ab = scores[:, 0:c2, 0:c2] * strict
    m_ak = (scores[:, 0:c2, c2:2 * c2] * strict).astype(BF16)
    m_rbk = jnp.concatenate([scores[:, c2:2 * c2, 0:c2] * incl,
                             scores[:, c2:2 * c2, c2:2 * c2] * incl], axis=2).astype(BF16)
    tinv = ident + m_ab
    mp = m_ab
    for _ in range(5):
        mpb = mp.astype(BF16)
        mp = _bnn(mpb, mpb)
        tinv = tinv + _bnn(tinv.astype(BF16), mp.astype(BF16))
    rhs = jnp.concatenate([a_t, _bnn(m_ak, v_t).astype(BF16)], axis=2)
    wu = _bnn(tinv.astype(BF16), rhs)
    w_t = wu[:, :, 0:c2].astype(BF16)
    u0 = wu[:, :, c2:2 * c2]

    g_st = state_ref[...]
    cps = tt // ch
    y_rows = [[None] * cps for _ in range(nseq)]
    for c in range(cps):
        pick = lambda x: jnp.concatenate(
            [x[(q * cps + c) * npr:(q * cps + c + 1) * npr] for q in range(nseq)], axis=0)
        g_b = g_st.astype(BF16)
        u = _bnt(pick(w_t), g_b) + pick(u0)
        uv = jnp.concatenate([u.astype(BF16), pick(v_t)], axis=1)
        y = _bnt(pick(r_t), g_b) + _bnn(pick(m_rbk), uv)
        y = y[:, 0:ch, :] + y[:, ch:c2, :]
        for q in range(nseq):
            y_rows[q][c] = jnp.concatenate([y[q * npr + pr] for pr in range(npr)], axis=1)
        dec_c = jnp.stack([dec[(q * cps + c) * ch:(q * cps + c) * ch + 1, pr * LANES:(pr + 1) * LANES]
                           for q in range(nseq) for pr in range(npr)])
        g_st = g_st * dec_c + _btn(uv, pick(bke))
    state_ref[...] = g_st
    y = jnp.concatenate([row for rows in y_rows for row in rows], axis=0)

    inv = 1.0 / RWKV_HEAD_DIM
    mean = _seg_sum(y, bd, split=True) * inv
    yc = y - mean
    var = _seg_sum(yc * yc, bd) * inv
    yn = yc * lax.rsqrt(var + GN_EPS) * lnw_ref[...] + lnb_ref[...]
    bonus = _seg_sum(r * k2 * rk_ref[...], bd) * v
    out = ((yn + bonus) * gate).astype(BF16)
    for q in range(nseq):
        o_ref[q] = out[q * tt:(q + 1) * tt, :]


def _rwkv_call(p, mu, w0, wup, a0, aup, gup, k_k, k_a, r_k, ln_w, ln_b):
    b, s, _ = p.shape
    tt = RWKV_TOKEN_TILE
    full = lambda a: pl.BlockSpec(a.shape, lambda i, j: (0,) * a.ndim)
    consts = (mu, w0, wup, a0, aup, gup, k_k, k_a, r_k, ln_w, ln_b)
    return pl.pallas_call(
        functools.partial(_rwkv_kernel, tt=tt),
        grid=(b // RWKV_SEQS, s // tt),
        in_specs=[pl.BlockSpec((RWKV_SEQS, tt, RWKV_MIX_DIM), lambda i, j: (i, j, 0))]
                 + [full(c) for c in consts],
        out_specs=pl.BlockSpec((RWKV_SEQS, tt, RWKV_DIM), lambda i, j: (i, j, 0)),
        out_shape=jax.ShapeDtypeStruct((b, s, RWKV_DIM), BF16),
        scratch_shapes=[pltpu.VMEM((SUBLANES, RWKV_MIX_DIM), F32),
                        pltpu.VMEM((RWKV_SEQS * RWKV_PAIRS, LANES, LANES), F32)],
        compiler_params=_params("arbitrary", "arbitrary"),
        name="rwkv",
    )(p, *consts)


def _merge_kernel(x_ref, ya_ref, yb_ref, mg_ref, wa_ref, wb_ref, wo_ref, fg_ref, xo_ref, h_ref):
    ga = jax.nn.sigmoid(mg_ref[:, 0:D_MODEL].astype(F32))
    gb = jax.nn.sigmoid(mg_ref[:, D_MODEL:2 * D_MODEL].astype(F32))
    merged = ga * _nn(ya_ref[...], wa_ref[...]) + gb * _nn(yb_ref[...], wb_ref[...])
    xn = x_ref[...] + _nn(merged.astype(BF16), wo_ref[...])
    xo_ref[...] = xn
    ms = jnp.mean(xn * xn, axis=-1, keepdims=True)
    h_ref[...] = (xn * lax.rsqrt(ms + RMS_EPS) * fg_ref[...]).astype(BF16)


def _merge_call(x, ya, yb, mg, wa_pad, wb, wo, fgain):
    b, s, _ = x.shape
    return pl.pallas_call(
        _merge_kernel,
        grid=(b, s // ROW_TILE),
        in_specs=[_row_spec(D_MODEL), _row_spec(ATTN_DIM), _row_spec(RWKV_DIM), _row_spec(2 * D_MODEL),
                  _const_spec(wa_pad), _const_spec(wb), _const_spec(wo), _const_spec(fgain)],
        out_specs=[_row_spec(D_MODEL), _row_spec(D_MODEL)],
        out_shape=[jax.ShapeDtypeStruct((b, s, D_MODEL), F32),
                   jax.ShapeDtypeStruct((b, s, D_MODEL), BF16)],
        compiler_params=_params("parallel", "parallel"),
        name="merge",
    )(x, ya, yb, mg, wa_pad, wb, wo, fgain)


def _ffn_kernel(x_ref, h_ref, wu_ref, wd_ref, o_ref, *, fc):
    h = h_ref[...]
    acc = x_ref[...]
    for c in range(D_FF // fc):
        up = jnp.maximum(_nn(h, wu_ref[:, c * fc:(c + 1) * fc]), 0.0)
        acc = acc + _nn((up * up).astype(BF16), wd_ref[c * fc:(c + 1) * fc, :])
    o_ref[...] = acc


def _ffn_call(x, h, wu, wd, fc=1024):
    b, s, _ = x.shape
    return pl.pallas_call(
        functools.partial(_ffn_kernel, fc=fc),
        grid=(b, s // ROW_TILE),
        in_specs=[_row_spec(D_MODEL), _row_spec(D_MODEL), _const_spec(wu), _const_spec(wd)],
        out_specs=_row_spec(D_MODEL),
        out_shape=jax.ShapeDtypeStruct((b, s, D_MODEL), F32),
        compiler_params=_params("parallel", "parallel"),
        name="ffn",
    )(x, h, wu, wd)


def _pair_heads(w):
    rows = w.shape[0]
    w = w.reshape(rows, KV_GROUPS, HEADS_PER_GROUP, HEAD_DIM).transpose(0, 2, 1, 3)
    return w.reshape(rows, ATTN_DIM)


def _proj_weight(w_in):
    q = _pair_heads(w_in[:, 0:ATTN_DIM])
    kv = w_in[:, ATTN_DIM:ATTN_DIM + 6 * KV_DIM]
    gates = jnp.pad(w_in[:, ATTN_DIM + 6 * KV_DIM:NSA_DIM], ((0, 0), (0, LANES - NSA_GATES)))
    rest = w_in[:, NSA_DIM:]
    return jnp.concatenate([q, kv, gates, rest], axis=1).astype(BF16)


def _values_t(v, tile):
    b, s, _ = v.shape
    vt = v.reshape(b, s // tile, tile, KV_DIM).transpose(0, 1, 3, 2)
    row = jnp.arange(KV_DIM)[:, None] // HEAD_DIM
    one = jnp.ones((), v.dtype)
    return jnp.stack([jnp.where(row == g, vt, one) for g in range(KV_GROUPS)], axis=1)


def _compress_weights(w1, w2):
    w = w1.reshape(2, 2, CMP_STRIDE, HEAD_DIM, CMP_HIDDEN).transpose(0, 2, 3, 1, 4)
    zw = jnp.zeros_like(w)
    w1g = jnp.stack([jnp.stack([w if gg == g else zw for gg in range(KV_GROUPS)], axis=2)
                     for g in range(KV_GROUPS)], axis=1)
    w1g = w1g.reshape(2, KV_GROUPS, CMP_STRIDE * KV_DIM, 2 * CMP_HIDDEN)
    z2 = jnp.zeros_like(w2)
    w2g = jnp.stack([jnp.concatenate([w2 if gg == g else z2 for gg in range(KV_GROUPS)], axis=2)
                     for g in range(KV_GROUPS)], axis=1)
    return w1g.astype(BF16), w2g.astype(BF16)


def kernel(x, mix_norm, w_in, q_gain, k_gain, cmp_pos, cmp_w1, cmp_w2, w_attn_branch, tok_mix, w0,
           w_lora_up, a0, a_lora_up, g_lora_up, k_k, k_a, r_k, ln_x_w, ln_x_b, w_rwkv_branch, w_out,
           ffn_norm, w_ffn_up, w_ffn_down):
    b, s, _ = x.shape
    depth = w_in.shape[0]
    row = lambda v: v.reshape(1, -1)
    for l in range(depth):
        qg2 = jnp.tile(q_gain[l], KV_GROUPS).reshape(1, KV_DIM)
        kg2 = jnp.tile(k_gain[l, 1:3], (1, KV_GROUPS))
        q, kc, vc, ks, vs, kw, vw, gates, rw, mg = _proj_call(
            x, row(mix_norm[l]), _proj_weight(w_in[l]), qg2, kg2)

        w1g, w2g = _compress_weights(cmp_w1[l], cmp_w2[l])
        k_cmp, v_cmp = _compress_call(kc, vc, w1g, cmp_w1[l].astype(BF16), w2g,
                                      cmp_pos[l].reshape(2, 1, CMP_BLOCK * HEAD_DIM),
                                      jnp.tile(k_gain[l, 0], KV_GROUPS).reshape(1, KV_DIM))

        ya = _nsa_call(q, gates, k_cmp, v_cmp.transpose(0, 2, 1), ks,
                       _values_t(vs, NSA_KEY_TILE), kw, _values_t(vw, NSA_QUERY_TILE))

        zero = jnp.zeros((W_LORA, RWKV_DIM), F32)
        wup = jnp.concatenate([w_lora_up[l], zero], axis=0).astype(BF16)
        aup = jnp.concatenate([zero, a_lora_up[l]], axis=0).astype(BF16)
        yb = _rwkv_call(rw, row(tok_mix[l]), row(w0[l]), wup, row(a0[l]), aup,
                        g_lora_up[l].astype(BF16), row(k_k[l]), row(k_a[l]), row(r_k[l]),
                        row(ln_x_w[l]), row(ln_x_b[l]))

        wa_pad = _pair_heads(w_attn_branch[l].T).T.astype(BF16)
        x, h = _merge_call(x, ya, yb, mg, wa_pad, w_rwkv_branch[l].astype(BF16),
                           w_out[l].astype(BF16), row(ffn_norm[l]))
        x = _ffn_call(x, h, w_ffn_up[l].astype(BF16), w_ffn_down[l].astype(BF16))
    return x
```

```python
import functools

import jax
import jax.numpy as jnp
from jax import lax
from jax.experimental import pallas as pl
from jax.experimental.pallas import tpu as pltpu

F32 = jnp.float32
BF16 = jnp.bfloat16
I32 = jnp.int32

D_MODEL = 1024
ATTN_HEADS = 8
HEAD_DIM = 64
KV_GROUPS = 2
HEADS_PER_GROUP = ATTN_HEADS // KV_GROUPS
ATTN_DIM = ATTN_HEADS * HEAD_DIM
KV_DIM = KV_GROUPS * HEAD_DIM
N_NSA_BRANCHES = 3
CMP_BLOCK = 32
CMP_STRIDE = 16
CMP_HIDDEN = 2 * HEAD_DIM
SLC_BLOCK = 64
N_SELECT = 16
WINDOW = 512
FORCE_BONUS = 1000.0
RWKV_HEADS = 8
RWKV_HEAD_DIM = 64
RWKV_DIM = RWKV_HEADS * RWKV_HEAD_DIM
W_LORA = 64
A_LORA = 64
G_LORA = 128
RWKV_MIX_DIM = 3 * RWKV_DIM + W_LORA + A_LORA + G_LORA
D_FF = 4 * D_MODEL
NSA_GATES = N_NSA_BRANCHES * ATTN_HEADS
NSA_DIM = ATTN_DIM + 6 * KV_DIM + NSA_GATES
RMS_EPS = 1e-6
GN_EPS = 64e-5
NEG_INF = -1e30

LOG2_64 = 6
LANES = 128
SUBLANES = 8
MXU_COLS = 256
Q_SCALE = HEAD_DIM ** -0.5 * 1.4426950408889634
DECAY_SCALE = 0.6065306597126334
RWKV_PAIRS = RWKV_HEADS // 2
RWKV_CHUNK = 64
RWKV_SEQS = 2
RWKV_TOKEN_TILE = 128
ROW_TILE = 512
NSA_QUERY_TILE = 256
NSA_KEY_TILE = 256
VMEM_LIMIT = 48 * 1024 * 1024

C_Q = 0
C_KV = C_Q + ATTN_DIM
C_GATE = C_KV + 6 * KV_DIM
C_RWKV = C_GATE + LANES
C_MERGE = C_RWKV + RWKV_MIX_DIM


def _nn(a, b):
    return lax.dot_general(a, b, (((1,), (0,)), ((), ())), preferred_element_type=F32)


def _nt(a, b):
    return lax.dot_general(a, b, (((1,), (1,)), ((), ())), preferred_element_type=F32)


def _bnn(a, b):
    return lax.dot_general(a, b, (((2,), (1,)), ((0,), (0,))), preferred_element_type=F32)


def _bnt(a, b):
    return lax.dot_general(a, b, (((2,), (2,)), ((0,), (0,))), preferred_element_type=F32)


def _btn(a, b):
    return lax.dot_general(a, b, (((1,), (1,)), ((0,), (0,))), preferred_element_type=F32)


def _split2(x):
    hi = x.astype(BF16)
    lo = (x - hi.astype(F32)).astype(BF16)
    return hi, lo


def _iota(shape, dim):
    return lax.broadcasted_iota(I32, shape, dim)


def _params(*sem):
    return pltpu.CompilerParams(dimension_semantics=sem, vmem_limit_bytes=VMEM_LIMIT)


def _group_rms(x, gain):
    lane = _iota((1, LANES), 1)
    sq = x * x
    lo = lane < HEAD_DIM
    s0 = jnp.sum(jnp.where(lo, sq, 0.0), axis=-1, keepdims=True)
    s1 = jnp.sum(jnp.where(lo, 0.0, sq), axis=-1, keepdims=True)
    ms = jnp.where(lo, s0, s1) * (1.0 / HEAD_DIM)
    return x * lax.rsqrt(ms + RMS_EPS) * gain


def _proj_kernel(x_ref, g_ref, w_ref, qg_ref, kg_ref,
                 q_ref, kc_ref, vc_ref, ks_ref, vs_ref, kw_ref, vw_ref, gt_ref, rw_ref, mg_ref):
    x = x_ref[...]
    ms = jnp.mean(x * x, axis=-1, keepdims=True)
    u = (x * lax.rsqrt(ms + RMS_EPS) * g_ref[...]).astype(BF16)

    def col(c0, width):
        return _nn(u, w_ref[:, c0:c0 + width])

    for piece in range(ATTN_DIM // MXU_COLS):
        qq = col(C_Q + piece * MXU_COLS, MXU_COLS)
        for half in range(MXU_COLS // LANES):
            slot = slice(piece * MXU_COLS + half * LANES, piece * MXU_COLS + (half + 1) * LANES)
            qn = _group_rms(qq[:, half * LANES:(half + 1) * LANES], qg_ref[...])
            q_ref[:, slot] = (qn * Q_SCALE).astype(BF16)
    kvc = col(C_KV, MXU_COLS)
    kc_ref[...] = kvc[:, 0:KV_DIM]
    vc_ref[...] = kvc[:, KV_DIM:2 * KV_DIM]
    kvs = col(C_KV + MXU_COLS, MXU_COLS)
    ks_ref[...] = _group_rms(kvs[:, 0:KV_DIM], kg_ref[0:1, :]).astype(BF16)
    vs_ref[...] = kvs[:, KV_DIM:2 * KV_DIM].astype(BF16)
    kvw = col(C_KV + 2 * MXU_COLS, MXU_COLS)
    kw_ref[...] = _group_rms(kvw[:, 0:KV_DIM], kg_ref[1:2, :]).astype(BF16)
    vw_ref[...] = kvw[:, KV_DIM:2 * KV_DIM].astype(BF16)
    gt_ref[...] = col(C_GATE, LANES)
    for c in range(RWKV_MIX_DIM // MXU_COLS):
        rw_ref[:, c * MXU_COLS:(c + 1) * MXU_COLS] = col(C_RWKV + c * MXU_COLS, MXU_COLS)
    for c in range(2 * D_MODEL // 512):
        mg_ref[:, c * 512:(c + 1) * 512] = col(C_MERGE + c * 512, 512).astype(BF16)


def _row_spec(width):
    return pl.BlockSpec((None, ROW_TILE, width), lambda i, j: (i, j, 0))


def _const_spec(a, buffers=2):
    return pl.BlockSpec(a.shape, lambda i, j: (0,) * a.ndim, pipeline_mode=pl.Buffered(buffers))


def _proj_call(x, gain, w_all, qg2, kg2):
    b, s, _ = x.shape
    widths = [(ATTN_DIM, BF16), (KV_DIM, F32), (KV_DIM, F32), (KV_DIM, BF16), (KV_DIM, BF16),
              (KV_DIM, BF16), (KV_DIM, BF16), (LANES, F32), (RWKV_MIX_DIM, F32), (2 * D_MODEL, BF16)]
    return pl.pallas_call(
        _proj_kernel,
        grid=(b, s // ROW_TILE),
        in_specs=[_row_spec(D_MODEL), _const_spec(gain), _const_spec(w_all, buffers=1),
                  _const_spec(qg2), _const_spec(kg2)],
        out_specs=[_row_spec(w) for w, _ in widths],
        out_shape=[jax.ShapeDtypeStruct((b, s, w), dt) for w, dt in widths],
        compiler_params=_params("parallel", "parallel"),
        name="proj",
    )(x, gain, w_all, qg2, kg2)


def _compress_kernel(kc_ref, vc_ref, w1g_ref, w1_ref, w2g_ref, pos_ref, kg_ref, ko_ref, vo_ref, *, nc):
    rowi = _iota((nc, 1), 0)
    for which, (src, dst) in enumerate(((kc_ref, ko_ref), (vc_ref, vo_ref))):
        xcat = jnp.concatenate([src[0, pl.ds(l, nc, stride=CMP_STRIDE), :] for l in range(CMP_STRIDE)],
                               axis=1).astype(BF16)
        pos8 = jnp.broadcast_to(pos_ref[which], (SUBLANES, CMP_BLOCK * HEAD_DIM)).astype(BF16)
        bias = _nn(pos8, w1_ref[which])[0:1, :]
        out = jnp.zeros((nc, KV_DIM), F32)
        for g in range(KV_GROUPS):
            hab = _nn(xcat, w1g_ref[which, g])
            hid = hab[:, 0:CMP_HIDDEN] + pltpu.roll(hab[:, CMP_HIDDEN:2 * CMP_HIDDEN], nc - 1, axis=0) + bias
            out = out + _nn(jax.nn.gelu(hid).astype(BF16), w2g_ref[which, g])
        if which == 0:
            out = _group_rms(out, kg_ref[...])
        dst[0] = jnp.where(rowi < nc - 1, out, 0.0).astype(BF16)


def _compress_call(kc, vc, w1g, w1, w2g, pos_flat, kgain2):
    b, s, _ = kc.shape
    nc = s // CMP_STRIDE
    seq = pl.BlockSpec((1, s, KV_DIM), lambda i: (i, 0, 0))
    full = lambda a: pl.BlockSpec(a.shape, lambda i: (0,) * a.ndim)
    out = pl.BlockSpec((1, nc, KV_DIM), lambda i: (i, 0, 0))
    return pl.pallas_call(
        functools.partial(_compress_kernel, nc=nc),
        grid=(b,),
        in_specs=[seq, seq, full(w1g), full(w1), full(w2g), full(pos_flat), full(kgain2)],
        out_specs=[out, out],
        out_shape=[jax.ShapeDtypeStruct((b, nc, KV_DIM), BF16)] * 2,
        compiler_params=_params("parallel"),
        name="compress",
    )(kc, vc, w1g, w1, w2g, pos_flat, kgain2)


def _nsa_kernel(q_ref, gt_ref, kc_ref, vct_ref, ks_ref, vst_ref, kw_ref, vwt_ref, y_ref,
                m_s, acc_s, sb_s, rk_s, *, tq, tk, nb, nc):
    nh = ATTN_HEADS
    hp = HEADS_PER_GROUP
    cols = hp * tq
    wk = WINDOW + tq
    bpt = tk // SLC_BLOCK
    qi = pl.program_id(1)
    q0 = qi * tq
    tlane = q0 + _iota((1, tq), 1)
    tile_h = lambda a, reps: jnp.concatenate([a] * reps, axis=1)
    eye_d = jnp.where(_iota((LANES, LANES), 0) == _iota((LANES, LANES), 1), 1.0, 0.0).astype(BF16)
    eye_q = jnp.where(_iota((tq, tq), 0) == _iota((tq, tq), 1), 1.0, 0.0).astype(BF16)
    drow = _iota((LANES, 1), 0)
    own = [(drow >= g * HEAD_DIM) & (drow < (g + 1) * HEAD_DIM) for g in range(KV_GROUPS)]

    lane = _iota((1, LANES), 1)
    glanes = [jnp.where((lane >= g * HEAD_DIM) & (lane < (g + 1) * HEAD_DIM), 1.0, 0.0).astype(BF16)
              for g in range(KV_GROUPS)]
    qall = jnp.concatenate([q_ref[0, :, h * LANES:(h + 1) * LANES] * glanes[g]
                            for g in range(KV_GROUPS) for h in range(hp)], axis=0)
    q_t = _nt(eye_d, qall).astype(BF16)

    ncol = _iota((nc, 1), 0)
    cvalid = (ncol * CMP_STRIDE + (CMP_BLOCK - 1) <= tlane) & (ncol < nc - 1)
    s_c = _nn(kc_ref[0], q_t) + tile_h(jnp.where(cvalid, 0.0, NEG_INF), nh)
    e_c = jnp.exp2(s_c - jnp.max(s_c, axis=0, keepdims=True))
    p_c = e_c * (1.0 / jnp.sum(e_c, axis=0, keepdims=True)) * tile_h(jnp.where(cvalid, 1.0, 0.0), nh)
    p_cb = p_c.astype(BF16)
    o_c = [_nn(vct_ref[0], p_cb[:, g * cols:(g + 1) * cols]) for g in range(KV_GROUPS)]

    jcol = _iota((nb, 1), 0)
    ncmp = _iota((nb, nc), 1)
    ov = ((ncmp * CMP_STRIDE <= jcol * SLC_BLOCK + (SLC_BLOCK - 1))
          & (ncmp * CMP_STRIDE + (CMP_BLOCK - 1) >= jcol * SLC_BLOCK))
    ov = jnp.where(ov, 1.0, 0.0).astype(BF16)
    cur = jnp.right_shift(tlane, LOG2_64)
    forced = (jcol == 0) | (jcol == cur) | (jcol == cur - 1)
    causal_blk = jcol * SLC_BLOCK <= tlane
    imps = []
    for g in range(KV_GROUPS):
        c0 = g * cols
        psum = (p_c[:, c0:c0 + tq] + p_c[:, c0 + tq:c0 + 2 * tq]
                + p_c[:, c0 + 2 * tq:c0 + 3 * tq] + p_c[:, c0 + 3 * tq:c0 + 4 * tq])
        hi, lo = _split2(psum)
        imp = _nn(ov, hi) + _nn(ov, lo)
        imps.append(jnp.where(causal_blk, imp + jnp.where(forced, FORCE_BONUS, 0.0), -1.0))

    rk_s[...] = jnp.zeros(rk_s.shape, F32)
    jsub = _iota((SUBLANES, 1), 0)
    last_blk = jnp.right_shift(q0 + tq - 1, LOG2_64)
    for sg in range(nb // SUBLANES):
        @pl.when(sg * SUBLANES <= last_blk)
        def _(sg=sg):
            for g, imp in enumerate(imps):
                groups = [imp[r:r + SUBLANES, :] for r in range(0, nb, SUBLANES)]
                ranks = [jnp.zeros((SUBLANES, tq), F32) for _ in groups]
                for i in range(sg * SUBLANES, (sg + 1) * SUBLANES):
                    ri = imp[i:i + 1, :]
                    for gi, grp in enumerate(groups):
                        r0 = gi * SUBLANES
                        if r0 + SUBLANES - 1 < i:
                            ahead = jnp.where(ri > grp, 1.0, 0.0)
                        elif r0 > i:
                            ahead = jnp.where(ri >= grp, 1.0, 0.0)
                        else:
                            ahead = jnp.where(jsub > i - r0, jnp.where(ri >= grp, 1.0, 0.0),
                                              jnp.where(ri > grp, 1.0, 0.0))
                        ranks[gi] = ranks[gi] + ahead
                rk_s[g] = rk_s[g] + jnp.concatenate(ranks, axis=0)
    sb_s[...] = jnp.where(rk_s[...] < float(min(N_SELECT, nb)), 0.0, NEG_INF)

    m_s[...] = jnp.full(m_s.shape, NEG_INF, F32)
    acc_s[...] = jnp.zeros(acc_s.shape, F32)

    def sweep(tiles):
        scores = [_nn(ks_ref[0, pl.ds(pl.multiple_of(kt * tk, tk), tk), :], q_t).astype(BF16)
                  for kt in tiles]
        m_run = m_s[0:1, :]
        acc_run = [acc_s[:, g * cols:(g + 1) * cols] for g in range(KV_GROUPS)]
        m_out = [None] * KV_GROUPS
        for kt, s in zip(tiles, scores):
            causal = (kt * tk + _iota((tk, 1), 0)) <= tlane
            for g in range(KV_GROUPS):
                cs = slice(g * cols, (g + 1) * cols)
                bias = jnp.concatenate(
                    [jnp.broadcast_to(sb_s[g, pl.ds(kt * bpt + i, 1), :], (SLC_BLOCK, tq))
                     for i in range(bpt)], axis=0)
                sm = s[:, cs] + tile_h(jnp.where(causal, bias, NEG_INF).astype(BF16), hp)
                m_prev = m_run[:, cs] if m_out[g] is None else m_out[g]
                m_new = jnp.maximum(m_prev, jnp.max(sm, axis=0, keepdims=True).astype(F32))
                p = jnp.exp2(sm - m_new.astype(BF16))
                acc_run[g] = jnp.exp2(m_prev - m_new) * acc_run[g] + _nn(vst_ref[0, g, kt], p)
                m_out[g] = m_new
        for g in range(KV_GROUPS):
            cs = slice(g * cols, (g + 1) * cols)
            m_s[:, cs] = jnp.broadcast_to(m_out[g], (SUBLANES, cols))
            acc_s[:, cs] = acc_run[g]

    n_tiles = qi // (tk // tq) + 1

    def quad(i, _):
        sweep([4 * i + t for t in range(4)])
        return 0

    lax.fori_loop(0, n_tiles // 4, quad, 0)
    rest = n_tiles % 4
    done = n_tiles - rest

    @pl.when(rest >= 2)
    def _():
        sweep([done, done + 1])

    @pl.when(rest % 2 == 1)
    def _():
        sweep([n_tiles - 1])

    w0 = pl.multiple_of(jnp.maximum(q0 - WINDOW, 0), tq)
    dist = tlane - (w0 + _iota((wk, 1), 0))
    s_w = (_nn(kw_ref[0, pl.ds(w0, wk), :], q_t).astype(BF16)
           + tile_h(jnp.where((dist >= 0) & (dist < WINDOW), 0.0, NEG_INF).astype(BF16), nh))
    p_w = jnp.exp2(s_w - jnp.max(s_w, axis=0, keepdims=True))
    wb = w0 // tq

    g_hi, g_lo = _split2(jax.nn.sigmoid(gt_ref[0]))
    gate_t = _nt(eye_d, g_hi) + _nt(eye_d, g_lo)
    gated = []
    for g in range(KV_GROUPS):
        cs = slice(g * cols, (g + 1) * cols)
        den_row = (1 - g) * HEAD_DIM
        acc = acc_s[:, cs]
        o_s = acc * (1.0 / acc[den_row:den_row + 1, :])
        v_w = jnp.concatenate([vwt_ref[0, g, wb + i] for i in range(wk // tq)], axis=1)
        acc_w = _nn(v_w, p_w[:, cs])
        o_w = acc_w * (1.0 / acc_w[den_row:den_row + 1, :])
        for h in range(hp):
            r0 = (g * hp + h) * N_NSA_BRANCHES
            hs = slice(h * tq, (h + 1) * tq)
            gated.append(gate_t[r0:r0 + 1, :] * o_c[g][:, hs] + gate_t[r0 + 1:r0 + 2, :] * o_s[:, hs]
                         + gate_t[r0 + 2:r0 + 3, :] * o_w[:, hs])
    for h in range(hp):
        o = jnp.where(own[0], gated[h], gated[hp + h]).astype(BF16)
        y_ref[0, :, h * LANES:(h + 1) * LANES] = _nt(eye_q, o).astype(BF16)


def _nsa_call(q, gates, kc, vct, ks, vst, kw, vwt):
    b, s, _ = q.shape
    tq = NSA_QUERY_TILE
    nb = s // SLC_BLOCK
    nc = kc.shape[1]
    tk = vst.shape[-1]
    tile = lambda w: pl.BlockSpec((1, tq, w), lambda i, j: (i, j, 0))
    seq = lambda a: pl.BlockSpec((1,) + a.shape[1:], lambda i, j: (i,) + (0,) * (a.ndim - 1))
    return pl.pallas_call(
        functools.partial(_nsa_kernel, tq=tq, tk=tk, nb=nb, nc=nc),
        grid=(b, s // tq),
        in_specs=[tile(ATTN_DIM), tile(LANES), seq(kc), seq(vct), seq(ks), seq(vst), seq(kw), seq(vwt)],
        out_specs=tile(ATTN_DIM),
        out_shape=jax.ShapeDtypeStruct((b, s, ATTN_DIM), BF16),
        scratch_shapes=[pltpu.VMEM((SUBLANES, ATTN_HEADS * tq), F32),
                        pltpu.VMEM((LANES, ATTN_HEADS * tq), F32),
                        pltpu.VMEM((KV_GROUPS, nb, tq), F32),
                        pltpu.VMEM((KV_GROUPS, nb, tq), F32)],
        compiler_params=_params("parallel", "arbitrary"),
        name="nsa",
    )(q, gates, kc, vct, ks, vst, kw, vwt)


def _seg_sum(x, bd, split=False):
    parts = _split2(x) if split else (x.astype(BF16),)
    w = bd.shape[0]
    return jnp.concatenate([sum(_nn(part[:, c:c + w], bd) for part in parts)
                            for c in range(0, x.shape[1], w)], axis=1)


def _stack_heads(x):
    lane = _iota((1, LANES), 1)
    lo = lane < RWKV_HEAD_DIM
    return jnp.concatenate([jnp.where(lo, x, 0.0), jnp.where(lo, 0.0, x)], axis=0)


def _rwkv_kernel(p_ref, mu_ref, w0_ref, wup_ref, a0_ref, aup_ref, gup_ref, kk_ref, ka_ref, rk_ref,
                 lnw_ref, lnb_ref, o_ref, carry_ref, state_ref, *, tt):
    ch = RWKV_CHUNK
    c2 = 2 * ch
    d = RWKV_DIM
    nseq = RWKV_SEQS
    nch = nseq * tt // ch
    npr = RWKV_PAIRS

    @pl.when(pl.program_id(1) == 0)
    def _():
        carry_ref[...] = jnp.zeros_like(carry_ref)
        state_ref[...] = jnp.zeros_like(state_ref)

    p = jnp.concatenate([p_ref[q] for q in range(nseq)], axis=0)
    rowi = _iota((nseq * tt, 1), 0)
    prev = pltpu.roll(p, 1, axis=0)
    for q in range(nseq):
        prev = jnp.where(rowi == q * tt, carry_ref[q:q + 1, :], prev)
        carry_ref[q:q + 1, :] = p[(q + 1) * tt - 1:(q + 1) * tt, :]
    pm = p + (prev - p) * mu_ref[...]

    r = pm[:, 0:d]
    k = pm[:, d:2 * d]
    v = pm[:, 2 * d:3 * d]
    wa = pm[:, 3 * d:3 * d + LANES]
    gl = pm[:, 3 * d + LANES:3 * d + 2 * LANES]

    z = w0_ref[...] + _nn(jnp.tanh(wa).astype(BF16), wup_ref[...])
    lw = -DECAY_SCALE * jax.nn.sigmoid(z)
    a = jax.nn.sigmoid(a0_ref[...] + _nn(wa.astype(BF16), aup_ref[...]))
    gate = _nn(jax.nn.sigmoid(gl).astype(BF16), gup_ref[...])

    seg = (jnp.right_shift(_iota((MXU_COLS, MXU_COLS), 0), LOG2_64)
           == jnp.right_shift(_iota((MXU_COLS, MXU_COLS), 1), LOG2_64))
    bd = jnp.where(seg, 1.0, 0.0).astype(BF16)
    kk = k * kk_ref[...]
    kk = kk * lax.rsqrt(jnp.maximum(_seg_sum(kk * kk, bd), 1e-24))
    k2 = k * (1.0 + (a - 1.0) * ka_ref[...])

    ti = _iota((nseq * tt, nseq * tt), 0)
    tj = _iota((nseq * tt, nseq * tt), 1)
    same_chunk = jnp.right_shift(ti, LOG2_64) == jnp.right_shift(tj, LOG2_64)
    tri = jnp.where(same_chunk & (tj <= ti), 1.0, 0.0).astype(BF16)
    hi, lo = _split2(lw)
    cum = _nn(tri, hi) + _nn(tri, lo)
    tot = jnp.concatenate([jnp.broadcast_to(cum[(c + 1) * ch - 1:(c + 1) * ch, :], (ch, d))
                           for c in range(nch)], axis=0)
    e_inv = jnp.exp(-cum)
    e_end = jnp.exp(tot - cum)
    dec = jnp.exp(tot)
    bb = kk * a

    def tiles(x):
        return jnp.stack([_stack_heads(x[c * ch:(c + 1) * ch, pr * LANES:(pr + 1) * LANES])
                          for c in range(nch) for pr in range(npr)]).astype(BF16)

    a_t = tiles(-kk * jnp.exp(cum - lw))
    r_t = tiles(r * jnp.exp(cum))
    b_h = tiles(bb * e_inv)
    k_h = tiles(k2 * e_inv)
    bke = jnp.concatenate([tiles(bb * e_end), tiles(k2 * e_end)], axis=1)
    v_t = tiles(v)

    ri = _iota((c2, c2), 0)
    ci = _iota((c2, c2), 1)
    same = jnp.right_shift(ri, LOG2_64) == jnp.right_shift(ci, LOG2_64)
    strict = jnp.where(same & (ci < ri), 1.0, 0.0)
    incl = jnp.where(same & (ci <= ri), 1.0, 0.0)
    ident = jnp.where(ri == ci, 1.0, 0.0)

    scores = _bnt(jnp.concatenate([a_t, r_t], axis=1), jnp.concatenate([b_h, k_h], axis=1))
    m_ab = scores[:, 0:c2, 0:c2] * strict
    m_ak = (scores[:, 0:c2, c2:2 * c2] * strict).astype(BF16)
    m_rbk = jnp.concatenate([scores[:, c2:2 * c2, 0:c2] * incl,
                             scores[:, c2:2 * c2, c2:2 * c2] * incl], axis=2).astype(BF16)
    tinv = ident + m_ab
    mp = m_ab
    for _ in range(5):
        mpb = mp.astype(BF16)
        mp = _bnn(mpb, mpb)
        tinv = tinv + _bnn(tinv.astype(BF16), mp.astype(BF16))
    rhs = jnp.concatenate([a_t, _bnn(m_ak, v_t).astype(BF16)], axis=2)
    wu = _bnn(tinv.astype(BF16), rhs)
    w_t = wu[:, :, 0:c2].astype(BF16)
    u0 = wu[:, :, c2:2 * c2]

    g_st = state_ref[...]
    cps = tt // ch
    y_rows = [[None] * cps for _ in range(nseq)]
    for c in range(cps):
        pick = lambda x: jnp.concatenate(
            [x[(q * cps + c) * npr:(q * cps + c + 1) * npr] for q in range(nseq)], axis=0)
        g_b = g_st.astype(BF16)
        u = _bnt(pick(w_t), g_b) + pick(u0)
        uv = jnp.concatenate([u.astype(BF16), pick(v_t)], axis=1)
        y = _bnt(pick(r_t), g_b) + _bnn(pick(m_rbk), uv)
        y = y[:, 0:ch, :] + y[:, ch:c2, :]
        for q in range(nseq):
            y_rows[q][c] = jnp.concatenate([y[q * npr + pr] for pr in range(npr)], axis=1)
        dec_c = jnp.stack([dec[(q * cps + c) * ch:(q * cps + c) * ch + 1, pr * LANES:(pr + 1) * LANES]
                           for q in range(nseq) for pr in range(npr)])
        g_st = g_st * dec_c + _btn(uv, pick(bke))
    state_ref[...] = g_st
    y = jnp.concatenate([row for rows in y_rows for row in rows], axis=0)

    inv = 1.0 / RWKV_HEAD_DIM
    mean = _seg_sum(y, bd, split=True) * inv
    yc = y - mean
    var = _seg_sum(yc * yc, bd) * inv
    yn = yc * lax.rsqrt(var + GN_EPS) * lnw_ref[...] + lnb_ref[...]
    bonus = _seg_sum(r * k2 * rk_ref[...], bd) * v
    out = ((yn + bonus) * gate).astype(BF16)
    for q in range(nseq):
        o_ref[q] = out[q * tt:(q + 1) * tt, :]


def _rwkv_call(p, mu, w0, wup, a0, aup, gup, k_k, k_a, r_k, ln_w, ln_b):
    b, s, _ = p.shape
    tt = RWKV_TOKEN_TILE
    full = lambda a: pl.BlockSpec(a.shape, lambda i, j: (0,) * a.ndim)
    consts = (mu, w0, wup, a0, aup, gup, k_k, k_a, r_k, ln_w, ln_b)
    return pl.pallas_call(
        functools.partial(_rwkv_kernel, tt=tt),
        grid=(b // RWKV_SEQS, s // tt),
        in_specs=[pl.BlockSpec((RWKV_SEQS, tt, RWKV_MIX_DIM), lambda i, j: (i, j, 0))]
                 + [full(c) for c in consts],
        out_specs=pl.BlockSpec((RWKV_SEQS, tt, RWKV_DIM), lambda i, j: (i, j, 0)),
        out_shape=jax.ShapeDtypeStruct((b, s, RWKV_DIM), BF16),
        scratch_shapes=[pltpu.VMEM((SUBLANES, RWKV_MIX_DIM), F32),
                        pltpu.VMEM((RWKV_SEQS * RWKV_PAIRS, LANES, LANES), F32)],
        compiler_params=_params("arbitrary", "arbitrary"),
        name="rwkv",
    )(p, *consts)


def _merge_kernel(x_ref, ya_ref, yb_ref, mg_ref, wa_ref, wb_ref, wo_ref, fg_ref, xo_ref, h_ref):
    ga = jax.nn.sigmoid(mg_ref[:, 0:D_MODEL].astype(F32))
    gb = jax.nn.sigmoid(mg_ref[:, D_MODEL:2 * D_MODEL].astype(F32))
    merged = ga * _nn(ya_ref[...], wa_ref[...]) + gb * _nn(yb_ref[...], wb_ref[...])
    xn = x_ref[...] + _nn(merged.astype(BF16), wo_ref[...])
    xo_ref[...] = xn
    ms = jnp.mean(xn * xn, axis=-1, keepdims=True)
    h_ref[...] = (xn * lax.rsqrt(ms + RMS_EPS) * fg_ref[...]).astype(BF16)


def _merge_call(x, ya, yb, mg, wa_pad, wb, wo, fgain):
    b, s, _ = x.shape
    return pl.pallas_call(
        _merge_kernel,
        grid=(b, s // ROW_TILE),
        in_specs=[_row_spec(D_MODEL), _row_spec(ATTN_DIM), _row_spec(RWKV_DIM), _row_spec(2 * D_MODEL),
                  _const_spec(wa_pad), _const_spec(wb), _const_spec(wo), _const_spec(fgain)],
        out_specs=[_row_spec(D_MODEL), _row_spec(D_MODEL)],
        out_shape=[jax.ShapeDtypeStruct((b, s, D_MODEL), F32),
                   jax.ShapeDtypeStruct((b, s, D_MODEL), BF16)],
        compiler_params=_params("parallel", "parallel"),
        name="merge",
    )(x, ya, yb, mg, wa_pad, wb, wo, fgain)


def _ffn_kernel(x_ref, h_ref, wu_ref, wd_ref, o_ref, *, fc):
    h = h_ref[...]
    acc = x_ref[...]
    for c in range(D_FF // fc):
        up = jnp.maximum(_nn(h, wu_ref[:, c * fc:(c + 1) * fc]), 0.0)
        acc = acc + _nn((up * up).astype(BF16), wd_ref[c * fc:(c + 1) * fc, :])
    o_ref[...] = acc


def _ffn_call(x, h, wu, wd, fc=1024):
    b, s, _ = x.shape
    return pl.pallas_call(
        functools.partial(_ffn_kernel, fc=fc),
        grid=(b, s // ROW_TILE),
        in_specs=[_row_spec(D_MODEL), _row_spec(D_MODEL), _const_spec(wu), _const_spec(wd)],
        out_specs=_row_spec(D_MODEL),
        out_shape=jax.ShapeDtypeStruct((b, s, D_MODEL), F32),
        compiler_params=_params("parallel", "parallel"),
        name="ffn",
    )(x, h, wu, wd)


def _pair_heads(w):
    rows = w.shape[0]
    w = w.reshape(rows, KV_GROUPS, HEADS_PER_GROUP, HEAD_DIM).transpose(0, 2, 1, 3)
    return w.reshape(rows, ATTN_DIM)


def _proj_weight(w_in):
    q = _pair_heads(w_in[:, 0:ATTN_DIM])
    kv = w_in[:, ATTN_DIM:ATTN_DIM + 6 * KV_DIM]
    gates = jnp.pad(w_in[:, ATTN_DIM + 6 * KV_DIM:NSA_DIM], ((0, 0), (0, LANES - NSA_GATES)))
    rest = w_in[:, NSA_DIM:]
    return jnp.concatenate([q, kv, gates, rest], axis=1).astype(BF16)


def _values_t(v, tile):
    b, s, _ = v.shape
    vt = v.reshape(b, s // tile, tile, KV_DIM).transpose(0, 1, 3, 2)
    row = jnp.arange(KV_DIM)[:, None] // HEAD_DIM
    one = jnp.ones((), v.dtype)
    return jnp.stack([jnp.where(row == g, vt, one) for g in range(KV_GROUPS)], axis=1)


def _compress_weights(w1, w2):
    w = w1.reshape(2, 2, CMP_STRIDE, HEAD_DIM, CMP_HIDDEN).transpose(0, 2, 3, 1, 4)
    zw = jnp.zeros_like(w)
    w1g = jnp.stack([jnp.stack([w if gg == g else zw for gg in range(KV_GROUPS)], axis=2)
                     for g in range(KV_GROUPS)], axis=1)
    w1g = w1g.reshape(2, KV_GROUPS, CMP_STRIDE * KV_DIM, 2 * CMP_HIDDEN)
    z2 = jnp.zeros_like(w2)
    w2g = jnp.stack([jnp.concatenate([w2 if gg == g else z2 for gg in range(KV_GROUPS)], axis=2)
                     for g in range(KV_GROUPS)], axis=1)
    return w1g.astype(BF16), w2g.astype(BF16)


def kernel(x, mix_norm, w_in, q_gain, k_gain, cmp_pos, cmp_w1, cmp_w2, w_attn_branch, tok_mix, w0,
           w_lora_up, a0, a_lora_up, g_lora_up, k_k, k_a, r_k, ln_x_w, ln_x_b, w_rwkv_branch, w_out,
           ffn_norm, w_ffn_up, w_ffn_down):
    b, s, _ = x.shape
    depth = w_in.shape[0]
    row = lambda v: v.reshape(1, -1)
    for l in range(depth):
        qg2 = jnp.tile(q_gain[l], KV_GROUPS).reshape(1, KV_DIM)
        kg2 = jnp.tile(k_gain[l, 1:3], (1, KV_GROUPS))
        q, kc, vc, ks, vs, kw, vw, gates, rw, mg = _proj_call(
            x, row(mix_norm[l]), _proj_weight(w_in[l]), qg2, kg2)

        w1g, w2g = _compress_weights(cmp_w1[l], cmp_w2[l])
        k_cmp, v_cmp = _compress_call(kc, vc, w1g, cmp_w1[l].astype(BF16), w2g,
                                      cmp_pos[l].reshape(2, 1, CMP_BLOCK * HEAD_DIM),
                                      jnp.tile(k_gain[l, 0], KV_GROUPS).reshape(1, KV_DIM))

        ya = _nsa_call(q, gates, k_cmp, v_cmp.transpose(0, 2, 1), ks,
                       _values_t(vs, NSA_KEY_TILE), kw, _values_t(vw, NSA_QUERY_TILE))

        zero = jnp.zeros((W_LORA, RWKV_DIM), F32)
        wup = jnp.concatenate([w_lora_up[l], zero], axis=0).astype(BF16)
        aup = jnp.concatenate([zero, a_lora_up[l]], axis=0).astype(BF16)
        yb = _rwkv_call(rw, row(tok_mix[l]), row(w0[l]), wup, row(a0[l]), aup,
                        g_lora_up[l].astype(BF16), row(k_k[l]), row(k_a[l]), row(r_k[l]),
                        row(ln_x_w[l]), row(ln_x_b[l]))

        wa_pad = _pair_heads(w_attn_branch[l].T).T.astype(BF16)
        x, h = _merge_call(x, ya, yb, mg, wa_pad, w_rwkv_branch[l].astype(BF16),
                           w_out[l].astype(BF16), row(ffn_norm[l]))
        x = _ffn_call(x, h, w_ffn_up[l].astype(BF16), w_ffn_down[l].astype(BF16))
    return x
```

```python
import functools

import jax
import jax.numpy as jnp
from jax import lax
from jax.experimental import pallas as pl
from jax.experimental.pallas import tpu as pltpu

F32 = jnp.float32
BF16 = jnp.bfloat16
I32 = jnp.int32

D_MODEL = 1024
ATTN_HEADS = 8
HEAD_DIM = 64
KV_GROUPS = 2
HEADS_PER_GROUP = ATTN_HEADS // KV_GROUPS
ATTN_DIM = ATTN_HEADS * HEAD_DIM
KV_DIM = KV_GROUPS * HEAD_DIM
N_NSA_BRANCHES = 3
CMP_BLOCK = 32
CMP_STRIDE = 16
CMP_HIDDEN = 2 * HEAD_DIM
SLC_BLOCK = 64
N_SELECT = 16
WINDOW = 512
FORCE_BONUS = 1000.0
RWKV_HEADS = 8
RWKV_HEAD_DIM = 64
RWKV_DIM = RWKV_HEADS * RWKV_HEAD_DIM
W_LORA = 64
A_LORA = 64
G_LORA = 128
RWKV_MIX_DIM = 3 * RWKV_DIM + W_LORA + A_LORA + G_LORA
D_FF = 4 * D_MODEL
NSA_GATES = N_NSA_BRANCHES * ATTN_HEADS
NSA_DIM = ATTN_DIM + 6 * KV_DIM + NSA_GATES
RMS_EPS = 1e-6
GN_EPS = 64e-5
NEG_INF = -1e30

LOG2_64 = 6
LANES = 128
SUBLANES = 8
MXU_COLS = 256
Q_SCALE = HEAD_DIM ** -0.5 * 1.4426950408889634
DECAY_SCALE = 0.6065306597126334
RWKV_PAIRS = RWKV_HEADS // 2
RWKV_CHUNK = 64
RWKV_SEQS = 2
RWKV_TOKEN_TILE = 128
ROW_TILE = 512
NSA_QUERY_TILE = 256
NSA_KEY_TILE = 256
VMEM_LIMIT = 48 * 1024 * 1024

C_Q = 0
C_KV = C_Q + ATTN_DIM
C_GATE = C_KV + 6 * KV_DIM
C_RWKV = C_GATE + LANES
C_MERGE = C_RWKV + RWKV_MIX_DIM


def _nn(a, b):
    return lax.dot_general(a, b, (((1,), (0,)), ((), ())), preferred_element_type=F32)


def _nt(a, b):
    return lax.dot_general(a, b, (((1,), (1,)), ((), ())), preferred_element_type=F32)


def _bnn(a, b):
    return lax.dot_general(a, b, (((2,), (1,)), ((0,), (0,))), preferred_element_type=F32)


def _bnt(a, b):
    return lax.dot_general(a, b, (((2,), (2,)), ((0,), (0,))), preferred_element_type=F32)


def _btn(a, b):
    return lax.dot_general(a, b, (((1,), (1,)), ((0,), (0,))), preferred_element_type=F32)


def _split2(x):
    hi = x.astype(BF16)
    lo = (x - hi.astype(F32)).astype(BF16)
    return hi, lo


def _iota(shape, dim):
    return lax.broadcasted_iota(I32, shape, dim)


def _params(*sem):
    return pltpu.CompilerParams(dimension_semantics=sem, vmem_limit_bytes=VMEM_LIMIT)


def _group_rms(x, gain):
    lane = _iota((1, LANES), 1)
    sq = x * x
    lo = lane < HEAD_DIM
    s0 = jnp.sum(jnp.where(lo, sq, 0.0), axis=-1, keepdims=True)
    s1 = jnp.sum(jnp.where(lo, 0.0, sq), axis=-1, keepdims=True)
    ms = jnp.where(lo, s0, s1) * (1.0 / HEAD_DIM)
    return x * lax.rsqrt(ms + RMS_EPS) * gain


def _proj_kernel(x_ref, g_ref, w_ref, qg_ref, kg_ref,
                 q_ref, kc_ref, vc_ref, ks_ref, vs_ref, kw_ref, vw_ref, gt_ref, rw_ref, mg_ref):
    x = x_ref[...]
    ms = jnp.mean(x * x, axis=-1, keepdims=True)
    u = (x * lax.rsqrt(ms + RMS_EPS) * g_ref[...]).astype(BF16)

    def col(c0, width):
        return _nn(u, w_ref[:, c0:c0 + width])

    for piece in range(ATTN_DIM // MXU_COLS):
        qq = col(C_Q + piece * MXU_COLS, MXU_COLS)
        for half in range(MXU_COLS // LANES):
            slot = slice(piece * MXU_COLS + half * LANES, piece * MXU_COLS + (half + 1) * LANES)
            qn = _group_rms(qq[:, half * LANES:(half + 1) * LANES], qg_ref[...])
            q_ref[:, slot] = (qn * Q_SCALE).astype(BF16)
    kvc = col(C_KV, MXU_COLS)
    kc_ref[...] = kvc[:, 0:KV_DIM]
    vc_ref[...] = kvc[:, KV_DIM:2 * KV_DIM]
    kvs = col(C_KV + MXU_COLS, MXU_COLS)
    ks_ref[...] = _group_rms(kvs[:, 0:KV_DIM], kg_ref[0:1, :]).astype(BF16)
    vs_ref[...] = kvs[:, KV_DIM:2 * KV_DIM].astype(BF16)
    kvw = col(C_KV + 2 * MXU_COLS, MXU_COLS)
    kw_ref[...] = _group_rms(kvw[:, 0:KV_DIM], kg_ref[1:2, :]).astype(BF16)
    vw_ref[...] = kvw[:, KV_DIM:2 * KV_DIM].astype(BF16)
    gt_ref[...] = col(C_GATE, LANES)
    for c in range(RWKV_MIX_DIM // MXU_COLS):
        rw_ref[:, c * MXU_COLS:(c + 1) * MXU_COLS] = col(C_RWKV + c * MXU_COLS, MXU_COLS)
    for c in range(2 * D_MODEL // 512):
        mg_ref[:, c * 512:(c + 1) * 512] = col(C_MERGE + c * 512, 512).astype(BF16)


def _row_spec(width):
    return pl.BlockSpec((None, ROW_TILE, width), lambda i, j: (i, j, 0))


def _const_spec(a, buffers=2):
    return pl.BlockSpec(a.shape, lambda i, j: (0,) * a.ndim, pipeline_mode=pl.Buffered(buffers))


def _proj_call(x, gain, w_all, qg2, kg2):
    b, s, _ = x.shape
    widths = [(ATTN_DIM, BF16), (KV_DIM, F32), (KV_DIM, F32), (KV_DIM, BF16), (KV_DIM, BF16),
              (KV_DIM, BF16), (KV_DIM, BF16), (LANES, F32), (RWKV_MIX_DIM, F32), (2 * D_MODEL, BF16)]
    return pl.pallas_call(
        _proj_kernel,
        grid=(b, s // ROW_TILE),
        in_specs=[_row_spec(D_MODEL), _const_spec(gain), _const_spec(w_all, buffers=1),
                  _const_spec(qg2), _const_spec(kg2)],
        out_specs=[_row_spec(w) for w, _ in widths],
        out_shape=[jax.ShapeDtypeStruct((b, s, w), dt) for w, dt in widths],
        compiler_params=_params("parallel", "parallel"),
        name="proj",
    )(x, gain, w_all, qg2, kg2)


def _compress_kernel(kc_ref, vc_ref, w1g_ref, w1_ref, w2g_ref, pos_ref, kg_ref, ko_ref, vo_ref, *, nc):
    rowi = _iota((nc, 1), 0)
    for which, (src, dst) in enumerate(((kc_ref, ko_ref), (vc_ref, vo_ref))):
        xcat = jnp.concatenate([src[0, pl.ds(l, nc, stride=CMP_STRIDE), :] for l in range(CMP_STRIDE)],
                               axis=1).astype(BF16)
        pos8 = jnp.broadcast_to(pos_ref[which], (SUBLANES, CMP_BLOCK * HEAD_DIM)).astype(BF16)
        bias = _nn(pos8, w1_ref[which])[0:1, :]
        out = jnp.zeros((nc, KV_DIM), F32)
        for g in range(KV_GROUPS):
            hab = _nn(xcat, w1g_ref[which, g])
            hid = hab[:, 0:CMP_HIDDEN] + pltpu.roll(hab[:, CMP_HIDDEN:2 * CMP_HIDDEN], nc - 1, axis=0) + bias
            out = out + _nn(jax.nn.gelu(hid).astype(BF16), w2g_ref[which, g])
        if which == 0:
            out = _group_rms(out, kg_ref[...])
        dst[0] = jnp.where(rowi < nc - 1, out, 0.0).astype(BF16)


def _compress_call(kc, vc, w1g, w1, w2g, pos_flat, kgain2):
    b, s, _ = kc.shape
    nc = s // CMP_STRIDE
    seq = pl.BlockSpec((1, s, KV_DIM), lambda i: (i, 0, 0))
    full = lambda a: pl.BlockSpec(a.shape, lambda i: (0,) * a.ndim)
    out = pl.BlockSpec((1, nc, KV_DIM), lambda i: (i, 0, 0))
    return pl.pallas_call(
        functools.partial(_compress_kernel, nc=nc),
        grid=(b,),
        in_specs=[seq, seq, full(w1g), full(w1), full(w2g), full(pos_flat), full(kgain2)],
        out_specs=[out, out],
        out_shape=[jax.ShapeDtypeStruct((b, nc, KV_DIM), BF16)] * 2,
        compiler_params=_params("parallel"),
        name="compress",
    )(kc, vc, w1g, w1, w2g, pos_flat, kgain2)


def _nsa_kernel(q_ref, gt_ref, kc_ref, vct_ref, ks_ref, vst_ref, kw_ref, vwt_ref, y_ref,
                m_s, acc_s, sb_s, rk_s, *, tq, tk, nb, nc):
    nh = ATTN_HEADS
    hp = HEADS_PER_GROUP
    cols = hp * tq
    wk = WINDOW + tq
    bpt = tk // SLC_BLOCK
    qi = pl.program_id(1)
    q0 = qi * tq
    tlane = q0 + _iota((1, tq), 1)
    tile_h = lambda a, reps: jnp.concatenate([a] * reps, axis=1)
    eye_d = jnp.where(_iota((LANES, LANES), 0) == _iota((LANES, LANES), 1), 1.0, 0.0).astype(BF16)
    eye_q = jnp.where(_iota((tq, tq), 0) == _iota((tq, tq), 1), 1.0, 0.0).astype(BF16)
    drow = _iota((LANES, 1), 0)
    own = [(drow >= g * HEAD_DIM) & (drow < (g + 1) * HEAD_DIM) for g in range(KV_GROUPS)]

    lane = _iota((1, LANES), 1)
    glanes = [jnp.where((lane >= g * HEAD_DIM) & (lane < (g + 1) * HEAD_DIM), 1.0, 0.0).astype(BF16)
              for g in range(KV_GROUPS)]
    qall = jnp.concatenate([q_ref[0, :, h * LANES:(h + 1) * LANES] * glanes[g]
                            for g in range(KV_GROUPS) for h in range(hp)], axis=0)
    q_t = _nt(eye_d, qall).astype(BF16)

    ncol = _iota((nc, 1), 0)
    cvalid = (ncol * CMP_STRIDE + (CMP_BLOCK - 1) <= tlane) & (ncol < nc - 1)
    s_c = _nn(kc_ref[0], q_t) + tile_h(jnp.where(cvalid, 0.0, NEG_INF), nh)
    e_c = jnp.exp2(s_c - jnp.max(s_c, axis=0, keepdims=True))
    p_c = e_c * (1.0 / jnp.sum(e_c, axis=0, keepdims=True)) * tile_h(jnp.where(cvalid, 1.0, 0.0), nh)
    p_cb = p_c.astype(BF16)
    o_c = [_nn(vct_ref[0], p_cb[:, g * cols:(g + 1) * cols]) for g in range(KV_GROUPS)]

    jcol = _iota((nb, 1), 0)
    ncmp = _iota((nb, nc), 1)
    ov = ((ncmp * CMP_STRIDE <= jcol * SLC_BLOCK + (SLC_BLOCK - 1))
          & (ncmp * CMP_STRIDE + (CMP_BLOCK - 1) >= jcol * SLC_BLOCK))
    ov = jnp.where(ov, 1.0, 0.0).astype(BF16)
    cur = jnp.right_shift(tlane, LOG2_64)
    forced = (jcol == 0) | (jcol == cur) | (jcol == cur - 1)
    causal_blk = jcol * SLC_BLOCK <= tlane
    imps = []
    for g in range(KV_GROUPS):
        c0 = g * cols
        psum = (p_c[:, c0:c0 + tq] + p_c[:, c0 + tq:c0 + 2 * tq]
                + p_c[:, c0 + 2 * tq:c0 + 3 * tq] + p_c[:, c0 + 3 * tq:c0 + 4 * tq])
        hi, lo = _split2(psum)
        imp = _nn(ov, hi) + _nn(ov, lo)
        imps.append(jnp.where(causal_blk, imp + jnp.where(forced, FORCE_BONUS, 0.0), -1.0))

    rk_s[...] = jnp.zeros(rk_s.shape, F32)
    jsub = _iota((SUBLANES, 1), 0)
    last_blk = jnp.right_shift(q0 + tq - 1, LOG2_64)
    for sg in range(nb // SUBLANES):
        @pl.when(sg * SUBLANES <= last_blk)
        def _(sg=sg):
            for g, imp in enumerate(imps):
                groups = [imp[r:r + SUBLANES, :] for r in range(0, nb, SUBLANES)]
                ranks = [jnp.zeros((SUBLANES, tq), F32) for _ in groups]
                for i in range(sg * SUBLANES, (sg + 1) * SUBLANES):
                    ri = imp[i:i + 1, :]
                    for gi, grp in enumerate(groups):
                        r0 = gi * SUBLANES
                        if r0 + SUBLANES - 1 < i:
                            ahead = jnp.where(ri > grp, 1.0, 0.0)
                        elif r0 > i:
                            ahead = jnp.where(ri >= grp, 1.0, 0.0)
                        else:
                            ahead = jnp.where(jsub > i - r0, jnp.where(ri >= grp, 1.0, 0.0),
                                              jnp.where(ri > grp, 1.0, 0.0))
                        ranks[gi] = ranks[gi] + ahead
                rk_s[g] = rk_s[g] + jnp.concatenate(ranks, axis=0)
    sb_s[...] = jnp.where(rk_s[...] < float(min(N_SELECT, nb)), 0.0, NEG_INF)

    m_s[...] = jnp.full(m_s.shape, NEG_INF, F32)
    acc_s[...] = jnp.zeros(acc_s.shape, F32)

    def sweep(tiles):
        scores = [_nn(ks_ref[0, pl.ds(pl.multiple_of(kt * tk, tk), tk), :], q_t).astype(BF16)
                  for kt in tiles]
        m_run = m_s[0:1, :]
        acc_run = [acc_s[:, g * cols:(g + 1) * cols] for g in range(KV_GROUPS)]
        m_out = [None] * KV_GROUPS
        for kt, s in zip(tiles, scores):
            causal = (kt * tk + _iota((tk, 1), 0)) <= tlane
            for g in range(KV_GROUPS):
                cs = slice(g * cols, (g + 1) * cols)
                bias = jnp.concatenate(
                    [jnp.broadcast_to(sb_s[g, pl.ds(kt * bpt + i, 1), :], (SLC_BLOCK, tq))
                     for i in range(bpt)], axis=0)
                sm = s[:, cs] + tile_h(jnp.where(causal, bias, NEG_INF).astype(BF16), hp)
                m_prev = m_run[:, cs] if m_out[g] is None else m_out[g]
                m_new = jnp.maximum(m_prev, jnp.max(sm, axis=0, keepdims=True).astype(F32))
                p = jnp.exp2(sm - m_new.astype(BF16))
                acc_run[g] = jnp.exp2(m_prev - m_new) * acc_run[g] + _nn(vst_ref[0, g, kt], p)
                m_out[g] = m_new
        for g in range(KV_GROUPS):
            cs = slice(g * cols, (g + 1) * cols)
            m_s[:, cs] = jnp.broadcast_to(m_out[g], (SUBLANES, cols))
            acc_s[:, cs] = acc_run[g]

    n_tiles = qi // (tk // tq) + 1

    def quad(i, _):
        sweep([4 * i + t for t in range(4)])
        return 0

    lax.fori_loop(0, n_tiles // 4, quad, 0)
    rest = n_tiles % 4
    done = n_tiles - rest

    @pl.when(rest >= 2)
    def _():
        sweep([done, done + 1])

    @pl.when(rest % 2 == 1)
    def _():
        sweep([n_tiles - 1])

    w0 = pl.multiple_of(jnp.maximum(q0 - WINDOW, 0), tq)
    dist = tlane - (w0 + _iota((wk, 1), 0))
    s_w = (_nn(kw_ref[0, pl.ds(w0, wk), :], q_t).astype(BF16)
           + tile_h(jnp.where((dist >= 0) & (dist < WINDOW), 0.0, NEG_INF).astype(BF16), nh))
    p_w = jnp.exp2(s_w - jnp.max(s_w, axis=0, keepdims=True))
    wb = w0 // tq

    g_hi, g_lo = _split2(jax.nn.sigmoid(gt_ref[0]))
    gate_t = _nt(eye_d, g_hi) + _nt(eye_d, g_lo)
    gated = []
    for g in range(KV_GROUPS):
        cs = slice(g * cols, (g + 1) * cols)
        den_row = (1 - g) * HEAD_DIM
        acc = acc_s[:, cs]
        o_s = acc * (1.0 / acc[den_row:den_row + 1, :])
        v_w = jnp.concatenate([vwt_ref[0, g, wb + i] for i in range(wk // tq)], axis=1)
        acc_w = _nn(v_w, p_w[:, cs])
        o_w = acc_w * (1.0 / acc_w[den_row:den_row + 1, :])
        for h in range(hp):
            r0 = (g * hp + h) * N_NSA_BRANCHES
            hs = slice(h * tq, (h + 1) * tq)
            gated.append(gate_t[r0:r0 + 1, :] * o_c[g][:, hs] + gate_t[r0 + 1:r0 + 2, :] * o_s[:, hs]
                         + gate_t[r0 + 2:r0 + 3, :] * o_w[:, hs])
    for h in range(hp):
        o = jnp.where(own[0], gated[h], gated[hp + h]).astype(BF16)
        y_ref[0, :, h * LANES:(h + 1) * LANES] = _nt(eye_q, o).astype(BF16)


def _nsa_call(q, gates, kc, vct, ks, vst, kw, vwt):
    b, s, _ = q.shape
    tq = NSA_QUERY_TILE
    nb = s // SLC_BLOCK
    nc = kc.shape[1]
    tk = vst.shape[-1]
    tile = lambda w: pl.BlockSpec((1, tq, w), lambda i, j: (i, j, 0))
    seq = lambda a: pl.BlockSpec((1,) + a.shape[1:], lambda i, j: (i,) + (0,) * (a.ndim - 1))
    return pl.pallas_call(
        functools.partial(_nsa_kernel, tq=tq, tk=tk, nb=nb, nc=nc),
        grid=(b, s // tq),
        in_specs=[tile(ATTN_DIM), tile(LANES), seq(kc), seq(vct), seq(ks), seq(vst), seq(kw), seq(vwt)],
        out_specs=tile(ATTN_DIM),
        out_shape=jax.ShapeDtypeStruct((b, s, ATTN_DIM), BF16),
        scratch_shapes=[pltpu.VMEM((SUBLANES, ATTN_HEADS * tq), F32),
                        pltpu.VMEM((LANES, ATTN_HEADS * tq), F32),
                        pltpu.VMEM((KV_GROUPS, nb, tq), F32),
                        pltpu.VMEM((KV_GROUPS, nb, tq), F32)],
        compiler_params=_params("parallel", "arbitrary"),
        name="nsa",
    )(q, gates, kc, vct, ks, vst, kw, vwt)


def _seg_sum(x, bd, split=False):
    parts = _split2(x) if split else (x.astype(BF16),)
    w = bd.shape[0]
    return jnp.concatenate([sum(_nn(part[:, c:c + w], bd) for part in parts)
                            for c in range(0, x.shape[1], w)], axis=1)


def _stack_heads(x):
    lane = _iota((1, LANES), 1)
    lo = lane < RWKV_HEAD_DIM
    return jnp.concatenate([jnp.where(lo, x, 0.0), jnp.where(lo, 0.0, x)], axis=0)


def _rwkv_kernel(p_ref, mu_ref, w0_ref, wup_ref, a0_ref, aup_ref, gup_ref, kk_ref, ka_ref, rk_ref,
                 lnw_ref, lnb_ref, o_ref, carry_ref, state_ref, *, tt):
    ch = RWKV_CHUNK
    c2 = 2 * ch
    d = RWKV_DIM
    nseq = RWKV_SEQS
    nch = nseq * tt // ch
    npr = RWKV_PAIRS

    @pl.when(pl.program_id(1) == 0)
    def _():
        carry_ref[...] = jnp.zeros_like(carry_ref)
        state_ref[...] = jnp.zeros_like(state_ref)

    p = jnp.concatenate([p_ref[q] for q in range(nseq)], axis=0)
    rowi = _iota((nseq * tt, 1), 0)
    prev = pltpu.roll(p, 1, axis=0)
    for q in range(nseq):
        prev = jnp.where(rowi == q * tt, carry_ref[q:q + 1, :], prev)
        carry_ref[q:q + 1, :] = p[(q + 1) * tt - 1:(q + 1) * tt, :]
    pm = p + (prev - p) * mu_ref[...]

    r = pm[:, 0:d]
    k = pm[:, d:2 * d]
    v = pm[:, 2 * d:3 * d]
    wa = pm[:, 3 * d:3 * d + LANES]
    gl = pm[:, 3 * d + LANES:3 * d + 2 * LANES]

    z = w0_ref[...] + _nn(jnp.tanh(wa).astype(BF16), wup_ref[...])
    lw = -DECAY_SCALE * jax.nn.sigmoid(z)
    a = jax.nn.sigmoid(a0_ref[...] + _nn(wa.astype(BF16), aup_ref[...]))
    gate = _nn(jax.nn.sigmoid(gl).astype(BF16), gup_ref[...])

    seg = (jnp.right_shift(_iota((MXU_COLS, MXU_COLS), 0), LOG2_64)
           == jnp.right_shift(_iota((MXU_COLS, MXU_COLS), 1), LOG2_64))
    bd = jnp.where(seg, 1.0, 0.0).astype(BF16)
    kk = k * kk_ref[...]
    kk = kk * lax.rsqrt(jnp.maximum(_seg_sum(kk * kk, bd), 1e-24))
    k2 = k * (1.0 + (a - 1.0) * ka_ref[...])

    ti = _iota((nseq * tt, nseq * tt), 0)
    tj = _iota((nseq * tt, nseq * tt), 1)
    same_chunk = jnp.right_shift(ti, LOG2_64) == jnp.right_shift(tj, LOG2_64)
    tri = jnp.where(same_chunk & (tj <= ti), 1.0, 0.0).astype(BF16)
    hi, lo = _split2(lw)
    cum = _nn(tri, hi) + _nn(tri, lo)
    tot = jnp.concatenate([jnp.broadcast_to(cum[(c + 1) * ch - 1:(c + 1) * ch, :], (ch, d))
                           for c in range(nch)], axis=0)
    e_inv = jnp.exp(-cum)
    e_end = jnp.exp(tot - cum)
    dec = jnp.exp(tot)
    bb = kk * a

    def tiles(x):
        return jnp.stack([_stack_heads(x[c * ch:(c + 1) * ch, pr * LANES:(pr + 1) * LANES])
                          for c in range(nch) for pr in range(npr)]).astype(BF16)

    a_t = tiles(-kk * jnp.exp(cum - lw))
    r_t = tiles(r * jnp.exp(cum))
    b_h = tiles(bb * e_inv)
    k_h = tiles(k2 * e_inv)
    bke = jnp.concatenate([tiles(bb * e_end), tiles(k2 * e_end)], axis=1)
    v_t = tiles(v)

    ri = _iota((c2, c2), 0)
    ci = _iota((c2, c2), 1)
    same = jnp.right_shift(ri, LOG2_64) == jnp.right_shift(ci, LOG2_64)
    strict = jnp.where(same & (ci < ri), 1.0, 0.0)
    incl = jnp.where(same & (ci <= ri), 1.0, 0.0)
    ident = jnp.where(ri == ci, 1.0, 0.0)

    scores = _bnt(jnp.concatenate([a_t, r_t], axis=1), jnp.concatenate([b_h, k_h], axis=1))
    m_ab = scores[:, 0:c2, 0:c2] * strict
    m_ak = (scores[:, 0:c2, c2:2 * c2] * strict).astype(BF16)
    m_rbk = jnp.concatenate([scores[:, c2:2 * c2, 0:c2] * incl,
                             scores[:, c2:2 * c2, c2:2 * c2] * incl], axis=2).astype(BF16)
    tinv = ident + m_ab
    mp = m_ab
    for _ in range(5):
        mpb = mp.astype(BF16)
        mp = _bnn(mpb, mpb)
        tinv = tinv + _bnn(tinv.astype(BF16), mp.astype(BF16))
    rhs = jnp.concatenate([a_t, _bnn(m_ak, v_t).astype(BF16)], axis=2)
    wu = _bnn(tinv.astype(BF16), rhs)
    w_t = wu[:, :, 0:c2].astype(BF16)
    u0 = wu[:, :, c2:2 * c2]

    g_st = state_ref[...]
    cps = tt // ch
    y_rows = [[None] * cps for _ in range(nseq)]
    for c in range(cps):
        pick = lambda x: jnp.concatenate(
            [x[(q * cps + c) * npr:(q * cps + c + 1) * npr] for q in range(nseq)], axis=0)
        g_b = g_st.astype(BF16)
        u = _bnt(pick(w_t), g_b) + pick(u0)
        uv = jnp.concatenate([u.astype(BF16), pick(v_t)], axis=1)
        y = _bnt(pick(r_t), g_b) + _bnn(pick(m_rbk), uv)
        y = y[:, 0:ch, :] + y[:, ch:c2, :]
        for q in range(nseq):
            y_rows[q][c] = jnp.concatenate([y[q * npr + pr] for pr in range(npr)], axis=1)
        dec_c = jnp.stack([dec[(q * cps + c) * ch:(q * cps + c) * ch + 1, pr * LANES:(pr + 1) * LANES]
                           for q in range(nseq) for pr in range(npr)])
        g_st = g_st * dec_c + _btn(uv, pick(bke))
    state_ref[...] = g_st
    y = jnp.concatenate([row for rows in y_rows for row in rows], axis=0)

    inv = 1.0 / RWKV_HEAD_DIM
    mean = _seg_sum(y, bd, split=True) * inv
    yc = y - mean
    var = _seg_sum(yc * yc, bd) * inv
    yn = yc * lax.rsqrt(var + GN_EPS) * lnw_ref[...] + lnb_ref[...]
    bonus = _seg_sum(r * k2 * rk_ref[...], bd) * v
    out = ((yn + bonus) * gate).astype(BF16)
    for q in range(nseq):
        o_ref[q] = out[q * tt:(q + 1) * tt, :]


def _rwkv_call(p, mu, w0, wup, a0, aup, gup, k_k, k_a, r_k, ln_w, ln_b):
    b, s, _ = p.shape
    tt = RWKV_TOKEN_TILE
    full = lambda a: pl.BlockSpec(a.shape, lambda i, j: (0,) * a.ndim)
    consts = (mu, w0, wup, a0, aup, gup, k_k, k_a, r_k, ln_w, ln_b)
    return pl.pallas_call(
        functools.partial(_rwkv_kernel, tt=tt),
        grid=(b // RWKV_SEQS, s // tt),
        in_specs=[pl.BlockSpec((RWKV_SEQS, tt, RWKV_MIX_DIM), lambda i, j: (i, j, 0))]
                 + [full(c) for c in consts],
        out_specs=pl.BlockSpec((RWKV_SEQS, tt, RWKV_DIM), lambda i, j: (i, j, 0)),
        out_shape=jax.ShapeDtypeStruct((b, s, RWKV_DIM), BF16),
        scratch_shapes=[pltpu.VMEM((SUBLANES, RWKV_MIX_DIM), F32),
                        pltpu.VMEM((RWKV_SEQS * RWKV_PAIRS, LANES, LANES), F32)],
        compiler_params=_params("arbitrary", "arbitrary"),
        name="rwkv",
    )(p, *consts)


def _merge_kernel(x_ref, ya_ref, yb_ref, mg_ref, wa_ref, wb_ref, wo_ref, fg_ref, xo_ref, h_ref):
    ga = jax.nn.sigmoid(mg_ref[:, 0:D_MODEL].astype(F32))
    gb = jax.nn.sigmoid(mg_ref[:, D_MODEL:2 * D_MODEL].astype(F32))
    merged = ga * _nn(ya_ref[...], wa_ref[...]) + gb * _nn(yb_ref[...], wb_ref[...])
    xn = x_ref[...] + _nn(merged.astype(BF16), wo_ref[...])
    xo_ref[...] = xn
    ms = jnp.mean(xn * xn, axis=-1, keepdims=True)
    h_ref[...] = (xn * lax.rsqrt(ms + RMS_EPS) * fg_ref[...]).astype(BF16)


def _merge_call(x, ya, yb, mg, wa_pad, wb, wo, fgain):
    b, s, _ = x.shape
    return pl.pallas_call(
        _merge_kernel,
        grid=(b, s // ROW_TILE),
        in_specs=[_row_spec(D_MODEL), _row_spec(ATTN_DIM), _row_spec(RWKV_DIM), _row_spec(2 * D_MODEL),
                  _const_spec(wa_pad), _const_spec(wb), _const_spec(wo), _const_spec(fgain)],
        out_specs=[_row_spec(D_MODEL), _row_spec(D_MODEL)],
        out_shape=[jax.ShapeDtypeStruct((b, s, D_MODEL), F32),
                   jax.ShapeDtypeStruct((b, s, D_MODEL), BF16)],
        compiler_params=_params("parallel", "parallel"),
        name="merge",
    )(x, ya, yb, mg, wa_pad, wb, wo, fgain)


def _ffn_kernel(x_ref, h_ref, wu_ref, wd_ref, o_ref, *, fc):
    h = h_ref[...]
    acc = x_ref[...]
    for c in range(D_FF // fc):
        up = jnp.maximum(_nn(h, wu_ref[:, c * fc:(c + 1) * fc]), 0.0)
        acc = acc + _nn((up * up).astype(BF16), wd_ref[c * fc:(c + 1) * fc, :])
    o_ref[...] = acc


def _ffn_call(x, h, wu, wd, fc=1024):
    b, s, _ = x.shape
    return pl.pallas_call(
        functools.partial(_ffn_kernel, fc=fc),
        grid=(b, s // ROW_TILE),
        in_specs=[_row_spec(D_MODEL), _row_spec(D_MODEL), _const_spec(wu), _const_spec(wd)],
        out_specs=_row_spec(D_MODEL),
        out_shape=jax.ShapeDtypeStruct((b, s, D_MODEL), F32),
        compiler_params=_params("parallel", "parallel"),
        name="ffn",
    )(x, h, wu, wd)


def _pair_heads(w):
    rows = w.shape[0]
    w = w.reshape(rows, KV_GROUPS, HEADS_PER_GROUP, HEAD_DIM).transpose(0, 2, 1, 3)
    return w.reshape(rows, ATTN_DIM)


def _proj_weight(w_in):
    q = _pair_heads(w_in[:, 0:ATTN_DIM])
    kv = w_in[:, ATTN_DIM:ATTN_DIM + 6 * KV_DIM]
    gates = jnp.pad(w_in[:, ATTN_DIM + 6 * KV_DIM:NSA_DIM], ((0, 0), (0, LANES - NSA_GATES)))
    rest = w_in[:, NSA_DIM:]
    return jnp.concatenate([q, kv, gates, rest], axis=1).astype(BF16)


def _values_t(v, tile):
    b, s, _ = v.shape
    vt = v.reshape(b, s // tile, tile, KV_DIM).transpose(0, 1, 3, 2)
    row = jnp.arange(KV_DIM)[:, None] // HEAD_DIM
    one = jnp.ones((), v.dtype)
    return jnp.stack([jnp.where(row == g, vt, one) for g in range(KV_GROUPS)], axis=1)


def _compress_weights(w1, w2):
    w = w1.reshape(2, 2, CMP_STRIDE, HEAD_DIM, CMP_HIDDEN).transpose(0, 2, 3, 1, 4)
    zw = jnp.zeros_like(w)
    w1g = jnp.stack([jnp.stack([w if gg == g else zw for gg in range(KV_GROUPS)], axis=2)
                     for g in range(KV_GROUPS)], axis=1)
    w1g = w1g.reshape(2, KV_GROUPS, CMP_STRIDE * KV_DIM, 2 * CMP_HIDDEN)
    z2 = jnp.zeros_like(w2)
    w2g = jnp.stack([jnp.concatenate([w2 if gg == g else z2 for gg in range(KV_GROUPS)], axis=2)
                     for g in range(KV_GROUPS)], axis=1)
    return w1g.astype(BF16), w2g.astype(BF16)


def kernel(x, mix_norm, w_in, q_gain, k_gain, cmp_pos, cmp_w1, cmp_w2, w_attn_branch, tok_mix, w0,
           w_lora_up, a0, a_lora_up, g_lora_up, k_k, k_a, r_k, ln_x_w, ln_x_b, w_rwkv_branch, w_out,
           ffn_norm, w_ffn_up, w_ffn_down):
    b, s, d = x.shape
    assert d == D_MODEL and b % RWKV_SEQS == 0, (b, d)
    assert s % ROW_TILE == 0 and s >= WINDOW + NSA_QUERY_TILE, s
    assert ROW_TILE % NSA_KEY_TILE == 0 and NSA_KEY_TILE % NSA_QUERY_TILE == 0 and WINDOW % NSA_QUERY_TILE == 0
    assert ROW_TILE % RWKV_TOKEN_TILE == 0 and RWKV_TOKEN_TILE % RWKV_CHUNK == 0
    depth = w_in.shape[0]
    row = lambda v: v.reshape(1, -1)
    for l in range(depth):
        qg2 = jnp.tile(q_gain[l], KV_GROUPS).reshape(1, KV_DIM)
        kg2 = jnp.tile(k_gain[l, 1:3], (1, KV_GROUPS))
        q, kc, vc, ks, vs, kw, vw, gates, rw, mg = _proj_call(
            x, row(mix_norm[l]), _proj_weight(w_in[l]), qg2, kg2)

        w1g, w2g = _compress_weights(cmp_w1[l], cmp_w2[l])
        k_cmp, v_cmp = _compress_call(kc, vc, w1g, cmp_w1[l].astype(BF16), w2g,
                                      cmp_pos[l].reshape(2, 1, CMP_BLOCK * HEAD_DIM),
                                      jnp.tile(k_gain[l, 0], KV_GROUPS).reshape(1, KV_DIM))

        ya = _nsa_call(q, gates, k_cmp, v_cmp.transpose(0, 2, 1), ks,
                       _values_t(vs, NSA_KEY_TILE), kw, _values_t(vw, NSA_QUERY_TILE))

        zero = jnp.zeros((W_LORA, RWKV_DIM), F32)
        wup = jnp.concatenate([w_lora_up[l], zero], axis=0).astype(BF16)
        aup = jnp.concatenate([zero, a_lora_up[l]], axis=0).astype(BF16)
        yb = _rwkv_call(rw, row(tok_mix[l]), row(w0[l]), wup, row(a0[l]), aup,
                        g_lora_up[l].astype(BF16), row(k_k[l]), row(k_a[l]), row(r_k[l]),
                        row(ln_x_w[l]), row(ln_x_b[l]))

        wa_pad = _pair_heads(w_attn_branch[l].T).T.astype(BF16)
        x, h = _merge_call(x, ya, yb, mg, wa_pad, w_rwkv_branch[l].astype(BF16),
                           w_out[l].astype(BF16), row(ffn_norm[l]))
        x = _ffn_call(x, h, w_ffn_up[l].astype(BF16), w_ffn_down[l].astype(BF16))
    return x
```

```python
import functools

import jax
import jax.numpy as jnp
from jax import lax
from jax.experimental import pallas as pl
from jax.experimental.pallas import tpu as pltpu

F32 = jnp.float32
BF16 = jnp.bfloat16
I32 = jnp.int32

D_MODEL = 1024
ATTN_HEADS = 8
HEAD_DIM = 64
KV_GROUPS = 2
HEADS_PER_GROUP = ATTN_HEADS // KV_GROUPS
ATTN_DIM = ATTN_HEADS * HEAD_DIM
KV_DIM = KV_GROUPS * HEAD_DIM
N_NSA_BRANCHES = 3
CMP_BLOCK = 32
CMP_STRIDE = 16
CMP_HIDDEN = 2 * HEAD_DIM
SLC_BLOCK = 64
N_SELECT = 16
WINDOW = 512
FORCE_BONUS = 1000.0
RWKV_HEADS = 8
RWKV_HEAD_DIM = 64
RWKV_DIM = RWKV_HEADS * RWKV_HEAD_DIM
W_LORA = 64
A_LORA = 64
G_LORA = 128
RWKV_MIX_DIM = 3 * RWKV_DIM + W_LORA + A_LORA + G_LORA
D_FF = 4 * D_MODEL
NSA_GATES = N_NSA_BRANCHES * ATTN_HEADS
NSA_DIM = ATTN_DIM + 6 * KV_DIM + NSA_GATES
RMS_EPS = 1e-6
GN_EPS = 64e-5
NEG_INF = -1e30

LOG2_64 = 6
LANES = 128
SUBLANES = 8
MXU_COLS = 256
Q_SCALE = HEAD_DIM ** -0.5 * 1.4426950408889634
DECAY_SCALE = 0.6065306597126334
RWKV_PAIRS = RWKV_HEADS // 2
RWKV_CHUNK = 64
RWKV_SEQS = 2
RWKV_TOKEN_TILE = 128
ROW_TILE = 512
NSA_QUERY_TILE = 256
NSA_KEY_TILE = 256
VMEM_LIMIT = 48 * 1024 * 1024

C_Q = 0
C_KV = C_Q + ATTN_DIM
C_GATE = C_KV + 6 * KV_DIM
C_RWKV = C_GATE + LANES
C_MERGE = C_RWKV + RWKV_MIX_DIM


def _nn(a, b):
    return lax.dot_general(a, b, (((1,), (0,)), ((), ())), preferred_element_type=F32)


def _nt(a, b):
    return lax.dot_general(a, b, (((1,), (1,)), ((), ())), preferred_element_type=F32)


def _bnn(a, b):
    return lax.dot_general(a, b, (((2,), (1,)), ((0,), (0,))), preferred_element_type=F32)


def _bnt(a, b):
    return lax.dot_general(a, b, (((2,), (2,)), ((0,), (0,))), preferred_element_type=F32)


def _btn(a, b):
    return lax.dot_general(a, b, (((1,), (1,)), ((0,), (0,))), preferred_element_type=F32)


def _split2(x):
    hi = x.astype(BF16)
    lo = (x - hi.astype(F32)).astype(BF16)
    return hi, lo


def _iota(shape, dim):
    return lax.broadcasted_iota(I32, shape, dim)


def _params(*sem):
    return pltpu.CompilerParams(dimension_semantics=sem, vmem_limit_bytes=VMEM_LIMIT)


def _group_rms(x, gain):
    lane = _iota((1, LANES), 1)
    sq = x * x
    lo = lane < HEAD_DIM
    s0 = jnp.sum(jnp.where(lo, sq, 0.0), axis=-1, keepdims=True)
    s1 = jnp.sum(jnp.where(lo, 0.0, sq), axis=-1, keepdims=True)
    ms = jnp.where(lo, s0, s1) * (1.0 / HEAD_DIM)
    return x * lax.rsqrt(ms + RMS_EPS) * gain


def _proj_kernel(x_ref, g_ref, w_ref, qg_ref, kg_ref,
                 q_ref, kc_ref, vc_ref, ks_ref, vs_ref, kw_ref, vw_ref, gt_ref, rw_ref, mg_ref):
    x = x_ref[...]
    ms = jnp.mean(x * x, axis=-1, keepdims=True)
    u = (x * lax.rsqrt(ms + RMS_EPS) * g_ref[...]).astype(BF16)

    def col(c0, width):
        return _nn(u, w_ref[:, c0:c0 + width])

    for piece in range(ATTN_DIM // MXU_COLS):
        qq = col(C_Q + piece * MXU_COLS, MXU_COLS)
        for half in range(MXU_COLS // LANES):
            slot = slice(piece * MXU_COLS + half * LANES, piece * MXU_COLS + (half + 1) * LANES)
            qn = _group_rms(qq[:, half * LANES:(half + 1) * LANES], qg_ref[...])
            q_ref[:, slot] = (qn * Q_SCALE).astype(BF16)
    kvc = col(C_KV, MXU_COLS)
    kc_ref[...] = kvc[:, 0:KV_DIM]
    vc_ref[...] = kvc[:, KV_DIM:2 * KV_DIM]
    kvs = col(C_KV + MXU_COLS, MXU_COLS)
    ks_ref[...] = _group_rms(kvs[:, 0:KV_DIM], kg_ref[0:1, :]).astype(BF16)
    vs_ref[...] = kvs[:, KV_DIM:2 * KV_DIM].astype(BF16)
    kvw = col(C_KV + 2 * MXU_COLS, MXU_COLS)
    kw_ref[...] = _group_rms(kvw[:, 0:KV_DIM], kg_ref[1:2, :]).astype(BF16)
    vw_ref[...] = kvw[:, KV_DIM:2 * KV_DIM].astype(BF16)
    gt_ref[...] = col(C_GATE, LANES)
    for c in range(RWKV_MIX_DIM // MXU_COLS):
        rw_ref[:, c * MXU_COLS:(c + 1) * MXU_COLS] = col(C_RWKV + c * MXU_COLS, MXU_COLS)
    for c in range(2 * D_MODEL // 512):
        mg_ref[:, c * 512:(c + 1) * 512] = col(C_MERGE + c * 512, 512).astype(BF16)


def _row_spec(width):
    return pl.BlockSpec((None, ROW_TILE, width), lambda i, j: (i, j, 0))


def _const_spec(a, buffers=2):
    return pl.BlockSpec(a.shape, lambda i, j: (0,) * a.ndim, pipeline_mode=pl.Buffered(buffers))


def _proj_call(x, gain, w_all, qg2, kg2):
    b, s, _ = x.shape
    widths = [(ATTN_DIM, BF16), (KV_DIM, F32), (KV_DIM, F32), (KV_DIM, BF16), (KV_DIM, BF16),
              (KV_DIM, BF16), (KV_DIM, BF16), (LANES, F32), (RWKV_MIX_DIM, F32), (2 * D_MODEL, BF16)]
    return pl.pallas_call(
        _proj_kernel,
        grid=(b, s // ROW_TILE),
        in_specs=[_row_spec(D_MODEL), _const_spec(gain), _const_spec(w_all, buffers=1),
                  _const_spec(qg2), _const_spec(kg2)],
        out_specs=[_row_spec(w) for w, _ in widths],
        out_shape=[jax.ShapeDtypeStruct((b, s, w), dt) for w, dt in widths],
        compiler_params=_params("parallel", "parallel"),
        name="proj",
    )(x, gain, w_all, qg2, kg2)


def _compress_kernel(kc_ref, vc_ref, w1g_ref, w1_ref, w2g_ref, pos_ref, kg_ref, ko_ref, vo_ref, *, nc):
    rowi = _iota((nc, 1), 0)
    for which, (src, dst) in enumerate(((kc_ref, ko_ref), (vc_ref, vo_ref))):
        xcat = jnp.concatenate([src[0, pl.ds(l, nc, stride=CMP_STRIDE), :] for l in range(CMP_STRIDE)],
                               axis=1).astype(BF16)
        pos8 = jnp.broadcast_to(pos_ref[which], (SUBLANES, CMP_BLOCK * HEAD_DIM)).astype(BF16)
        bias = _nn(pos8, w1_ref[which])[0:1, :]
        out = jnp.zeros((nc, KV_DIM), F32)
        for g in range(KV_GROUPS):
            hab = _nn(xcat, w1g_ref[which, g])
            hid = hab[:, 0:CMP_HIDDEN] + pltpu.roll(hab[:, CMP_HIDDEN:2 * CMP_HIDDEN], nc - 1, axis=0) + bias
            out = out + _nn(jax.nn.gelu(hid).astype(BF16), w2g_ref[which, g])
        if which == 0:
            out = _group_rms(out, kg_ref[...])
        dst[0] = jnp.where(rowi < nc - 1, out, 0.0).astype(BF16)


def _compress_call(kc, vc, w1g, w1, w2g, pos_flat, kgain2):
    b, s, _ = kc.shape
    nc = s // CMP_STRIDE
    seq = pl.BlockSpec((1, s, KV_DIM), lambda i: (i, 0, 0))
    full = lambda a: pl.BlockSpec(a.shape, lambda i: (0,) * a.ndim)
    out = pl.BlockSpec((1, nc, KV_DIM), lambda i: (i, 0, 0))
    return pl.pallas_call(
        functools.partial(_compress_kernel, nc=nc),
        grid=(b,),
        in_specs=[seq, seq, full(w1g), full(w1), full(w2g), full(pos_flat), full(kgain2)],
        out_specs=[out, out],
        out_shape=[jax.ShapeDtypeStruct((b, nc, KV_DIM), BF16)] * 2,
        compiler_params=_params("parallel"),
        name="compress",
    )(kc, vc, w1g, w1, w2g, pos_flat, kgain2)


def _nsa_kernel(q_ref, gt_ref, kc_ref, vct_ref, ks_ref, vst_ref, kw_ref, vwt_ref, y_ref,
                m_s, acc_s, sb_s, rk_s, *, tq, tk, nb, nc):
    hp = HEADS_PER_GROUP
    cols = hp * tq
    wk = WINDOW + tq
    bpt = tk // SLC_BLOCK
    qi = pl.program_id(1)
    q0 = qi * tq
    tlane = q0 + _iota((1, tq), 1)
    tile_h = lambda a, reps: jnp.concatenate([a] * reps, axis=1)
    eye_d = jnp.where(_iota((LANES, LANES), 0) == _iota((LANES, LANES), 1), 1.0, 0.0).astype(BF16)
    eye_q = jnp.where(_iota((tq, tq), 0) == _iota((tq, tq), 1), 1.0, 0.0).astype(BF16)
    drow = _iota((LANES, 1), 0)
    own = [(drow >= g * HEAD_DIM) & (drow < (g + 1) * HEAD_DIM) for g in range(KV_GROUPS)]

    lane = _iota((1, LANES), 1)
    glanes = [jnp.where((lane >= g * HEAD_DIM) & (lane < (g + 1) * HEAD_DIM), 1.0, 0.0).astype(BF16)
              for g in range(KV_GROUPS)]
    qall = jnp.concatenate([q_ref[0, :, h * LANES:(h + 1) * LANES] * glanes[g]
                            for g in range(KV_GROUPS) for h in range(hp)], axis=0)
    q_t = _nt(eye_d, qall).astype(BF16)

    ncol = _iota((nc, 1), 0)
    cvalid = (ncol * CMP_STRIDE + (CMP_BLOCK - 1) <= tlane) & (ncol < nc - 1)
    c_bias = tile_h(jnp.where(cvalid, 0.0, NEG_INF), hp)
    c_keep = tile_h(jnp.where(cvalid, 1.0, 0.0), hp)
    p_c, o_c = [], []
    for g in range(KV_GROUPS):
        s_c = _nn(kc_ref[0], q_t[:, g * cols:(g + 1) * cols]) + c_bias
        e_c = jnp.exp2(s_c - jnp.max(s_c, axis=0, keepdims=True))
        p_c.append(e_c * (1.0 / jnp.sum(e_c, axis=0, keepdims=True)) * c_keep)
        o_c.append(_nn(vct_ref[0], p_c[g].astype(BF16)))

    jcol = _iota((nb, 1), 0)
    ncmp = _iota((nb, nc), 1)
    ov = ((ncmp * CMP_STRIDE <= jcol * SLC_BLOCK + (SLC_BLOCK - 1))
          & (ncmp * CMP_STRIDE + (CMP_BLOCK - 1) >= jcol * SLC_BLOCK))
    ov = jnp.where(ov, 1.0, 0.0).astype(BF16)
    cur = jnp.right_shift(tlane, LOG2_64)
    forced = (jcol == 0) | (jcol == cur) | (jcol == cur - 1)
    causal_blk = jcol * SLC_BLOCK <= tlane
    imps = []
    for g in range(KV_GROUPS):
        psum = (p_c[g][:, 0:tq] + p_c[g][:, tq:2 * tq]
                + p_c[g][:, 2 * tq:3 * tq] + p_c[g][:, 3 * tq:4 * tq])
        hi, lo = _split2(psum)
        imp = _nn(ov, hi) + _nn(ov, lo)
        imps.append(jnp.where(causal_blk, imp + jnp.where(forced, FORCE_BONUS, 0.0), -1.0))

    rk_s[...] = jnp.zeros(rk_s.shape, F32)
    jsub = _iota((SUBLANES, 1), 0)
    last_blk = jnp.right_shift(q0 + tq - 1, LOG2_64)
    for sg in range(nb // SUBLANES):
        @pl.when(sg * SUBLANES <= last_blk)
        def _(sg=sg):
            for g, imp in enumerate(imps):
                groups = [imp[r:r + SUBLANES, :] for r in range(0, nb, SUBLANES)]
                ranks = [jnp.zeros((SUBLANES, tq), F32) for _ in groups]
                for i in range(sg * SUBLANES, (sg + 1) * SUBLANES):
                    ri = imp[i:i + 1, :]
                    for gi, grp in enumerate(groups):
                        r0 = gi * SUBLANES
                        if r0 + SUBLANES - 1 < i:
                            ahead = jnp.where(ri > grp, 1.0, 0.0)
                        elif r0 > i:
                            ahead = jnp.where(ri >= grp, 1.0, 0.0)
                        else:
                            ahead = jnp.where(jsub > i - r0, jnp.where(ri >= grp, 1.0, 0.0),
                                              jnp.where(ri > grp, 1.0, 0.0))
                        ranks[gi] = ranks[gi] + ahead
                rk_s[g] = rk_s[g] + jnp.concatenate(ranks, axis=0)
    sb_s[...] = jnp.where(rk_s[...] < float(min(N_SELECT, nb)), 0.0, NEG_INF)

    m_s[...] = jnp.full(m_s.shape, NEG_INF, F32)
    acc_s[...] = jnp.zeros(acc_s.shape, F32)

    def sweep(tiles):
        scores = [_nn(ks_ref[0, pl.ds(pl.multiple_of(kt * tk, tk), tk), :], q_t).astype(BF16)
                  for kt in tiles]
        m_run = m_s[0:1, :]
        acc_run = [acc_s[:, g * cols:(g + 1) * cols] for g in range(KV_GROUPS)]
        m_out = [None] * KV_GROUPS
        for kt, s in zip(tiles, scores):
            causal = (kt * tk + _iota((tk, 1), 0)) <= tlane
            for g in range(KV_GROUPS):
                cs = slice(g * cols, (g + 1) * cols)
                bias = jnp.concatenate(
                    [jnp.broadcast_to(sb_s[g, pl.ds(kt * bpt + i, 1), :], (SLC_BLOCK, tq))
                     for i in range(bpt)], axis=0)
                sm = s[:, cs] + tile_h(jnp.where(causal, bias, NEG_INF).astype(BF16), hp)
                m_prev = m_run[:, cs] if m_out[g] is None else m_out[g]
                m_new = jnp.maximum(m_prev, jnp.max(sm, axis=0, keepdims=True).astype(F32))
                p = jnp.exp2(sm - m_new.astype(BF16))
                acc_run[g] = jnp.exp2(m_prev - m_new) * acc_run[g] + _nn(vst_ref[0, g, kt], p)
                m_out[g] = m_new
        for g in range(KV_GROUPS):
            cs = slice(g * cols, (g + 1) * cols)
            m_s[:, cs] = jnp.broadcast_to(m_out[g], (SUBLANES, cols))
            acc_s[:, cs] = acc_run[g]

    n_tiles = qi // (tk // tq) + 1

    def quad(i, _):
        sweep([4 * i + t for t in range(4)])
        return 0

    lax.fori_loop(0, n_tiles // 4, quad, 0)
    rest = n_tiles % 4
    done = n_tiles - rest

    @pl.when(rest >= 2)
    def _():
        sweep([done, done + 1])

    @pl.when(rest % 2 == 1)
    def _():
        sweep([n_tiles - 1])

    w0 = pl.multiple_of(jnp.maximum(q0 - WINDOW, 0), tq)
    dist = tlane - (w0 + _iota((wk, 1), 0))
    k_w = kw_ref[0, pl.ds(w0, wk), :]
    w_bias = tile_h(jnp.where((dist >= 0) & (dist < WINDOW), 0.0, NEG_INF).astype(BF16), hp)
    wb = w0 // tq

    g_hi, g_lo = _split2(jax.nn.sigmoid(gt_ref[0]))
    gate_t = _nt(eye_d, g_hi) + _nt(eye_d, g_lo)
    gated = []
    for g in range(KV_GROUPS):
        cs = slice(g * cols, (g + 1) * cols)
        den_row = (1 - g) * HEAD_DIM
        acc = acc_s[:, cs]
        o_s = acc * (1.0 / acc[den_row:den_row + 1, :])
        v_w = jnp.concatenate([vwt_ref[0, g, wb + i] for i in range(wk // tq)], axis=1)
        s_w = _nn(k_w, q_t[:, cs]).astype(BF16) + w_bias
        acc_w = _nn(v_w, jnp.exp2(s_w - jnp.max(s_w, axis=0, keepdims=True)))
        o_w = acc_w * (1.0 / acc_w[den_row:den_row + 1, :])
        for h in range(hp):
            r0 = (g * hp + h) * N_NSA_BRANCHES
            hs = slice(h * tq, (h + 1) * tq)
            gated.append(gate_t[r0:r0 + 1, :] * o_c[g][:, hs] + gate_t[r0 + 1:r0 + 2, :] * o_s[:, hs]
                         + gate_t[r0 + 2:r0 + 3, :] * o_w[:, hs])
    for h in range(hp):
        o = jnp.where(own[0], gated[h], gated[hp + h]).astype(BF16)
        y_ref[0, :, h * LANES:(h + 1) * LANES] = _nt(eye_q, o).astype(BF16)


def _nsa_call(q, gates, kc, vct, ks, vst, kw, vwt):
    b, s, _ = q.shape
    tq = NSA_QUERY_TILE
    nb = s // SLC_BLOCK
    nc = kc.shape[1]
    tk = vst.shape[-1]
    tile = lambda w: pl.BlockSpec((1, tq, w), lambda i, j: (i, j, 0))
    seq = lambda a: pl.BlockSpec((1,) + a.shape[1:], lambda i, j: (i,) + (0,) * (a.ndim - 1))
    return pl.pallas_call(
        functools.partial(_nsa_kernel, tq=tq, tk=tk, nb=nb, nc=nc),
        grid=(b, s // tq),
        in_specs=[tile(ATTN_DIM), tile(LANES), seq(kc), seq(vct), seq(ks), seq(vst), seq(kw), seq(vwt)],
        out_specs=tile(ATTN_DIM),
        out_shape=jax.ShapeDtypeStruct((b, s, ATTN_DIM), BF16),
        scratch_shapes=[pltpu.VMEM((SUBLANES, ATTN_HEADS * tq), F32),
                        pltpu.VMEM((LANES, ATTN_HEADS * tq), F32),
                        pltpu.VMEM((KV_GROUPS, nb, tq), F32),
                        pltpu.VMEM((KV_GROUPS, nb, tq), F32)],
        compiler_params=_params("parallel", "arbitrary"),
        name="nsa",
    )(q, gates, kc, vct, ks, vst, kw, vwt)


def _seg_sum(x, bd, split=False):
    parts = _split2(x) if split else (x.astype(BF16),)
    w = bd.shape[0]
    return jnp.concatenate([sum(_nn(part[:, c:c + w], bd) for part in parts)
                            for c in range(0, x.shape[1], w)], axis=1)


def _stack_heads(x):
    lane = _iota((1, LANES), 1)
    lo = lane < RWKV_HEAD_DIM
    return jnp.concatenate([jnp.where(lo, x, 0.0), jnp.where(lo, 0.0, x)], axis=0)


def _rwkv_kernel(p_ref, mu_ref, w0_ref, wup_ref, a0_ref, aup_ref, gup_ref, kk_ref, ka_ref, rk_ref,
                 lnw_ref, lnb_ref, o_ref, carry_ref, state_ref, *, tt):
    ch = RWKV_CHUNK
    c2 = 2 * ch
    d = RWKV_DIM
    nseq = RWKV_SEQS
    nch = nseq * tt // ch
    npr = RWKV_PAIRS

    @pl.when(pl.program_id(1) == 0)
    def _():
        carry_ref[...] = jnp.zeros_like(carry_ref)
        state_ref[...] = jnp.zeros_like(state_ref)

    p = jnp.concatenate([p_ref[q] for q in range(nseq)], axis=0)
    rowi = _iota((nseq * tt, 1), 0)
    prev = pltpu.roll(p, 1, axis=0)
    for q in range(nseq):
        prev = jnp.where(rowi == q * tt, carry_ref[q:q + 1, :], prev)
        carry_ref[q:q + 1, :] = p[(q + 1) * tt - 1:(q + 1) * tt, :]
    pm = p + (prev - p) * mu_ref[...]

    r = pm[:, 0:d]
    k = pm[:, d:2 * d]
    v = pm[:, 2 * d:3 * d]
    wa = pm[:, 3 * d:3 * d + LANES]
    gl = pm[:, 3 * d + LANES:3 * d + 2 * LANES]

    z = w0_ref[...] + _nn(jnp.tanh(wa).astype(BF16), wup_ref[...])
    lw = -DECAY_SCALE * jax.nn.sigmoid(z)
    a = jax.nn.sigmoid(a0_ref[...] + _nn(wa.astype(BF16), aup_ref[...]))
    gate = _nn(jax.nn.sigmoid(gl).astype(BF16), gup_ref[...])

    seg = (jnp.right_shift(_iota((MXU_COLS, MXU_COLS), 0), LOG2_64)
           == jnp.right_shift(_iota((MXU_COLS, MXU_COLS), 1), LOG2_64))
    bd = jnp.where(seg, 1.0, 0.0).astype(BF16)
    kk = k * kk_ref[...]
    kk = kk * lax.rsqrt(jnp.maximum(_seg_sum(kk * kk, bd), 1e-24))
    k2 = k * (1.0 + (a - 1.0) * ka_ref[...])

    ti = _iota((nseq * tt, nseq * tt), 0)
    tj = _iota((nseq * tt, nseq * tt), 1)
    same_chunk = jnp.right_shift(ti, LOG2_64) == jnp.right_shift(tj, LOG2_64)
    tri = jnp.where(same_chunk & (tj <= ti), 1.0, 0.0).astype(BF16)
    hi, lo = _split2(lw)
    cum = _nn(tri, hi) + _nn(tri, lo)
    tot = jnp.concatenate([jnp.broadcast_to(cum[(c + 1) * ch - 1:(c + 1) * ch, :], (ch, d))
                           for c in range(nch)], axis=0)
    e_inv = jnp.exp(-cum)
    e_end = jnp.exp(tot - cum)
    dec = jnp.exp(tot)
    bb = kk * a

    def tiles(x):
        return jnp.stack([_stack_heads(x[c * ch:(c + 1) * ch, pr * LANES:(pr + 1) * LANES])
                          for c in range(nch) for pr in range(npr)]).astype(BF16)

    a_t = tiles(-kk * jnp.exp(cum - lw))
    r_t = tiles(r * jnp.exp(cum))
    b_h = tiles(bb * e_inv)
    k_h = tiles(k2 * e_inv)
    bke = jnp.concatenate([tiles(bb * e_end), tiles(k2 * e_end)], axis=1)
    v_t = tiles(v)

    ri = _iota((c2, c2), 0)
    ci = _iota((c2, c2), 1)
    same = jnp.right_shift(ri, LOG2_64) == jnp.right_shift(ci, LOG2_64)
    strict = jnp.where(same & (ci < ri), 1.0, 0.0)
    incl = jnp.where(same & (ci <= ri), 1.0, 0.0)
    ident = jnp.where(ri == ci, 1.0, 0.0)

    scores = _bnt(jnp.concatenate([a_t, r_t], axis=1), jnp.concatenate([b_h, k_h], axis=1))
    m_ab = scores[:, 0:c2, 0:c2] * strict
    m_ak = (scores[:, 0:c2, c2:2 * c2] * strict).astype(BF16)
    m_rbk = jnp.concatenate([scores[:, c2:2 * c2, 0:c2] * incl,
                             scores[:, c2:2 * c2, c2:2 * c2] * incl], axis=2).astype(BF16)
    tinv = ident + m_ab
    mp = m_ab
    for _ in range(5):
        mpb = mp.astype(BF16)
        mp = _bnn(mpb, mpb)
        tinv = tinv + _bnn(tinv.astype(BF16), mp.astype(BF16))
    rhs = jnp.concatenate([a_t, _bnn(m_ak, v_t).astype(BF16)], axis=2)
    wu = _bnn(tinv.astype(BF16), rhs)
    w_t = wu[:, :, 0:c2].astype(BF16)
    u0 = wu[:, :, c2:2 * c2]

    g_st = state_ref[...]
    cps = tt // ch
    y_rows = [[None] * cps for _ in range(nseq)]
    for c in range(cps):
        pick = lambda x: jnp.concatenate(
            [x[(q * cps + c) * npr:(q * cps + c + 1) * npr] for q in range(nseq)], axis=0)
        g_b = g_st.astype(BF16)
        u = _bnt(pick(w_t), g_b) + pick(u0)
        uv = jnp.concatenate([u.astype(BF16), pick(v_t)], axis=1)
        y = _bnt(pick(r_t), g_b) + _bnn(pick(m_rbk), uv)
        y = y[:, 0:ch, :] + y[:, ch:c2, :]
        for q in range(nseq):
            y_rows[q][c] = jnp.concatenate([y[q * npr + pr] for pr in range(npr)], axis=1)
        dec_c = jnp.stack([dec[(q * cps + c) * ch:(q * cps + c) * ch + 1, pr * LANES:(pr + 1) * LANES]
                           for q in range(nseq) for pr in range(npr)])
        g_st = g_st * dec_c + _btn(uv, pick(bke))
    state_ref[...] = g_st
    y = jnp.concatenate([row for rows in y_rows for row in rows], axis=0)

    inv = 1.0 / RWKV_HEAD_DIM
    mean = _seg_sum(y, bd, split=True) * inv
    yc = y - mean
    var = _seg_sum(yc * yc, bd) * inv
    yn = yc * lax.rsqrt(var + GN_EPS) * lnw_ref[...] + lnb_ref[...]
    bonus = _seg_sum(r * k2 * rk_ref[...], bd) * v
    out = ((yn + bonus) * gate).astype(BF16)
    for q in range(nseq):
        o_ref[q] = out[q * tt:(q + 1) * tt, :]


def _rwkv_call(p, mu, w0, wup, a0, aup, gup, k_k, k_a, r_k, ln_w, ln_b):
    b, s, _ = p.shape
    tt = RWKV_TOKEN_TILE
    full = lambda a: pl.BlockSpec(a.shape, lambda i, j: (0,) * a.ndim)
    consts = (mu, w0, wup, a0, aup, gup, k_k, k_a, r_k, ln_w, ln_b)
    return pl.pallas_call(
        functools.partial(_rwkv_kernel, tt=tt),
        grid=(b // RWKV_SEQS, s // tt),
        in_specs=[pl.BlockSpec((RWKV_SEQS, tt, RWKV_MIX_DIM), lambda i, j: (i, j, 0))]
                 + [full(c) for c in consts],
        out_specs=pl.BlockSpec((RWKV_SEQS, tt, RWKV_DIM), lambda i, j: (i, j, 0)),
        out_shape=jax.ShapeDtypeStruct((b, s, RWKV_DIM), BF16),
        scratch_shapes=[pltpu.VMEM((SUBLANES, RWKV_MIX_DIM), F32),
                        pltpu.VMEM((RWKV_SEQS * RWKV_PAIRS, LANES, LANES), F32)],
        compiler_params=_params("arbitrary", "arbitrary"),
        name="rwkv",
    )(p, *consts)


def _merge_kernel(x_ref, ya_ref, yb_ref, mg_ref, wa_ref, wb_ref, wo_ref, fg_ref, xo_ref, h_ref):
    ga = jax.nn.sigmoid(mg_ref[:, 0:D_MODEL].astype(F32))
    gb = jax.nn.sigmoid(mg_ref[:, D_MODEL:2 * D_MODEL].astype(F32))
    merged = ga * _nn(ya_ref[...], wa_ref[...]) + gb * _nn(yb_ref[...], wb_ref[...])
    xn = x_ref[...] + _nn(merged.astype(BF16), wo_ref[...])
    xo_ref[...] = xn
    ms = jnp.mean(xn * xn, axis=-1, keepdims=True)
    h_ref[...] = (xn * lax.rsqrt(ms + RMS_EPS) * fg_ref[...]).astype(BF16)


def _merge_call(x, ya, yb, mg, wa_pad, wb, wo, fgain):
    b, s, _ = x.shape
    return pl.pallas_call(
        _merge_kernel,
        grid=(b, s // ROW_TILE),
        in_specs=[_row_spec(D_MODEL), _row_spec(ATTN_DIM), _row_spec(RWKV_DIM), _row_spec(2 * D_MODEL),
                  _const_spec(wa_pad), _const_spec(wb), _const_spec(wo), _const_spec(fgain)],
        out_specs=[_row_spec(D_MODEL), _row_spec(D_MODEL)],
        out_shape=[jax.ShapeDtypeStruct((b, s, D_MODEL), F32),
                   jax.ShapeDtypeStruct((b, s, D_MODEL), BF16)],
        compiler_params=_params("parallel", "parallel"),
        name="merge",
    )(x, ya, yb, mg, wa_pad, wb, wo, fgain)


def _ffn_kernel(x_ref, h_ref, wu_ref, wd_ref, o_ref, *, fc):
    h = h_ref[...]
    acc = x_ref[...]
    for c in range(D_FF // fc):
        up = jnp.maximum(_nn(h, wu_ref[:, c * fc:(c + 1) * fc]), 0.0)
        acc = acc + _nn((up * up).astype(BF16), wd_ref[c * fc:(c + 1) * fc, :])
    o_ref[...] = acc


def _ffn_call(x, h, wu, wd, fc=1024):
    b, s, _ = x.shape
    return pl.pallas_call(
        functools.partial(_ffn_kernel, fc=fc),
        grid=(b, s // ROW_TILE),
        in_specs=[_row_spec(D_MODEL), _row_spec(D_MODEL), _const_spec(wu), _const_spec(wd)],
        out_specs=_row_spec(D_MODEL),
        out_shape=jax.ShapeDtypeStruct((b, s, D_MODEL), F32),
        compiler_params=_params("parallel", "parallel"),
        name="ffn",
    )(x, h, wu, wd)


def _pair_heads(w):
    rows = w.shape[0]
    w = w.reshape(rows, KV_GROUPS, HEADS_PER_GROUP, HEAD_DIM).transpose(0, 2, 1, 3)
    return w.reshape(rows, ATTN_DIM)


def _proj_weight(w_in):
    q = _pair_heads(w_in[:, 0:ATTN_DIM])
    kv = w_in[:, ATTN_DIM:ATTN_DIM + 6 * KV_DIM]
    gates = jnp.pad(w_in[:, ATTN_DIM + 6 * KV_DIM:NSA_DIM], ((0, 0), (0, LANES - NSA_GATES)))
    rest = w_in[:, NSA_DIM:]
    return jnp.concatenate([q, kv, gates, rest], axis=1).astype(BF16)


def _values_t(v, tile):
    b, s, _ = v.shape
    vt = v.reshape(b, s // tile, tile, KV_DIM).transpose(0, 1, 3, 2)
    row = jnp.arange(KV_DIM)[:, None] // HEAD_DIM
    one = jnp.ones((), v.dtype)
    return jnp.stack([jnp.where(row == g, vt, one) for g in range(KV_GROUPS)], axis=1)


def _compress_weights(w1, w2):
    w = w1.reshape(2, 2, CMP_STRIDE, HEAD_DIM, CMP_HIDDEN).transpose(0, 2, 3, 1, 4)
    zw = jnp.zeros_like(w)
    w1g = jnp.stack([jnp.stack([w if gg == g else zw for gg in range(KV_GROUPS)], axis=2)
                     for g in range(KV_GROUPS)], axis=1)
    w1g = w1g.reshape(2, KV_GROUPS, CMP_STRIDE * KV_DIM, 2 * CMP_HIDDEN)
    z2 = jnp.zeros_like(w2)
    w2g = jnp.stack([jnp.concatenate([w2 if gg == g else z2 for gg in range(KV_GROUPS)], axis=2)
                     for g in range(KV_GROUPS)], axis=1)
    return w1g.astype(BF16), w2g.astype(BF16)


def kernel(x, mix_norm, w_in, q_gain, k_gain, cmp_pos, cmp_w1, cmp_w2, w_attn_branch, tok_mix, w0,
           w_lora_up, a0, a_lora_up, g_lora_up, k_k, k_a, r_k, ln_x_w, ln_x_b, w_rwkv_branch, w_out,
           ffn_norm, w_ffn_up, w_ffn_down):
    b, s, d = x.shape
    assert d == D_MODEL and b % RWKV_SEQS == 0, (b, d)
    assert s % ROW_TILE == 0 and s >= WINDOW + NSA_QUERY_TILE, s
    assert ROW_TILE % NSA_KEY_TILE == 0 and NSA_KEY_TILE % NSA_QUERY_TILE == 0 and WINDOW % NSA_QUERY_TILE == 0
    assert ROW_TILE % RWKV_TOKEN_TILE == 0 and RWKV_TOKEN_TILE % RWKV_CHUNK == 0
    depth = w_in.shape[0]
    row = lambda v: v.reshape(1, -1)
    for l in range(depth):
        qg2 = jnp.tile(q_gain[l], KV_GROUPS).reshape(1, KV_DIM)
        kg2 = jnp.tile(k_gain[l, 1:3], (1, KV_GROUPS))
        q, kc, vc, ks, vs, kw, vw, gates, rw, mg = _proj_call(
            x, row(mix_norm[l]), _proj_weight(w_in[l]), qg2, kg2)

        w1g, w2g = _compress_weights(cmp_w1[l], cmp_w2[l])
        k_cmp, v_cmp = _compress_call(kc, vc, w1g, cmp_w1[l].astype(BF16), w2g,
                                      cmp_pos[l].reshape(2, 1, CMP_BLOCK * HEAD_DIM),
                                      jnp.tile(k_gain[l, 0], KV_GROUPS).reshape(1, KV_DIM))

        ya = _nsa_call(q, gates, k_cmp, v_cmp.transpose(0, 2, 1), ks,
                       _values_t(vs, NSA_KEY_TILE), kw, _values_t(vw, NSA_QUERY_TILE))

        zero = jnp.zeros((W_LORA, RWKV_DIM), F32)
        wup = jnp.concatenate([w_lora_up[l], zero], axis=0).astype(BF16)
        aup = jnp.concatenate([zero, a_lora_up[l]], axis=0).astype(BF16)
        yb = _rwkv_call(rw, row(tok_mix[l]), row(w0[l]), wup, row(a0[l]), aup,
                        g_lora_up[l].astype(BF16), row(k_k[l]), row(k_a[l]), row(r_k[l]),
                        row(ln_x_w[l]), row(ln_x_b[l]))

        wa_pad = _pair_heads(w_attn_branch[l].T).T.astype(BF16)
        x, h = _merge_call(x, ya, yb, mg, wa_pad, w_rwkv_branch[l].astype(BF16),
                           w_out[l].astype(BF16), row(ffn_norm[l]))
        x = _ffn_call(x, h, w_ffn_up[l].astype(BF16), w_ffn_down[l].astype(BF16))
    return x
```
